```python
import math
import jax, jax.numpy as jnp
from jax import lax
import numpy as np

D_MODEL = 1024
BATCH = 2
SEQ = 8192
DEPTH = 2
DEC_BATCH = 8
DEC_SEQ = 16
PAST_LEN = 4096

CHUNK = 64
N_ATTN_LAYERS = (DEPTH + 1) // 2
N_SSM_LAYERS = DEPTH // 2
EPS = 1e-6
F32 = jnp.float32

A_HEADS = 8
A_KV_HEADS = 2
A_GROUP = A_HEADS // A_KV_HEADS
HEAD_DIM = 64
A_WIDTH = A_HEADS * HEAD_DIM
KV_WIDTH = A_KV_HEADS * HEAD_DIM
WINDOW = 128
BAND_CHUNKS = WINDOW // CHUNK
ROPE_DIM = HEAD_DIM // 4
ROPE_THETA = 500000.0
ATTN_SCALE = HEAD_DIM ** -0.5
WIN_CACHE = min(WINDOW, PAST_LEN)
WIN_PROMPT = min(WINDOW, SEQ)

B_WIDTH = D_MODEL // 2
B_CONV = 3
AB_IN = A_WIDTH + 2 * KV_WIDTH + 3 * B_WIDTH
AB_OUT_IN = A_WIDTH + B_WIDTH

SSM_INNER = 2 * D_MODEL
SSM_HEADDIM = 64
SSM_HEADS = SSM_INNER // SSM_HEADDIM
SSM_GROUPS = 4
SSM_HPG = SSM_HEADS // SSM_GROUPS
SSM_STATE = 128
SSM_CONV = 4
SSM_GN = SSM_GROUPS * SSM_STATE
SSM_CONV_CH = SSM_INNER + 2 * SSM_GN
SSM_IN = 2 * SSM_INNER + 2 * SSM_GN + SSM_HEADS

D_FF = -(-8 * D_MODEL // (3 * 256)) * 256

kernel_name = 'hybrid_chunk_streaming_encoder_step'


def rmsnorm(x, g):
    xf = x.astype(F32)
    xf = xf * lax.rsqrt(jnp.mean(xf * xf, axis=-1, keepdims=True) + EPS)
    return (xf * g.astype(F32)).astype(x.dtype)


def rope_partial(x, pos):
    half = ROPE_DIM // 2
    inv_freq = jnp.power(ROPE_THETA, -jnp.arange(half, dtype=F32) / half)
    ang = pos[:, None] * inv_freq[None, :]
    cos = jnp.cos(ang)[:, None, :].astype(x.dtype)
    sin = jnp.sin(ang)[:, None, :].astype(x.dtype)
    x1 = x[..., :half]
    x2 = x[..., half:ROPE_DIM]
    return jnp.concatenate([x1 * cos - x2 * sin, x2 * cos + x1 * sin, x[..., ROPE_DIM:]], axis=-1)


def sink_attention(q, k, v, sinks, mask):
    s = jnp.einsum('...qkgd,...skd->...kgqs', q, k).astype(F32) * ATTN_SCALE
    if mask is not None:
        s = jnp.where(mask, s, -jnp.inf)
    sink = sinks.astype(F32).reshape(A_KV_HEADS, A_GROUP, 1, 1)
    m = jnp.maximum(jnp.max(s, axis=-1, keepdims=True), sink)
    p = jnp.exp(s - m)
    p = p / (jnp.sum(p, axis=-1, keepdims=True) + jnp.exp(sink - m))
    return jnp.einsum('...kgqs,...skd->...qkgd', p.astype(v.dtype), v)


def band_attention(q, k, v, sinks):
    b, L = q.shape[0], q.shape[1]
    nc = L // CHUNK
    nb = BAND_CHUNKS + 1
    qb = q.reshape(b, nc, CHUNK, A_KV_HEADS, A_GROUP, HEAD_DIM)

    def bands(t):
        tp = jnp.pad(t, ((0, 0), (BAND_CHUNKS * CHUNK, 0), (0, 0), (0, 0)))
        tp = tp.reshape(b, nc + BAND_CHUNKS, CHUNK, A_KV_HEADS, HEAD_DIM)
        return jnp.concatenate([tp[:, j:j + nc] for j in range(nb)], axis=2)

    key_chunk = jnp.arange(nc)[:, None] - BAND_CHUNKS + jnp.arange(nb * CHUNK)[None, :] // CHUNK
    mask = (key_chunk >= 0)[None, :, None, None, None, :]
    o = sink_attention(qb, bands(k), bands(v), sinks, mask)
    return o.reshape(b, L, A_WIDTH)


def causal_dwconv(u, hist, w):
    width = w.shape[0]
    L = u.shape[1]
    up = jnp.concatenate([hist.astype(u.dtype), u], axis=1)
    y = up[:, 0:L] * w[0]
    for j in range(1, width):
        y = y + up[:, j:j + L] * w[j]
    return y, up[:, L:]


def mixer_attn_conv(h, pos, sconv_hist, kv_cache, w_in, w_out, sinks, conv_w):
    b, L, _ = h.shape
    p = h @ w_in
    o1 = A_WIDTH
    o2 = o1 + KV_WIDTH
    o3 = o2 + KV_WIDTH
    o4 = o3 + B_WIDTH
    o5 = o4 + B_WIDTH
    q = rope_partial(p[..., :o1].reshape(b, L, A_HEADS, HEAD_DIM), pos)
    k = rope_partial(p[..., o1:o2].reshape(b, L, A_KV_HEADS, HEAD_DIM), pos)
    v = p[..., o2:o3].reshape(b, L, A_KV_HEADS, HEAD_DIM)
    gate_b = p[..., o3:o4]
    gate_c = p[..., o4:o5]
    hx = p[..., o5:]
    qg = q.reshape(b, L, A_KV_HEADS, A_GROUP, HEAD_DIM)
    if kv_cache is None:
        attn = band_attention(qg, k, v, sinks)
        new_k = k[:, L - WIN_PROMPT:]
        new_v = v[:, L - WIN_PROMPT:]
    else:
        cache_k, cache_v = kv_cache
        k_all = jnp.concatenate([cache_k.astype(k.dtype), k], axis=1)
        v_all = jnp.concatenate([cache_v.astype(v.dtype), v], axis=1)
        attn = sink_attention(qg, k_all, v_all, sinks, None).reshape(b, L, A_WIDTH)
        new_k = k
        new_v = v
    conv_y, new_hist = causal_dwconv(gate_c * hx, sconv_hist, conv_w)
    out = jnp.concatenate([attn, gate_b * conv_y], axis=-1) @ w_out
    return out, new_k, new_v, new_hist


def ssd_scan(x, dt, a, bm, cm, h0):
    b, L = x.shape[0], x.shape[1]
    lc = min(CHUNK, L)
    nc = L // lc
    x = x.reshape(b, nc, lc, SSM_GROUPS, SSM_HPG, SSM_HEADDIM)
    dt = dt.reshape(b, nc, lc, SSM_GROUPS, SSM_HPG)
    bm = bm.reshape(b, nc, lc, SSM_GROUPS, SSM_STATE)
    cm = cm.reshape(b, nc, lc, SSM_GROUPS, SSM_STATE)
    h0 = h0.reshape(b, SSM_GROUPS, SSM_HPG, SSM_HEADDIM, SSM_STATE)
    acum = jnp.cumsum(dt * a.reshape(SSM_GROUPS, SSM_HPG), axis=2)
    causal = jnp.tril(jnp.ones((lc, lc), dtype=bool))[None, None, :, :, None, None]
    seg = acum[:, :, :, None] - acum[:, :, None, :]
    decay = jnp.exp(jnp.where(causal, seg, -jnp.inf))
    cb = jnp.einsum('bclgn,bcsgn->bclsg', cm, bm)
    wgt = cb[..., None] * decay * dt[:, :, None]
    y_diag = jnp.einsum('bclsgr,bcsgrp->bclgrp', wgt, x)
    decay_end = jnp.exp(acum[:, :, -1:] - acum) * dt
    states = jnp.einsum('bclgn,bclgr,bclgrp->bcgrpn', bm, decay_end, x)
    chunk_decay = jnp.exp(acum[:, :, -1])

    def step(hc, inp):
        dec, st = inp
        return hc * dec[..., None, None] + st, hc

    h_last, h_prev = lax.scan(step, h0, (jnp.moveaxis(chunk_decay, 1, 0), jnp.moveaxis(states, 1, 0)))
    h_prev = jnp.moveaxis(h_prev, 0, 1)
    y_off = jnp.einsum('bclgn,bcgrpn->bclgrp', cm, h_prev) * jnp.exp(acum)[..., None]
    y = (y_diag + y_off).reshape(b, L, SSM_HEADS, SSM_HEADDIM)
    return y, h_last.reshape(b, SSM_HEADS, SSM_HEADDIM, SSM_STATE)


def mixer_ssd(h, conv_hist, ssm_state, w_in, conv_w, conv_b, dt_bias, a_log, d_skip, norm_g, w_out):
    b, L, _ = h.shape
    p = h @ w_in
    z = p[..., :SSM_INNER]
    xbc = p[..., SSM_INNER:SSM_INNER + SSM_CONV_CH]
    dt_raw = p[..., SSM_INNER + SSM_CONV_CH:]
    xbc, new_conv = causal_dwconv(xbc, conv_hist, conv_w)
    xbc = jax.nn.silu(xbc + conv_b)
    xs = xbc[..., :SSM_INNER].reshape(b, L, SSM_HEADS, SSM_HEADDIM).astype(F32)
    bm = xbc[..., SSM_INNER:SSM_INNER + SSM_GN].reshape(b, L, SSM_GROUPS, SSM_STATE).astype(F32)
    cm = xbc[..., SSM_INNER + SSM_GN:].reshape(b, L, SSM_GROUPS, SSM_STATE).astype(F32)
    dt = jax.nn.softplus(dt_raw.astype(F32) + dt_bias.astype(F32))
    a = -jnp.exp(a_log.astype(F32))
    y, new_state = ssd_scan(xs, dt, a, bm, cm, ssm_state.astype(F32))
    y = y + d_skip.astype(F32)[:, None] * xs
    g = y.reshape(b, L, SSM_INNER) * jax.nn.silu(z.astype(F32))
    g = g.reshape(b, L, SSM_GROUPS, SSM_INNER // SSM_GROUPS)
    g = g * lax.rsqrt(jnp.mean(g * g, axis=-1, keepdims=True) + EPS)
    g = g.reshape(b, L, SSM_INNER) * norm_g.astype(F32)
    out = g.astype(h.dtype) @ w_out
    return out, new_conv, new_state.astype(ssm_state.dtype)


def swiglu(h, w_gate, w_up, w_down):
    return (jax.nn.silu(h @ w_gate) * (h @ w_up)) @ w_down


def setup_inputs(seed: int = 0) -> dict:
    key = jax.random.key(seed)
    ks = jax.random.split(key, 26)
    NA, NS = N_ATTN_LAYERS, N_SSM_LAYERS

    def nrm(k, shape, scale):
        return jax.random.normal(k, shape, F32) * scale

    dt0 = jnp.exp(jax.random.uniform(ks[20], (NS, SSM_HEADS), F32, math.log(1e-3), math.log(1e-1)))
    return {
        'x_prompt': nrm(ks[0], (BATCH, SEQ, D_MODEL), 1.0),
        'x_sample': nrm(ks[1], (DEC_BATCH, DEC_SEQ, D_MODEL), 1.0),
        'cache_attn_k': nrm(ks[2], (NA, DEC_BATCH, WIN_CACHE, A_KV_HEADS, HEAD_DIM), 1.0),
        'cache_attn_v': nrm(ks[3], (NA, DEC_BATCH, WIN_CACHE, A_KV_HEADS, HEAD_DIM), 1.0),
        'state_sconv': nrm(ks[4], (NA, DEC_BATCH, B_CONV - 1, B_WIDTH), 1.0),
        'state_ssm_conv': nrm(ks[5], (NS, DEC_BATCH, SSM_CONV - 1, SSM_CONV_CH), 1.0),
        'state_ssm': nrm(ks[6], (NS, DEC_BATCH, SSM_HEADS, SSM_HEADDIM, SSM_STATE), 0.1),
        'norm_g': 1.0 + nrm(ks[7], (DEPTH, 4, D_MODEL), 0.05),
        'ab_w_in': nrm(ks[8], (NA, D_MODEL, AB_IN), D_MODEL ** -0.5),
        'ab_w_out': nrm(ks[9], (NA, AB_OUT_IN, D_MODEL), AB_OUT_IN ** -0.5),
        'attn_sinks': nrm(ks[10], (NA, A_HEADS), 0.5),
        'sconv_w': nrm(ks[11], (NA, B_CONV, B_WIDTH), B_CONV ** -0.5),
        'ssm_w_in': nrm(ks[12], (NS, D_MODEL, SSM_IN), D_MODEL ** -0.5),
        'ssm_conv_w': nrm(ks[13], (NS, SSM_CONV, SSM_CONV_CH), SSM_CONV ** -0.5),
        'ssm_conv_b': nrm(ks[14], (NS, SSM_CONV_CH), 0.02),
        'ssm_dt_bias': dt0 + jnp.log(-jnp.expm1(-dt0)),
        'ssm_a_log': jnp.log(jax.random.uniform(ks[15], (NS, SSM_HEADS), F32, 1.0, 16.0)),
        'ssm_d': 1.0 + nrm(ks[16], (NS, SSM_HEADS), 0.1),
        'ssm_norm_g': 1.0 + nrm(ks[17], (NS, SSM_INNER), 0.05),
        'ssm_w_out': nrm(ks[18], (NS, SSM_INNER, D_MODEL), SSM_INNER ** -0.5),
        'ffn_w_gate': nrm(ks[19], (DEPTH, D_MODEL, D_FF), D_MODEL ** -0.5),
        'ffn_w_up': nrm(ks[21], (DEPTH, D_MODEL, D_FF), D_MODEL ** -0.5),
        'ffn_w_down': nrm(ks[22], (DEPTH, D_FF, D_MODEL), D_FF ** -0.5),
    }


def reference(x_prompt, x_sample, cache_attn_k, cache_attn_v, state_sconv, state_ssm_conv, state_ssm,
              norm_g, ab_w_in, ab_w_out, attn_sinks, sconv_w, ssm_w_in, ssm_conv_w, ssm_conv_b,
              ssm_dt_bias, ssm_a_log, ssm_d, ssm_norm_g, ssm_w_out, ffn_w_gate, ffn_w_up, ffn_w_down):
    pos_p = jnp.arange(SEQ, dtype=F32)
    pos_s = PAST_LEN + jnp.arange(DEC_SEQ, dtype=F32)
    xp, xs = x_prompt, x_sample
    kp_l, vp_l, scp_l, ks_l, vs_l, scs_l = [], [], [], [], [], []
    ccp_l, ssp_l, ccs_l, sss_l = [], [], [], []
    for layer in range(DEPTH):
        g = norm_g[layer]
        i = layer // 2
        hp = rmsnorm(xp, g[0])
        hs = rmsnorm(xs, g[0])
        if layer % 2 == 0:
            wts = (ab_w_in[i], ab_w_out[i], attn_sinks[i], sconv_w[i])
            zero_hist = jnp.zeros((BATCH, B_CONV - 1, B_WIDTH), xp.dtype)
            mp, nkp, nvp, nscp = mixer_attn_conv(hp, pos_p, zero_hist, None, *wts)
            ms, nks, nvs, nscs = mixer_attn_conv(hs, pos_s, state_sconv[i],
                                                 (cache_attn_k[i], cache_attn_v[i]), *wts)
            kp_l.append(nkp)
            vp_l.append(nvp)
            scp_l.append(nscp)
            ks_l.append(nks)
            vs_l.append(nvs)
            scs_l.append(nscs)
        else:
            wts = (ssm_w_in[i], ssm_conv_w[i], ssm_conv_b[i], ssm_dt_bias[i], ssm_a_log[i],
                   ssm_d[i], ssm_norm_g[i], ssm_w_out[i])
            zero_conv = jnp.zeros((BATCH, SSM_CONV - 1, SSM_CONV_CH), xp.dtype)
            zero_state = jnp.zeros((BATCH, SSM_HEADS, SSM_HEADDIM, SSM_STATE), state_ssm.dtype)
            mp, nccp, nssp = mixer_ssd(hp, zero_conv, zero_state, *wts)
            ms, nccs, nsss = mixer_ssd(hs, state_ssm_conv[i], state_ssm[i], *wts)
            ccp_l.append(nccp)
            ssp_l.append(nssp)
            ccs_l.append(nccs)
            sss_l.append(nsss)
        xp = xp + rmsnorm(mp, g[1])
        xs = xs + rmsnorm(ms, g[1])
        fw = (ffn_w_gate[layer], ffn_w_up[layer], ffn_w_down[layer])
        xp = xp + rmsnorm(swiglu(rmsnorm(xp, g[2]), *fw), g[3])
        xs = xs + rmsnorm(swiglu(rmsnorm(xs, g[2]), *fw), g[3])
    return (xp, xs,
            jnp.stack(kp_l), jnp.stack(vp_l), jnp.stack(scp_l), jnp.stack(ccp_l), jnp.stack(ssp_l),
            jnp.stack(ks_l), jnp.stack(vs_l), jnp.stack(scs_l), jnp.stack(ccs_l), jnp.stack(sss_l))
```

```python
import functools
import math

import jax
import jax.numpy as jnp
from jax import lax
from jax.experimental import pallas as pl
from jax.experimental.pallas import tpu as pltpu

F32 = jnp.float32
BF16 = jnp.bfloat16

D_MODEL = 1024
CHUNK = 64
EPS = 1e-6
PAST_LEN = 4096

A_HEADS = 8
A_KV_HEADS = 2
A_GROUP = A_HEADS // A_KV_HEADS
HEAD_DIM = 64
A_WIDTH = A_HEADS * HEAD_DIM
KV_WIDTH = A_KV_HEADS * HEAD_DIM
WINDOW = 128
ROPE_DIM = HEAD_DIM // 4
ROPE_HALF = ROPE_DIM // 2
ROPE_THETA = 500000.0
ATTN_SCALE = HEAD_DIM ** -0.5

B_WIDTH = D_MODEL // 2
B_CONV = 3
AB_IN = A_WIDTH + 2 * KV_WIDTH + 3 * B_WIDTH

SSM_INNER = 2 * D_MODEL
SSM_HEADDIM = 64
SSM_HEADS = SSM_INNER // SSM_HEADDIM
SSM_GROUPS = 4
SSM_GROUP_W = SSM_INNER // SSM_GROUPS
SSM_STATE = 128
SSM_CONV = 4
SSM_GN = SSM_GROUPS * SSM_STATE
SSM_CONV_CH = SSM_INNER + 2 * SSM_GN

D_FF = -(-8 * D_MODEL // (3 * 256)) * 256

LANES = 128
DT_REP = LANES // SSM_HEADS
VMEM_LIMIT = 56 * 1024 * 1024


def _rms(x, g):
    return x * lax.rsqrt(jnp.mean(x * x, axis=-1, keepdims=True) + EPS) * g


def _dot(a, b):
    return jnp.dot(a, b, preferred_element_type=F32)


def _dot_nt(a, b):
    return lax.dot_general(a, b, (((1,), (1,)), ((), ())), preferred_element_type=F32)


def _const_spec(shape):
    return pl.BlockSpec(shape, lambda *_: (0,) * len(shape), pipeline_mode=pl.Buffered(1))


def _params():
    return pltpu.CompilerParams(dimension_semantics=("arbitrary", "arbitrary"),
                                vmem_limit_bytes=VMEM_LIMIT)


def _rope_freq(shape):
    lane = lax.broadcasted_iota(jnp.int32, shape, 1)
    d = lane % HEAD_DIM
    expo = -((d % ROPE_HALF).astype(F32)) / ROPE_HALF
    freq = jnp.power(jnp.full(shape, ROPE_THETA, F32), expo)
    return jnp.where(d < ROPE_DIM, freq, 0.0), d


def _attn_in_kernel(pos0, tm, x_ref, g_ref, w_ref, hist_ref, cw_ref,
                    q_ref, k_ref, v_ref, gated_ref, nh_ref,
                    cr_ref, sr_ref, ubuf_ref):
    b = pl.program_id(0)
    i = pl.program_id(1)

    @pl.when((b == 0) & (i == 0))
    def _():
        freq, _ = _rope_freq((tm, LANES))
        ang = lax.broadcasted_iota(jnp.int32, (tm, LANES), 0).astype(F32) * freq
        cr_ref[...] = jnp.cos(ang)
        sr_ref[...] = jnp.sin(ang)

    freq1, d1 = _rope_freq((1, LANES))
    base = (pos0 + i * tm).astype(F32) * freq1
    cb = jnp.cos(base)
    sb = jnp.sin(base)
    cr = cr_ref[...]
    sr = sr_ref[...]
    cos_t = cr * cb - sr * sb
    sin_t = sr * cb + cr * sb
    m_lo = jnp.where(d1 < ROPE_HALF, -1.0, 0.0)
    m_hi = jnp.where((d1 >= ROPE_HALF) & (d1 < ROPE_DIM), 1.0, 0.0)

    def rope(t):
        partner = pltpu.roll(t, LANES - ROPE_HALF, 1) * m_lo + pltpu.roll(t, ROPE_HALF, 1) * m_hi
        return t * cos_t + partner * sin_t

    h = _rms(x_ref[...], g_ref[...]).astype(BF16)
    p = _dot(h, w_ref[...])

    for s in range(A_WIDTH // LANES):
        q_ref[:, s * LANES:(s + 1) * LANES] = (rope(p[:, s * LANES:(s + 1) * LANES]) * ATTN_SCALE).astype(BF16)
    o1 = A_WIDTH
    o2 = o1 + KV_WIDTH
    o3 = o2 + KV_WIDTH
    o4 = o3 + B_WIDTH
    o5 = o4 + B_WIDTH
    k_ref[...] = rope(p[:, o1:o2])
    v_ref[...] = p[:, o2:o3]

    u = p[:, o4:o5] * p[:, o5:]

    @pl.when(i == 0)
    def _():
        ubuf_ref[6:8, :] = hist_ref[...]

    ubuf_ref[8:8 + tm, :] = u
    cw = cw_ref[...]
    y = ubuf_ref[6:6 + tm, :] * cw[0:1] + ubuf_ref[7:7 + tm, :] * cw[1:2] + u * cw[2:3]
    gated_ref[...] = (p[:, o3:o4] * y).astype(BF16)
    last = ubuf_ref[6 + tm:8 + tm, :]
    nh_ref[...] = last
    ubuf_ref[6:8, :] = last


def _attn_in(x, g, w, hist, cw, pos0, tm):
    bsz, seq, _ = x.shape
    grid = (bsz, seq // tm)
    tok = lambda width: pl.BlockSpec((None, tm, width), lambda b, i: (b, i, 0))
    return pl.pallas_call(
        functools.partial(_attn_in_kernel, pos0, tm),
        grid=grid,
        in_specs=[tok(D_MODEL), _const_spec((1, D_MODEL)), _const_spec((D_MODEL, AB_IN)),
                  pl.BlockSpec((None, B_CONV - 1, B_WIDTH), lambda b, i: (b, 0, 0)),
                  _const_spec((B_CONV, B_WIDTH))],
        out_specs=[tok(A_WIDTH), tok(KV_WIDTH), tok(KV_WIDTH), tok(B_WIDTH),
                   pl.BlockSpec((None, B_CONV - 1, B_WIDTH), lambda b, i: (b, 0, 0))],
        out_shape=[jax.ShapeDtypeStruct((bsz, seq, A_WIDTH), BF16),
                   jax.ShapeDtypeStruct((bsz, seq, KV_WIDTH), F32),
                   jax.ShapeDtypeStruct((bsz, seq, KV_WIDTH), F32),
                   jax.ShapeDtypeStruct((bsz, seq, B_WIDTH), BF16),
                   jax.ShapeDtypeStruct((bsz, B_CONV - 1, B_WIDTH), F32)],
        scratch_shapes=[pltpu.VMEM((tm, LANES), F32), pltpu.VMEM((tm, LANES), F32),
                        pltpu.VMEM((8 + tm, B_WIDTH), F32)],
        compiler_params=_params(),
        name="attn_in_proj",
    )(x, g, w, hist, cw)


def _attn_out_kernel(tq, cq, masked, sinks_ref, q_ref, kc_ref, vc_ref, kp_ref, vp_ref,
                     gated_ref, x_ref, wo_ref, g_ref, o_ref, kk_ref, vv_ref, att_ref):
    i = pl.program_id(1)
    kk_ref[0:WINDOW, :] = kp_ref[...].astype(BF16)
    kk_ref[WINDOW:WINDOW + tq, :] = kc_ref[...].astype(BF16)
    vv_ref[0:WINDOW, :] = vp_ref[...].astype(BF16)
    vv_ref[WINDOW:WINDOW + tq, :] = vc_ref[...].astype(BF16)
    nk = WINDOW + cq
    for c in range(tq // cq):
        r0 = c * cq
        for kh in range(A_KV_HEADS):
            hs = slice(kh * HEAD_DIM, (kh + 1) * HEAD_DIM)
            kt = kk_ref[r0:r0 + nk, hs]
            vt = vv_ref[r0:r0 + nk, hs]
            heads = [kh * A_GROUP + g for g in range(A_GROUP)]
            qs = jnp.concatenate(
                [q_ref[r0:r0 + cq, hh * HEAD_DIM:(hh + 1) * HEAD_DIM] for hh in heads], axis=0)
            s = _dot_nt(qs, kt)
            if masked and r0 < WINDOW:
                col = lax.broadcasted_iota(jnp.int32, s.shape, 1)
                s = jnp.where((col < WINDOW - r0) & (i == 0), -jnp.inf, s)
            sink = jnp.concatenate([jnp.full((cq, 1), sinks_ref[hh], F32) for hh in heads], axis=0)
            m = jnp.maximum(jnp.max(s, axis=-1, keepdims=True), sink)
            e = jnp.exp(s - m)
            den = jnp.sum(e, axis=-1, keepdims=True) + jnp.exp(sink - m)
            o = _dot(e.astype(BF16), vt) / den
            for g, hh in enumerate(heads):
                att_ref[r0:r0 + cq, hh * HEAD_DIM:(hh + 1) * HEAD_DIM] = o[g * cq:(g + 1) * cq].astype(BF16)
    acc = _dot(att_ref[...], wo_ref[0:A_WIDTH, :]) + _dot(gated_ref[...], wo_ref[A_WIDTH:, :])
    o_ref[...] = x_ref[...] + _rms(acc, g_ref[...])


def _attn_out(sinks, q, k, v, k_prev, v_prev, gated, x, wo, g, tq, cq, masked):
    bsz, seq, _ = q.shape
    grid = (bsz, seq // tq)
    tok = lambda width: pl.BlockSpec((None, tq, width), lambda b, i: (b, i, 0))
    blocks_per_tile = tq // WINDOW if tq >= WINDOW else 0
    if masked:
        prev = pl.BlockSpec((None, WINDOW, KV_WIDTH),
                            lambda b, i: (b, jnp.maximum(i * blocks_per_tile - 1, 0), 0))
    else:
        prev = pl.BlockSpec((None, WINDOW, KV_WIDTH), lambda b, i: (b, 0, 0))
    return pl.pallas_call(
        functools.partial(_attn_out_kernel, tq, cq, masked),
        grid=grid,
        in_specs=[pl.BlockSpec(memory_space=pltpu.SMEM),
                  tok(A_WIDTH), tok(KV_WIDTH), tok(KV_WIDTH), prev, prev, tok(B_WIDTH), tok(D_MODEL),
                  _const_spec((A_WIDTH + B_WIDTH, D_MODEL)), _const_spec((1, D_MODEL))],
        out_specs=tok(D_MODEL),
        out_shape=jax.ShapeDtypeStruct((bsz, seq, D_MODEL), F32),
        scratch_shapes=[pltpu.VMEM((WINDOW + tq, KV_WIDTH), BF16), pltpu.VMEM((WINDOW + tq, KV_WIDTH), BF16),
                        pltpu.VMEM((tq, A_WIDTH), BF16)],
        compiler_params=_params(),
        name="attn_out_proj",
    )(sinks, q, k, v, k_prev, v_prev, gated, x, wo, g)


def _ffn_kernel(x_ref, g2_ref, wg_ref, wu_ref, wd_ref, g3_ref, o_ref):
    x = x_ref[...]
    h = _rms(x, g2_ref[...]).astype(BF16)
    a = _dot(h, wg_ref[...])
    u = _dot(h, wu_ref[...])
    m = (a * jax.nn.sigmoid(a) * u).astype(BF16)
    f = _dot(m, wd_ref[...])
    o_ref[...] = x + _rms(f, g3_ref[...])


def _ffn(x, g2, wg, wu, wd, g3, tm):
    bsz, seq, _ = x.shape
    tok = pl.BlockSpec((None, tm, D_MODEL), lambda b, i: (b, i, 0))
    return pl.pallas_call(
        _ffn_kernel,
        grid=(bsz, seq // tm),
        in_specs=[tok, _const_spec((1, D_MODEL)), _const_spec((D_MODEL, D_FF)), _const_spec((D_MODEL, D_FF)),
                  _const_spec((D_FF, D_MODEL)), _const_spec((1, D_MODEL))],
        out_specs=tok,
        out_shape=jax.ShapeDtypeStruct(x.shape, F32),
        compiler_params=_params(),
        name="ffn",
    )(x, g2, wg, wu, wd, g3)


def _ssm_in_kernel(tm, x_ref, g_ref, wz_ref, wx_ref, wdt_ref, hist_ref, cw_ref, cb_ref, dtb_ref,
                   z_ref, xs_ref, bm_ref, cm_ref, dt_ref, nh_ref, buf_ref):
    i = pl.program_id(1)
    h = _rms(x_ref[...], g_ref[...]).astype(BF16)
    z_ref[...] = _dot(h, wz_ref[...])
    dt_ref[...] = jax.nn.softplus(_dot(h, wdt_ref[...]) + dtb_ref[...])
    raw = _dot(h, wx_ref[...])

    @pl.when(i == 0)
    def _():
        buf_ref[5:8, :] = hist_ref[...]

    buf_ref[8:8 + tm, :] = raw
    cw = cw_ref[...]
    y = buf_ref[5:5 + tm, :] * cw[0:1]
    y = y + buf_ref[6:6 + tm, :] * cw[1:2]
    y = y + buf_ref[7:7 + tm, :] * cw[2:3]
    y = y + raw * cw[3:4]
    y = y + cb_ref[...]
    y = y * jax.nn.sigmoid(y)
    xs_ref[...] = y[:, :SSM_INNER]
    bm_ref[...] = y[:, SSM_INNER:SSM_INNER + SSM_GN]
    cm_ref[...] = y[:, SSM_INNER + SSM_GN:]
    last = buf_ref[5 + tm:8 + tm, :]
    nh_ref[...] = last
    buf_ref[5:8, :] = last


def _ssm_in(x, g, wz, wx, wdt, hist, cw, cb, dtb, tm):
    bsz, seq, _ = x.shape
    tok = lambda width: pl.BlockSpec((None, tm, width), lambda b, i: (b, i, 0))
    hist_spec = pl.BlockSpec((None, SSM_CONV - 1, SSM_CONV_CH), lambda b, i: (b, 0, 0))
    return pl.pallas_call(
        functools.partial(_ssm_in_kernel, tm),
        grid=(bsz, seq // tm),
        in_specs=[tok(D_MODEL), _const_spec((1, D_MODEL)), _const_spec((D_MODEL, SSM_INNER)),
                  _const_spec((D_MODEL, SSM_CONV_CH)), _const_spec((D_MODEL, LANES)), hist_spec,
                  _const_spec((SSM_CONV, SSM_CONV_CH)), _const_spec((1, SSM_CONV_CH)), _const_spec((1, LANES))],
        out_specs=[tok(SSM_INNER), tok(SSM_INNER), tok(SSM_GN), tok(SSM_GN), tok(LANES), hist_spec],
        out_shape=[jax.ShapeDtypeStruct((bsz, seq, SSM_INNER), F32),
                   jax.ShapeDtypeStruct((bsz, seq, SSM_INNER), F32),
                   jax.ShapeDtypeStruct((bsz, seq, SSM_GN), F32),
                   jax.ShapeDtypeStruct((bsz, seq, SSM_GN), F32),
                   jax.ShapeDtypeStruct((bsz, seq, LANES), F32),
                   jax.ShapeDtypeStruct((bsz, SSM_CONV - 1, SSM_CONV_CH), F32)],
        scratch_shapes=[pltpu.VMEM((8 + tm, SSM_CONV_CH), F32)],
        compiler_params=_params(),
        name="ssm_in_proj",
    )(x, g, wz, wx, wdt, hist, cw, cb, dtb)


def _split3(v):
    t1 = v.astype(BF16)
    r = v - t1.astype(F32)
    t2 = r.astype(BF16)
    t3 = (r - t2.astype(F32)).astype(BF16)
    return t1, t2, t3


def _pad_rows(a, rows):
    if a.shape[0] == rows:
        return a
    return jnp.concatenate([a, jnp.zeros((rows - a.shape[0], a.shape[1]), a.dtype)], axis=0)


def _ssd_kernel(ts, cl, xs_ref, z_ref, bm_ref, cm_ref, dt_ref, x_ref, h0_ref, alog_ref, dskip_ref,
                ng_ref, wo_ref, g_ref, o_ref, hout_ref, ht_ref, y_ref, e3_ref):
    b = pl.program_id(0)
    i = pl.program_id(1)
    p = SSM_HEADDIM

    @pl.when((b == 0) & (i == 0))
    def _():
        kk = lax.broadcasted_iota(jnp.int32, (LANES, SSM_INNER), 0)
        cc = lax.broadcasted_iota(jnp.int32, (LANES, SSM_INNER), 1)
        hit = (kk % SSM_HEADS == cc // p) & (kk < 3 * SSM_HEADS)
        e3_ref[...] = jnp.where(hit, 1.0, 0.0).astype(BF16)

    @pl.when(i == 0)
    def _():
        ht_ref[...] = h0_ref[...].T

    lane = lax.broadcasted_iota(jnp.int32, (cl, LANES), 1)
    row = lax.broadcasted_iota(jnp.int32, (cl, LANES), 0)
    causal2 = (lane % p) <= row
    left = lane < p
    tcol = lax.broadcasted_iota(jnp.int32, (cl, 3 * cl), 1)
    trow = lax.broadcasted_iota(jnp.int32, (cl, 3 * cl), 0)
    tril3 = jnp.where((tcol % cl) <= trow, 1.0, 0.0).astype(BF16)
    icol = lax.broadcasted_iota(jnp.int32, (cl, SSM_INNER), 1)
    irow = lax.broadcasted_iota(jnp.int32, (cl, SSM_INNER), 0)
    diag = (icol % p) == irow
    a_rep = -jnp.exp(alog_ref[...])
    e3 = e3_ref[...]

    def expand(v):
        t1, t2, t3 = _split3(v)
        t = jnp.where(lane < SSM_HEADS, t1,
                      jnp.where(lane < 2 * SSM_HEADS, t2,
                                jnp.where(lane < 3 * SSM_HEADS, t3, jnp.zeros_like(t3))))
        return _dot(t, e3)

    for c in range(ts // cl):
        rows = slice(c * cl, (c + 1) * cl)
        dtc = dt_ref[rows, :]
        d1, d2, d3 = _split3(dtc * a_rep)
        acum = _dot(tril3, jnp.concatenate([d1, d2, d3], axis=0))
        a_exp = expand(acum)
        dt_exp = expand(dtc)
        a_src = jnp.sum(jnp.where(diag, a_exp, 0.0), axis=0, keepdims=True)
        a_last = a_exp[cl - 1:cl, :]
        xdt = xs_ref[rows, :] * dt_exp
        for gi in range(SSM_GROUPS):
            gcols = slice(gi * SSM_GROUP_W, (gi + 1) * SSM_GROUP_W)
            scol = slice(gi * SSM_STATE, (gi + 1) * SSM_STATE)
            bmg = bm_ref[rows, scol]
            cmg = cm_ref[rows, scol].astype(BF16)
            bmp = _pad_rows(bmg, p)
            cb = _dot_nt(cmg, bmp.astype(BF16))
            cb2 = jnp.concatenate([cb, cb], axis=1)
            parts = []
            for jj in range(SSM_GROUP_W // LANES):
                cols = slice(gi * SSM_GROUP_W + jj * LANES, gi * SSM_GROUP_W + (jj + 1) * LANES)
                seg = a_exp[:, cols] - a_src[:, cols]
                dec = jnp.exp(jnp.where(causal2, seg, -jnp.inf))
                w2 = (dec * cb2).astype(BF16)
                x2 = xdt[:, cols]
                top = _pad_rows(jnp.where(left, x2, 0.0), p).astype(BF16)
                bot = _pad_rows(jnp.where(left, 0.0, x2), p).astype(BF16)
                parts.append(_dot(w2, jnp.concatenate([top, bot], axis=0)))
            y_diag = jnp.concatenate(parts, axis=1)
            hprev = ht_ref[:, gcols]
            y_off = _dot(cmg, hprev.astype(BF16)) * jnp.exp(a_exp[:, gcols])
            y_ref[rows, gcols] = y_diag + y_off
            xw = xdt[:, gcols] * jnp.exp(a_last[:, gcols] - a_exp[:, gcols])
            st = _dot(bmp.T.astype(BF16), _pad_rows(xw, p).astype(BF16))
            ht_ref[:, gcols] = hprev * jnp.exp(a_last[:, gcols]) + st

    y = y_ref[...] + dskip_ref[...] * xs_ref[...]
    zz = z_ref[...]
    gt = y * (zz * jax.nn.sigmoid(zz))
    normed = []
    for gi in range(SSM_GROUPS):
        gg = gt[:, gi * SSM_GROUP_W:(gi + 1) * SSM_GROUP_W]
        normed.append(gg * lax.rsqrt(jnp.mean(gg * gg, axis=-1, keepdims=True) + EPS))
    gn = (jnp.concatenate(normed, axis=1) * ng_ref[...]).astype(BF16)
    out = _dot(gn, wo_ref[...])
    o_ref[...] = x_ref[...] + _rms(out, g_ref[...])

    @pl.when(i == pl.num_programs(1) - 1)
    def _():
        hout_ref[...] = ht_ref[...].T


def _ssd(xs, z, bm, cm, dt, x, h0, alog, dskip, ng, wo, g, ts, cl):
    bsz, seq, _ = x.shape
    tok = lambda width: pl.BlockSpec((None, ts, width), lambda b, i: (b, i, 0))
    st_spec = pl.BlockSpec((None, SSM_INNER, SSM_STATE), lambda b, i: (b, 0, 0))
    return pl.pallas_call(
        functools.partial(_ssd_kernel, ts, cl),
        grid=(bsz, seq // ts),
        in_specs=[tok(SSM_INNER), tok(SSM_INNER), tok(SSM_GN), tok(SSM_GN), tok(LANES), tok(D_MODEL), st_spec,
                  _const_spec((1, LANES)), _const_spec((1, SSM_INNER)), _const_spec((1, SSM_INNER)),
                  _const_spec((SSM_INNER, D_MODEL)), _const_spec((1, D_MODEL))],
        out_specs=[tok(D_MODEL), st_spec],
        out_shape=[jax.ShapeDtypeStruct((bsz, seq, D_MODEL), F32),
                   jax.ShapeDtypeStruct((bsz, SSM_INNER, SSM_STATE), F32)],
        scratch_shapes=[pltpu.VMEM((SSM_STATE, SSM_INNER), F32), pltpu.VMEM((ts, SSM_INNER), F32),
                        pltpu.VMEM((LANES, SSM_INNER), BF16)],
        compiler_params=_params(),
        name="ssd_scan_out_proj",
    )(xs, z, bm, cm, dt, x, h0, alog, dskip, ng, wo, g)


def _run(x, pos0, sconv_hist, k_cache, v_cache, conv_hist, ssm_state, wts, tiles):
    bsz, seq, _ = x.shape
    t_in, t_att, c_att, t_ffn, t_sin, t_ssd, c_ssd = tiles
    row = lambda v: v.reshape(1, -1)
    ng = wts["norm_g"]

    q, k, v, gated, new_sconv = _attn_in(x, row(ng[0, 0]), wts["ab_w_in"], sconv_hist, wts["sconv_w"], pos0, t_in)
    masked = k_cache is None
    k_prev, v_prev = (k, v) if masked else (k_cache, v_cache)
    x = _attn_out(wts["sinks"], q, k, v, k_prev, v_prev, gated, x, wts["ab_w_out"], row(ng[0, 1]),
                  t_att, c_att, masked)
    ffn_shape = (1, bsz * seq, D_MODEL) if seq < t_ffn else x.shape
    x = _ffn(x.reshape(ffn_shape), row(ng[0, 2]), wts["wg"][0], wts["wu"][0], wts["wd"][0], row(ng[0, 3]),
             t_ffn).reshape(x.shape)

    z, xs, bm, cm, dt, new_conv = _ssm_in(x, row(ng[1, 0]), wts["w_z"], wts["w_xbc"], wts["w_dt"], conv_hist,
                                          wts["ssm_conv_w"], wts["ssm_conv_b"], wts["dt_bias"], t_sin)
    x, new_state = _ssd(xs, z, bm, cm, dt, x, ssm_state.reshape(bsz, SSM_INNER, SSM_STATE), wts["a_log"],
                        wts["d_skip"], wts["ssm_norm_g"], wts["ssm_w_out"], row(ng[1, 1]), t_ssd, c_ssd)
    x = _ffn(x.reshape(ffn_shape), row(ng[1, 2]), wts["wg"][1], wts["wu"][1], wts["wd"][1], row(ng[1, 3]),
             t_ffn).reshape(x.shape)

    new_k = k[:, seq - min(WINDOW, seq):].reshape(bsz, -1, A_KV_HEADS, HEAD_DIM)
    new_v = v[:, seq - min(WINDOW, seq):].reshape(bsz, -1, A_KV_HEADS, HEAD_DIM)
    new_state = new_state.reshape(bsz, SSM_HEADS, SSM_HEADDIM, SSM_STATE)
    return x, new_k, new_v, new_sconv, new_conv, new_state


def kernel(x_prompt, x_sample, cache_attn_k, cache_attn_v, state_sconv, state_ssm_conv, state_ssm, norm_g,
           ab_w_in, ab_w_out, attn_sinks, sconv_w, ssm_w_in, ssm_conv_w, ssm_conv_b, ssm_dt_bias, ssm_a_log,
           ssm_d, ssm_norm_g, ssm_w_out, ffn_w_gate, ffn_w_up, ffn_w_down):
    w_in1 = ssm_w_in[0]
    wts = {
        "norm_g": norm_g,
        "ab_w_in": ab_w_in[0].astype(BF16),
        "ab_w_out": ab_w_out[0].astype(BF16),
        "sinks": attn_sinks[0],
        "sconv_w": sconv_w[0],
        "w_z": w_in1[:, :SSM_INNER].astype(BF16),
        "w_xbc": w_in1[:, SSM_INNER:SSM_INNER + SSM_CONV_CH].astype(BF16),
        "w_dt": jnp.tile(w_in1[:, SSM_INNER + SSM_CONV_CH:], (1, DT_REP)).astype(BF16),
        "ssm_conv_w": ssm_conv_w[0],
        "ssm_conv_b": ssm_conv_b[0].reshape(1, -1),
        "dt_bias": jnp.tile(ssm_dt_bias[0], DT_REP).reshape(1, -1),
        "a_log": jnp.tile(ssm_a_log[0], DT_REP).reshape(1, -1),
        "d_skip": jnp.repeat(ssm_d[0], SSM_HEADDIM).reshape(1, -1),
        "ssm_norm_g": ssm_norm_g[0].reshape(1, -1),
        "ssm_w_out": ssm_w_out[0].astype(BF16),
        "wg": ffn_w_gate.astype(BF16),
        "wu": ffn_w_up.astype(BF16),
        "wd": ffn_w_down.astype(BF16),
    }
    bp = x_prompt.shape[0]
    bs, ls = x_sample.shape[0], x_sample.shape[1]

    yp, kp, vp, scp, ccp, ssp = _run(
        x_prompt, 0,
        jnp.zeros((bp, B_CONV - 1, B_WIDTH), F32), None, None,
        jnp.zeros((bp, SSM_CONV - 1, SSM_CONV_CH), F32),
        jnp.zeros((bp, SSM_HEADS, SSM_HEADDIM, SSM_STATE), F32),
        wts, (512, 256, CHUNK, 512, 256, 128, CHUNK))
    ys, ks, vs, scs, ccs, sss = _run(
        x_sample, PAST_LEN,
        state_sconv[0], cache_attn_k[0].reshape(bs, -1, KV_WIDTH), cache_attn_v[0].reshape(bs, -1, KV_WIDTH),
        state_ssm_conv[0], state_ssm[0],
        wts, (ls, ls, ls, bs * ls, ls, ls, ls))
    lead = lambda a: a[None]
    return (yp, ys, lead(kp), lead(vp), lead(scp), lead(ccp), lead(ssp),
            lead(ks), lead(vs), lead(scs), lead(ccs), lead(sss))
```

```python
import functools

import jax
import jax.numpy as jnp
from jax import lax
from jax.experimental import pallas as pl
from jax.experimental.pallas import tpu as pltpu

F32 = jnp.float32
BF16 = jnp.bfloat16

D_MODEL = 1024
CHUNK = 64
EPS = 1e-6
PAST_LEN = 4096

A_HEADS = 8
A_KV_HEADS = 2
A_GROUP = A_HEADS // A_KV_HEADS
HEAD_DIM = 64
A_WIDTH = A_HEADS * HEAD_DIM
KV_WIDTH = A_KV_HEADS * HEAD_DIM
WINDOW = 128
ROPE_DIM = HEAD_DIM // 4
ROPE_HALF = ROPE_DIM // 2
ROPE_THETA = 500000.0
ATTN_SCALE = HEAD_DIM ** -0.5

B_WIDTH = D_MODEL // 2
B_CONV = 3
AB_IN = A_WIDTH + 2 * KV_WIDTH + 3 * B_WIDTH

SSM_INNER = 2 * D_MODEL
SSM_HEADDIM = 64
SSM_HEADS = SSM_INNER // SSM_HEADDIM
SSM_GROUPS = 4
SSM_GROUP_W = SSM_INNER // SSM_GROUPS
SSM_STATE = 128
SSM_CONV = 4
SSM_GN = SSM_GROUPS * SSM_STATE
SSM_CONV_CH = SSM_INNER + 2 * SSM_GN

D_FF = -(-8 * D_MODEL // (3 * 256)) * 256

LANES = 128
SUBLANES = 8
MXU_DIM = 256
DT_REP = LANES // SSM_HEADS
VMEM_LIMIT = 56 * 1024 * 1024


def _rms(x, g):
    return x * lax.rsqrt(jnp.mean(x * x, axis=-1, keepdims=True) + EPS) * g


def _dot(a, b):
    return jnp.dot(a, b, preferred_element_type=F32)


def _dot_nt(a, b):
    return lax.dot_general(a, b, (((1,), (1,)), ((), ())), preferred_element_type=F32)


def _const_spec(shape):
    return pl.BlockSpec(shape, lambda *_: (0,) * len(shape), pipeline_mode=pl.Buffered(1))


def _params():
    return pltpu.CompilerParams(dimension_semantics=("arbitrary", "arbitrary"),
                                vmem_limit_bytes=VMEM_LIMIT)


def _causal_conv(carry_ref, hist_ref, nh_ref, u, cw, first):
    width = cw.shape[0]
    rows = u.shape[0]

    @pl.when(first)
    def _():
        carry_ref[SUBLANES - (width - 1):SUBLANES, :] = hist_ref[...]

    ext = jnp.concatenate([carry_ref[...], u], axis=0)
    y = pltpu.roll(ext, width - 1, 0)[SUBLANES:] * cw[0:1]
    for j in range(1, width - 1):
        y = y + pltpu.roll(ext, width - 1 - j, 0)[SUBLANES:] * cw[j:j + 1]
    y = y + u * cw[width - 1:width]
    carry_ref[...] = u[rows - SUBLANES:]
    nh_ref[...] = ext[SUBLANES + rows - (width - 1):]
    return y


def _rope_freq(shape):
    lane = lax.broadcasted_iota(jnp.int32, shape, 1)
    d = lane % HEAD_DIM
    expo = -((d % ROPE_HALF).astype(F32)) / ROPE_HALF
    freq = jnp.power(jnp.full(shape, ROPE_THETA, F32), expo)
    return jnp.where(d < ROPE_DIM, freq, 0.0), d


def _attn_in_kernel(pos0, tm, x_ref, g_ref, w_ref, hist_ref, cw_ref,
                    q_ref, k_ref, v_ref, gated_ref, nh_ref,
                    cr_ref, sr_ref, carry_ref):
    b = pl.program_id(0)
    i = pl.program_id(1)

    @pl.when((b == 0) & (i == 0))
    def _():
        freq, _ = _rope_freq((tm, LANES))
        ang = lax.broadcasted_iota(jnp.int32, (tm, LANES), 0).astype(F32) * freq
        cr_ref[...] = jnp.cos(ang)
        sr_ref[...] = jnp.sin(ang)

    freq1, d1 = _rope_freq((1, LANES))
    base = (pos0 + i * tm).astype(F32) * freq1
    cb = jnp.cos(base)
    sb = jnp.sin(base)
    cr = cr_ref[...]
    sr = sr_ref[...]
    cos_t = cr * cb - sr * sb
    sin_t = sr * cb + cr * sb
    m_lo = jnp.where(d1 < ROPE_HALF, -1.0, 0.0)
    m_hi = jnp.where((d1 >= ROPE_HALF) & (d1 < ROPE_DIM), 1.0, 0.0)

    def rope(t):
        partner = pltpu.roll(t, LANES - ROPE_HALF, 1) * m_lo + pltpu.roll(t, ROPE_HALF, 1) * m_hi
        return t * cos_t + partner * sin_t

    h = _rms(x_ref[...], g_ref[...]).astype(BF16)
    p = _dot(h, w_ref[...])

    for s in range(A_WIDTH // LANES):
        q_ref[:, s * LANES:(s + 1) * LANES] = (rope(p[:, s * LANES:(s + 1) * LANES]) * ATTN_SCALE).astype(BF16)
    o1 = A_WIDTH
    o2 = o1 + KV_WIDTH
    o3 = o2 + KV_WIDTH
    o4 = o3 + B_WIDTH
    o5 = o4 + B_WIDTH
    k_ref[...] = rope(p[:, o1:o2])
    v_ref[...] = p[:, o2:o3]

    u = p[:, o4:o5] * p[:, o5:]
    y = _causal_conv(carry_ref, hist_ref, nh_ref, u, cw_ref[...], i == 0)
    gated_ref[...] = (p[:, o3:o4] * y).astype(BF16)


def _attn_in(x, g, w, hist, cw, pos0, tm):
    bsz, seq, _ = x.shape
    grid = (bsz, seq // tm)
    tok = lambda width: pl.BlockSpec((None, tm, width), lambda b, i: (b, i, 0))
    return pl.pallas_call(
        functools.partial(_attn_in_kernel, pos0, tm),
        grid=grid,
        in_specs=[tok(D_MODEL), _const_spec((1, D_MODEL)), _const_spec((D_MODEL, AB_IN)),
                  pl.BlockSpec((None, B_CONV - 1, B_WIDTH), lambda b, i: (b, 0, 0)),
                  _const_spec((B_CONV, B_WIDTH))],
        out_specs=[tok(A_WIDTH), tok(KV_WIDTH), tok(KV_WIDTH), tok(B_WIDTH),
                   pl.BlockSpec((None, B_CONV - 1, B_WIDTH), lambda b, i: (b, 0, 0))],
        out_shape=[jax.ShapeDtypeStruct((bsz, seq, A_WIDTH), BF16),
                   jax.ShapeDtypeStruct((bsz, seq, KV_WIDTH), F32),
                   jax.ShapeDtypeStruct((bsz, seq, KV_WIDTH), F32),
                   jax.ShapeDtypeStruct((bsz, seq, B_WIDTH), BF16),
                   jax.ShapeDtypeStruct((bsz, B_CONV - 1, B_WIDTH), F32)],
        scratch_shapes=[pltpu.VMEM((tm, LANES), F32), pltpu.VMEM((tm, LANES), F32),
                        pltpu.VMEM((SUBLANES, B_WIDTH), F32)],
        compiler_params=_params(),
        name="attn_in_proj",
    )(x, g, w, hist, cw)


def _attn_kernel(tq, cq, masked, sinks_ref, q_ref, kc_ref, vc_ref, kp_ref, vp_ref, att_ref,
                 kk_ref, vx_ref, s_ref, e_ref):
    i = pl.program_id(1)
    nk = WINDOW + cq
    nkp = MXU_DIM
    rows_kv = WINDOW + tq
    hd = HEAD_DIM

    kk_ref[0:WINDOW, :] = kp_ref[...].astype(BF16)
    kk_ref[WINDOW:rows_kv, :] = kc_ref[...].astype(BF16)
    kk_ref[rows_kv:, :] = jnp.zeros((kk_ref.shape[0] - rows_kv, KV_WIDTH), BF16)
    ones = jnp.ones((rows_kv, hd), F32)
    vall = jnp.concatenate([vp_ref[...], vc_ref[...]], axis=0)
    vx_ref[0:rows_kv, :] = jnp.concatenate([vall[:, :hd], ones, vall[:, hd:], ones], axis=1)
    vx_ref[rows_kv:, :] = jnp.zeros((vx_ref.shape[0] - rows_kv, 2 * LANES), F32)

    col = lax.broadcasted_iota(jnp.int32, (A_GROUP * cq, nkp), 1)
    krow = lax.broadcasted_iota(jnp.int32, (nkp, LANES), 0)
    klane = lax.broadcasted_iota(jnp.int32, (nkp, LANES), 1)
    sink_row = jnp.where(klane < hd, 0.0, 1.0)
    bodies = [(c, kh) for c in range(tq // cq) for kh in range(A_KV_HEADS)]
    fills = []
    for kh in range(A_KV_HEADS):
        sink = jnp.concatenate(
            [jnp.full((cq, nkp), sinks_ref[kh * A_GROUP + g], F32) for g in range(A_GROUP)], axis=0)
        fills.append(jnp.where(col == nk, sink, -jnp.inf))

    for n, (c, kh) in enumerate(bodies):
        r0 = c * cq
        heads = [kh * A_GROUP + g for g in range(A_GROUP)]
        kt = kk_ref[r0:r0 + nkp, kh * hd:(kh + 1) * hd]
        qs = jnp.concatenate([q_ref[r0:r0 + cq, hh * hd:(hh + 1) * hd] for hh in heads], axis=0)
        s = _dot_nt(qs, kt)
        valid = col < nk
        if masked and r0 < WINDOW:
            valid = valid & ((col >= WINDOW - r0) | (i > 0))
        s_ref[n] = jnp.where(valid, s, fills[kh])

    for n in range(len(bodies)):
        s = s_ref[n]
        m = jnp.max(s, axis=-1, keepdims=True)
        e_ref[n] = jnp.exp(s - m).astype(BF16)

    for n, (c, kh) in enumerate(bodies):
        r0 = c * cq
        vt = jnp.where(krow == nk, sink_row, vx_ref[r0:r0 + nkp, kh * LANES:(kh + 1) * LANES]).astype(BF16)
        ox = _dot(e_ref[n], vt)
        o = (ox / pltpu.roll(ox, hd, 1))[:, :hd]
        for g in range(A_GROUP):
            hh = kh * A_GROUP + g
            att_ref[r0:r0 + cq, hh * hd:(hh + 1) * hd] = o[g * cq:(g + 1) * cq].astype(BF16)


def _attn(sinks, q, k, v, k_prev, v_prev, tq, cq, masked):
    bsz, seq, _ = q.shape
    grid = (bsz, seq // tq)
    tok = lambda width: pl.BlockSpec((None, tq, width), lambda b, i: (b, i, 0))
    if masked:
        blocks_per_tile = tq // WINDOW
        prev = pl.BlockSpec((None, WINDOW, KV_WIDTH),
                            lambda b, i: (b, jnp.maximum(i * blocks_per_tile - 1, 0), 0))
    else:
        prev = pl.BlockSpec((None, WINDOW, KV_WIDTH), lambda b, i: (b, 0, 0))
    n_bodies = (tq // cq) * A_KV_HEADS
    kv_rows = (tq // cq - 1) * cq + MXU_DIM
    return pl.pallas_call(
        functools.partial(_attn_kernel, tq, cq, masked),
        grid=grid,
        in_specs=[pl.BlockSpec(memory_space=pltpu.SMEM),
                  tok(A_WIDTH), tok(KV_WIDTH), tok(KV_WIDTH), prev, prev],
        out_specs=tok(A_WIDTH),
        out_shape=jax.ShapeDtypeStruct((bsz, seq, A_WIDTH), BF16),
        scratch_shapes=[pltpu.VMEM((kv_rows, KV_WIDTH), BF16), pltpu.VMEM((kv_rows, 2 * LANES), F32),
                        pltpu.VMEM((n_bodies, A_GROUP * cq, MXU_DIM), F32),
                        pltpu.VMEM((n_bodies, A_GROUP * cq, MXU_DIM), BF16)],
        compiler_params=_params(),
        name="band_attention",
    )(sinks, q, k, v, k_prev, v_prev)


def _proj_ffn_kernel(n_in, *refs):
    a_refs = refs[:n_in]
    wo_ref, x_ref, g1_ref, g2_ref, wg_ref, wu_ref, wd_ref, g3_ref, o_ref = refs[n_in:]
    acc = None
    off = 0
    for a_ref in a_refs:
        kdim = a_ref.shape[-1]
        part = _dot(a_ref[...], wo_ref[off:off + kdim, :])
        acc = part if acc is None else acc + part
        off += kdim
    x = x_ref[...] + _rms(acc, g1_ref[...])
    h = _rms(x, g2_ref[...]).astype(BF16)
    a = _dot(h, wg_ref[...])
    u = _dot(h, wu_ref[...])
    m = (a * jax.nn.sigmoid(a) * u).astype(BF16)
    f = _dot(m, wd_ref[...])
    o_ref[...] = x + _rms(f, g3_ref[...])


def _proj_ffn(mix, wo, x, g1, g2, wg, wu, wd, g3, tm):
    bsz, seq, _ = x.shape
    tok = lambda width: pl.BlockSpec((None, tm, width), lambda b, i: (b, i, 0))
    return pl.pallas_call(
        functools.partial(_proj_ffn_kernel, len(mix)),
        grid=(bsz, seq // tm),
        in_specs=[tok(a.shape[-1]) for a in mix] + [
            _const_spec(wo.shape), tok(D_MODEL), _const_spec((1, D_MODEL)), _const_spec((1, D_MODEL)),
            _const_spec((D_MODEL, D_FF)), _const_spec((D_MODEL, D_FF)), _const_spec((D_FF, D_MODEL)),
            _const_spec((1, D_MODEL))],
        out_specs=tok(D_MODEL),
        out_shape=jax.ShapeDtypeStruct(x.shape, F32),
        compiler_params=_params(),
        name="out_proj_ffn",
    )(*mix, wo, x, g1, g2, wg, wu, wd, g3)


def _split3(v):
    t1 = v.astype(BF16)
    r = v - t1.astype(F32)
    t2 = r.astype(BF16)
    t3 = (r - t2.astype(F32)).astype(BF16)
    return t1, t2, t3


def _pack3(v):
    t1, t2, t3 = _split3(v)
    lane = lax.broadcasted_iota(jnp.int32, v.shape, 1)
    return jnp.where(lane < SSM_HEADS, t1,
                     jnp.where(lane < 2 * SSM_HEADS, t2,
                               jnp.where(lane < 3 * SSM_HEADS, t3, jnp.zeros_like(t3))))


def _ssm_in_kernel(tm, cl, x_ref, g_ref, wz_ref, wx_ref, wdt_ref, hist_ref, cw_ref, cb_ref, dtb_ref, alog_ref,
                   z_ref, xs_ref, bm_ref, cm_ref, ta_ref, tdt_ref, nh_ref, carry_ref):
    i = pl.program_id(1)
    h = _rms(x_ref[...], g_ref[...]).astype(BF16)
    z_ref[...] = _dot(h, wz_ref[...])

    dt = jax.nn.softplus(_dot(h, wdt_ref[...]) + dtb_ref[...])
    tdt_ref[...] = _pack3(dt)
    d1, d2, d3 = _split3(dt * (-jnp.exp(alog_ref[...])))
    tcol = lax.broadcasted_iota(jnp.int32, (cl, 3 * cl), 1) % cl
    trow = lax.broadcasted_iota(jnp.int32, (cl, 3 * cl), 0)
    tril = jnp.where(tcol <= trow, 1.0, 0.0).astype(BF16)
    acum = []
    for c in range(tm // cl):
        rows = slice(c * cl, (c + 1) * cl)
        acum.append(_dot(tril, jnp.concatenate([d1[rows], d2[rows], d3[rows]], axis=0)))
    ta_ref[...] = _pack3(jnp.concatenate(acum, axis=0))

    raw = _dot(h, wx_ref[...])
    y = _causal_conv(carry_ref, hist_ref, nh_ref, raw, cw_ref[...], i == 0)
    y = y + cb_ref[...]
    y = y * jax.nn.sigmoid(y)
    xs_ref[...] = y[:, :SSM_INNER]
    bm_ref[...] = y[:, SSM_INNER:SSM_INNER + SSM_GN].astype(BF16)
    cm_ref[...] = y[:, SSM_INNER + SSM_GN:].astype(BF16)


def _ssm_in(x, g, wz, wx, wdt, hist, cw, cb, dtb, alog, tm, cl):
    bsz, seq, _ = x.shape
    tok = lambda width: pl.BlockSpec((None, tm, width), lambda b, i: (b, i, 0))
    hist_spec = pl.BlockSpec((None, SSM_CONV - 1, SSM_CONV_CH), lambda b, i: (b, 0, 0))
    return pl.pallas_call(
        functools.partial(_ssm_in_kernel, tm, cl),
        grid=(bsz, seq // tm),
        in_specs=[tok(D_MODEL), _const_spec((1, D_MODEL)), _const_spec((D_MODEL, SSM_INNER)),
                  _const_spec((D_MODEL, SSM_CONV_CH)), _const_spec((D_MODEL, LANES)), hist_spec,
                  _const_spec((SSM_CONV, SSM_CONV_CH)), _const_spec((1, SSM_CONV_CH)), _const_spec((1, LANES)),
                  _const_spec((1, LANES))],
        out_specs=[tok(SSM_INNER), tok(SSM_INNER), tok(SSM_GN), tok(SSM_GN), tok(LANES), tok(LANES), hist_spec],
        out_shape=[jax.ShapeDtypeStruct((bsz, seq, SSM_INNER), F32),
                   jax.ShapeDtypeStruct((bsz, seq, SSM_INNER), F32),
                   jax.ShapeDtypeStruct((bsz, seq, SSM_GN), BF16),
                   jax.ShapeDtypeStruct((bsz, seq, SSM_GN), BF16),
                   jax.ShapeDtypeStruct((bsz, seq, LANES), BF16),
                   jax.ShapeDtypeStruct((bsz, seq, LANES), BF16),
                   jax.ShapeDtypeStruct((bsz, SSM_CONV - 1, SSM_CONV_CH), F32)],
        scratch_shapes=[pltpu.VMEM((SUBLANES, SSM_CONV_CH), F32)],
        compiler_params=_params(),
        name="ssm_in_proj",
    )(x, g, wz, wx, wdt, hist, cw, cb, dtb, alog)


def _pad_rows(a, rows):
    if a.shape[0] == rows:
        return a
    return jnp.concatenate([a, jnp.zeros((rows - a.shape[0], a.shape[1]), a.dtype)], axis=0)


def _ssd_kernel(ts, cl, xs_ref, z_ref, bm_ref, cm_ref, ta_ref, tdt_ref, h0_ref, dskip_ref, ng_ref,
                gn_ref, hout_ref, ht_ref, y_ref, e3_ref, aexp_ref, xdt_ref, st_ref):
    b = pl.program_id(0)
    i = pl.program_id(1)
    p = SSM_HEADDIM
    nc = ts // cl

    @pl.when((b == 0) & (i == 0))
    def _():
        kk = lax.broadcasted_iota(jnp.int32, (LANES, SSM_INNER), 0)
        cc = lax.broadcasted_iota(jnp.int32, (LANES, SSM_INNER), 1)
        hit = (kk % SSM_HEADS == cc // p) & (kk < 3 * SSM_HEADS)
        e3_ref[...] = jnp.where(hit, 1.0, 0.0).astype(BF16)

    @pl.when(i == 0)
    def _():
        ht_ref[...] = h0_ref[...].T

    lane = lax.broadcasted_iota(jnp.int32, (cl, LANES), 1)
    row = lax.broadcasted_iota(jnp.int32, (cl, LANES), 0)
    causal2 = (lane % p) <= row
    left = lane < p
    icol = lax.broadcasted_iota(jnp.int32, (cl, SSM_INNER), 1)
    irow = lax.broadcasted_iota(jnp.int32, (cl, SSM_INNER), 0)
    diag = (icol % p) == irow
    e3 = e3_ref[...]

    aexp_ref[...] = _dot(ta_ref[...], e3)
    xdt_ref[...] = xs_ref[...] * _dot(tdt_ref[...], e3)

    for c in range(nc):
        rows = slice(c * cl, (c + 1) * cl)
        a_exp = aexp_ref[rows, :]
        a_src = jnp.sum(jnp.where(diag, a_exp, 0.0), axis=0, keepdims=True)
        a_last = a_exp[cl - 1:cl, :]
        xw = (xdt_ref[rows, :] * jnp.exp(a_last - a_exp)).astype(BF16)
        for gi in range(SSM_GROUPS):
            gcols = slice(gi * SSM_GROUP_W, (gi + 1) * SSM_GROUP_W)
            scol = slice(gi * SSM_STATE, (gi + 1) * SSM_STATE)
            bmp = _pad_rows(bm_ref[rows, scol], p)
            cb = _dot_nt(cm_ref[rows, scol], bmp)
            cb2 = jnp.concatenate([cb, cb], axis=1)
            for jj in range(SSM_GROUP_W // LANES):
                cols = slice(gi * SSM_GROUP_W + jj * LANES, gi * SSM_GROUP_W + (jj + 1) * LANES)
                seg = a_exp[:, cols] - a_src[:, cols]
                dec = jnp.exp(jnp.where(causal2, seg, -jnp.inf))
                w2 = (dec * cb2).astype(BF16)
                x2 = xdt_ref[rows, cols]
                top = _pad_rows(jnp.where(left, x2, 0.0), p).astype(BF16)
                bot = _pad_rows(jnp.where(left, 0.0, x2), p).astype(BF16)
                y_ref[rows, cols] = _dot(w2, jnp.concatenate([top, bot], axis=0))
            st_ref[c, :, gcols] = _dot(bmp.astype(F32).T.astype(BF16), _pad_rows(xw[:, gcols], p))

    for c in range(nc):
        rows = slice(c * cl, (c + 1) * cl)
        a_exp = aexp_ref[rows, :]
        hprev = ht_ref[...]
        hb = hprev.astype(BF16)
        for gi in range(SSM_GROUPS):
            gcols = slice(gi * SSM_GROUP_W, (gi + 1) * SSM_GROUP_W)
            scol = slice(gi * SSM_STATE, (gi + 1) * SSM_STATE)
            y_off = _dot(cm_ref[rows, scol], hb[:, gcols]) * jnp.exp(a_exp[:, gcols])
            y_ref[rows, gcols] = y_ref[rows, gcols] + y_off
        ht_ref[...] = hprev * jnp.exp(a_exp[cl - 1:cl, :]) + st_ref[c]

    y = y_ref[...] + dskip_ref[...] * xs_ref[...]
    zz = z_ref[...]
    gt = y * (zz * jax.nn.sigmoid(zz))
    normed = []
    for gi in range(SSM_GROUPS):
        gg = gt[:, gi * SSM_GROUP_W:(gi + 1) * SSM_GROUP_W]
        normed.append(gg * lax.rsqrt(jnp.mean(gg * gg, axis=-1, keepdims=True) + EPS))
    gn_ref[...] = (jnp.concatenate(normed, axis=1) * ng_ref[...]).astype(BF16)

    @pl.when(i == pl.num_programs(1) - 1)
    def _():
        hout_ref[...] = ht_ref[...].T


def _ssd(xs, z, bm, cm, ta, tdt, h0, dskip, ng, ts, cl):
    bsz, seq, _ = xs.shape
    tok = lambda width: pl.BlockSpec((None, ts, width), lambda b, i: (b, i, 0))
    st_spec = pl.BlockSpec((None, SSM_INNER, SSM_STATE), lambda b, i: (b, 0, 0))
    return pl.pallas_call(
        functools.partial(_ssd_kernel, ts, cl),
        grid=(bsz, seq // ts),
        in_specs=[tok(SSM_INNER), tok(SSM_INNER), tok(SSM_GN), tok(SSM_GN), tok(LANES), tok(LANES), st_spec,
                  _const_spec((1, SSM_INNER)), _const_spec((1, SSM_INNER))],
        out_specs=[tok(SSM_INNER), st_spec],
        out_shape=[jax.ShapeDtypeStruct((bsz, seq, SSM_INNER), BF16),
                   jax.ShapeDtypeStruct((bsz, SSM_INNER, SSM_STATE), F32)],
        scratch_shapes=[pltpu.VMEM((SSM_STATE, SSM_INNER), F32), pltpu.VMEM((ts, SSM_INNER), F32),
                        pltpu.VMEM((LANES, SSM_INNER), BF16), pltpu.VMEM((ts, SSM_INNER), F32),
                        pltpu.VMEM((ts, SSM_INNER), F32), pltpu.VMEM((ts // cl, SSM_STATE, SSM_INNER), F32)],
        compiler_params=_params(),
        name="ssd_scan",
    )(xs, z, bm, cm, ta, tdt, h0, dskip, ng)


def _run(x, pos0, sconv_hist, k_cache, v_cache, conv_hist, ssm_state, wts, tiles):
    bsz, seq, _ = x.shape
    t_in, t_att, c_att, t_ffn, t_sin, t_ssd, c_ssd = tiles
    row = lambda v: v.reshape(1, -1)
    ng = wts["norm_g"]
    flat = (lambda a: a.reshape(1, bsz * seq, a.shape[-1])) if seq < t_ffn else (lambda a: a)

    q, k, v, gated, new_sconv = _attn_in(x, row(ng[0, 0]), wts["ab_w_in"], sconv_hist, wts["sconv_w"], pos0, t_in)
    masked = k_cache is None
    k_prev, v_prev = (k, v) if masked else (k_cache, v_cache)
    att = _attn(wts["sinks"], q, k, v, k_prev, v_prev, t_att, c_att, masked)
    x = _proj_ffn([flat(att), flat(gated)], wts["ab_w_out"], flat(x), row(ng[0, 1]), row(ng[0, 2]),
                  wts["wg"][0], wts["wu"][0], wts["wd"][0], row(ng[0, 3]), t_ffn).reshape(x.shape)

    z, xs, bm, cm, ta, tdt, new_conv = _ssm_in(
        x, row(ng[1, 0]), wts["w_z"], wts["w_xbc"], wts["w_dt"], conv_hist, wts["ssm_conv_w"],
        wts["ssm_conv_b"], wts["dt_bias"], wts["a_log"], t_sin, c_ssd)
    gn, new_state = _ssd(xs, z, bm, cm, ta, tdt, ssm_state.reshape(bsz, SSM_INNER, SSM_STATE),
                         wts["d_skip"], wts["ssm_norm_g"], t_ssd, c_ssd)
    x = _proj_ffn([flat(gn)], wts["ssm_w_out"], flat(x), row(ng[1, 1]), row(ng[1, 2]),
                  wts["wg"][1], wts["wu"][1], wts["wd"][1], row(ng[1, 3]), t_ffn).reshape(x.shape)

    new_k = k[:, seq - min(WINDOW, seq):].reshape(bsz, -1, A_KV_HEADS, HEAD_DIM)
    new_v = v[:, seq - min(WINDOW, seq):].reshape(bsz, -1, A_KV_HEADS, HEAD_DIM)
    new_state = new_state.reshape(bsz, SSM_HEADS, SSM_HEADDIM, SSM_STATE)
    return x, new_k, new_v, new_sconv, new_conv, new_state


def kernel(x_prompt, x_sample, cache_attn_k, cache_attn_v, state_sconv, state_ssm_conv, state_ssm, norm_g,
           ab_w_in, ab_w_out, attn_sinks, sconv_w, ssm_w_in, ssm_conv_w, ssm_conv_b, ssm_dt_bias, ssm_a_log,
           ssm_d, ssm_norm_g, ssm_w_out, ffn_w_gate, ffn_w_up, ffn_w_down):
    w_in1 = ssm_w_in[0]
    wts = {
        "norm_g": norm_g,
        "ab_w_in": ab_w_in[0].astype(BF16),
        "ab_w_out": ab_w_out[0].astype(BF16),
        "sinks": attn_sinks[0],
        "sconv_w": sconv_w[0],
        "w_z": w_in1[:, :SSM_INNER].astype(BF16),
        "w_xbc": w_in1[:, SSM_INNER:SSM_INNER + SSM_CONV_CH].astype(BF16),
        "w_dt": jnp.tile(w_in1[:, SSM_INNER + SSM_CONV_CH:], (1, DT_REP)).astype(BF16),
        "ssm_conv_w": ssm_conv_w[0],
        "ssm_conv_b": ssm_conv_b[0].reshape(1, -1),
        "dt_bias": jnp.tile(ssm_dt_bias[0], DT_REP).reshape(1, -1),
        "a_log": jnp.tile(ssm_a_log[0], DT_REP).reshape(1, -1),
        "d_skip": jnp.repeat(ssm_d[0], SSM_HEADDIM).reshape(1, -1),
        "ssm_norm_g": ssm_norm_g[0].reshape(1, -1),
        "ssm_w_out": ssm_w_out[0].astype(BF16),
        "wg": ffn_w_gate.astype(BF16),
        "wu": ffn_w_up.astype(BF16),
        "wd": ffn_w_down.astype(BF16),
    }
    bp = x_prompt.shape[0]
    bs, ls = x_sample.shape[0], x_sample.shape[1]

    yp, kp, vp, scp, ccp, ssp = _run(
        x_prompt, 0,
        jnp.zeros((bp, B_CONV - 1, B_WIDTH), F32), None, None,
        jnp.zeros((bp, SSM_CONV - 1, SSM_CONV_CH), F32),
        jnp.zeros((bp, SSM_HEADS, SSM_HEADDIM, SSM_STATE), F32),
        wts, (512, 256, CHUNK, 512, 256, 256, CHUNK))
    ys, ks, vs, scs, ccs, sss = _run(
        x_sample, PAST_LEN,
        state_sconv[0], cache_attn_k[0].reshape(bs, -1, KV_WIDTH), cache_attn_v[0].reshape(bs, -1, KV_WIDTH),
        state_ssm_conv[0], state_ssm[0],
        wts, (ls, ls, ls, bs * ls, ls, ls, ls))
    lead = lambda a: a[None]
    return (yp, ys, lead(kp), lead(vp), lead(scp), lead(ccp), lead(ssp),
            lead(ks), lead(vs), lead(scs), lead(ccs), lead(sss))
```

```python
import functools

import jax
import jax.numpy as jnp
from jax import lax
from jax.experimental import pallas as pl
from jax.experimental.pallas import tpu as pltpu

F32 = jnp.float32
BF16 = jnp.bfloat16

D_MODEL = 1024
CHUNK = 64
EPS = 1e-6
PAST_LEN = 4096

A_HEADS = 8
A_KV_HEADS = 2
A_GROUP = A_HEADS // A_KV_HEADS
HEAD_DIM = 64
A_WIDTH = A_HEADS * HEAD_DIM
KV_WIDTH = A_KV_HEADS * HEAD_DIM
WINDOW = 128
ROPE_DIM = HEAD_DIM // 4
ROPE_HALF = ROPE_DIM // 2
ROPE_THETA = 500000.0
ATTN_SCALE = HEAD_DIM ** -0.5

B_WIDTH = D_MODEL // 2
B_CONV = 3
AB_IN = A_WIDTH + 2 * KV_WIDTH + 3 * B_WIDTH

SSM_INNER = 2 * D_MODEL
SSM_HEADDIM = 64
SSM_HEADS = SSM_INNER // SSM_HEADDIM
SSM_GROUPS = 4
SSM_GROUP_W = SSM_INNER // SSM_GROUPS
SSM_STATE = 128
SSM_CONV = 4
SSM_GN = SSM_GROUPS * SSM_STATE
SSM_CONV_CH = SSM_INNER + 2 * SSM_GN

D_FF = -(-8 * D_MODEL // (3 * 256)) * 256

LANES = 128
SUBLANES = 8
MXU_DIM = 256
DT_REP = LANES // SSM_HEADS
VMEM_LIMIT = 56 * 1024 * 1024
COL_CHUNK = MXU_DIM
FF_CHUNKS = (6 * MXU_DIM, 5 * MXU_DIM)
assert sum(FF_CHUNKS) == D_FF


def _rms(x, g):
    return x * lax.rsqrt(jnp.mean(x * x, axis=-1, keepdims=True) + EPS) * g


def _dot(a, b):
    return jnp.dot(a, b, preferred_element_type=F32)


def _dot_nt(a, b):
    return lax.dot_general(a, b, (((1,), (1,)), ((), ())), preferred_element_type=F32)


def _silu(x):
    return x * jax.nn.sigmoid(x)


def _const_spec(shape):
    return pl.BlockSpec(shape, lambda *_: (0,) * len(shape), pipeline_mode=pl.Buffered(1))


def _params():
    return pltpu.CompilerParams(dimension_semantics=("arbitrary", "arbitrary"),
                                vmem_limit_bytes=VMEM_LIMIT)


def _causal_conv(carry_ref, nh_ref, u, cw, cols):
    width = cw.shape[0]
    rows, ch = u.shape
    n = rows // SUBLANES
    prev3 = jnp.concatenate([carry_ref[:, cols], u[:rows - SUBLANES]], axis=0).reshape(n, SUBLANES, ch)
    u3 = u.reshape(n, SUBLANES, ch)
    sub = lax.broadcasted_iota(jnp.int32, (1, SUBLANES, ch), 1)
    y = None
    for j in range(width - 1):
        k = width - 1 - j
        shifted = pltpu.roll(jnp.where(sub >= SUBLANES - k, prev3, u3), k, 1).reshape(rows, ch)
        term = shifted * cw[j:j + 1]
        y = term if y is None else y + term
    y = y + u * cw[width - 1:width]
    carry_ref[:, cols] = u[rows - SUBLANES:]
    nh_ref[:, cols] = u[rows - (width - 1):]
    return y


def _rope_freq(shape):
    lane = lax.broadcasted_iota(jnp.int32, shape, 1)
    d = lane % HEAD_DIM
    expo = -((d % ROPE_HALF).astype(F32)) / ROPE_HALF
    freq = jnp.power(jnp.full(shape, ROPE_THETA, F32), expo)
    return jnp.where(d < ROPE_DIM, freq, 0.0), d


def _attn_in_kernel(pos0, tm, x_ref, g_ref, w_ref, hist_ref, cw_ref,
                    q_ref, k_ref, v_ref, gated_ref, nh_ref,
                    cr_ref, sr_ref, carry_ref, p_ref):
    b = pl.program_id(0)
    i = pl.program_id(1)

    @pl.when((b == 0) & (i == 0))
    def _():
        freq, _ = _rope_freq((tm, LANES))
        ang = lax.broadcasted_iota(jnp.int32, (tm, LANES), 0).astype(F32) * freq
        cr_ref[...] = jnp.cos(ang)
        sr_ref[...] = jnp.sin(ang)

    @pl.when(i == 0)
    def _():
        carry_ref[SUBLANES - (B_CONV - 1):, :] = hist_ref[...]

    freq1, d1 = _rope_freq((1, LANES))
    base = (pos0 + i * tm).astype(F32) * freq1
    cb = jnp.cos(base)
    sb = jnp.sin(base)
    cr = cr_ref[...]
    sr = sr_ref[...]
    cos_t = cr * cb - sr * sb
    sin_t = sr * cb + cr * sb
    m_lo = jnp.where(d1 < ROPE_HALF, -1.0, 0.0)
    m_hi = jnp.where((d1 >= ROPE_HALF) & (d1 < ROPE_DIM), 1.0, 0.0)

    def rope(t):
        partner = pltpu.roll(t, LANES - ROPE_HALF, 1) * m_lo + pltpu.roll(t, ROPE_HALF, 1) * m_hi
        return t * cos_t + partner * sin_t

    h = _rms(x_ref[...], g_ref[...]).astype(BF16)
    o1 = A_WIDTH
    o2 = o1 + KV_WIDTH
    o3 = o2 + KV_WIDTH
    o4 = o3 + B_WIDTH
    o5 = o4 + B_WIDTH

    for n in range(AB_IN // MXU_DIM):
        cols = slice(n * MXU_DIM, (n + 1) * MXU_DIM)
        p_ref[:, cols] = _dot(h, w_ref[:, cols])

    u = p_ref[:, o4:o5] * p_ref[:, o5:]
    y = _causal_conv(carry_ref, nh_ref, u, cw_ref[...], slice(None))
    gated_ref[...] = (p_ref[:, o3:o4] * y).astype(BF16)
    k_ref[...] = rope(p_ref[:, o1:o2])
    v_ref[...] = p_ref[:, o2:o3]
    for s in range(A_WIDTH // LANES):
        q_ref[:, s * LANES:(s + 1) * LANES] = (rope(p_ref[:, s * LANES:(s + 1) * LANES]) * ATTN_SCALE).astype(BF16)


def _attn_in(x, g, w, hist, cw, pos0, tm):
    bsz, seq, _ = x.shape
    grid = (bsz, seq // tm)
    tok = lambda width: pl.BlockSpec((None, tm, width), lambda b, i: (b, i, 0))
    return pl.pallas_call(
        functools.partial(_attn_in_kernel, pos0, tm),
        grid=grid,
        in_specs=[tok(D_MODEL), _const_spec((1, D_MODEL)), _const_spec((D_MODEL, AB_IN)),
                  pl.BlockSpec((None, B_CONV - 1, B_WIDTH), lambda b, i: (b, 0, 0)),
                  _const_spec((B_CONV, B_WIDTH))],
        out_specs=[tok(A_WIDTH), tok(KV_WIDTH), tok(KV_WIDTH), tok(B_WIDTH),
                   pl.BlockSpec((None, B_CONV - 1, B_WIDTH), lambda b, i: (b, 0, 0))],
        out_shape=[jax.ShapeDtypeStruct((bsz, seq, A_WIDTH), BF16),
                   jax.ShapeDtypeStruct((bsz, seq, KV_WIDTH), F32),
                   jax.ShapeDtypeStruct((bsz, seq, KV_WIDTH), F32),
                   jax.ShapeDtypeStruct((bsz, seq, B_WIDTH), BF16),
                   jax.ShapeDtypeStruct((bsz, B_CONV - 1, B_WIDTH), F32)],
        scratch_shapes=[pltpu.VMEM((tm, LANES), F32), pltpu.VMEM((tm, LANES), F32),
                        pltpu.VMEM((SUBLANES, B_WIDTH), F32), pltpu.VMEM((tm, AB_IN), F32)],
        compiler_params=_params(),
        name="attn_in_proj",
    )(x, g, w, hist, cw)


def _attn_kernel(tq, cq, masked, sinks_ref, q_ref, kc_ref, vc_ref, kp_ref, vp_ref, att_ref,
                 kk_ref, vx_ref, s_ref, e_ref):
    i = pl.program_id(1)
    nk = WINDOW + cq
    nkp = MXU_DIM
    rows_kv = WINDOW + tq
    hd = HEAD_DIM

    kk_ref[0:WINDOW, :] = kp_ref[...].astype(BF16)
    kk_ref[WINDOW:rows_kv, :] = kc_ref[...].astype(BF16)
    kk_ref[rows_kv:, :] = jnp.zeros((kk_ref.shape[0] - rows_kv, KV_WIDTH), BF16)
    ones = jnp.ones((rows_kv, hd), F32)
    vall = jnp.concatenate([vp_ref[...], vc_ref[...]], axis=0)
    vx_ref[0:rows_kv, :] = jnp.concatenate([vall[:, :hd], ones, vall[:, hd:], ones], axis=1)
    vx_ref[rows_kv:, :] = jnp.zeros((vx_ref.shape[0] - rows_kv, 2 * LANES), F32)

    col = lax.broadcasted_iota(jnp.int32, (A_GROUP * cq, nkp), 1)
    krow = lax.broadcasted_iota(jnp.int32, (nkp, LANES), 0)
    klane = lax.broadcasted_iota(jnp.int32, (nkp, LANES), 1)
    sink_row = jnp.where(klane < hd, 0.0, 1.0)
    bodies = [(c, kh) for c in range(tq // cq) for kh in range(A_KV_HEADS)]
    fills = []
    for kh in range(A_KV_HEADS):
        sink = jnp.concatenate(
            [jnp.full((cq, nkp), sinks_ref[kh * A_GROUP + g], F32) for g in range(A_GROUP)], axis=0)
        fills.append(jnp.where(col == nk, sink, -jnp.inf))

    for n, (c, kh) in enumerate(bodies):
        r0 = c * cq
        heads = [kh * A_GROUP + g for g in range(A_GROUP)]
        kt = kk_ref[r0:r0 + nkp, kh * hd:(kh + 1) * hd]
        qs = jnp.concatenate([q_ref[r0:r0 + cq, hh * hd:(hh + 1) * hd] for hh in heads], axis=0)
        s = _dot_nt(qs, kt)
        valid = col < nk
        if masked and r0 < WINDOW:
            valid = valid & ((col >= WINDOW - r0) | (i > 0))
        s_ref[n] = jnp.where(valid, s, fills[kh])

    for n in range(len(bodies)):
        s = s_ref[n]
        m = jnp.max(s, axis=-1, keepdims=True)
        e_ref[n] = jnp.exp(s - m).astype(BF16)

    for n, (c, kh) in enumerate(bodies):
        r0 = c * cq
        vt = jnp.where(krow == nk, sink_row, vx_ref[r0:r0 + nkp, kh * LANES:(kh + 1) * LANES]).astype(BF16)
        ox = _dot(e_ref[n], vt)
        o = (ox / pltpu.roll(ox, hd, 1))[:, :hd]
        for g in range(A_GROUP):
            hh = kh * A_GROUP + g
            att_ref[r0:r0 + cq, hh * hd:(hh + 1) * hd] = o[g * cq:(g + 1) * cq].astype(BF16)


def _attn(sinks, q, k, v, k_prev, v_prev, tq, cq, masked):
    bsz, seq, _ = q.shape
    grid = (bsz, seq // tq)
    tok = lambda width: pl.BlockSpec((None, tq, width), lambda b, i: (b, i, 0))
    if masked:
        blocks_per_tile = tq // WINDOW
        prev = pl.BlockSpec((None, WINDOW, KV_WIDTH),
                            lambda b, i: (b, jnp.maximum(i * blocks_per_tile - 1, 0), 0))
    else:
        prev = pl.BlockSpec((None, WINDOW, KV_WIDTH), lambda b, i: (b, 0, 0))
    n_bodies = (tq // cq) * A_KV_HEADS
    kv_rows = (tq // cq - 1) * cq + MXU_DIM
    return pl.pallas_call(
        functools.partial(_attn_kernel, tq, cq, masked),
        grid=grid,
        in_specs=[pl.BlockSpec(memory_space=pltpu.SMEM),
                  tok(A_WIDTH), tok(KV_WIDTH), tok(KV_WIDTH), prev, prev],
        out_specs=tok(A_WIDTH),
        out_shape=jax.ShapeDtypeStruct((bsz, seq, A_WIDTH), BF16),
        scratch_shapes=[pltpu.VMEM((kv_rows, KV_WIDTH), BF16), pltpu.VMEM((kv_rows, 2 * LANES), F32),
                        pltpu.VMEM((n_bodies, A_GROUP * cq, MXU_DIM), F32),
                        pltpu.VMEM((n_bodies, A_GROUP * cq, MXU_DIM), BF16)],
        compiler_params=_params(),
        name="band_attention",
    )(sinks, q, k, v, k_prev, v_prev)


def _proj_ffn_kernel(n_in, gate, *refs):
    a_refs = refs[:n_in]
    refs = refs[n_in:]
    if gate:
        z_ref, ng_ref = refs[:2]
        refs = refs[2:]
    wo_ref, x_ref, g1_ref, g2_ref, wg_ref, wu_ref, wd_ref, g3_ref, o_ref = refs

    acc = None
    off = 0
    if gate:
        (y_ref,) = a_refs
        for gi in range(SSM_GROUPS):
            gcols = slice(gi * SSM_GROUP_W, (gi + 1) * SSM_GROUP_W)
            gg = y_ref[:, gcols].astype(F32) * _silu(z_ref[:, gcols].astype(F32))
            gg = gg * lax.rsqrt(jnp.mean(gg * gg, axis=-1, keepdims=True) + EPS)
            part = _dot((gg * ng_ref[:, gcols]).astype(BF16), wo_ref[gcols, :])
            acc = part if acc is None else acc + part
    else:
        for a_ref in a_refs:
            kdim = a_ref.shape[-1]
            part = _dot(a_ref[...], wo_ref[off:off + kdim, :])
            acc = part if acc is None else acc + part
            off += kdim

    x = x_ref[...] + _rms(acc, g1_ref[...])
    h = _rms(x, g2_ref[...]).astype(BF16)
    f = None
    off = 0
    for width in FF_CHUNKS:
        fc = slice(off, off + width)
        m = (_silu(_dot(h, wg_ref[:, fc])) * _dot(h, wu_ref[:, fc])).astype(BF16)
        part = _dot(m, wd_ref[fc, :])
        f = part if f is None else f + part
        off += width
    o_ref[...] = x + _rms(f, g3_ref[...])


def _proj_ffn(mix, gate_args, wo, x, g1, g2, wg, wu, wd, g3, tm):
    bsz, seq, _ = x.shape
    tok = lambda width: pl.BlockSpec((None, tm, width), lambda b, i: (b, i, 0))
    gate = gate_args is not None
    extra, extra_specs = [], []
    if gate:
        z, ng = gate_args
        extra, extra_specs = [z, ng], [tok(SSM_INNER), _const_spec((1, SSM_INNER))]
    return pl.pallas_call(
        functools.partial(_proj_ffn_kernel, len(mix), gate),
        grid=(bsz, seq // tm),
        in_specs=[tok(a.shape[-1]) for a in mix] + extra_specs + [
            _const_spec(wo.shape), tok(D_MODEL), _const_spec((1, D_MODEL)), _const_spec((1, D_MODEL)),
            _const_spec((D_MODEL, D_FF)), _const_spec((D_MODEL, D_FF)), _const_spec((D_FF, D_MODEL)),
            _const_spec((1, D_MODEL))],
        out_specs=tok(D_MODEL),
        out_shape=jax.ShapeDtypeStruct(x.shape, F32),
        compiler_params=_params(),
        name="out_proj_ffn",
    )(*mix, *extra, wo, x, g1, g2, wg, wu, wd, g3)


def _split3(v):
    t1 = v.astype(BF16)
    r = v - t1.astype(F32)
    t2 = r.astype(BF16)
    t3 = (r - t2.astype(F32)).astype(BF16)
    return t1, t2, t3


def _pack3(v):
    t1, t2, t3 = _split3(v)
    lane = lax.broadcasted_iota(jnp.int32, v.shape, 1)
    return jnp.where(lane < SSM_HEADS, t1,
                     jnp.where(lane < 2 * SSM_HEADS, t2,
                               jnp.where(lane < 3 * SSM_HEADS, t3, jnp.zeros_like(t3))))


def _ssm_in_kernel(tm, cl, x_ref, g_ref, wz_ref, wx_ref, wdt_ref, hist_ref, cw_ref, cb_ref, dtb_ref, alog_ref,
                   z_ref, xs_ref, bm_ref, cm_ref, ta_ref, tdt_ref, nh_ref, carry_ref):
    i = pl.program_id(1)

    @pl.when(i == 0)
    def _():
        carry_ref[SUBLANES - (SSM_CONV - 1):, :] = hist_ref[...]

    h = _rms(x_ref[...], g_ref[...]).astype(BF16)

    dt = jax.nn.softplus(_dot(h, wdt_ref[...]) + dtb_ref[...])
    tdt_ref[...] = _pack3(dt)
    d1, d2, d3 = _split3(dt * (-jnp.exp(alog_ref[...])))
    tcol = lax.broadcasted_iota(jnp.int32, (cl, 3 * cl), 1) % cl
    trow = lax.broadcasted_iota(jnp.int32, (cl, 3 * cl), 0)
    tril = jnp.where(tcol <= trow, 1.0, 0.0).astype(BF16)
    acum = []
    for c in range(tm // cl):
        rows = slice(c * cl, (c + 1) * cl)
        acum.append(_dot(tril, jnp.concatenate([d1[rows], d2[rows], d3[rows]], axis=0)))
    ta_ref[...] = _pack3(jnp.concatenate(acum, axis=0))

    for n in range(SSM_INNER // COL_CHUNK):
        cols = slice(n * COL_CHUNK, (n + 1) * COL_CHUNK)
        z_ref[:, cols] = _dot(h, wz_ref[:, cols]).astype(BF16)

    for n in range(SSM_CONV_CH // COL_CHUNK):
        cols = slice(n * COL_CHUNK, (n + 1) * COL_CHUNK)
        raw = _dot(h, wx_ref[:, cols])
        y = _silu(_causal_conv(carry_ref, nh_ref, raw, cw_ref[:, cols], cols) + cb_ref[:, cols])
        if cols.stop <= SSM_INNER:
            xs_ref[:, cols] = y
        elif cols.stop <= SSM_INNER + SSM_GN:
            bm_ref[:, cols.start - SSM_INNER:cols.stop - SSM_INNER] = y.astype(BF16)
        else:
            lo = SSM_INNER + SSM_GN
            cm_ref[:, cols.start - lo:cols.stop - lo] = y.astype(BF16)


def _ssm_in(x, g, wz, wx, wdt, hist, cw, cb, dtb, alog, tm, cl):
    bsz, seq, _ = x.shape
    tok = lambda width: pl.BlockSpec((None, tm, width), lambda b, i: (b, i, 0))
    hist_spec = pl.BlockSpec((None, SSM_CONV - 1, SSM_CONV_CH), lambda b, i: (b, 0, 0))
    return pl.pallas_call(
        functools.partial(_ssm_in_kernel, tm, cl),
        grid=(bsz, seq // tm),
        in_specs=[tok(D_MODEL), _const_spec((1, D_MODEL)), _const_spec((D_MODEL, SSM_INNER)),
                  _const_spec((D_MODEL, SSM_CONV_CH)), _const_spec((D_MODEL, LANES)), hist_spec,
                  _const_spec((SSM_CONV, SSM_CONV_CH)), _const_spec((1, SSM_CONV_CH)), _const_spec((1, LANES)),
                  _const_spec((1, LANES))],
        out_specs=[tok(SSM_INNER), tok(SSM_INNER), tok(SSM_GN), tok(SSM_GN), tok(LANES), tok(LANES), hist_spec],
        out_shape=[jax.ShapeDtypeStruct((bsz, seq, SSM_INNER), BF16),
                   jax.ShapeDtypeStruct((bsz, seq, SSM_INNER), F32),
                   jax.ShapeDtypeStruct((bsz, seq, SSM_GN), BF16),
                   jax.ShapeDtypeStruct((bsz, seq, SSM_GN), BF16),
                   jax.ShapeDtypeStruct((bsz, seq, LANES), BF16),
                   jax.ShapeDtypeStruct((bsz, seq, LANES), BF16),
                   jax.ShapeDtypeStruct((bsz, SSM_CONV - 1, SSM_CONV_CH), F32)],
        scratch_shapes=[pltpu.VMEM((SUBLANES, SSM_CONV_CH), F32)],
        compiler_params=_params(),
        name="ssm_in_proj",
    )(x, g, wz, wx, wdt, hist, cw, cb, dtb, alog)


def _pad_rows(a, rows):
    if a.shape[0] == rows:
        return a
    return jnp.concatenate([a, jnp.zeros((rows - a.shape[0], a.shape[1]), a.dtype)], axis=0)


def _ssd_kernel(ts, cl, xs_ref, bm_ref, cm_ref, ta_ref, tdt_ref, h0_ref, dskip_ref,
                yo_ref, hout_ref, ht_ref, y_ref, e3_ref, aexp_ref, xdt_ref, st_ref, cb_ref, bmt_ref, w2_ref, rhs_ref):
    b = pl.program_id(0)
    i = pl.program_id(1)
    p = SSM_HEADDIM
    nc = ts // cl
    pairs = SSM_GROUP_W // LANES

    @pl.when((b == 0) & (i == 0))
    def _():
        kk = lax.broadcasted_iota(jnp.int32, (LANES, SSM_INNER), 0)
        cc = lax.broadcasted_iota(jnp.int32, (LANES, SSM_INNER), 1)
        hit = (kk % SSM_HEADS == cc // p) & (kk < 3 * SSM_HEADS)
        e3_ref[...] = jnp.where(hit, 1.0, 0.0).astype(BF16)

    @pl.when(i == 0)
    def _():
        ht_ref[...] = h0_ref[...].T

    lane = lax.broadcasted_iota(jnp.int32, (cl, LANES), 1)
    row = lax.broadcasted_iota(jnp.int32, (cl, LANES), 0)
    causal2 = (lane % p) <= row
    left = lane < p
    icol = lax.broadcasted_iota(jnp.int32, (cl, SSM_INNER), 1)
    irow = lax.broadcasted_iota(jnp.int32, (cl, SSM_INNER), 0)
    diag = (icol % p) == irow
    groups = [(gi, slice(gi * SSM_GROUP_W, (gi + 1) * SSM_GROUP_W), slice(gi * SSM_STATE, (gi + 1) * SSM_STATE))
              for gi in range(SSM_GROUPS)]
    chunks = [(c, slice(c * cl, (c + 1) * cl)) for c in range(nc)]

    for gi, gcols, _ in groups:
        e3g = e3_ref[:, gcols]
        aexp_ref[:, gcols] = _dot(ta_ref[...], e3g)
        xdt_ref[:, gcols] = xs_ref[:, gcols] * _dot(tdt_ref[...], e3g)

    a_src = [jnp.sum(jnp.where(diag, aexp_ref[rows, :], 0.0), axis=0, keepdims=True) for _, rows in chunks]

    for c, rows in chunks:
        for gi, _, scol in groups:
            bmp = _pad_rows(bm_ref[rows, scol], p)
            cbm = _dot_nt(cm_ref[rows, scol], bmp)
            cb_ref[c * SSM_GROUPS + gi] = jnp.concatenate([cbm, cbm], axis=1)
            bmt_ref[c * SSM_GROUPS + gi] = bmp.astype(F32).T.astype(BF16)

    for c, rows in chunks:
        for gi, _, _ in groups:
            for jj in range(pairs):
                idx = (c * SSM_GROUPS + gi) * pairs + jj
                cols = slice(gi * SSM_GROUP_W + jj * LANES, gi * SSM_GROUP_W + (jj + 1) * LANES)
                seg = aexp_ref[rows, cols] - a_src[c][:, cols]
                dec = jnp.exp(jnp.where(causal2, seg, -jnp.inf))
                w2_ref[idx] = (dec * cb_ref[c * SSM_GROUPS + gi]).astype(BF16)
                x2 = xdt_ref[rows, cols]
                top = _pad_rows(jnp.where(left, x2, 0.0), p).astype(BF16)
                bot = _pad_rows(jnp.where(left, 0.0, x2), p).astype(BF16)
                rhs_ref[idx] = jnp.concatenate([top, bot], axis=0)

    for c, rows in chunks:
        for gi, _, _ in groups:
            for jj in range(pairs):
                idx = (c * SSM_GROUPS + gi) * pairs + jj
                cols = slice(gi * SSM_GROUP_W + jj * LANES, gi * SSM_GROUP_W + (jj + 1) * LANES)
                y_ref[rows, cols] = _dot(w2_ref[idx], rhs_ref[idx])

    for c, rows in chunks:
        for gi, gcols, _ in groups:
            a_g = aexp_ref[rows, gcols]
            xw = (xdt_ref[rows, gcols] * jnp.exp(a_g[cl - 1:cl, :] - a_g)).astype(BF16)
            st_ref[c, :, gcols] = _dot(bmt_ref[c * SSM_GROUPS + gi], _pad_rows(xw, p))

    for c, rows in chunks:
        for gi, gcols, scol in groups:
            hprev = ht_ref[:, gcols]
            a_g = aexp_ref[rows, gcols]
            y_off = _dot(cm_ref[rows, scol], hprev.astype(BF16)) * jnp.exp(a_g)
            y_ref[rows, gcols] = y_ref[rows, gcols] + y_off
            ht_ref[:, gcols] = hprev * jnp.exp(a_g[cl - 1:cl, :]) + st_ref[c, :, gcols]

    yo_ref[...] = (y_ref[...] + dskip_ref[...] * xs_ref[...]).astype(BF16)

    @pl.when(i == pl.num_programs(1) - 1)
    def _():
        hout_ref[...] = ht_ref[...].T


def _ssd(xs, bm, cm, ta, tdt, h0, dskip, ts, cl):
    bsz, seq, _ = xs.shape
    tok = lambda width: pl.BlockSpec((None, ts, width), lambda b, i: (b, i, 0))
    st_spec = pl.BlockSpec((None, SSM_INNER, SSM_STATE), lambda b, i: (b, 0, 0))
    nc = ts // cl
    n_cg = nc * SSM_GROUPS
    n_pairs = n_cg * (SSM_GROUP_W // LANES)
    return pl.pallas_call(
        functools.partial(_ssd_kernel, ts, cl),
        grid=(bsz, seq // ts),
        in_specs=[tok(SSM_INNER), tok(SSM_GN), tok(SSM_GN), tok(LANES), tok(LANES), st_spec,
                  _const_spec((1, SSM_INNER))],
        out_specs=[tok(SSM_INNER), st_spec],
        out_shape=[jax.ShapeDtypeStruct((bsz, seq, SSM_INNER), BF16),
                   jax.ShapeDtypeStruct((bsz, SSM_INNER, SSM_STATE), F32)],
        scratch_shapes=[pltpu.VMEM((SSM_STATE, SSM_INNER), F32), pltpu.VMEM((ts, SSM_INNER), F32),
                        pltpu.VMEM((LANES, SSM_INNER), BF16), pltpu.VMEM((ts, SSM_INNER), F32),
                        pltpu.VMEM((ts, SSM_INNER), F32), pltpu.VMEM((nc, SSM_STATE, SSM_INNER), F32),
                        pltpu.VMEM((n_cg, cl, LANES), F32), pltpu.VMEM((n_cg, SSM_STATE, SSM_HEADDIM), BF16),
                        pltpu.VMEM((n_pairs, cl, LANES), BF16), pltpu.VMEM((n_pairs, LANES, LANES), BF16)],
        compiler_params=_params(),
        name="ssd_scan",
    )(xs, bm, cm, ta, tdt, h0, dskip)


def _run(x, pos0, sconv_hist, k_cache, v_cache, conv_hist, ssm_state, wts, tiles):
    bsz, seq, _ = x.shape
    t_in, t_att, c_att, t_ffn, t_sin, t_ssd, c_ssd = tiles
    row = lambda v: v.reshape(1, -1)
    ng = wts["norm_g"]
    flat = (lambda a: a.reshape(1, bsz * seq, a.shape[-1])) if seq < t_ffn else (lambda a: a)

    q, k, v, gated, new_sconv = _attn_in(x, row(ng[0, 0]), wts["ab_w_in"], sconv_hist, wts["sconv_w"], pos0, t_in)
    masked = k_cache is None
    k_prev, v_prev = (k, v) if masked else (k_cache, v_cache)
    att = _attn(wts["sinks"], q, k, v, k_prev, v_prev, t_att, c_att, masked)
    x = _proj_ffn([flat(att), flat(gated)], None, wts["ab_w_out"], flat(x), row(ng[0, 1]), row(ng[0, 2]),
                  wts["wg"][0], wts["wu"][0], wts["wd"][0], row(ng[0, 3]), t_ffn).reshape(x.shape)

    z, xs, bm, cm, ta, tdt, new_conv = _ssm_in(
        x, row(ng[1, 0]), wts["w_z"], wts["w_xbc"], wts["w_dt"], conv_hist, wts["ssm_conv_w"],
        wts["ssm_conv_b"], wts["dt_bias"], wts["a_log"], t_sin, c_ssd)
    y, new_state = _ssd(xs, bm, cm, ta, tdt, ssm_state.reshape(bsz, SSM_INNER, SSM_STATE),
                        wts["d_skip"], t_ssd, c_ssd)
    x = _proj_ffn([flat(y)], (flat(z), wts["ssm_norm_g"]), wts["ssm_w_out"], flat(x), row(ng[1, 1]),
                  row(ng[1, 2]), wts["wg"][1], wts["wu"][1], wts["wd"][1], row(ng[1, 3]), t_ffn).reshape(x.shape)

    new_k = k[:, seq - min(WINDOW, seq):].reshape(bsz, -1, A_KV_HEADS, HEAD_DIM)
    new_v = v[:, seq - min(WINDOW, seq):].reshape(bsz, -1, A_KV_HEADS, HEAD_DIM)
    new_state = new_state.reshape(bsz, SSM_HEADS, SSM_HEADDIM, SSM_STATE)
    return x, new_k, new_v, new_sconv, new_conv, new_state


def kernel(x_prompt, x_sample, cache_attn_k, cache_attn_v, state_sconv, state_ssm_conv, state_ssm, norm_g,
           ab_w_in, ab_w_out, attn_sinks, sconv_w, ssm_w_in, ssm_conv_w, ssm_conv_b, ssm_dt_bias, ssm_a_log,
           ssm_d, ssm_norm_g, ssm_w_out, ffn_w_gate, ffn_w_up, ffn_w_down):
    w_in1 = ssm_w_in[0]
    wts = {
        "norm_g": norm_g,
        "ab_w_in": ab_w_in[0].astype(BF16),
        "ab_w_out": ab_w_out[0].astype(BF16),
        "sinks": attn_sinks[0],
        "sconv_w": sconv_w[0],
        "w_z": w_in1[:, :SSM_INNER].astype(BF16),
        "w_xbc": w_in1[:, SSM_INNER:SSM_INNER + SSM_CONV_CH].astype(BF16),
        "w_dt": jnp.tile(w_in1[:, SSM_INNER + SSM_CONV_CH:], (1, DT_REP)).astype(BF16),
        "ssm_conv_w": ssm_conv_w[0],
        "ssm_conv_b": ssm_conv_b[0].reshape(1, -1),
        "dt_bias": jnp.tile(ssm_dt_bias[0], DT_REP).reshape(1, -1),
        "a_log": jnp.tile(ssm_a_log[0], DT_REP).reshape(1, -1),
        "d_skip": jnp.repeat(ssm_d[0], SSM_HEADDIM).reshape(1, -1),
        "ssm_norm_g": ssm_norm_g[0].reshape(1, -1),
        "ssm_w_out": ssm_w_out[0].astype(BF16),
        "wg": ffn_w_gate.astype(BF16),
        "wu": ffn_w_up.astype(BF16),
        "wd": ffn_w_down.astype(BF16),
    }
    bp = x_prompt.shape[0]
    bs, ls = x_sample.shape[0], x_sample.shape[1]

    yp, kp, vp, scp, ccp, ssp = _run(
        x_prompt, 0,
        jnp.zeros((bp, B_CONV - 1, B_WIDTH), F32), None, None,
        jnp.zeros((bp, SSM_CONV - 1, SSM_CONV_CH), F32),
        jnp.zeros((bp, SSM_HEADS, SSM_HEADDIM, SSM_STATE), F32),
        wts, (512, 256, CHUNK, 512, 512, 256, CHUNK))
    ys, ks, vs, scs, ccs, sss = _run(
        x_sample, PAST_LEN,
        state_sconv[0], cache_attn_k[0].reshape(bs, -1, KV_WIDTH), cache_attn_v[0].reshape(bs, -1, KV_WIDTH),
        state_ssm_conv[0], state_ssm[0],
        wts, (ls, ls, ls, bs * ls, ls, ls, ls))
    lead = lambda a: a[None]
    return (yp, ys, lead(kp), lead(vp), lead(scp), lead(ccp), lead(ssp),
            lead(ks), lead(vs), lead(scs), lead(ccs), lead(sss))
```

```python
import functools

import jax
import jax.numpy as jnp
from jax import lax
from jax.experimental import pallas as pl
from jax.experimental.pallas import tpu as pltpu

F32 = jnp.float32
BF16 = jnp.bfloat16

D_MODEL = 1024
CHUNK = 64
EPS = 1e-6
PAST_LEN = 4096

A_HEADS = 8
A_KV_HEADS = 2
A_GROUP = A_HEADS // A_KV_HEADS
HEAD_DIM = 64
A_WIDTH = A_HEADS * HEAD_DIM
KV_WIDTH = A_KV_HEADS * HEAD_DIM
WINDOW = 128
ROPE_DIM = HEAD_DIM // 4
ROPE_HALF = ROPE_DIM // 2
ROPE_THETA = 500000.0
ATTN_SCALE = HEAD_DIM ** -0.5

B_WIDTH = D_MODEL // 2
B_CONV = 3
AB_IN = A_WIDTH + 2 * KV_WIDTH + 3 * B_WIDTH

SSM_INNER = 2 * D_MODEL
SSM_HEADDIM = 64
SSM_HEADS = SSM_INNER // SSM_HEADDIM
SSM_GROUPS = 4
SSM_GROUP_W = SSM_INNER // SSM_GROUPS
SSM_STATE = 128
SSM_CONV = 4
SSM_GN = SSM_GROUPS * SSM_STATE
SSM_CONV_CH = SSM_INNER + 2 * SSM_GN

D_FF = -(-8 * D_MODEL // (3 * 256)) * 256

LANES = 128
SUBLANES = 8
MXU_DIM = 256
DT_REP = LANES // SSM_HEADS
VMEM_LIMIT = 56 * 1024 * 1024
COL_CHUNK = MXU_DIM
FF_CHUNKS = (6 * MXU_DIM, 5 * MXU_DIM)
assert sum(FF_CHUNKS) == D_FF


def _rms(x, g):
    return x * lax.rsqrt(jnp.mean(x * x, axis=-1, keepdims=True) + EPS) * g


def _dot(a, b):
    return jnp.dot(a, b, preferred_element_type=F32)


def _dot_nt(a, b):
    return lax.dot_general(a, b, (((1,), (1,)), ((), ())), preferred_element_type=F32)


def _silu(x):
    return x * jax.nn.sigmoid(x)


def _const_spec(shape):
    return pl.BlockSpec(shape, lambda *_: (0,) * len(shape), pipeline_mode=pl.Buffered(1))


def _layer_spec(shape, layer):
    return pl.BlockSpec((None,) + shape, lambda *_: (layer,) + (0,) * len(shape), pipeline_mode=pl.Buffered(1))


def _params():
    return pltpu.CompilerParams(dimension_semantics=("arbitrary", "arbitrary"),
                                vmem_limit_bytes=VMEM_LIMIT)


def _causal_conv(carry_ref, nh_ref, u, cw, cols):
    width = cw.shape[0]
    rows, ch = u.shape
    n = rows // SUBLANES
    prev3 = jnp.concatenate([carry_ref[:, cols], u[:rows - SUBLANES]], axis=0).reshape(n, SUBLANES, ch)
    u3 = u.reshape(n, SUBLANES, ch)
    sub = lax.broadcasted_iota(jnp.int32, (1, SUBLANES, ch), 1)
    y = None
    for j in range(width - 1):
        k = width - 1 - j
        shifted = pltpu.roll(jnp.where(sub >= SUBLANES - k, prev3, u3), k, 1).reshape(rows, ch)
        term = shifted * cw[j:j + 1]
        y = term if y is None else y + term
    y = y + u * cw[width - 1:width]
    carry_ref[:, cols] = u[rows - SUBLANES:]
    nh_ref[:, cols] = u[rows - (width - 1):]
    return y


def _rope_freq(shape):
    lane = lax.broadcasted_iota(jnp.int32, shape, 1)
    d = lane % HEAD_DIM
    expo = -((d % ROPE_HALF).astype(F32)) / ROPE_HALF
    freq = jnp.power(jnp.full(shape, ROPE_THETA, F32), expo)
    return jnp.where(d < ROPE_DIM, freq, 0.0), d


def _attn_in_kernel(pos0, tm, x_ref, g_ref, w_ref, hist_ref, cw_ref,
                    q_ref, k_ref, v_ref, gated_ref, nh_ref,
                    cr_ref, sr_ref, carry_ref, p_ref):
    b = pl.program_id(0)
    i = pl.program_id(1)

    @pl.when((b == 0) & (i == 0))
    def _():
        freq, _ = _rope_freq((tm, LANES))
        ang = lax.broadcasted_iota(jnp.int32, (tm, LANES), 0).astype(F32) * freq
        cr_ref[...] = jnp.cos(ang)
        sr_ref[...] = jnp.sin(ang)

    @pl.when(i == 0)
    def _():
        carry_ref[SUBLANES - (B_CONV - 1):, :] = hist_ref[...]

    freq1, d1 = _rope_freq((1, LANES))
    base = (pos0 + i * tm).astype(F32) * freq1
    cb = jnp.cos(base)
    sb = jnp.sin(base)
    cr = cr_ref[...]
    sr = sr_ref[...]
    cos_t = cr * cb - sr * sb
    sin_t = sr * cb + cr * sb
    m_lo = jnp.where(d1 < ROPE_HALF, -1.0, 0.0)
    m_hi = jnp.where((d1 >= ROPE_HALF) & (d1 < ROPE_DIM), 1.0, 0.0)

    def rope(t):
        partner = pltpu.roll(t, LANES - ROPE_HALF, 1) * m_lo + pltpu.roll(t, ROPE_HALF, 1) * m_hi
        return t * cos_t + partner * sin_t

    h = _rms(x_ref[...], g_ref[...]).astype(BF16)
    o1 = A_WIDTH
    o2 = o1 + KV_WIDTH
    o3 = o2 + KV_WIDTH
    o4 = o3 + B_WIDTH
    o5 = o4 + B_WIDTH

    for n in range(AB_IN // MXU_DIM):
        cols = slice(n * MXU_DIM, (n + 1) * MXU_DIM)
        p_ref[:, cols] = _dot(h, w_ref[:, cols])

    u = p_ref[:, o4:o5] * p_ref[:, o5:]
    y = _causal_conv(carry_ref, nh_ref, u, cw_ref[...], slice(None))
    gated_ref[...] = (p_ref[:, o3:o4] * y).astype(BF16)
    k_ref[...] = rope(p_ref[:, o1:o2])
    v_ref[...] = p_ref[:, o2:o3]
    for s in range(A_WIDTH // LANES):
        q_ref[:, s * LANES:(s + 1) * LANES] = (rope(p_ref[:, s * LANES:(s + 1) * LANES]) * ATTN_SCALE).astype(BF16)


def _attn_in(x, g, w, hist, cw, pos0, tm):
    bsz, seq, _ = x.shape
    grid = (bsz, seq // tm)
    tok = lambda width: pl.BlockSpec((None, tm, width), lambda b, i: (b, i, 0))
    return pl.pallas_call(
        functools.partial(_attn_in_kernel, pos0, tm),
        grid=grid,
        in_specs=[tok(D_MODEL), _const_spec((1, D_MODEL)), _const_spec((D_MODEL, AB_IN)),
                  pl.BlockSpec((None, B_CONV - 1, B_WIDTH), lambda b, i: (b, 0, 0)),
                  _const_spec((B_CONV, B_WIDTH))],
        out_specs=[tok(A_WIDTH), tok(KV_WIDTH), tok(KV_WIDTH), tok(B_WIDTH),
                   pl.BlockSpec((None, B_CONV - 1, B_WIDTH), lambda b, i: (b, 0, 0))],
        out_shape=[jax.ShapeDtypeStruct((bsz, seq, A_WIDTH), BF16),
                   jax.ShapeDtypeStruct((bsz, seq, KV_WIDTH), F32),
                   jax.ShapeDtypeStruct((bsz, seq, KV_WIDTH), F32),
                   jax.ShapeDtypeStruct((bsz, seq, B_WIDTH), BF16),
                   jax.ShapeDtypeStruct((bsz, B_CONV - 1, B_WIDTH), F32)],
        scratch_shapes=[pltpu.VMEM((tm, LANES), F32), pltpu.VMEM((tm, LANES), F32),
                        pltpu.VMEM((SUBLANES, B_WIDTH), F32), pltpu.VMEM((tm, AB_IN), F32)],
        compiler_params=_params(),
        name="attn_in_proj",
    )(x, g, w, hist, cw)


def _attn_kernel(tq, cq, masked, sinks_ref, q_ref, kc_ref, vc_ref, kp_ref, vp_ref, att_ref,
                 kk_ref, vx_ref, s_ref, e_ref):
    i = pl.program_id(1)
    nk = WINDOW + cq
    nkp = MXU_DIM
    rows_kv = WINDOW + tq
    hd = HEAD_DIM

    kk_ref[0:WINDOW, :] = kp_ref[...].astype(BF16)
    kk_ref[WINDOW:rows_kv, :] = kc_ref[...].astype(BF16)
    kk_ref[rows_kv:, :] = jnp.zeros((kk_ref.shape[0] - rows_kv, KV_WIDTH), BF16)
    ones = jnp.ones((rows_kv, hd), F32)
    vall = jnp.concatenate([vp_ref[...], vc_ref[...]], axis=0)
    vx_ref[0:rows_kv, :] = jnp.concatenate([vall[:, :hd], ones, vall[:, hd:], ones], axis=1)
    vx_ref[rows_kv:, :] = jnp.zeros((vx_ref.shape[0] - rows_kv, 2 * LANES), F32)

    col = lax.broadcasted_iota(jnp.int32, (A_GROUP * cq, nkp), 1)
    krow = lax.broadcasted_iota(jnp.int32, (nkp, LANES), 0)
    klane = lax.broadcasted_iota(jnp.int32, (nkp, LANES), 1)
    sink_row = jnp.where(klane < hd, 0.0, 1.0)
    bodies = [(c, kh) for c in range(tq // cq) for kh in range(A_KV_HEADS)]
    fills = []
    for kh in range(A_KV_HEADS):
        sink = jnp.concatenate(
            [jnp.full((cq, nkp), sinks_ref[kh * A_GROUP + g], F32) for g in range(A_GROUP)], axis=0)
        fills.append(jnp.where(col == nk, sink, -jnp.inf))

    for n, (c, kh) in enumerate(bodies):
        r0 = c * cq
        heads = [kh * A_GROUP + g for g in range(A_GROUP)]
        kt = kk_ref[r0:r0 + nkp, kh * hd:(kh + 1) * hd]
        qs = jnp.concatenate([q_ref[r0:r0 + cq, hh * hd:(hh + 1) * hd] for hh in heads], axis=0)
        s = _dot_nt(qs, kt)
        valid = col < nk
        if masked and r0 < WINDOW:
            valid = valid & ((col >= WINDOW - r0) | (i > 0))
        s_ref[n] = jnp.where(valid, s, fills[kh])

    for n in range(len(bodies)):
        s = s_ref[n]
        m = jnp.max(s, axis=-1, keepdims=True)
        e_ref[n] = jnp.exp(s - m).astype(BF16)

    for n, (c, kh) in enumerate(bodies):
        r0 = c * cq
        vt = jnp.where(krow == nk, sink_row, vx_ref[r0:r0 + nkp, kh * LANES:(kh + 1) * LANES]).astype(BF16)
        ox = _dot(e_ref[n], vt)
        o = (ox / pltpu.roll(ox, hd, 1))[:, :hd]
        for g in range(A_GROUP):
            hh = kh * A_GROUP + g
            att_ref[r0:r0 + cq, hh * hd:(hh + 1) * hd] = o[g * cq:(g + 1) * cq].astype(BF16)


def _attn(sinks, q, k, v, k_prev, v_prev, tq, cq, masked):
    bsz, seq, _ = q.shape
    grid = (bsz, seq // tq)
    tok = lambda width: pl.BlockSpec((None, tq, width), lambda b, i: (b, i, 0))
    if masked:
        blocks_per_tile = tq // WINDOW
        prev = pl.BlockSpec((None, WINDOW, KV_WIDTH),
                            lambda b, i: (b, jnp.maximum(i * blocks_per_tile - 1, 0), 0))
    else:
        prev = pl.BlockSpec((None, WINDOW, KV_WIDTH), lambda b, i: (b, 0, 0))
    n_bodies = (tq // cq) * A_KV_HEADS
    kv_rows = (tq // cq - 1) * cq + MXU_DIM
    return pl.pallas_call(
        functools.partial(_attn_kernel, tq, cq, masked),
        grid=grid,
        in_specs=[pl.BlockSpec(memory_space=pltpu.SMEM),
                  tok(A_WIDTH), tok(KV_WIDTH), tok(KV_WIDTH), prev, prev],
        out_specs=tok(A_WIDTH),
        out_shape=jax.ShapeDtypeStruct((bsz, seq, A_WIDTH), BF16),
        scratch_shapes=[pltpu.VMEM((kv_rows, KV_WIDTH), BF16), pltpu.VMEM((kv_rows, 2 * LANES), F32),
                        pltpu.VMEM((n_bodies, A_GROUP * cq, MXU_DIM), F32),
                        pltpu.VMEM((n_bodies, A_GROUP * cq, MXU_DIM), BF16)],
        compiler_params=_params(),
        name="band_attention",
    )(sinks, q, k, v, k_prev, v_prev)


def _proj_ffn_kernel(n_in, gate, *refs):
    a_refs = refs[:n_in]
    refs = refs[n_in:]
    if gate:
        z_ref, ng_ref = refs[:2]
        refs = refs[2:]
    wo_ref, x_ref, g1_ref, g2_ref, wg_ref, wu_ref, wd_ref, g3_ref, o_ref = refs

    acc = None
    off = 0
    if gate:
        (y_ref,) = a_refs
        for gi in range(SSM_GROUPS):
            gcols = slice(gi * SSM_GROUP_W, (gi + 1) * SSM_GROUP_W)
            gg = y_ref[:, gcols].astype(F32) * _silu(z_ref[:, gcols].astype(F32))
            gg = gg * lax.rsqrt(jnp.mean(gg * gg, axis=-1, keepdims=True) + EPS)
            part = _dot((gg * ng_ref[:, gcols]).astype(BF16), wo_ref[gcols, :])
            acc = part if acc is None else acc + part
    else:
        for a_ref in a_refs:
            kdim = a_ref.shape[-1]
            part = _dot(a_ref[...], wo_ref[off:off + kdim, :])
            acc = part if acc is None else acc + part
            off += kdim

    x = x_ref[...] + _rms(acc, g1_ref[...])
    h = _rms(x, g2_ref[...]).astype(BF16)
    f = None
    off = 0
    for width in FF_CHUNKS:
        fc = slice(off, off + width)
        m = (_silu(_dot(h, wg_ref[:, fc])) * _dot(h, wu_ref[:, fc])).astype(BF16)
        part = _dot(m, wd_ref[fc, :])
        f = part if f is None else f + part
        off += width
    o_ref[...] = x + _rms(f, g3_ref[...])


def _proj_ffn(mix, gate_args, wo, x, g1, g2, wg, wu, wd, layer, g3, tm):
    bsz, seq, _ = x.shape
    tok = lambda width: pl.BlockSpec((None, tm, width), lambda b, i: (b, i, 0))
    gate = gate_args is not None
    extra, extra_specs = [], []
    if gate:
        z, ng = gate_args
        extra, extra_specs = [z, ng], [tok(SSM_INNER), _const_spec((1, SSM_INNER))]
    return pl.pallas_call(
        functools.partial(_proj_ffn_kernel, len(mix), gate),
        grid=(bsz, seq // tm),
        in_specs=[tok(a.shape[-1]) for a in mix] + extra_specs + [
            _const_spec(wo.shape), tok(D_MODEL), _const_spec((1, D_MODEL)), _const_spec((1, D_MODEL)),
            _layer_spec((D_MODEL, D_FF), layer), _layer_spec((D_MODEL, D_FF), layer),
            _layer_spec((D_FF, D_MODEL), layer), _const_spec((1, D_MODEL))],
        out_specs=tok(D_MODEL),
        out_shape=jax.ShapeDtypeStruct(x.shape, F32),
        compiler_params=_params(),
        name="out_proj_ffn",
    )(*mix, *extra, wo, x, g1, g2, wg, wu, wd, g3)


def _split3(v):
    t1 = v.astype(BF16)
    r = v - t1.astype(F32)
    t2 = r.astype(BF16)
    t3 = (r - t2.astype(F32)).astype(BF16)
    return t1, t2, t3


def _pack3(v):
    t1, t2, t3 = _split3(v)
    lane = lax.broadcasted_iota(jnp.int32, v.shape, 1)
    return jnp.where(lane < SSM_HEADS, t1,
                     jnp.where(lane < 2 * SSM_HEADS, t2,
                               jnp.where(lane < 3 * SSM_HEADS, t3, jnp.zeros_like(t3))))


def _ssm_in_kernel(tm, cl, x_ref, g_ref, w_ref, wdt_ref, hist_ref, cw_ref, cb_ref, dtb_ref, alog_ref,
                   z_ref, xs_ref, bm_ref, cm_ref, ta_ref, tdt_ref, nh_ref, carry_ref):
    i = pl.program_id(1)

    @pl.when(i == 0)
    def _():
        carry_ref[SUBLANES - (SSM_CONV - 1):, :] = hist_ref[...]

    h = _rms(x_ref[...], g_ref[...]).astype(BF16)

    dt = jax.nn.softplus(_dot(h, wdt_ref[...]) + dtb_ref[...])
    tdt_ref[...] = _pack3(dt)
    d1, d2, d3 = _split3(dt * (-jnp.exp(alog_ref[...])))
    tcol = lax.broadcasted_iota(jnp.int32, (cl, 3 * cl), 1) % cl
    trow = lax.broadcasted_iota(jnp.int32, (cl, 3 * cl), 0)
    tril = jnp.where(tcol <= trow, 1.0, 0.0).astype(BF16)
    acum = []
    for c in range(tm // cl):
        rows = slice(c * cl, (c + 1) * cl)
        acum.append(_dot(tril, jnp.concatenate([d1[rows], d2[rows], d3[rows]], axis=0)))
    ta_ref[...] = _pack3(jnp.concatenate(acum, axis=0))

    for n in range(SSM_INNER // COL_CHUNK):
        cols = slice(n * COL_CHUNK, (n + 1) * COL_CHUNK)
        z_ref[:, cols] = _dot(h, w_ref[:, cols]).astype(BF16)

    for n in range(SSM_CONV_CH // COL_CHUNK):
        cols = slice(n * COL_CHUNK, (n + 1) * COL_CHUNK)
        raw = _dot(h, w_ref[:, SSM_INNER + cols.start:SSM_INNER + cols.stop])
        y = _silu(_causal_conv(carry_ref, nh_ref, raw, cw_ref[:, cols], cols) + cb_ref[:, cols])
        if cols.stop <= SSM_INNER:
            xs_ref[:, cols] = y
        elif cols.stop <= SSM_INNER + SSM_GN:
            bm_ref[:, cols.start - SSM_INNER:cols.stop - SSM_INNER] = y.astype(BF16)
        else:
            lo = SSM_INNER + SSM_GN
            cm_ref[:, cols.start - lo:cols.stop - lo] = y.astype(BF16)


def _ssm_in(x, g, w, wdt, hist, cw, cb, dtb, alog, tm, cl):
    bsz, seq, _ = x.shape
    tok = lambda width: pl.BlockSpec((None, tm, width), lambda b, i: (b, i, 0))
    hist_spec = pl.BlockSpec((None, SSM_CONV - 1, SSM_CONV_CH), lambda b, i: (b, 0, 0))
    return pl.pallas_call(
        functools.partial(_ssm_in_kernel, tm, cl),
        grid=(bsz, seq // tm),
        in_specs=[tok(D_MODEL), _const_spec((1, D_MODEL)), _const_spec(w.shape), _const_spec((D_MODEL, LANES)),
                  hist_spec,
                  _const_spec((SSM_CONV, SSM_CONV_CH)), _const_spec((1, SSM_CONV_CH)), _const_spec((1, LANES)),
                  _const_spec((1, LANES))],
        out_specs=[tok(SSM_INNER), tok(SSM_INNER), tok(SSM_GN), tok(SSM_GN), tok(LANES), tok(LANES), hist_spec],
        out_shape=[jax.ShapeDtypeStruct((bsz, seq, SSM_INNER), BF16),
                   jax.ShapeDtypeStruct((bsz, seq, SSM_INNER), F32),
                   jax.ShapeDtypeStruct((bsz, seq, SSM_GN), BF16),
                   jax.ShapeDtypeStruct((bsz, seq, SSM_GN), BF16),
                   jax.ShapeDtypeStruct((bsz, seq, LANES), BF16),
                   jax.ShapeDtypeStruct((bsz, seq, LANES), BF16),
                   jax.ShapeDtypeStruct((bsz, SSM_CONV - 1, SSM_CONV_CH), F32)],
        scratch_shapes=[pltpu.VMEM((SUBLANES, SSM_CONV_CH), F32)],
        compiler_params=_params(),
        name="ssm_in_proj",
    )(x, g, w, wdt, hist, cw, cb, dtb, alog)


def _pad_rows(a, rows):
    if a.shape[0] == rows:
        return a
    return jnp.concatenate([a, jnp.zeros((rows - a.shape[0], a.shape[1]), a.dtype)], axis=0)


def _ssd_kernel(ts, cl, xs_ref, bm_ref, cm_ref, ta_ref, tdt_ref, h0_ref, dskip_ref,
                yo_ref, hout_ref, ht_ref, y_ref, e3_ref, aexp_ref, xdt_ref, st_ref, cb_ref, bmt_ref, w2_ref, rhs_ref):
    b = pl.program_id(0)
    i = pl.program_id(1)
    p = SSM_HEADDIM
    nc = ts // cl
    pairs = SSM_GROUP_W // LANES

    @pl.when((b == 0) & (i == 0))
    def _():
        kk = lax.broadcasted_iota(jnp.int32, (LANES, SSM_INNER), 0)
        cc = lax.broadcasted_iota(jnp.int32, (LANES, SSM_INNER), 1)
        hit = (kk % SSM_HEADS == cc // p) & (kk < 3 * SSM_HEADS)
        e3_ref[...] = jnp.where(hit, 1.0, 0.0).astype(BF16)

    @pl.when(i == 0)
    def _():
        ht_ref[...] = h0_ref[...].T

    lane = lax.broadcasted_iota(jnp.int32, (cl, LANES), 1)
    row = lax.broadcasted_iota(jnp.int32, (cl, LANES), 0)
    causal2 = (lane % p) <= row
    left = lane < p
    diag2 = (lane % p) == row
    groups = [(gi, slice(gi * SSM_GROUP_W, (gi + 1) * SSM_GROUP_W), slice(gi * SSM_STATE, (gi + 1) * SSM_STATE))
              for gi in range(SSM_GROUPS)]
    chunks = [(c, slice(c * cl, (c + 1) * cl)) for c in range(nc)]

    for gi, gcols, _ in groups:
        e3g = e3_ref[:, gcols]
        aexp_ref[:, gcols] = _dot(ta_ref[...], e3g)
        xdt_ref[:, gcols] = xs_ref[:, gcols] * _dot(tdt_ref[...], e3g)

    for c, rows in chunks:
        for gi, _, scol in groups:
            bmp = _pad_rows(bm_ref[rows, scol], p)
            cbm = _dot_nt(cm_ref[rows, scol], bmp)
            cb_ref[c * SSM_GROUPS + gi] = jnp.concatenate([cbm, cbm], axis=1)
            bmt_ref[c * SSM_GROUPS + gi] = bmp.astype(F32).T.astype(BF16)

    for c, rows in chunks:
        for gi, _, _ in groups:
            for jj in range(pairs):
                idx = (c * SSM_GROUPS + gi) * pairs + jj
                cols = slice(gi * SSM_GROUP_W + jj * LANES, gi * SSM_GROUP_W + (jj + 1) * LANES)
                a_pair = aexp_ref[rows, cols]
                a_src = jnp.sum(jnp.where(diag2, a_pair, 0.0), axis=0, keepdims=True)
                seg = a_pair - a_src
                dec = jnp.exp(jnp.where(causal2, seg, -jnp.inf))
                w2_ref[idx] = (dec * cb_ref[c * SSM_GROUPS + gi]).astype(BF16)
                x2 = xdt_ref[rows, cols]
                top = _pad_rows(jnp.where(left, x2, 0.0), p).astype(BF16)
                bot = _pad_rows(jnp.where(left, 0.0, x2), p).astype(BF16)
                rhs_ref[idx] = jnp.concatenate([top, bot], axis=0)

    for c, rows in chunks:
        for gi, _, _ in groups:
            for jj in range(pairs):
                idx = (c * SSM_GROUPS + gi) * pairs + jj
                cols = slice(gi * SSM_GROUP_W + jj * LANES, gi * SSM_GROUP_W + (jj + 1) * LANES)
                y_ref[rows, cols] = _dot(w2_ref[idx], rhs_ref[idx])

    for c, rows in chunks:
        for gi, gcols, _ in groups:
            a_g = aexp_ref[rows, gcols]
            xw = (xdt_ref[rows, gcols] * jnp.exp(a_g[cl - 1:cl, :] - a_g)).astype(BF16)
            st_ref[c, :, gcols] = _dot(bmt_ref[c * SSM_GROUPS + gi], _pad_rows(xw, p))

    for c, rows in chunks:
        for gi, gcols, scol in groups:
            hprev = ht_ref[:, gcols]
            a_g = aexp_ref[rows, gcols]
            y_off = _dot(cm_ref[rows, scol], hprev.astype(BF16)) * jnp.exp(a_g)
            y_ref[rows, gcols] = y_ref[rows, gcols] + y_off
            ht_ref[:, gcols] = hprev * jnp.exp(a_g[cl - 1:cl, :]) + st_ref[c, :, gcols]

    yo_ref[...] = (y_ref[...] + dskip_ref[...] * xs_ref[...]).astype(BF16)

    @pl.when(i == pl.num_programs(1) - 1)
    def _():
        hout_ref[...] = ht_ref[...].T


def _ssd(xs, bm, cm, ta, tdt, h0, dskip, ts, cl):
    bsz, seq, _ = xs.shape
    tok = lambda width: pl.BlockSpec((None, ts, width), lambda b, i: (b, i, 0))
    st_spec = pl.BlockSpec((None, SSM_INNER, SSM_STATE), lambda b, i: (b, 0, 0))
    nc = ts // cl
    n_cg = nc * SSM_GROUPS
    n_pairs = n_cg * (SSM_GROUP_W // LANES)
    return pl.pallas_call(
        functools.partial(_ssd_kernel, ts, cl),
        grid=(bsz, seq // ts),
        in_specs=[tok(SSM_INNER), tok(SSM_GN), tok(SSM_GN), tok(LANES), tok(LANES), st_spec,
                  _const_spec((1, SSM_INNER))],
        out_specs=[tok(SSM_INNER), st_spec],
        out_shape=[jax.ShapeDtypeStruct((bsz, seq, SSM_INNER), BF16),
                   jax.ShapeDtypeStruct((bsz, SSM_INNER, SSM_STATE), F32)],
        scratch_shapes=[pltpu.VMEM((SSM_STATE, SSM_INNER), F32), pltpu.VMEM((ts, SSM_INNER), F32),
                        pltpu.VMEM((LANES, SSM_INNER), BF16), pltpu.VMEM((ts, SSM_INNER), F32),
                        pltpu.VMEM((ts, SSM_INNER), F32), pltpu.VMEM((nc, SSM_STATE, SSM_INNER), F32),
                        pltpu.VMEM((n_cg, cl, LANES), F32), pltpu.VMEM((n_cg, SSM_STATE, SSM_HEADDIM), BF16),
                        pltpu.VMEM((n_pairs, cl, LANES), BF16), pltpu.VMEM((n_pairs, LANES, LANES), BF16)],
        compiler_params=_params(),
        name="ssd_scan",
    )(xs, bm, cm, ta, tdt, h0, dskip)


def _run(x, pos0, sconv_hist, k_cache, v_cache, conv_hist, ssm_state, wts, tiles):
    bsz, seq, _ = x.shape
    t_in, t_att, c_att, t_ffn, t_sin, t_ssd, c_ssd = tiles
    row = lambda v: v.reshape(1, -1)
    ng = wts["norm_g"]
    flat = (lambda a: a.reshape(1, bsz * seq, a.shape[-1])) if seq < t_ffn else (lambda a: a)

    q, k, v, gated, new_sconv = _attn_in(x, row(ng[0, 0]), wts["ab_w_in"], sconv_hist, wts["sconv_w"], pos0, t_in)
    masked = k_cache is None
    k_prev, v_prev = (k, v) if masked else (k_cache, v_cache)
    att = _attn(wts["sinks"], q, k, v, k_prev, v_prev, t_att, c_att, masked)
    x = _proj_ffn([flat(att), flat(gated)], None, wts["ab_w_out"], flat(x), row(ng[0, 1]), row(ng[0, 2]),
                  wts["wg"], wts["wu"], wts["wd"], 0, row(ng[0, 3]), t_ffn).reshape(x.shape)

    z, xs, bm, cm, ta, tdt, new_conv = _ssm_in(
        x, row(ng[1, 0]), wts["ssm_w_in"], wts["w_dt"], conv_hist, wts["ssm_conv_w"],
        wts["ssm_conv_b"], wts["dt_bias"], wts["a_log"], t_sin, c_ssd)
    y, new_state = _ssd(xs, bm, cm, ta, tdt, ssm_state.reshape(bsz, SSM_INNER, SSM_STATE),
                        wts["d_skip"], t_ssd, c_ssd)
    x = _proj_ffn([flat(y)], (flat(z), wts["ssm_norm_g"]), wts["ssm_w_out"], flat(x), row(ng[1, 1]),
                  row(ng[1, 2]), wts["wg"], wts["wu"], wts["wd"], 1, row(ng[1, 3]), t_ffn).reshape(x.shape)

    new_k = k[:, seq - min(WINDOW, seq):].reshape(bsz, -1, A_KV_HEADS, HEAD_DIM)
    new_v = v[:, seq - min(WINDOW, seq):].reshape(bsz, -1, A_KV_HEADS, HEAD_DIM)
    new_state = new_state.reshape(bsz, SSM_HEADS, SSM_HEADDIM, SSM_STATE)
    return x, new_k, new_v, new_sconv, new_conv, new_state


def kernel(x_prompt, x_sample, cache_attn_k, cache_attn_v, state_sconv, state_ssm_conv, state_ssm, norm_g,
           ab_w_in, ab_w_out, attn_sinks, sconv_w, ssm_w_in, ssm_conv_w, ssm_conv_b, ssm_dt_bias, ssm_a_log,
           ssm_d, ssm_norm_g, ssm_w_out, ffn_w_gate, ffn_w_up, ffn_w_down):
    w_in1 = ssm_w_in[0]
    wts = {
        "norm_g": norm_g,
        "ab_w_in": ab_w_in[0].astype(BF16),
        "ab_w_out": ab_w_out[0].astype(BF16),
        "sinks": attn_sinks[0],
        "sconv_w": sconv_w[0],
        "ssm_w_in": w_in1.astype(BF16),
        "w_dt": jnp.tile(w_in1[:, SSM_INNER + SSM_CONV_CH:], (1, DT_REP)).astype(BF16),
        "ssm_conv_w": ssm_conv_w[0],
        "ssm_conv_b": ssm_conv_b[0].reshape(1, -1),
        "dt_bias": jnp.tile(ssm_dt_bias[0], DT_REP).reshape(1, -1),
        "a_log": jnp.tile(ssm_a_log[0], DT_REP).reshape(1, -1),
        "d_skip": jnp.repeat(ssm_d[0], SSM_HEADDIM).reshape(1, -1),
        "ssm_norm_g": ssm_norm_g[0].reshape(1, -1),
        "ssm_w_out": ssm_w_out[0].astype(BF16),
        "wg": ffn_w_gate.astype(BF16),
        "wu": ffn_w_up.astype(BF16),
        "wd": ffn_w_down.astype(BF16),
    }
    bp = x_prompt.shape[0]
    bs, ls = x_sample.shape[0], x_sample.shape[1]

    yp, kp, vp, scp, ccp, ssp = _run(
        x_prompt, 0,
        jnp.zeros((bp, B_CONV - 1, B_WIDTH), F32), None, None,
        jnp.zeros((bp, SSM_CONV - 1, SSM_CONV_CH), F32),
        jnp.zeros((bp, SSM_HEADS, SSM_HEADDIM, SSM_STATE), F32),
        wts, (512, 512, CHUNK, 512, 512, 512, CHUNK))
    ys, ks, vs, scs, ccs, sss = _run(
        x_sample, PAST_LEN,
        state_sconv[0], cache_attn_k[0].reshape(bs, -1, KV_WIDTH), cache_attn_v[0].reshape(bs, -1, KV_WIDTH),
        state_ssm_conv[0], state_ssm[0],
        wts, (ls, ls, ls, bs * ls, ls, ls, ls))
    lead = lambda a: a[None]
    return (yp, ys, lead(kp), lead(vp), lead(scp), lead(ccp), lead(ssp),
            lead(ks), lead(vs), lead(scs), lead(ccs), lead(sss))
```

```python
import functools

import jax
import jax.numpy as jnp
from jax import lax
from jax.experimental import pallas as pl
from jax.experimental.pallas import tpu as pltpu

F32 = jnp.float32
BF16 = jnp.bfloat16

D_MODEL = 1024
CHUNK = 64
EPS = 1e-6
PAST_LEN = 4096

A_HEADS = 8
A_KV_HEADS = 2
A_GROUP = A_HEADS // A_KV_HEADS
HEAD_DIM = 64
A_WIDTH = A_HEADS * HEAD_DIM
KV_WIDTH = A_KV_HEADS * HEAD_DIM
WINDOW = 128
ROPE_DIM = HEAD_DIM // 4
ROPE_HALF = ROPE_DIM // 2
ROPE_THETA = 500000.0
ATTN_SCALE = HEAD_DIM ** -0.5

B_WIDTH = D_MODEL // 2
B_CONV = 3
AB_IN = A_WIDTH + 2 * KV_WIDTH + 3 * B_WIDTH

SSM_INNER = 2 * D_MODEL
SSM_HEADDIM = 64
SSM_HEADS = SSM_INNER // SSM_HEADDIM
SSM_GROUPS = 4
SSM_GROUP_W = SSM_INNER // SSM_GROUPS
SSM_STATE = 128
SSM_CONV = 4
SSM_GN = SSM_GROUPS * SSM_STATE
SSM_CONV_CH = SSM_INNER + 2 * SSM_GN

D_FF = -(-8 * D_MODEL // (3 * 256)) * 256

LANES = 128
SUBLANES = 8
MXU_DIM = 256
DT_REP = LANES // SSM_HEADS
VMEM_LIMIT = 56 * 1024 * 1024
COL_CHUNK = MXU_DIM
FF_CHUNKS = (6 * MXU_DIM, 5 * MXU_DIM)
assert sum(FF_CHUNKS) == D_FF


def _rms(x, g):
    return x * lax.rsqrt(jnp.mean(x * x, axis=-1, keepdims=True) + EPS) * g


def _dot(a, b):
    return jnp.dot(a, b, preferred_element_type=F32)


def _dot_nt(a, b):
    return lax.dot_general(a, b, (((1,), (1,)), ((), ())), preferred_element_type=F32)


def _silu(x):
    return x * jax.nn.sigmoid(x)


def _const_spec(shape):
    return pl.BlockSpec(shape, lambda *_: (0,) * len(shape), pipeline_mode=pl.Buffered(1))


def _layer_spec(shape, layer):
    return pl.BlockSpec((None,) + shape, lambda *_: (layer,) + (0,) * len(shape), pipeline_mode=pl.Buffered(1))


def _params():
    return pltpu.CompilerParams(dimension_semantics=("arbitrary", "arbitrary"),
                                vmem_limit_bytes=VMEM_LIMIT)


def _causal_conv(carry_ref, nh_ref, u, cw, cols):
    width = cw.shape[0]
    rows, ch = u.shape
    n = rows // SUBLANES
    ext3 = jnp.concatenate([carry_ref[:, cols], u], axis=0).reshape(n + 1, SUBLANES, ch)
    u3 = ext3[1:]
    sub = lax.broadcasted_iota(jnp.int32, (1, SUBLANES, ch), 1)

    def shift(above, here, k):
        return pltpu.roll(jnp.where(sub >= SUBLANES - k, above, here), k, 1)

    if width == 4:
        s1 = shift(jnp.concatenate([ext3[:1], ext3[:-1]], axis=0), ext3, 1)
        b = s1 * cw[0:1] + ext3 * cw[1:2]
        y3 = shift(b[:-1], b[1:], 2) + (s1[1:] * cw[2:3] + u3 * cw[3:4])
    else:
        y3 = None
        for j in range(width - 1):
            term = shift(ext3[:-1], u3, width - 1 - j) * cw[j:j + 1]
            y3 = term if y3 is None else y3 + term
        y3 = y3 + u3 * cw[width - 1:width]
    carry_ref[:, cols] = u[rows - SUBLANES:]
    nh_ref[:, cols] = u[rows - (width - 1):]
    return y3.reshape(rows, ch)


def _rope_freq(shape):
    lane = lax.broadcasted_iota(jnp.int32, shape, 1)
    d = lane % HEAD_DIM
    expo = -((d % ROPE_HALF).astype(F32)) / ROPE_HALF
    freq = jnp.power(jnp.full(shape, ROPE_THETA, F32), expo)
    return jnp.where(d < ROPE_DIM, freq, 0.0), d


def _attn_in_kernel(pos0, tm, x_ref, g_ref, w_ref, hist_ref, cw_ref,
                    q_ref, k_ref, v_ref, gated_ref, nh_ref,
                    cr_ref, sr_ref, carry_ref, p_ref):
    b = pl.program_id(0)
    i = pl.program_id(1)

    @pl.when((b == 0) & (i == 0))
    def _():
        freq, _ = _rope_freq((tm, LANES))
        ang = lax.broadcasted_iota(jnp.int32, (tm, LANES), 0).astype(F32) * freq
        cr_ref[...] = jnp.cos(ang)
        sr_ref[...] = jnp.sin(ang)

    @pl.when(i == 0)
    def _():
        carry_ref[SUBLANES - (B_CONV - 1):, :] = hist_ref[...]

    freq1, d1 = _rope_freq((1, LANES))
    base = (pos0 + i * tm).astype(F32) * freq1
    cb = jnp.cos(base)
    sb = jnp.sin(base)
    cr = cr_ref[...]
    sr = sr_ref[...]
    cos_t = cr * cb - sr * sb
    sin_t = sr * cb + cr * sb
    m_lo = jnp.where(d1 < ROPE_HALF, -1.0, 0.0)
    m_hi = jnp.where((d1 >= ROPE_HALF) & (d1 < ROPE_DIM), 1.0, 0.0)

    def rope(t):
        partner = pltpu.roll(t, LANES - ROPE_HALF, 1) * m_lo + pltpu.roll(t, ROPE_HALF, 1) * m_hi
        return t * cos_t + partner * sin_t

    h = _rms(x_ref[...], g_ref[...]).astype(BF16)
    o1 = A_WIDTH
    o2 = o1 + KV_WIDTH
    o3 = o2 + KV_WIDTH
    o4 = o3 + B_WIDTH
    o5 = o4 + B_WIDTH

    for n in range(AB_IN // MXU_DIM):
        cols = slice(n * MXU_DIM, (n + 1) * MXU_DIM)
        p_ref[:, cols] = _dot(h, w_ref[:, cols])

    u = p_ref[:, o4:o5] * p_ref[:, o5:]
    y = _causal_conv(carry_ref, nh_ref, u, cw_ref[...], slice(None))
    gated_ref[...] = (p_ref[:, o3:o4] * y).astype(BF16)
    k_ref[...] = rope(p_ref[:, o1:o2])
    v_ref[...] = p_ref[:, o2:o3]
    for s in range(A_WIDTH // LANES):
        q_ref[:, s * LANES:(s + 1) * LANES] = (rope(p_ref[:, s * LANES:(s + 1) * LANES]) * ATTN_SCALE).astype(BF16)


def _attn_in(x, g, w, hist, cw, pos0, tm):
    bsz, seq, _ = x.shape
    grid = (bsz, seq // tm)
    tok = lambda width: pl.BlockSpec((None, tm, width), lambda b, i: (b, i, 0))
    return pl.pallas_call(
        functools.partial(_attn_in_kernel, pos0, tm),
        grid=grid,
        in_specs=[tok(D_MODEL), _const_spec((1, D_MODEL)), _const_spec((D_MODEL, AB_IN)),
                  pl.BlockSpec((None, B_CONV - 1, B_WIDTH), lambda b, i: (b, 0, 0)),
                  _const_spec((B_CONV, B_WIDTH))],
        out_specs=[tok(A_WIDTH), tok(KV_WIDTH), tok(KV_WIDTH), tok(B_WIDTH),
                   pl.BlockSpec((None, B_CONV - 1, B_WIDTH), lambda b, i: (b, 0, 0))],
        out_shape=[jax.ShapeDtypeStruct((bsz, seq, A_WIDTH), BF16),
                   jax.ShapeDtypeStruct((bsz, seq, KV_WIDTH), F32),
                   jax.ShapeDtypeStruct((bsz, seq, KV_WIDTH), F32),
                   jax.ShapeDtypeStruct((bsz, seq, B_WIDTH), BF16),
                   jax.ShapeDtypeStruct((bsz, B_CONV - 1, B_WIDTH), F32)],
        scratch_shapes=[pltpu.VMEM((tm, LANES), F32), pltpu.VMEM((tm, LANES), F32),
                        pltpu.VMEM((SUBLANES, B_WIDTH), F32), pltpu.VMEM((tm, AB_IN), F32)],
        compiler_params=_params(),
        name="attn_in_proj",
    )(x, g, w, hist, cw)


def _attn_kernel(tq, cq, masked, sinks_ref, q_ref, kc_ref, vc_ref, kp_ref, vp_ref, att_ref,
                 kk_ref, vx_ref, s_ref, e_ref):
    i = pl.program_id(1)
    nk = WINDOW + cq
    nkp = MXU_DIM
    rows_kv = WINDOW + tq
    hd = HEAD_DIM

    kk_ref[0:WINDOW, :] = kp_ref[...].astype(BF16)
    kk_ref[WINDOW:rows_kv, :] = kc_ref[...].astype(BF16)
    kk_ref[rows_kv:, :] = jnp.zeros((kk_ref.shape[0] - rows_kv, KV_WIDTH), BF16)
    ones = jnp.ones((rows_kv, hd), F32)
    vall = jnp.concatenate([vp_ref[...], vc_ref[...]], axis=0)
    vx_ref[0:rows_kv, :] = jnp.concatenate([vall[:, :hd], ones, vall[:, hd:], ones], axis=1)
    vx_ref[rows_kv:, :] = jnp.zeros((vx_ref.shape[0] - rows_kv, 2 * LANES), F32)

    col = lax.broadcasted_iota(jnp.int32, (A_GROUP * cq, nkp), 1)
    krow = lax.broadcasted_iota(jnp.int32, (nkp, LANES), 0)
    klane = lax.broadcasted_iota(jnp.int32, (nkp, LANES), 1)
    sink_row = jnp.where(klane < hd, 0.0, 1.0)
    bodies = [(c, kh) for c in range(tq // cq) for kh in range(A_KV_HEADS)]
    fills = []
    for kh in range(A_KV_HEADS):
        sink = jnp.concatenate(
            [jnp.full((cq, nkp), sinks_ref[kh * A_GROUP + g], F32) for g in range(A_GROUP)], axis=0)
        fills.append(jnp.where(col == nk, sink, -jnp.inf))

    for n, (c, kh) in enumerate(bodies):
        r0 = c * cq
        heads = [kh * A_GROUP + g for g in range(A_GROUP)]
        kt = kk_ref[r0:r0 + nkp, kh * hd:(kh + 1) * hd]
        qs = jnp.concatenate([q_ref[r0:r0 + cq, hh * hd:(hh + 1) * hd] for hh in heads], axis=0)
        s = _dot_nt(qs, kt)
        valid = col < nk
        if masked and r0 < WINDOW:
            valid = valid & ((col >= WINDOW - r0) | (i > 0))
        s_ref[n] = jnp.where(valid, s, fills[kh])

    for n in range(len(bodies)):
        s = s_ref[n]
        m = jnp.max(s, axis=-1, keepdims=True)
        e_ref[n] = jnp.exp(s - m).astype(BF16)

    for n, (c, kh) in enumerate(bodies):
        r0 = c * cq
        vt = jnp.where(krow == nk, sink_row, vx_ref[r0:r0 + nkp, kh * LANES:(kh + 1) * LANES]).astype(BF16)
        ox = _dot(e_ref[n], vt)
        o = (ox / pltpu.roll(ox, hd, 1))[:, :hd]
        for g in range(A_GROUP):
            hh = kh * A_GROUP + g
            att_ref[r0:r0 + cq, hh * hd:(hh + 1) * hd] = o[g * cq:(g + 1) * cq].astype(BF16)


def _attn(sinks, q, k, v, k_prev, v_prev, tq, cq, masked):
    bsz, seq, _ = q.shape
    grid = (bsz, seq // tq)
    tok = lambda width: pl.BlockSpec((None, tq, width), lambda b, i: (b, i, 0))
    if masked:
        blocks_per_tile = tq // WINDOW
        prev = pl.BlockSpec((None, WINDOW, KV_WIDTH),
                            lambda b, i: (b, jnp.maximum(i * blocks_per_tile - 1, 0), 0))
    else:
        prev = pl.BlockSpec((None, WINDOW, KV_WIDTH), lambda b, i: (b, 0, 0))
    n_bodies = (tq // cq) * A_KV_HEADS
    kv_rows = (tq // cq - 1) * cq + MXU_DIM
    return pl.pallas_call(
        functools.partial(_attn_kernel, tq, cq, masked),
        grid=grid,
        in_specs=[pl.BlockSpec(memory_space=pltpu.SMEM),
                  tok(A_WIDTH), tok(KV_WIDTH), tok(KV_WIDTH), prev, prev],
        out_specs=tok(A_WIDTH),
        out_shape=jax.ShapeDtypeStruct((bsz, seq, A_WIDTH), BF16),
        scratch_shapes=[pltpu.VMEM((kv_rows, KV_WIDTH), BF16), pltpu.VMEM((kv_rows, 2 * LANES), F32),
                        pltpu.VMEM((n_bodies, A_GROUP * cq, MXU_DIM), F32),
                        pltpu.VMEM((n_bodies, A_GROUP * cq, MXU_DIM), BF16)],
        compiler_params=_params(),
        name="band_attention",
    )(sinks, q, k, v, k_prev, v_prev)


def _proj_ffn_kernel(n_in, gate, *refs):
    a_refs = refs[:n_in]
    refs = refs[n_in:]
    if gate:
        z_ref, ng_ref = refs[:2]
        refs = refs[2:]
    wo_ref, x_ref, g1_ref, g2_ref, wg_ref, wu_ref, wd_ref, g3_ref, o_ref = refs

    acc = None
    off = 0
    if gate:
        (y_ref,) = a_refs
        for gi in range(SSM_GROUPS):
            gcols = slice(gi * SSM_GROUP_W, (gi + 1) * SSM_GROUP_W)
            gg = y_ref[:, gcols].astype(F32) * z_ref[:, gcols].astype(F32)
            gg = gg * lax.rsqrt(jnp.mean(gg * gg, axis=-1, keepdims=True) + EPS)
            part = _dot((gg * ng_ref[:, gcols]).astype(BF16), wo_ref[gcols, :])
            acc = part if acc is None else acc + part
    else:
        for a_ref in a_refs:
            kdim = a_ref.shape[-1]
            part = _dot(a_ref[...], wo_ref[off:off + kdim, :])
            acc = part if acc is None else acc + part
            off += kdim

    x = x_ref[...] + _rms(acc, g1_ref[...])
    h = _rms(x, g2_ref[...]).astype(BF16)
    f = None
    off = 0
    for width in FF_CHUNKS:
        fc = slice(off, off + width)
        m = (_silu(_dot(h, wg_ref[:, fc])) * _dot(h, wu_ref[:, fc])).astype(BF16)
        part = _dot(m, wd_ref[fc, :])
        f = part if f is None else f + part
        off += width
    o_ref[...] = x + _rms(f, g3_ref[...])


def _proj_ffn(mix, gate_args, wo, x, g1, g2, wg, wu, wd, layer, g3, tm):
    bsz, seq, _ = x.shape
    tok = lambda width: pl.BlockSpec((None, tm, width), lambda b, i: (b, i, 0))
    gate = gate_args is not None
    extra, extra_specs = [], []
    if gate:
        z, ng = gate_args
        extra, extra_specs = [z, ng], [tok(SSM_INNER), _const_spec((1, SSM_INNER))]
    return pl.pallas_call(
        functools.partial(_proj_ffn_kernel, len(mix), gate),
        grid=(bsz, seq // tm),
        in_specs=[tok(a.shape[-1]) for a in mix] + extra_specs + [
            _const_spec(wo.shape), tok(D_MODEL), _const_spec((1, D_MODEL)), _const_spec((1, D_MODEL)),
            _layer_spec((D_MODEL, D_FF), layer), _layer_spec((D_MODEL, D_FF), layer),
            _layer_spec((D_FF, D_MODEL), layer), _const_spec((1, D_MODEL))],
        out_specs=tok(D_MODEL),
        out_shape=jax.ShapeDtypeStruct(x.shape, F32),
        compiler_params=_params(),
        name="out_proj_ffn",
    )(*mix, *extra, wo, x, g1, g2, wg, wu, wd, g3)


def _split3(v):
    t1 = v.astype(BF16)
    r = v - t1.astype(F32)
    t2 = r.astype(BF16)
    t3 = (r - t2.astype(F32)).astype(BF16)
    return t1, t2, t3


def _pack3(v):
    t1, t2, t3 = _split3(v)
    lane = lax.broadcasted_iota(jnp.int32, v.shape, 1)
    return jnp.where(lane < SSM_HEADS, t1,
                     jnp.where(lane < 2 * SSM_HEADS, t2,
                               jnp.where(lane < 3 * SSM_HEADS, t3, jnp.zeros_like(t3))))


def _ssm_in_kernel(tm, cl, x_ref, g_ref, w_ref, wdt_ref, hist_ref, cw_ref, cb_ref, dtb_ref, alog_ref,
                   z_ref, xs_ref, bm_ref, cm_ref, ta_ref, tdt_ref, nh_ref, carry_ref):
    i = pl.program_id(1)

    @pl.when(i == 0)
    def _():
        carry_ref[SUBLANES - (SSM_CONV - 1):, :] = hist_ref[...]

    h = _rms(x_ref[...], g_ref[...]).astype(BF16)

    dt = jax.nn.softplus(_dot(h, wdt_ref[...]) + dtb_ref[...])
    tdt_ref[...] = _pack3(dt)
    d1, d2, d3 = _split3(dt * (-jnp.exp(alog_ref[...])))
    tcol = lax.broadcasted_iota(jnp.int32, (cl, 3 * cl), 1) % cl
    trow = lax.broadcasted_iota(jnp.int32, (cl, 3 * cl), 0)
    tril = jnp.where(tcol <= trow, 1.0, 0.0).astype(BF16)
    acum = []
    for c in range(tm // cl):
        rows = slice(c * cl, (c + 1) * cl)
        acum.append(_dot(tril, jnp.concatenate([d1[rows], d2[rows], d3[rows]], axis=0)))
    ta_ref[...] = _pack3(jnp.concatenate(acum, axis=0))

    for n in range(SSM_INNER // COL_CHUNK):
        cols = slice(n * COL_CHUNK, (n + 1) * COL_CHUNK)
        z_ref[:, cols] = _silu(_dot(h, w_ref[:, cols])).astype(BF16)

    for n in range(SSM_CONV_CH // COL_CHUNK):
        cols = slice(n * COL_CHUNK, (n + 1) * COL_CHUNK)
        raw = _dot(h, w_ref[:, SSM_INNER + cols.start:SSM_INNER + cols.stop])
        y = _silu(_causal_conv(carry_ref, nh_ref, raw, cw_ref[:, cols], cols) + cb_ref[:, cols])
        if cols.stop <= SSM_INNER:
            xs_ref[:, cols] = y
        elif cols.stop <= SSM_INNER + SSM_GN:
            bm_ref[:, cols.start - SSM_INNER:cols.stop - SSM_INNER] = y.astype(BF16)
        else:
            lo = SSM_INNER + SSM_GN
            cm_ref[:, cols.start - lo:cols.stop - lo] = y.astype(BF16)


def _ssm_in(x, g, w, wdt, hist, cw, cb, dtb, alog, tm, cl):
    bsz, seq, _ = x.shape
    tok = lambda width: pl.BlockSpec((None, tm, width), lambda b, i: (b, i, 0))
    hist_spec = pl.BlockSpec((None, SSM_CONV - 1, SSM_CONV_CH), lambda b, i: (b, 0, 0))
    return pl.pallas_call(
        functools.partial(_ssm_in_kernel, tm, cl),
        grid=(bsz, seq // tm),
        in_specs=[tok(D_MODEL), _const_spec((1, D_MODEL)), _const_spec(w.shape), _const_spec((D_MODEL, LANES)),
                  hist_spec,
                  _const_spec((SSM_CONV, SSM_CONV_CH)), _const_spec((1, SSM_CONV_CH)), _const_spec((1, LANES)),
                  _const_spec((1, LANES))],
        out_specs=[tok(SSM_INNER), tok(SSM_INNER), tok(SSM_GN), tok(SSM_GN), tok(LANES), tok(LANES), hist_spec],
        out_shape=[jax.ShapeDtypeStruct((bsz, seq, SSM_INNER), BF16),
                   jax.ShapeDtypeStruct((bsz, seq, SSM_INNER), F32),
                   jax.ShapeDtypeStruct((bsz, seq, SSM_GN), BF16),
                   jax.ShapeDtypeStruct((bsz, seq, SSM_GN), BF16),
                   jax.ShapeDtypeStruct((bsz, seq, LANES), BF16),
                   jax.ShapeDtypeStruct((bsz, seq, LANES), BF16),
                   jax.ShapeDtypeStruct((bsz, SSM_CONV - 1, SSM_CONV_CH), F32)],
        scratch_shapes=[pltpu.VMEM((SUBLANES, SSM_CONV_CH), F32)],
        compiler_params=_params(),
        name="ssm_in_proj",
    )(x, g, w, wdt, hist, cw, cb, dtb, alog)


def _pad_rows(a, rows):
    if a.shape[0] == rows:
        return a
    return jnp.concatenate([a, jnp.zeros((rows - a.shape[0], a.shape[1]), a.dtype)], axis=0)


def _ssd_kernel(ts, cl, xs_ref, bm_ref, cm_ref, ta_ref, tdt_ref, h0_ref, dskip_ref,
                yo_ref, hout_ref, ht_ref, y_ref, e3_ref, aexp_ref, xdt_ref, st_ref, cb_ref, bmt_ref, w2_ref, rhs_ref):
    b = pl.program_id(0)
    i = pl.program_id(1)
    p = SSM_HEADDIM
    nc = ts // cl
    pairs = SSM_GROUP_W // LANES

    @pl.when((b == 0) & (i == 0))
    def _():
        kk = lax.broadcasted_iota(jnp.int32, (LANES, SSM_INNER), 0)
        cc = lax.broadcasted_iota(jnp.int32, (LANES, SSM_INNER), 1)
        hit = (kk % SSM_HEADS == cc // p) & (kk < 3 * SSM_HEADS)
        e3_ref[...] = jnp.where(hit, 1.0, 0.0).astype(BF16)

    @pl.when(i == 0)
    def _():
        ht_ref[...] = h0_ref[...].T

    lane = lax.broadcasted_iota(jnp.int32, (cl, LANES), 1)
    row = lax.broadcasted_iota(jnp.int32, (cl, LANES), 0)
    causal2 = (lane % p) <= row
    left = lane.astype(F32).astype(BF16) < p
    diag2 = (lane % p) == row
    groups = [(gi, slice(gi * SSM_GROUP_W, (gi + 1) * SSM_GROUP_W), slice(gi * SSM_STATE, (gi + 1) * SSM_STATE))
              for gi in range(SSM_GROUPS)]
    chunks = [(c, slice(c * cl, (c + 1) * cl)) for c in range(nc)]

    for gi, gcols, _ in groups:
        e3g = e3_ref[:, gcols]
        aexp_ref[:, gcols] = _dot(ta_ref[...], e3g)
        xdt_ref[:, gcols] = (xs_ref[:, gcols] * _dot(tdt_ref[...], e3g)).astype(BF16)

    for c, rows in chunks:
        for gi, _, scol in groups:
            bmp = _pad_rows(bm_ref[rows, scol], p)
            cbm = _dot_nt(cm_ref[rows, scol], bmp)
            cb_ref[c * SSM_GROUPS + gi] = jnp.concatenate([cbm, cbm], axis=1)
            bmt_ref[c * SSM_GROUPS + gi] = bmp.astype(F32).T.astype(BF16)

    for c, rows in chunks:
        for gi, _, _ in groups:
            for jj in range(pairs):
                idx = (c * SSM_GROUPS + gi) * pairs + jj
                cols = slice(gi * SSM_GROUP_W + jj * LANES, gi * SSM_GROUP_W + (jj + 1) * LANES)
                a_pair = aexp_ref[rows, cols]
                a_src = jnp.sum(jnp.where(diag2, a_pair, 0.0), axis=0, keepdims=True)
                seg = a_pair - a_src
                dec = jnp.exp(jnp.where(causal2, seg, -jnp.inf))
                w2_ref[idx] = (dec * cb_ref[c * SSM_GROUPS + gi]).astype(BF16)
                x2 = xdt_ref[rows, cols]
                top = _pad_rows(jnp.where(left, x2, jnp.zeros_like(x2)), p)
                bot = _pad_rows(jnp.where(left, jnp.zeros_like(x2), x2), p)
                rhs_ref[idx] = jnp.concatenate([top, bot], axis=0)

    for c, rows in chunks:
        for gi, _, _ in groups:
            for jj in range(pairs):
                idx = (c * SSM_GROUPS + gi) * pairs + jj
                cols = slice(gi * SSM_GROUP_W + jj * LANES, gi * SSM_GROUP_W + (jj + 1) * LANES)
                y_ref[rows, cols] = _dot(w2_ref[idx], rhs_ref[idx])

    for c, rows in chunks:
        for gi, gcols, _ in groups:
            a_g = aexp_ref[rows, gcols]
            xw = xdt_ref[rows, gcols] * jnp.exp(a_g[cl - 1:cl, :] - a_g).astype(BF16)
            st_ref[c, :, gcols] = _dot(bmt_ref[c * SSM_GROUPS + gi], _pad_rows(xw, p))

    for c, rows in chunks:
        for gi, gcols, scol in groups:
            hprev = ht_ref[:, gcols]
            a_g = aexp_ref[rows, gcols]
            y_off = _dot(cm_ref[rows, scol], hprev.astype(BF16)) * jnp.exp(a_g)
            y_ref[rows, gcols] = y_ref[rows, gcols] + y_off
            ht_ref[:, gcols] = hprev * jnp.exp(a_g[cl - 1:cl, :]) + st_ref[c, :, gcols]

    yo_ref[...] = (y_ref[...] + dskip_ref[...] * xs_ref[...]).astype(BF16)

    @pl.when(i == pl.num_programs(1) - 1)
    def _():
        hout_ref[...] = ht_ref[...].T


def _ssd(xs, bm, cm, ta, tdt, h0, dskip, ts, cl):
    bsz, seq, _ = xs.shape
    tok = lambda width: pl.BlockSpec((None, ts, width), lambda b, i: (b, i, 0))
    st_spec = pl.BlockSpec((None, SSM_INNER, SSM_STATE), lambda b, i: (b, 0, 0))
    nc = ts // cl
    n_cg = nc * SSM_GROUPS
    n_pairs = n_cg * (SSM_GROUP_W // LANES)
    return pl.pallas_call(
        functools.partial(_ssd_kernel, ts, cl),
        grid=(bsz, seq // ts),
        in_specs=[tok(SSM_INNER), tok(SSM_GN), tok(SSM_GN), tok(LANES), tok(LANES), st_spec,
                  _const_spec((1, SSM_INNER))],
        out_specs=[tok(SSM_INNER), st_spec],
        out_shape=[jax.ShapeDtypeStruct((bsz, seq, SSM_INNER), BF16),
                   jax.ShapeDtypeStruct((bsz, SSM_INNER, SSM_STATE), F32)],
        scratch_shapes=[pltpu.VMEM((SSM_STATE, SSM_INNER), F32), pltpu.VMEM((ts, SSM_INNER), F32),
                        pltpu.VMEM((LANES, SSM_INNER), BF16), pltpu.VMEM((ts, SSM_INNER), F32),
                        pltpu.VMEM((ts, SSM_INNER), BF16), pltpu.VMEM((nc, SSM_STATE, SSM_INNER), F32),
                        pltpu.VMEM((n_cg, cl, LANES), F32), pltpu.VMEM((n_cg, SSM_STATE, SSM_HEADDIM), BF16),
                        pltpu.VMEM((n_pairs, cl, LANES), BF16), pltpu.VMEM((n_pairs, LANES, LANES), BF16)],
        compiler_params=_params(),
        name="ssd_scan",
    )(xs, bm, cm, ta, tdt, h0, dskip)


def _run(x, pos0, sconv_hist, k_cache, v_cache, conv_hist, ssm_state, wts, tiles):
    bsz, seq, _ = x.shape
    t_in, t_att, c_att, t_ffn, t_sin, t_ssd, c_ssd = tiles
    row = lambda v: v.reshape(1, -1)
    ng = wts["norm_g"]
    flat = (lambda a: a.reshape(1, bsz * seq, a.shape[-1])) if seq < t_ffn else (lambda a: a)

    q, k, v, gated, new_sconv = _attn_in(x, row(ng[0, 0]), wts["ab_w_in"], sconv_hist, wts["sconv_w"], pos0, t_in)
    masked = k_cache is None
    k_prev, v_prev = (k, v) if masked else (k_cache, v_cache)
    att = _attn(wts["sinks"], q, k, v, k_prev, v_prev, t_att, c_att, masked)
    x = _proj_ffn([flat(att), flat(gated)], None, wts["ab_w_out"], flat(x), row(ng[0, 1]), row(ng[0, 2]),
                  wts["wg"], wts["wu"], wts["wd"], 0, row(ng[0, 3]), t_ffn).reshape(x.shape)

    z, xs, bm, cm, ta, tdt, new_conv = _ssm_in(
        x, row(ng[1, 0]), wts["ssm_w_in"], wts["w_dt"], conv_hist, wts["ssm_conv_w"],
        wts["ssm_conv_b"], wts["dt_bias"], wts["a_log"], t_sin, c_ssd)
    y, new_state = _ssd(xs, bm, cm, ta, tdt, ssm_state.reshape(bsz, SSM_INNER, SSM_STATE),
                        wts["d_skip"], t_ssd, c_ssd)
    x = _proj_ffn([flat(y)], (flat(z), wts["ssm_norm_g"]), wts["ssm_w_out"], flat(x), row(ng[1, 1]),
                  row(ng[1, 2]), wts["wg"], wts["wu"], wts["wd"], 1, row(ng[1, 3]), t_ffn).reshape(x.shape)

    new_k = k[:, seq - min(WINDOW, seq):].reshape(bsz, -1, A_KV_HEADS, HEAD_DIM)
    new_v = v[:, seq - min(WINDOW, seq):].reshape(bsz, -1, A_KV_HEADS, HEAD_DIM)
    new_state = new_state.reshape(bsz, SSM_HEADS, SSM_HEADDIM, SSM_STATE)
    return x, new_k, new_v, new_sconv, new_conv, new_state


def kernel(x_prompt, x_sample, cache_attn_k, cache_attn_v, state_sconv, state_ssm_conv, state_ssm, norm_g,
           ab_w_in, ab_w_out, attn_sinks, sconv_w, ssm_w_in, ssm_conv_w, ssm_conv_b, ssm_dt_bias, ssm_a_log,
           ssm_d, ssm_norm_g, ssm_w_out, ffn_w_gate, ffn_w_up, ffn_w_down):
    w_in1 = ssm_w_in[0]
    wts = {
        "norm_g": norm_g,
        "ab_w_in": ab_w_in[0].astype(BF16),
        "ab_w_out": ab_w_out[0].astype(BF16),
        "sinks": attn_sinks[0],
        "sconv_w": sconv_w[0],
        "ssm_w_in": w_in1.astype(BF16),
        "w_dt": jnp.tile(w_in1[:, SSM_INNER + SSM_CONV_CH:], (1, DT_REP)).astype(BF16),
        "ssm_conv_w": ssm_conv_w[0],
        "ssm_conv_b": ssm_conv_b[0].reshape(1, -1),
        "dt_bias": jnp.tile(ssm_dt_bias[0], DT_REP).reshape(1, -1),
        "a_log": jnp.tile(ssm_a_log[0], DT_REP).reshape(1, -1),
        "d_skip": jnp.repeat(ssm_d[0], SSM_HEADDIM).reshape(1, -1),
        "ssm_norm_g": ssm_norm_g[0].reshape(1, -1),
        "ssm_w_out": ssm_w_out[0].astype(BF16),
        "wg": ffn_w_gate.astype(BF16),
        "wu": ffn_w_up.astype(BF16),
        "wd": ffn_w_down.astype(BF16),
    }
    bp = x_prompt.shape[0]
    bs, ls = x_sample.shape[0], x_sample.shape[1]

    yp, kp, vp, scp, ccp, ssp = _run(
        x_prompt, 0,
        jnp.zeros((bp, B_CONV - 1, B_WIDTH), F32), None, None,
        jnp.zeros((bp, SSM_CONV - 1, SSM_CONV_CH), F32),
        jnp.zeros((bp, SSM_HEADS, SSM_HEADDIM, SSM_STATE), F32),
        wts, (512, 512, CHUNK, 512, 512, 512, CHUNK))
    ys, ks, vs, scs, ccs, sss = _run(
        x_sample, PAST_LEN,
        state_sconv[0], cache_attn_k[0].reshape(bs, -1, KV_WIDTH), cache_attn_v[0].reshape(bs, -1, KV_WIDTH),
        state_ssm_conv[0], state_ssm[0],
        wts, (ls, ls, ls, bs * ls, ls, ls, ls))
    lead = lambda a: a[None]
    return (yp, ys, lead(kp), lead(vp), lead(scp), lead(ccp), lead(ssp),
            lead(ks), lead(vs), lead(scs), lead(ccs), lead(sss))
```

```python
import functools

import jax
import jax.numpy as jnp
from jax import lax
from jax.experimental import pallas as pl
from jax.experimental.pallas import tpu as pltpu

F32 = jnp.float32
BF16 = jnp.bfloat16

D_MODEL = 1024
CHUNK = 64
EPS = 1e-6
PAST_LEN = 4096

A_HEADS = 8
A_KV_HEADS = 2
A_GROUP = A_HEADS // A_KV_HEADS
HEAD_DIM = 64
A_WIDTH = A_HEADS * HEAD_DIM
KV_WIDTH = A_KV_HEADS * HEAD_DIM
WINDOW = 128
ROPE_DIM = HEAD_DIM // 4
ROPE_HALF = ROPE_DIM // 2
ROPE_THETA = 500000.0
ATTN_SCALE = HEAD_DIM ** -0.5

B_WIDTH = D_MODEL // 2
B_CONV = 3
AB_IN = A_WIDTH + 2 * KV_WIDTH + 3 * B_WIDTH

SSM_INNER = 2 * D_MODEL
SSM_HEADDIM = 64
SSM_HEADS = SSM_INNER // SSM_HEADDIM
SSM_GROUPS = 4
SSM_GROUP_W = SSM_INNER // SSM_GROUPS
SSM_STATE = 128
SSM_CONV = 4
SSM_GN = SSM_GROUPS * SSM_STATE
SSM_CONV_CH = SSM_INNER + 2 * SSM_GN

D_FF = -(-8 * D_MODEL // (3 * 256)) * 256

LANES = 128
SUBLANES = 8
MXU_DIM = 256
DT_REP = LANES // SSM_HEADS
VMEM_LIMIT = 56 * 1024 * 1024
COL_CHUNK = MXU_DIM
FF_CHUNKS = (6 * MXU_DIM, 5 * MXU_DIM)
assert sum(FF_CHUNKS) == D_FF


def _rms(x, g):
    return x * lax.rsqrt(jnp.mean(x * x, axis=-1, keepdims=True) + EPS) * g


def _dot(a, b):
    return jnp.dot(a, b, preferred_element_type=F32)


def _dot_nt(a, b):
    return lax.dot_general(a, b, (((1,), (1,)), ((), ())), preferred_element_type=F32)


def _silu(x):
    return x * jax.nn.sigmoid(x)


def _const_spec(shape):
    return pl.BlockSpec(shape, lambda *_: (0,) * len(shape), pipeline_mode=pl.Buffered(1))


def _layer_spec(shape, layer):
    return pl.BlockSpec((None,) + shape, lambda *_: (layer,) + (0,) * len(shape), pipeline_mode=pl.Buffered(1))


def _params():
    return pltpu.CompilerParams(dimension_semantics=("arbitrary", "arbitrary"),
                                vmem_limit_bytes=VMEM_LIMIT)


def _causal_conv(carry_ref, nh_ref, u, cw, cols):
    width = cw.shape[0]
    rows, ch = u.shape
    n = rows // SUBLANES
    ext3 = jnp.concatenate([carry_ref[:, cols], u], axis=0).reshape(n + 1, SUBLANES, ch)
    u3 = ext3[1:]
    sub = lax.broadcasted_iota(jnp.int32, (1, SUBLANES, ch), 1)

    def shift(above, here, k):
        return pltpu.roll(jnp.where(sub >= SUBLANES - k, above, here), k, 1)

    if width == 4:
        s1 = shift(jnp.concatenate([ext3[:1], ext3[:-1]], axis=0), ext3, 1)
        b = s1 * cw[0:1] + ext3 * cw[1:2]
        y3 = shift(b[:-1], b[1:], 2) + (s1[1:] * cw[2:3] + u3 * cw[3:4])
    else:
        y3 = None
        for j in range(width - 1):
            term = shift(ext3[:-1], u3, width - 1 - j) * cw[j:j + 1]
            y3 = term if y3 is None else y3 + term
        y3 = y3 + u3 * cw[width - 1:width]
    carry_ref[:, cols] = u[rows - SUBLANES:]
    nh_ref[:, cols] = u[rows - (width - 1):]
    return y3.reshape(rows, ch)


def _rope_freq(shape):
    lane = lax.broadcasted_iota(jnp.int32, shape, 1)
    d = lane % HEAD_DIM
    expo = -((d % ROPE_HALF).astype(F32)) / ROPE_HALF
    freq = jnp.power(jnp.full(shape, ROPE_THETA, F32), expo)
    return jnp.where(d < ROPE_DIM, freq, 0.0), d


def _attn_in_kernel(pos0, tm, x_ref, g_ref, w_ref, hist_ref, cw_ref,
                    q_ref, k_ref, v_ref, gated_ref, nh_ref,
                    cr_ref, sr_ref, carry_ref, p_ref):
    b = pl.program_id(0)
    i = pl.program_id(1)

    @pl.when((b == 0) & (i == 0))
    def _():
        freq, _ = _rope_freq((tm, LANES))
        ang = lax.broadcasted_iota(jnp.int32, (tm, LANES), 0).astype(F32) * freq
        cr_ref[...] = jnp.cos(ang)
        sr_ref[...] = jnp.sin(ang)

    @pl.when(i == 0)
    def _():
        carry_ref[SUBLANES - (B_CONV - 1):, :] = hist_ref[...]

    freq1, d1 = _rope_freq((1, LANES))
    base = (pos0 + i * tm).astype(F32) * freq1
    cb = jnp.cos(base)
    sb = jnp.sin(base)
    cr = cr_ref[...]
    sr = sr_ref[...]
    cos_t = cr * cb - sr * sb
    sin_t = sr * cb + cr * sb
    m_lo = jnp.where(d1 < ROPE_HALF, -1.0, 0.0)
    m_hi = jnp.where((d1 >= ROPE_HALF) & (d1 < ROPE_DIM), 1.0, 0.0)

    def rope(t):
        partner = pltpu.roll(t, LANES - ROPE_HALF, 1) * m_lo + pltpu.roll(t, ROPE_HALF, 1) * m_hi
        return t * cos_t + partner * sin_t

    h = _rms(x_ref[...], g_ref[...]).astype(BF16)
    o1 = A_WIDTH
    o2 = o1 + KV_WIDTH
    o3 = o2 + KV_WIDTH
    o4 = o3 + B_WIDTH
    o5 = o4 + B_WIDTH

    for n in range(AB_IN // MXU_DIM):
        cols = slice(n * MXU_DIM, (n + 1) * MXU_DIM)
        p_ref[:, cols] = _dot(h, w_ref[:, cols])

    u = p_ref[:, o4:o5] * p_ref[:, o5:]
    y = _causal_conv(carry_ref, nh_ref, u, cw_ref[...], slice(None))
    gated_ref[...] = (p_ref[:, o3:o4] * y).astype(BF16)
    k_ref[...] = rope(p_ref[:, o1:o2])
    v_ref[...] = p_ref[:, o2:o3]
    for s in range(A_WIDTH // LANES):
        q_ref[:, s * LANES:(s + 1) * LANES] = (rope(p_ref[:, s * LANES:(s + 1) * LANES]) * ATTN_SCALE).astype(BF16)


def _attn_in(x, g, w, hist, cw, pos0, tm):
    bsz, seq, _ = x.shape
    grid = (bsz, seq // tm)
    tok = lambda width: pl.BlockSpec((None, tm, width), lambda b, i: (b, i, 0))
    return pl.pallas_call(
        functools.partial(_attn_in_kernel, pos0, tm),
        grid=grid,
        in_specs=[tok(D_MODEL), _const_spec((1, D_MODEL)), _const_spec((D_MODEL, AB_IN)),
                  pl.BlockSpec((None, B_CONV - 1, B_WIDTH), lambda b, i: (b, 0, 0)),
                  _const_spec((B_CONV, B_WIDTH))],
        out_specs=[tok(A_WIDTH), tok(KV_WIDTH), tok(KV_WIDTH), tok(B_WIDTH),
                   pl.BlockSpec((None, B_CONV - 1, B_WIDTH), lambda b, i: (b, 0, 0))],
        out_shape=[jax.ShapeDtypeStruct((bsz, seq, A_WIDTH), BF16),
                   jax.ShapeDtypeStruct((bsz, seq, KV_WIDTH), F32),
                   jax.ShapeDtypeStruct((bsz, seq, KV_WIDTH), F32),
                   jax.ShapeDtypeStruct((bsz, seq, B_WIDTH), BF16),
                   jax.ShapeDtypeStruct((bsz, B_CONV - 1, B_WIDTH), F32)],
        scratch_shapes=[pltpu.VMEM((tm, LANES), F32), pltpu.VMEM((tm, LANES), F32),
                        pltpu.VMEM((SUBLANES, B_WIDTH), F32), pltpu.VMEM((tm, AB_IN), F32)],
        compiler_params=_params(),
        name="attn_in_proj",
    )(x, g, w, hist, cw)


def _attn_kernel(tq, cq, masked, sinks_ref, q_ref, kc_ref, vc_ref, kp_ref, vp_ref, att_ref,
                 kk_ref, vx_ref, s_ref, e_ref):
    i = pl.program_id(1)
    nk = WINDOW + cq
    nkp = MXU_DIM
    rows_kv = WINDOW + tq
    hd = HEAD_DIM

    kk_ref[0:WINDOW, :] = kp_ref[...].astype(BF16)
    kk_ref[WINDOW:rows_kv, :] = kc_ref[...].astype(BF16)
    kk_ref[rows_kv:, :] = jnp.zeros((kk_ref.shape[0] - rows_kv, KV_WIDTH), BF16)
    ones = jnp.ones((rows_kv, hd), F32)
    vall = jnp.concatenate([vp_ref[...], vc_ref[...]], axis=0)
    vx_ref[0:rows_kv, :] = jnp.concatenate([vall[:, :hd], ones, vall[:, hd:], ones], axis=1)
    vx_ref[rows_kv:, :] = jnp.zeros((vx_ref.shape[0] - rows_kv, 2 * LANES), F32)

    col = lax.broadcasted_iota(jnp.int32, (A_GROUP * cq, nkp), 1)
    krow = lax.broadcasted_iota(jnp.int32, (nkp, LANES), 0)
    klane = lax.broadcasted_iota(jnp.int32, (nkp, LANES), 1)
    sink_row = jnp.where(klane < hd, 0.0, 1.0)
    bodies = [(c, kh) for c in range(tq // cq) for kh in range(A_KV_HEADS)]
    fills = []
    for kh in range(A_KV_HEADS):
        sink = jnp.concatenate(
            [jnp.full((cq, nkp), sinks_ref[kh * A_GROUP + g], F32) for g in range(A_GROUP)], axis=0)
        fills.append(jnp.where(col == nk, sink, -jnp.inf))

    for n, (c, kh) in enumerate(bodies):
        r0 = c * cq
        heads = [kh * A_GROUP + g for g in range(A_GROUP)]
        kt = kk_ref[r0:r0 + nkp, kh * hd:(kh + 1) * hd]
        qs = jnp.concatenate([q_ref[r0:r0 + cq, hh * hd:(hh + 1) * hd] for hh in heads], axis=0)
        s = _dot_nt(qs, kt)
        valid = col < nk
        if masked and r0 < WINDOW:
            valid = valid & ((col >= WINDOW - r0) | (i > 0))
        s_ref[n] = jnp.where(valid, s, fills[kh])

    for n in range(len(bodies)):
        s = s_ref[n]
        m = jnp.max(s, axis=-1, keepdims=True)
        e_ref[n] = jnp.exp(s - m).astype(BF16)

    for n, (c, kh) in enumerate(bodies):
        r0 = c * cq
        vt = jnp.where(krow == nk, sink_row, vx_ref[r0:r0 + nkp, kh * LANES:(kh + 1) * LANES]).astype(BF16)
        ox = _dot(e_ref[n], vt)
        o = (ox / pltpu.roll(ox, hd, 1))[:, :hd]
        for g in range(A_GROUP):
            hh = kh * A_GROUP + g
            att_ref[r0:r0 + cq, hh * hd:(hh + 1) * hd] = o[g * cq:(g + 1) * cq].astype(BF16)


def _attn(sinks, q, k, v, k_prev, v_prev, tq, cq, masked):
    bsz, seq, _ = q.shape
    grid = (bsz, seq // tq)
    tok = lambda width: pl.BlockSpec((None, tq, width), lambda b, i: (b, i, 0))
    if masked:
        blocks_per_tile = tq // WINDOW
        prev = pl.BlockSpec((None, WINDOW, KV_WIDTH),
                            lambda b, i: (b, jnp.maximum(i * blocks_per_tile - 1, 0), 0))
    else:
        prev = pl.BlockSpec((None, WINDOW, KV_WIDTH), lambda b, i: (b, 0, 0))
    n_bodies = (tq // cq) * A_KV_HEADS
    kv_rows = (tq // cq - 1) * cq + MXU_DIM
    return pl.pallas_call(
        functools.partial(_attn_kernel, tq, cq, masked),
        grid=grid,
        in_specs=[pl.BlockSpec(memory_space=pltpu.SMEM),
                  tok(A_WIDTH), tok(KV_WIDTH), tok(KV_WIDTH), prev, prev],
        out_specs=tok(A_WIDTH),
        out_shape=jax.ShapeDtypeStruct((bsz, seq, A_WIDTH), BF16),
        scratch_shapes=[pltpu.VMEM((kv_rows, KV_WIDTH), BF16), pltpu.VMEM((kv_rows, 2 * LANES), F32),
                        pltpu.VMEM((n_bodies, A_GROUP * cq, MXU_DIM), F32),
                        pltpu.VMEM((n_bodies, A_GROUP * cq, MXU_DIM), BF16)],
        compiler_params=_params(),
        name="band_attention",
    )(sinks, q, k, v, k_prev, v_prev)


def _proj_ffn_kernel(n_in, gate, n_tiles, *refs):
    long_in, short_in = refs[:n_in], refs[n_in:2 * n_in]
    refs = refs[2 * n_in:]
    ng_ref = None
    if gate:
        ng_ref, refs = refs[0], refs[1:]
    wo_ref, xl_ref, xs_ref, g1_ref, g2_ref, wg_ref, wu_ref, wd_ref, g3_ref, ol_ref, os_ref = refs
    t = pl.program_id(0)

    def tile(ins, x_ref, o_ref):
        acc = None
        if gate:
            y_ref, z_ref = ins
            for gi in range(SSM_GROUPS):
                gcols = slice(gi * SSM_GROUP_W, (gi + 1) * SSM_GROUP_W)
                gg = y_ref[:, gcols].astype(F32) * z_ref[:, gcols].astype(F32)
                gg = gg * lax.rsqrt(jnp.mean(gg * gg, axis=-1, keepdims=True) + EPS)
                part = _dot((gg * ng_ref[:, gcols]).astype(BF16), wo_ref[gcols, :])
                acc = part if acc is None else acc + part
        else:
            off = 0
            for a_ref in ins:
                kdim = a_ref.shape[-1]
                part = _dot(a_ref[...], wo_ref[off:off + kdim, :])
                acc = part if acc is None else acc + part
                off += kdim

        x = x_ref[...] + _rms(acc, g1_ref[...])
        h = _rms(x, g2_ref[...]).astype(BF16)
        f = None
        off = 0
        for width in FF_CHUNKS:
            fc = slice(off, off + width)
            m = (_silu(_dot(h, wg_ref[:, fc])) * _dot(h, wu_ref[:, fc])).astype(BF16)
            part = _dot(m, wd_ref[fc, :])
            f = part if f is None else f + part
            off += width
        o_ref[...] = x + _rms(f, g3_ref[...])

    pl.when(t < n_tiles)(lambda: tile(long_in, xl_ref, ol_ref))
    pl.when(t == n_tiles)(lambda: tile(short_in, xs_ref, os_ref))


def _proj_ffn(mix_long, mix_short, ng, wo, x_long, x_short, g1, g2, wg, wu, wd, layer, g3, tm):
    bsz, seq, _ = x_long.shape
    tiles_per_seq = seq // tm
    n_tiles = bsz * tiles_per_seq
    n_short = x_short.shape[0] * x_short.shape[1]
    flat = lambda a: a.reshape(1, n_short, a.shape[-1])
    tile_idx = lambda t: jnp.minimum(t, n_tiles - 1)
    ltok = lambda width: pl.BlockSpec((None, tm, width),
                                      lambda t: (tile_idx(t) // tiles_per_seq, tile_idx(t) % tiles_per_seq, 0))
    stok = lambda width: pl.BlockSpec((None, n_short, width), lambda t: (0, 0, 0))
    gate = ng is not None
    ng_args, ng_specs = ([ng], [_const_spec((1, SSM_INNER))]) if gate else ([], [])
    out_long, out_short = pl.pallas_call(
        functools.partial(_proj_ffn_kernel, len(mix_long), gate, n_tiles),
        grid=(n_tiles + 1,),
        in_specs=[ltok(a.shape[-1]) for a in mix_long] + [stok(a.shape[-1]) for a in mix_short] + ng_specs + [
            _const_spec(wo.shape), ltok(D_MODEL), stok(D_MODEL), _const_spec((1, D_MODEL)), _const_spec((1, D_MODEL)),
            _layer_spec((D_MODEL, D_FF), layer), _layer_spec((D_MODEL, D_FF), layer),
            _layer_spec((D_FF, D_MODEL), layer), _const_spec((1, D_MODEL))],
        out_specs=[ltok(D_MODEL), stok(D_MODEL)],
        out_shape=[jax.ShapeDtypeStruct(x_long.shape, F32), jax.ShapeDtypeStruct((1, n_short, D_MODEL), F32)],
        compiler_params=pltpu.CompilerParams(dimension_semantics=("arbitrary",), vmem_limit_bytes=VMEM_LIMIT),
        name="out_proj_ffn",
    )(*mix_long, *[flat(a) for a in mix_short], *ng_args, wo, x_long, flat(x_short), g1, g2, wg, wu, wd, g3)
    return out_long, out_short.reshape(x_short.shape)


def _split3(v):
    t1 = v.astype(BF16)
    r = v - t1.astype(F32)
    t2 = r.astype(BF16)
    t3 = (r - t2.astype(F32)).astype(BF16)
    return t1, t2, t3


def _pack3(v):
    t1, t2, t3 = _split3(v)
    lane = lax.broadcasted_iota(jnp.int32, v.shape, 1)
    return jnp.where(lane < SSM_HEADS, t1,
                     jnp.where(lane < 2 * SSM_HEADS, t2,
                               jnp.where(lane < 3 * SSM_HEADS, t3, jnp.zeros_like(t3))))


def _ssm_in_kernel(tm, cl, x_ref, g_ref, w_ref, wdt_ref, hist_ref, cw_ref, cb_ref, dtb_ref, alog_ref,
                   z_ref, xs_ref, bm_ref, cm_ref, ta_ref, tdt_ref, nh_ref, carry_ref):
    i = pl.program_id(1)

    @pl.when(i == 0)
    def _():
        carry_ref[SUBLANES - (SSM_CONV - 1):, :] = hist_ref[...]

    h = _rms(x_ref[...], g_ref[...]).astype(BF16)

    dt = jax.nn.softplus(_dot(h, wdt_ref[...]) + dtb_ref[...])
    tdt_ref[...] = _pack3(dt)
    d1, d2, d3 = _split3(dt * (-jnp.exp(alog_ref[...])))
    tcol = lax.broadcasted_iota(jnp.int32, (cl, 3 * cl), 1) % cl
    trow = lax.broadcasted_iota(jnp.int32, (cl, 3 * cl), 0)
    tril = jnp.where(tcol <= trow, 1.0, 0.0).astype(BF16)
    acum = []
    for c in range(tm // cl):
        rows = slice(c * cl, (c + 1) * cl)
        acum.append(_dot(tril, jnp.concatenate([d1[rows], d2[rows], d3[rows]], axis=0)))
    ta_ref[...] = _pack3(jnp.concatenate(acum, axis=0))

    for n in range(SSM_INNER // COL_CHUNK):
        cols = slice(n * COL_CHUNK, (n + 1) * COL_CHUNK)
        z_ref[:, cols] = _silu(_dot(h, w_ref[:, cols])).astype(BF16)

    for n in range(SSM_CONV_CH // COL_CHUNK):
        cols = slice(n * COL_CHUNK, (n + 1) * COL_CHUNK)
        raw = _dot(h, w_ref[:, SSM_INNER + cols.start:SSM_INNER + cols.stop])
        y = _silu(_causal_conv(carry_ref, nh_ref, raw, cw_ref[:, cols], cols) + cb_ref[:, cols])
        if cols.stop <= SSM_INNER:
            xs_ref[:, cols] = y
        elif cols.stop <= SSM_INNER + SSM_GN:
            bm_ref[:, cols.start - SSM_INNER:cols.stop - SSM_INNER] = y.astype(BF16)
        else:
            lo = SSM_INNER + SSM_GN
            cm_ref[:, cols.start - lo:cols.stop - lo] = y.astype(BF16)


def _ssm_in(x, g, w, wdt, hist, cw, cb, dtb, alog, tm, cl):
    bsz, seq, _ = x.shape
    tok = lambda width: pl.BlockSpec((None, tm, width), lambda b, i: (b, i, 0))
    hist_spec = pl.BlockSpec((None, SSM_CONV - 1, SSM_CONV_CH), lambda b, i: (b, 0, 0))
    return pl.pallas_call(
        functools.partial(_ssm_in_kernel, tm, cl),
        grid=(bsz, seq // tm),
        in_specs=[tok(D_MODEL), _const_spec((1, D_MODEL)), _const_spec(w.shape), _const_spec((D_MODEL, LANES)),
                  hist_spec,
                  _const_spec((SSM_CONV, SSM_CONV_CH)), _const_spec((1, SSM_CONV_CH)), _const_spec((1, LANES)),
                  _const_spec((1, LANES))],
        out_specs=[tok(SSM_INNER), tok(SSM_INNER), tok(SSM_GN), tok(SSM_GN), tok(LANES), tok(LANES), hist_spec],
        out_shape=[jax.ShapeDtypeStruct((bsz, seq, SSM_INNER), BF16),
                   jax.ShapeDtypeStruct((bsz, seq, SSM_INNER), F32),
                   jax.ShapeDtypeStruct((bsz, seq, SSM_GN), BF16),
                   jax.ShapeDtypeStruct((bsz, seq, SSM_GN), BF16),
                   jax.ShapeDtypeStruct((bsz, seq, LANES), BF16),
                   jax.ShapeDtypeStruct((bsz, seq, LANES), BF16),
                   jax.ShapeDtypeStruct((bsz, SSM_CONV - 1, SSM_CONV_CH), F32)],
        scratch_shapes=[pltpu.VMEM((SUBLANES, SSM_CONV_CH), F32)],
        compiler_params=_params(),
        name="ssm_in_proj",
    )(x, g, w, wdt, hist, cw, cb, dtb, alog)


def _pad_rows(a, rows):
    if a.shape[0] == rows:
        return a
    return jnp.concatenate([a, jnp.zeros((rows - a.shape[0], a.shape[1]), a.dtype)], axis=0)


def _ssd_kernel(ts, cl, xs_ref, bm_ref, cm_ref, ta_ref, tdt_ref, h0_ref, dskip_ref,
                yo_ref, hout_ref, ht_ref, y_ref, e3_ref, aexp_ref, xdt_ref, st_ref, cb_ref, bmt_ref, w2_ref, rhs_ref):
    b = pl.program_id(0)
    i = pl.program_id(1)
    p = SSM_HEADDIM
    nc = ts // cl
    pairs = SSM_GROUP_W // LANES

    @pl.when((b == 0) & (i == 0))
    def _():
        kk = lax.broadcasted_iota(jnp.int32, (LANES, SSM_INNER), 0)
        cc = lax.broadcasted_iota(jnp.int32, (LANES, SSM_INNER), 1)
        hit = (kk % SSM_HEADS == cc // p) & (kk < 3 * SSM_HEADS)
        e3_ref[...] = jnp.where(hit, 1.0, 0.0).astype(BF16)

    @pl.when(i == 0)
    def _():
        ht_ref[...] = h0_ref[...].T

    lane = lax.broadcasted_iota(jnp.int32, (cl, LANES), 1)
    row = lax.broadcasted_iota(jnp.int32, (cl, LANES), 0)
    causal2 = (lane % p) <= row
    left = lane.astype(F32).astype(BF16) < p
    diag2 = (lane % p) == row
    groups = [(gi, slice(gi * SSM_GROUP_W, (gi + 1) * SSM_GROUP_W), slice(gi * SSM_STATE, (gi + 1) * SSM_STATE))
              for gi in range(SSM_GROUPS)]
    chunks = [(c, slice(c * cl, (c + 1) * cl)) for c in range(nc)]

    for gi, gcols, _ in groups:
        e3g = e3_ref[:, gcols]
        aexp_ref[:, gcols] = _dot(ta_ref[...], e3g)
        xdt_ref[:, gcols] = (xs_ref[:, gcols] * _dot(tdt_ref[...], e3g)).astype(BF16)

    for c, rows in chunks:
        for gi, _, scol in groups:
            bmp = _pad_rows(bm_ref[rows, scol], p)
            cbm = _dot_nt(cm_ref[rows, scol], bmp)
            cb_ref[c * SSM_GROUPS + gi] = jnp.concatenate([cbm, cbm], axis=1)
            bmt_ref[c * SSM_GROUPS + gi] = bmp.astype(F32).T.astype(BF16)

    for c, rows in chunks:
        for gi, _, _ in groups:
            for jj in range(pairs):
                idx = (c * SSM_GROUPS + gi) * pairs + jj
                cols = slice(gi * SSM_GROUP_W + jj * LANES, gi * SSM_GROUP_W + (jj + 1) * LANES)
                a_pair = aexp_ref[rows, cols]
                a_src = jnp.sum(jnp.where(diag2, a_pair, 0.0), axis=0, keepdims=True)
                seg = a_pair - a_src
                dec = jnp.exp(jnp.where(causal2, seg, -jnp.inf))
                w2_ref[idx] = (dec * cb_ref[c * SSM_GROUPS + gi]).astype(BF16)
                x2 = xdt_ref[rows, cols]
                top = _pad_rows(jnp.where(left, x2, jnp.zeros_like(x2)), p)
                bot = _pad_rows(jnp.where(left, jnp.zeros_like(x2), x2), p)
                rhs_ref[idx] = jnp.concatenate([top, bot], axis=0)

    for c, rows in chunks:
        for gi, _, _ in groups:
            for jj in range(pairs):
                idx = (c * SSM_GROUPS + gi) * pairs + jj
                cols = slice(gi * SSM_GROUP_W + jj * LANES, gi * SSM_GROUP_W + (jj + 1) * LANES)
                y_ref[rows, cols] = _dot(w2_ref[idx], rhs_ref[idx])

    for c, rows in chunks:
        for gi, gcols, _ in groups:
            a_g = aexp_ref[rows, gcols]
            xw = xdt_ref[rows, gcols] * jnp.exp(a_g[cl - 1:cl, :] - a_g).astype(BF16)
            st_ref[c, :, gcols] = _dot(bmt_ref[c * SSM_GROUPS + gi], _pad_rows(xw, p))

    for c, rows in chunks:
        for gi, gcols, scol in groups:
            hprev = ht_ref[:, gcols]
            a_g = aexp_ref[rows, gcols]
            y_off = _dot(cm_ref[rows, scol], hprev.astype(BF16)) * jnp.exp(a_g)
            y_ref[rows, gcols] = y_ref[rows, gcols] + y_off
            ht_ref[:, gcols] = hprev * jnp.exp(a_g[cl - 1:cl, :]) + st_ref[c, :, gcols]

    yo_ref[...] = (y_ref[...] + dskip_ref[...] * xs_ref[...]).astype(BF16)

    @pl.when(i == pl.num_programs(1) - 1)
    def _():
        hout_ref[...] = ht_ref[...].T


def _ssd(xs, bm, cm, ta, tdt, h0, dskip, ts, cl):
    bsz, seq, _ = xs.shape
    tok = lambda width: pl.BlockSpec((None, ts, width), lambda b, i: (b, i, 0))
    st_spec = pl.BlockSpec((None, SSM_INNER, SSM_STATE), lambda b, i: (b, 0, 0))
    nc = ts // cl
    n_cg = nc * SSM_GROUPS
    n_pairs = n_cg * (SSM_GROUP_W // LANES)
    return pl.pallas_call(
        functools.partial(_ssd_kernel, ts, cl),
        grid=(bsz, seq // ts),
        in_specs=[tok(SSM_INNER), tok(SSM_GN), tok(SSM_GN), tok(LANES), tok(LANES), st_spec,
                  _const_spec((1, SSM_INNER))],
        out_specs=[tok(SSM_INNER), st_spec],
        out_shape=[jax.ShapeDtypeStruct((bsz, seq, SSM_INNER), BF16),
                   jax.ShapeDtypeStruct((bsz, SSM_INNER, SSM_STATE), F32)],
        scratch_shapes=[pltpu.VMEM((SSM_STATE, SSM_INNER), F32), pltpu.VMEM((ts, SSM_INNER), F32),
                        pltpu.VMEM((LANES, SSM_INNER), BF16), pltpu.VMEM((ts, SSM_INNER), F32),
                        pltpu.VMEM((ts, SSM_INNER), BF16), pltpu.VMEM((nc, SSM_STATE, SSM_INNER), F32),
                        pltpu.VMEM((n_cg, cl, LANES), F32), pltpu.VMEM((n_cg, SSM_STATE, SSM_HEADDIM), BF16),
                        pltpu.VMEM((n_pairs, cl, LANES), BF16), pltpu.VMEM((n_pairs, LANES, LANES), BF16)],
        compiler_params=_params(),
        name="ssd_scan",
    )(xs, bm, cm, ta, tdt, h0, dskip)


def _layer0_mixers(x, pos0, sconv_hist, k_cache, v_cache, wts, tiles):
    t_in, t_att, c_att = tiles
    q, k, v, gated, new_sconv = _attn_in(x, wts["g"][0][0], wts["ab_w_in"], sconv_hist, wts["sconv_w"], pos0, t_in)
    masked = k_cache is None
    k_prev, v_prev = (k, v) if masked else (k_cache, v_cache)
    att = _attn(wts["sinks"], q, k, v, k_prev, v_prev, t_att, c_att, masked)
    bsz, seq, _ = x.shape
    new_k = k[:, seq - min(WINDOW, seq):].reshape(bsz, -1, A_KV_HEADS, HEAD_DIM)
    new_v = v[:, seq - min(WINDOW, seq):].reshape(bsz, -1, A_KV_HEADS, HEAD_DIM)
    return [att, gated], (new_k, new_v, new_sconv)


def _layer1_mixer(x, conv_hist, ssm_state, wts, tiles):
    t_sin, t_ssd, c_ssd = tiles
    bsz = x.shape[0]
    z, xs, bm, cm, ta, tdt, new_conv = _ssm_in(
        x, wts["g"][1][0], wts["ssm_w_in"], wts["w_dt"], conv_hist, wts["ssm_conv_w"],
        wts["ssm_conv_b"], wts["dt_bias"], wts["a_log"], t_sin, c_ssd)
    y, new_state = _ssd(xs, bm, cm, ta, tdt, ssm_state.reshape(bsz, SSM_INNER, SSM_STATE),
                        wts["d_skip"], t_ssd, c_ssd)
    return [y, z], (new_conv, new_state.reshape(bsz, SSM_HEADS, SSM_HEADDIM, SSM_STATE))


def kernel(x_prompt, x_sample, cache_attn_k, cache_attn_v, state_sconv, state_ssm_conv, state_ssm, norm_g,
           ab_w_in, ab_w_out, attn_sinks, sconv_w, ssm_w_in, ssm_conv_w, ssm_conv_b, ssm_dt_bias, ssm_a_log,
           ssm_d, ssm_norm_g, ssm_w_out, ffn_w_gate, ffn_w_up, ffn_w_down):
    w_in1 = ssm_w_in[0]
    wts = {
        "ab_w_in": ab_w_in[0].astype(BF16),
        "ab_w_out": ab_w_out[0].astype(BF16),
        "sinks": attn_sinks[0],
        "sconv_w": sconv_w[0],
        "ssm_w_in": w_in1.astype(BF16),
        "w_dt": jnp.tile(w_in1[:, SSM_INNER + SSM_CONV_CH:], (1, DT_REP)).astype(BF16),
        "ssm_conv_w": ssm_conv_w[0],
        "ssm_conv_b": ssm_conv_b[0].reshape(1, -1),
        "dt_bias": jnp.tile(ssm_dt_bias[0], DT_REP).reshape(1, -1),
        "a_log": jnp.tile(ssm_a_log[0], DT_REP).reshape(1, -1),
        "d_skip": jnp.repeat(ssm_d[0], SSM_HEADDIM).reshape(1, -1),
        "ssm_norm_g": ssm_norm_g[0].reshape(1, -1),
        "ssm_w_out": ssm_w_out[0].astype(BF16),
        "wg": ffn_w_gate.astype(BF16),
        "wu": ffn_w_up.astype(BF16),
        "wd": ffn_w_down.astype(BF16),
    }
    wts["g"] = [[norm_g[l, j].reshape(1, -1) for j in range(4)] for l in range(norm_g.shape[0])]
    g = wts["g"]
    bp = x_prompt.shape[0]
    bs, ls = x_sample.shape[0], x_sample.shape[1]
    t_ffn = 512

    mix_p, (kp, vp, scp) = _layer0_mixers(x_prompt, 0, jnp.zeros((bp, B_CONV - 1, B_WIDTH), F32), None, None,
                                          wts, (512, 512, CHUNK))
    mix_s, (ks, vs, scs) = _layer0_mixers(x_sample, PAST_LEN, state_sconv[0],
                                          cache_attn_k[0].reshape(bs, -1, KV_WIDTH),
                                          cache_attn_v[0].reshape(bs, -1, KV_WIDTH), wts, (ls, ls, ls))
    xp, xs = _proj_ffn(mix_p, mix_s, None, wts["ab_w_out"], x_prompt, x_sample, g[0][1], g[0][2],
                       wts["wg"], wts["wu"], wts["wd"], 0, g[0][3], t_ffn)

    mix_p, (ccp, ssp) = _layer1_mixer(xp, jnp.zeros((bp, SSM_CONV - 1, SSM_CONV_CH), F32),
                                      jnp.zeros((bp, SSM_HEADS, SSM_HEADDIM, SSM_STATE), F32), wts, (512, 512, CHUNK))
    mix_s, (ccs, sss) = _layer1_mixer(xs, state_ssm_conv[0], state_ssm[0], wts, (ls, ls, ls))
    yp, ys = _proj_ffn(mix_p, mix_s, wts["ssm_norm_g"], wts["ssm_w_out"], xp, xs, g[1][1], g[1][2],
                       wts["wg"], wts["wu"], wts["wd"], 1, g[1][3], t_ffn)

    lead = lambda a: a[None]
    return (yp, ys, lead(kp), lead(vp), lead(scp), lead(ccp), lead(ssp),
            lead(ks), lead(vs), lead(scs), lead(ccs), lead(sss))
```

```python
import functools

import jax
import jax.numpy as jnp
from jax import lax
from jax.experimental import pallas as pl
from jax.experimental.pallas import tpu as pltpu

F32 = jnp.float32
BF16 = jnp.bfloat16

D_MODEL = 1024
CHUNK = 64
EPS = 1e-6
PAST_LEN = 4096

A_HEADS = 8
A_KV_HEADS = 2
A_GROUP = A_HEADS // A_KV_HEADS
HEAD_DIM = 64
A_WIDTH = A_HEADS * HEAD_DIM
KV_WIDTH = A_KV_HEADS * HEAD_DIM
WINDOW = 128
ROPE_DIM = HEAD_DIM // 4
ROPE_HALF = ROPE_DIM // 2
ROPE_THETA = 500000.0
ATTN_SCALE = HEAD_DIM ** -0.5

B_WIDTH = D_MODEL // 2
B_CONV = 3
AB_IN = A_WIDTH + 2 * KV_WIDTH + 3 * B_WIDTH

SSM_INNER = 2 * D_MODEL
SSM_HEADDIM = 64
SSM_HEADS = SSM_INNER // SSM_HEADDIM
SSM_GROUPS = 4
SSM_GROUP_W = SSM_INNER // SSM_GROUPS
SSM_STATE = 128
SSM_CONV = 4
SSM_GN = SSM_GROUPS * SSM_STATE
SSM_CONV_CH = SSM_INNER + 2 * SSM_GN

D_FF = -(-8 * D_MODEL // (3 * 256)) * 256

LANES = 128
SUBLANES = 8
MXU_DIM = 256
DT_REP = LANES // SSM_HEADS
VMEM_LIMIT = 56 * 1024 * 1024
COL_CHUNK = MXU_DIM
FF_CHUNKS = (6 * MXU_DIM, 5 * MXU_DIM)
assert sum(FF_CHUNKS) == D_FF


def _rms(x, g):
    return x * lax.rsqrt(jnp.mean(x * x, axis=-1, keepdims=True) + EPS) * g


def _dot(a, b):
    return jnp.dot(a, b, preferred_element_type=F32)


def _dot_nt(a, b):
    return lax.dot_general(a, b, (((1,), (1,)), ((), ())), preferred_element_type=F32)


def _silu(x):
    return x * jax.nn.sigmoid(x)


def _const_spec(shape):
    return pl.BlockSpec(shape, lambda *_: (0,) * len(shape), pipeline_mode=pl.Buffered(1))


def _layer_spec(shape, layer):
    return pl.BlockSpec((None,) + shape, lambda *_: (layer,) + (0,) * len(shape), pipeline_mode=pl.Buffered(1))


def _params():
    return pltpu.CompilerParams(dimension_semantics=("arbitrary", "arbitrary"),
                                vmem_limit_bytes=VMEM_LIMIT)


def _causal_conv(carry_ref, nh_ref, u, cw, cols):
    width = cw.shape[0]
    rows, ch = u.shape
    n = rows // SUBLANES
    ext3 = jnp.concatenate([carry_ref[:, cols], u], axis=0).reshape(n + 1, SUBLANES, ch)
    u3 = ext3[1:]
    sub = lax.broadcasted_iota(jnp.int32, (1, SUBLANES, ch), 1)

    def shift(above, here, k):
        return pltpu.roll(jnp.where(sub >= SUBLANES - k, above, here), k, 1)

    if width == 4:
        s1 = shift(jnp.concatenate([ext3[:1], ext3[:-1]], axis=0), ext3, 1)
        b = s1 * cw[0:1] + ext3 * cw[1:2]
        y3 = shift(b[:-1], b[1:], 2) + (s1[1:] * cw[2:3] + u3 * cw[3:4])
    else:
        y3 = None
        for j in range(width - 1):
            term = shift(ext3[:-1], u3, width - 1 - j) * cw[j:j + 1]
            y3 = term if y3 is None else y3 + term
        y3 = y3 + u3 * cw[width - 1:width]
    carry_ref[:, cols] = u[rows - SUBLANES:]
    nh_ref[:, cols] = u[rows - (width - 1):]
    return y3.reshape(rows, ch)


def _rope_freq(shape):
    lane = lax.broadcasted_iota(jnp.int32, shape, 1)
    d = lane % HEAD_DIM
    expo = -((d % ROPE_HALF).astype(F32)) / ROPE_HALF
    freq = jnp.power(jnp.full(shape, ROPE_THETA, F32), expo)
    return jnp.where(d < ROPE_DIM, freq, 0.0), d


def _attn_in_kernel(pos0, tm, x_ref, g_ref, w_ref, hist_ref, cw_ref,
                    q_ref, k_ref, v_ref, gated_ref, nh_ref,
                    cr_ref, sr_ref, carry_ref, p_ref):
    b = pl.program_id(0)
    i = pl.program_id(1)

    @pl.when((b == 0) & (i == 0))
    def _():
        freq, _ = _rope_freq((tm, LANES))
        ang = lax.broadcasted_iota(jnp.int32, (tm, LANES), 0).astype(F32) * freq
        cr_ref[...] = jnp.cos(ang)
        sr_ref[...] = jnp.sin(ang)

    @pl.when(i == 0)
    def _():
        carry_ref[SUBLANES - (B_CONV - 1):, :] = hist_ref[...]

    freq1, d1 = _rope_freq((1, LANES))
    base = (pos0 + i * tm).astype(F32) * freq1
    cb = jnp.cos(base)
    sb = jnp.sin(base)
    cr = cr_ref[...]
    sr = sr_ref[...]
    cos_t = cr * cb - sr * sb
    sin_t = sr * cb + cr * sb
    m_lo = jnp.where(d1 < ROPE_HALF, -1.0, 0.0)
    m_hi = jnp.where((d1 >= ROPE_HALF) & (d1 < ROPE_DIM), 1.0, 0.0)

    def rope(t):
        partner = pltpu.roll(t, LANES - ROPE_HALF, 1) * m_lo + pltpu.roll(t, ROPE_HALF, 1) * m_hi
        return t * cos_t + partner * sin_t

    h = _rms(x_ref[...], g_ref[...]).astype(BF16)
    o1 = A_WIDTH
    o2 = o1 + KV_WIDTH
    o3 = o2 + KV_WIDTH
    o4 = o3 + B_WIDTH
    o5 = o4 + B_WIDTH

    for n in range(AB_IN // MXU_DIM):
        cols = slice(n * MXU_DIM, (n + 1) * MXU_DIM)
        p_ref[:, cols] = _dot(h, w_ref[:, cols])

    u = p_ref[:, o4:o5] * p_ref[:, o5:]
    y = _causal_conv(carry_ref, nh_ref, u, cw_ref[...], slice(None))
    gated_ref[...] = (p_ref[:, o3:o4] * y).astype(BF16)
    k_ref[...] = rope(p_ref[:, o1:o2])
    v_ref[...] = p_ref[:, o2:o3]
    for s in range(A_WIDTH // LANES):
        q_ref[:, s * LANES:(s + 1) * LANES] = (rope(p_ref[:, s * LANES:(s + 1) * LANES]) * ATTN_SCALE).astype(BF16)


def _attn_in(x, g, w, hist, cw, pos0, tm):
    bsz, seq, _ = x.shape
    grid = (bsz, seq // tm)
    tok = lambda width: pl.BlockSpec((None, tm, width), lambda b, i: (b, i, 0))
    return pl.pallas_call(
        functools.partial(_attn_in_kernel, pos0, tm),
        grid=grid,
        in_specs=[tok(D_MODEL), _const_spec((1, D_MODEL)), _const_spec((D_MODEL, AB_IN)),
                  pl.BlockSpec((None, B_CONV - 1, B_WIDTH), lambda b, i: (b, 0, 0)),
                  _const_spec((B_CONV, B_WIDTH))],
        out_specs=[tok(A_WIDTH), tok(KV_WIDTH), tok(KV_WIDTH), tok(B_WIDTH),
                   pl.BlockSpec((None, B_CONV - 1, B_WIDTH), lambda b, i: (b, 0, 0))],
        out_shape=[jax.ShapeDtypeStruct((bsz, seq, A_WIDTH), BF16),
                   jax.ShapeDtypeStruct((bsz, seq, KV_WIDTH), F32),
                   jax.ShapeDtypeStruct((bsz, seq, KV_WIDTH), F32),
                   jax.ShapeDtypeStruct((bsz, seq, B_WIDTH), BF16),
                   jax.ShapeDtypeStruct((bsz, B_CONV - 1, B_WIDTH), F32)],
        scratch_shapes=[pltpu.VMEM((tm, LANES), F32), pltpu.VMEM((tm, LANES), F32),
                        pltpu.VMEM((SUBLANES, B_WIDTH), F32), pltpu.VMEM((tm, AB_IN), F32)],
        compiler_params=_params(),
        name="attn_in_proj",
    )(x, g, w, hist, cw)


def _attn_kernel(tq, cq, masked, sinks_ref, q_ref, kc_ref, vc_ref, kp_ref, vp_ref, att_ref,
                 kk_ref, vx_ref, s_ref, e_ref):
    i = pl.program_id(1)
    nk = WINDOW + cq
    nkp = MXU_DIM
    rows_kv = WINDOW + tq
    hd = HEAD_DIM

    kk_ref[0:WINDOW, :] = kp_ref[...].astype(BF16)
    kk_ref[WINDOW:rows_kv, :] = kc_ref[...].astype(BF16)
    kk_ref[rows_kv:, :] = jnp.zeros((kk_ref.shape[0] - rows_kv, KV_WIDTH), BF16)
    vall = jnp.concatenate([vp_ref[...], vc_ref[...]], axis=0)
    vx_ref[0:rows_kv, :] = jnp.concatenate([vall[:, :hd], vall[:, :hd], vall[:, hd:], vall[:, hd:]], axis=1)
    vx_ref[rows_kv:, :] = jnp.zeros((vx_ref.shape[0] - rows_kv, 2 * LANES), F32)

    col = lax.broadcasted_iota(jnp.int32, (A_GROUP * cq, nkp), 1)
    krow = lax.broadcasted_iota(jnp.int32, (nkp, LANES), 0)
    ones = jnp.ones((nkp, LANES), BF16)
    bodies = [(c, kh) for c in range(tq // cq) for kh in range(A_KV_HEADS)]
    fills = []
    for kh in range(A_KV_HEADS):
        sink = jnp.concatenate(
            [jnp.full((cq, nkp), sinks_ref[kh * A_GROUP + g], F32) for g in range(A_GROUP)], axis=0)
        fills.append(jnp.where(col == nk, sink, -jnp.inf))

    for n, (c, kh) in enumerate(bodies):
        r0 = c * cq
        heads = [kh * A_GROUP + g for g in range(A_GROUP)]
        kt = kk_ref[r0:r0 + nkp, kh * hd:(kh + 1) * hd]
        qs = jnp.concatenate([q_ref[r0:r0 + cq, hh * hd:(hh + 1) * hd] for hh in heads], axis=0)
        s = _dot_nt(qs, kt)
        valid = col < nk
        if masked and r0 < WINDOW:
            valid = valid & ((col >= WINDOW - r0) | (i > 0))
        s_ref[n] = jnp.where(valid, s, fills[kh])

    for n in range(len(bodies)):
        s = s_ref[n]
        m = jnp.max(s, axis=-1, keepdims=True)
        e_ref[n] = jnp.exp(s - m).astype(BF16)

    for n, (c, kh) in enumerate(bodies):
        r0 = c * cq
        vt = jnp.where(krow == nk, 0.0, vx_ref[r0:r0 + nkp, kh * LANES:(kh + 1) * LANES]).astype(BF16)
        e = e_ref[n]
        o = _dot(e, vt) / _dot(e, ones)
        for g in range(A_GROUP):
            hh = kh * A_GROUP + g
            lanes = slice((hh % 2) * hd, (hh % 2 + 1) * hd)
            att_ref[r0:r0 + cq, hh * hd:(hh + 1) * hd] = o[g * cq:(g + 1) * cq, lanes].astype(BF16)


def _attn(sinks, q, k, v, k_prev, v_prev, tq, cq, masked):
    bsz, seq, _ = q.shape
    grid = (bsz, seq // tq)
    tok = lambda width: pl.BlockSpec((None, tq, width), lambda b, i: (b, i, 0))
    if masked:
        blocks_per_tile = tq // WINDOW
        prev = pl.BlockSpec((None, WINDOW, KV_WIDTH),
                            lambda b, i: (b, jnp.maximum(i * blocks_per_tile - 1, 0), 0))
    else:
        prev = pl.BlockSpec((None, WINDOW, KV_WIDTH), lambda b, i: (b, 0, 0))
    n_bodies = (tq // cq) * A_KV_HEADS
    kv_rows = (tq // cq - 1) * cq + MXU_DIM
    return pl.pallas_call(
        functools.partial(_attn_kernel, tq, cq, masked),
        grid=grid,
        in_specs=[pl.BlockSpec(memory_space=pltpu.SMEM),
                  tok(A_WIDTH), tok(KV_WIDTH), tok(KV_WIDTH), prev, prev],
        out_specs=tok(A_WIDTH),
        out_shape=jax.ShapeDtypeStruct((bsz, seq, A_WIDTH), BF16),
        scratch_shapes=[pltpu.VMEM((kv_rows, KV_WIDTH), BF16), pltpu.VMEM((kv_rows, 2 * LANES), F32),
                        pltpu.VMEM((n_bodies, A_GROUP * cq, MXU_DIM), F32),
                        pltpu.VMEM((n_bodies, A_GROUP * cq, MXU_DIM), BF16)],
        compiler_params=_params(),
        name="band_attention",
    )(sinks, q, k, v, k_prev, v_prev)


def _proj_ffn_kernel(n_in, gate, n_tiles, *refs):
    long_in, short_in = refs[:n_in], refs[n_in:2 * n_in]
    refs = refs[2 * n_in:]
    ng_ref = None
    if gate:
        ng_ref, refs = refs[0], refs[1:]
    wo_ref, xl_ref, xs_ref, g1_ref, g2_ref, wg_ref, wu_ref, wd_ref, g3_ref, ol_ref, os_ref = refs
    t = pl.program_id(0)

    def tile(ins, x_ref, o_ref):
        acc = None
        if gate:
            y_ref, z_ref = ins
            for gi in range(SSM_GROUPS):
                gcols = slice(gi * SSM_GROUP_W, (gi + 1) * SSM_GROUP_W)
                gg = y_ref[:, gcols].astype(F32) * z_ref[:, gcols].astype(F32)
                gg = gg * lax.rsqrt(jnp.mean(gg * gg, axis=-1, keepdims=True) + EPS)
                part = _dot((gg * ng_ref[:, gcols]).astype(BF16), wo_ref[gcols, :])
                acc = part if acc is None else acc + part
        else:
            off = 0
            for a_ref in ins:
                kdim = a_ref.shape[-1]
                part = _dot(a_ref[...], wo_ref[off:off + kdim, :])
                acc = part if acc is None else acc + part
                off += kdim

        x = x_ref[...] + _rms(acc, g1_ref[...])
        h = _rms(x, g2_ref[...]).astype(BF16)
        f = None
        off = 0
        for width in FF_CHUNKS:
            fc = slice(off, off + width)
            m = (_silu(_dot(h, wg_ref[:, fc])) * _dot(h, wu_ref[:, fc])).astype(BF16)
            part = _dot(m, wd_ref[fc, :])
            f = part if f is None else f + part
            off += width
        o_ref[...] = x + _rms(f, g3_ref[...])

    pl.when(t < n_tiles)(lambda: tile(long_in, xl_ref, ol_ref))
    pl.when(t == n_tiles)(lambda: tile(short_in, xs_ref, os_ref))


def _proj_ffn(mix_long, mix_short, ng, wo, x_long, x_short, g1, g2, wg, wu, wd, layer, g3, tm):
    bsz, seq, _ = x_long.shape
    tiles_per_seq = seq // tm
    n_tiles = bsz * tiles_per_seq
    n_short = x_short.shape[0] * x_short.shape[1]
    flat = lambda a: a.reshape(1, n_short, a.shape[-1])
    tile_idx = lambda t: jnp.minimum(t, n_tiles - 1)
    ltok = lambda width: pl.BlockSpec((None, tm, width),
                                      lambda t: (tile_idx(t) // tiles_per_seq, tile_idx(t) % tiles_per_seq, 0))
    stok = lambda width: pl.BlockSpec((None, n_short, width), lambda t: (0, 0, 0))
    gate = ng is not None
    ng_args, ng_specs = ([ng], [_const_spec((1, SSM_INNER))]) if gate else ([], [])
    out_long, out_short = pl.pallas_call(
        functools.partial(_proj_ffn_kernel, len(mix_long), gate, n_tiles),
        grid=(n_tiles + 1,),
        in_specs=[ltok(a.shape[-1]) for a in mix_long] + [stok(a.shape[-1]) for a in mix_short] + ng_specs + [
            _const_spec(wo.shape), ltok(D_MODEL), stok(D_MODEL), _const_spec((1, D_MODEL)), _const_spec((1, D_MODEL)),
            _layer_spec((D_MODEL, D_FF), layer), _layer_spec((D_MODEL, D_FF), layer),
            _layer_spec((D_FF, D_MODEL), layer), _const_spec((1, D_MODEL))],
        out_specs=[ltok(D_MODEL), stok(D_MODEL)],
        out_shape=[jax.ShapeDtypeStruct(x_long.shape, F32), jax.ShapeDtypeStruct((1, n_short, D_MODEL), F32)],
        compiler_params=pltpu.CompilerParams(dimension_semantics=("arbitrary",), vmem_limit_bytes=VMEM_LIMIT),
        name="out_proj_ffn",
    )(*mix_long, *[flat(a) for a in mix_short], *ng_args, wo, x_long, flat(x_short), g1, g2, wg, wu, wd, g3)
    return out_long, out_short.reshape(x_short.shape)


def _split3(v):
    t1 = v.astype(BF16)
    r = v - t1.astype(F32)
    t2 = r.astype(BF16)
    t3 = (r - t2.astype(F32)).astype(BF16)
    return t1, t2, t3


def _pack3(v):
    t1, t2, t3 = _split3(v)
    lane = lax.broadcasted_iota(jnp.int32, v.shape, 1)
    return jnp.where(lane < SSM_HEADS, t1,
                     jnp.where(lane < 2 * SSM_HEADS, t2,
                               jnp.where(lane < 3 * SSM_HEADS, t3, jnp.zeros_like(t3))))


def _ssm_in_kernel(tm, cl, x_ref, g_ref, w_ref, wdt_ref, hist_ref, cw_ref, cb_ref, dtb_ref, alog_ref,
                   z_ref, xs_ref, bm_ref, cm_ref, ta_ref, tdt_ref, nh_ref, carry_ref):
    i = pl.program_id(1)

    @pl.when(i == 0)
    def _():
        carry_ref[SUBLANES - (SSM_CONV - 1):, :] = hist_ref[...]

    h = _rms(x_ref[...], g_ref[...]).astype(BF16)

    dt = jax.nn.softplus(_dot(h, wdt_ref[...]) + dtb_ref[...])
    tdt_ref[...] = _pack3(dt)
    d1, d2, d3 = _split3(dt * (-jnp.exp(alog_ref[...])))
    tcol = lax.broadcasted_iota(jnp.int32, (cl, 3 * cl), 1) % cl
    trow = lax.broadcasted_iota(jnp.int32, (cl, 3 * cl), 0)
    tril = jnp.where(tcol <= trow, 1.0, 0.0).astype(BF16)
    acum = []
    for c in range(tm // cl):
        rows = slice(c * cl, (c + 1) * cl)
        acum.append(_dot(tril, jnp.concatenate([d1[rows], d2[rows], d3[rows]], axis=0)))
    ta_ref[...] = _pack3(jnp.concatenate(acum, axis=0))

    for n in range(SSM_INNER // COL_CHUNK):
        cols = slice(n * COL_CHUNK, (n + 1) * COL_CHUNK)
        z_ref[:, cols] = _silu(_dot(h, w_ref[:, cols])).astype(BF16)

    for n in range(SSM_CONV_CH // COL_CHUNK):
        cols = slice(n * COL_CHUNK, (n + 1) * COL_CHUNK)
        raw = _dot(h, w_ref[:, SSM_INNER + cols.start:SSM_INNER + cols.stop])
        y = _silu(_causal_conv(carry_ref, nh_ref, raw, cw_ref[:, cols], cols) + cb_ref[:, cols])
        if cols.stop <= SSM_INNER:
            xs_ref[:, cols] = y
        elif cols.stop <= SSM_INNER + SSM_GN:
            bm_ref[:, cols.start - SSM_INNER:cols.stop - SSM_INNER] = y.astype(BF16)
        else:
            lo = SSM_INNER + SSM_GN
            cm_ref[:, cols.start - lo:cols.stop - lo] = y.astype(BF16)


def _ssm_in(x, g, w, wdt, hist, cw, cb, dtb, alog, tm, cl):
    bsz, seq, _ = x.shape
    tok = lambda width: pl.BlockSpec((None, tm, width), lambda b, i: (b, i, 0))
    hist_spec = pl.BlockSpec((None, SSM_CONV - 1, SSM_CONV_CH), lambda b, i: (b, 0, 0))
    return pl.pallas_call(
        functools.partial(_ssm_in_kernel, tm, cl),
        grid=(bsz, seq // tm),
        in_specs=[tok(D_MODEL), _const_spec((1, D_MODEL)), _const_spec(w.shape), _const_spec((D_MODEL, LANES)),
                  hist_spec,
                  _const_spec((SSM_CONV, SSM_CONV_CH)), _const_spec((1, SSM_CONV_CH)), _const_spec((1, LANES)),
                  _const_spec((1, LANES))],
        out_specs=[tok(SSM_INNER), tok(SSM_INNER), tok(SSM_GN), tok(SSM_GN), tok(LANES), tok(LANES), hist_spec],
        out_shape=[jax.ShapeDtypeStruct((bsz, seq, SSM_INNER), BF16),
                   jax.ShapeDtypeStruct((bsz, seq, SSM_INNER), F32),
                   jax.ShapeDtypeStruct((bsz, seq, SSM_GN), BF16),
                   jax.ShapeDtypeStruct((bsz, seq, SSM_GN), BF16),
                   jax.ShapeDtypeStruct((bsz, seq, LANES), BF16),
                   jax.ShapeDtypeStruct((bsz, seq, LANES), BF16),
                   jax.ShapeDtypeStruct((bsz, SSM_CONV - 1, SSM_CONV_CH), F32)],
        scratch_shapes=[pltpu.VMEM((SUBLANES, SSM_CONV_CH), F32)],
        compiler_params=_params(),
        name="ssm_in_proj",
    )(x, g, w, wdt, hist, cw, cb, dtb, alog)


def _pad_rows(a, rows):
    if a.shape[0] == rows:
        return a
    return jnp.concatenate([a, jnp.zeros((rows - a.shape[0], a.shape[1]), a.dtype)], axis=0)


def _ssd_kernel(ts, cl, xs_ref, bm_ref, cm_ref, ta_ref, tdt_ref, h0_ref, dskip_ref,
                yo_ref, hout_ref, ht_ref, y_ref, e3_ref, aexp_ref, xdt_ref, st_ref, cb_ref, bmt_ref, w2_ref, rhs_ref):
    b = pl.program_id(0)
    i = pl.program_id(1)
    p = SSM_HEADDIM
    nc = ts // cl
    pairs = SSM_GROUP_W // LANES

    @pl.when((b == 0) & (i == 0))
    def _():
        kk = lax.broadcasted_iota(jnp.int32, (LANES, SSM_INNER), 0)
        cc = lax.broadcasted_iota(jnp.int32, (LANES, SSM_INNER), 1)
        hit = (kk % SSM_HEADS == cc // p) & (kk < 3 * SSM_HEADS)
        e3_ref[...] = jnp.where(hit, 1.0, 0.0).astype(BF16)

    @pl.when(i == 0)
    def _():
        ht_ref[...] = h0_ref[...].T

    lane = lax.broadcasted_iota(jnp.int32, (cl, LANES), 1)
    row = lax.broadcasted_iota(jnp.int32, (cl, LANES), 0)
    causal2 = (lane % p) <= row
    left = lane.astype(F32).astype(BF16) < p
    diag2 = (lane % p) == row
    groups = [(gi, slice(gi * SSM_GROUP_W, (gi + 1) * SSM_GROUP_W), slice(gi * SSM_STATE, (gi + 1) * SSM_STATE))
              for gi in range(SSM_GROUPS)]
    chunks = [(c, slice(c * cl, (c + 1) * cl)) for c in range(nc)]

    for gi, gcols, _ in groups:
        e3g = e3_ref[:, gcols]
        aexp_ref[:, gcols] = _dot(ta_ref[...], e3g)
        xdt_ref[:, gcols] = (xs_ref[:, gcols] * _dot(tdt_ref[...], e3g)).astype(BF16)

    for c, rows in chunks:
        for gi, _, scol in groups:
            bmp = _pad_rows(bm_ref[rows, scol], p)
            cbm = _dot_nt(cm_ref[rows, scol], bmp)
            cb_ref[c * SSM_GROUPS + gi] = jnp.concatenate([cbm, cbm], axis=1)
            bmt_ref[c * SSM_GROUPS + gi] = bmp.astype(F32).T.astype(BF16)

    for c, rows in chunks:
        for gi, _, _ in groups:
            for jj in range(pairs):
                idx = (c * SSM_GROUPS + gi) * pairs + jj
                cols = slice(gi * SSM_GROUP_W + jj * LANES, gi * SSM_GROUP_W + (jj + 1) * LANES)
                a_pair = aexp_ref[rows, cols]
                a_src = jnp.sum(jnp.where(diag2, a_pair, 0.0), axis=0, keepdims=True)
                seg = a_pair - a_src
                dec = jnp.exp(jnp.where(causal2, seg, -jnp.inf))
                w2_ref[idx] = (dec * cb_ref[c * SSM_GROUPS + gi]).astype(BF16)
                x2 = xdt_ref[rows, cols]
                top = _pad_rows(jnp.where(left, x2, jnp.zeros_like(x2)), p)
                bot = _pad_rows(jnp.where(left, jnp.zeros_like(x2), x2), p)
                rhs_ref[idx] = jnp.concatenate([top, bot], axis=0)

    for c, rows in chunks:
        for gi, _, _ in groups:
            for jj in range(pairs):
                idx = (c * SSM_GROUPS + gi) * pairs + jj
                cols = slice(gi * SSM_GROUP_W + jj * LANES, gi * SSM_GROUP_W + (jj + 1) * LANES)
                y_ref[rows, cols] = _dot(w2_ref[idx], rhs_ref[idx])

    for c, rows in chunks:
        for gi, gcols, _ in groups:
            a_g = aexp_ref[rows, gcols]
            xw = xdt_ref[rows, gcols] * jnp.exp(a_g[cl - 1:cl, :] - a_g).astype(BF16)
            st_ref[c, :, gcols] = _dot(bmt_ref[c * SSM_GROUPS + gi], _pad_rows(xw, p))

    for c, rows in chunks:
        for gi, gcols, scol in groups:
            hprev = ht_ref[:, gcols]
            a_g = aexp_ref[rows, gcols]
            y_off = _dot(cm_ref[rows, scol], hprev.astype(BF16)) * jnp.exp(a_g)
            y_ref[rows, gcols] = y_ref[rows, gcols] + y_off
            ht_ref[:, gcols] = hprev * jnp.exp(a_g[cl - 1:cl, :]) + st_ref[c, :, gcols]

    yo_ref[...] = (y_ref[...] + dskip_ref[...] * xs_ref[...]).astype(BF16)

    @pl.when(i == pl.num_programs(1) - 1)
    def _():
        hout_ref[...] = ht_ref[...].T


def _ssd(xs, bm, cm, ta, tdt, h0, dskip, ts, cl):
    bsz, seq, _ = xs.shape
    tok = lambda width: pl.BlockSpec((None, ts, width), lambda b, i: (b, i, 0))
    st_spec = pl.BlockSpec((None, SSM_INNER, SSM_STATE), lambda b, i: (b, 0, 0))
    nc = ts // cl
    n_cg = nc * SSM_GROUPS
    n_pairs = n_cg * (SSM_GROUP_W // LANES)
    return pl.pallas_call(
        functools.partial(_ssd_kernel, ts, cl),
        grid=(bsz, seq // ts),
        in_specs=[tok(SSM_INNER), tok(SSM_GN), tok(SSM_GN), tok(LANES), tok(LANES), st_spec,
                  _const_spec((1, SSM_INNER))],
        out_specs=[tok(SSM_INNER), st_spec],
        out_shape=[jax.ShapeDtypeStruct((bsz, seq, SSM_INNER), BF16),
                   jax.ShapeDtypeStruct((bsz, SSM_INNER, SSM_STATE), F32)],
        scratch_shapes=[pltpu.VMEM((SSM_STATE, SSM_INNER), F32), pltpu.VMEM((ts, SSM_INNER), F32),
                        pltpu.VMEM((LANES, SSM_INNER), BF16), pltpu.VMEM((ts, SSM_INNER), F32),
                        pltpu.VMEM((ts, SSM_INNER), BF16), pltpu.VMEM((nc, SSM_STATE, SSM_INNER), F32),
                        pltpu.VMEM((n_cg, cl, LANES), F32), pltpu.VMEM((n_cg, SSM_STATE, SSM_HEADDIM), BF16),
                        pltpu.VMEM((n_pairs, cl, LANES), BF16), pltpu.VMEM((n_pairs, LANES, LANES), BF16)],
        compiler_params=_params(),
        name="ssd_scan",
    )(xs, bm, cm, ta, tdt, h0, dskip)


def _layer0_mixers(x, pos0, sconv_hist, k_cache, v_cache, wts, tiles):
    t_in, t_att, c_att = tiles
    q, k, v, gated, new_sconv = _attn_in(x, wts["g"][0][0], wts["ab_w_in"], sconv_hist, wts["sconv_w"], pos0, t_in)
    masked = k_cache is None
    k_prev, v_prev = (k, v) if masked else (k_cache, v_cache)
    att = _attn(wts["sinks"], q, k, v, k_prev, v_prev, t_att, c_att, masked)
    bsz, seq, _ = x.shape
    new_k = k[:, seq - min(WINDOW, seq):].reshape(bsz, -1, A_KV_HEADS, HEAD_DIM)
    new_v = v[:, seq - min(WINDOW, seq):].reshape(bsz, -1, A_KV_HEADS, HEAD_DIM)
    return [att, gated], (new_k, new_v, new_sconv)


def _layer1_mixer(x, conv_hist, ssm_state, wts, tiles):
    t_sin, t_ssd, c_ssd = tiles
    bsz = x.shape[0]
    z, xs, bm, cm, ta, tdt, new_conv = _ssm_in(
        x, wts["g"][1][0], wts["ssm_w_in"], wts["w_dt"], conv_hist, wts["ssm_conv_w"],
        wts["ssm_conv_b"], wts["dt_bias"], wts["a_log"], t_sin, c_ssd)
    y, new_state = _ssd(xs, bm, cm, ta, tdt, ssm_state.reshape(bsz, SSM_INNER, SSM_STATE),
                        wts["d_skip"], t_ssd, c_ssd)
    return [y, z], (new_conv, new_state.reshape(bsz, SSM_HEADS, SSM_HEADDIM, SSM_STATE))


def kernel(x_prompt, x_sample, cache_attn_k, cache_attn_v, state_sconv, state_ssm_conv, state_ssm, norm_g,
           ab_w_in, ab_w_out, attn_sinks, sconv_w, ssm_w_in, ssm_conv_w, ssm_conv_b, ssm_dt_bias, ssm_a_log,
           ssm_d, ssm_norm_g, ssm_w_out, ffn_w_gate, ffn_w_up, ffn_w_down):
    w_in1 = ssm_w_in[0]
    wts = {
        "ab_w_in": ab_w_in[0].astype(BF16),
        "ab_w_out": ab_w_out[0].astype(BF16),
        "sinks": attn_sinks[0],
        "sconv_w": sconv_w[0],
        "ssm_w_in": w_in1.astype(BF16),
        "w_dt": jnp.tile(w_in1[:, SSM_INNER + SSM_CONV_CH:], (1, DT_REP)).astype(BF16),
        "ssm_conv_w": ssm_conv_w[0],
        "ssm_conv_b": ssm_conv_b[0].reshape(1, -1),
        "dt_bias": jnp.tile(ssm_dt_bias[0], DT_REP).reshape(1, -1),
        "a_log": jnp.tile(ssm_a_log[0], DT_REP).reshape(1, -1),
        "d_skip": jnp.repeat(ssm_d[0], SSM_HEADDIM).reshape(1, -1),
        "ssm_norm_g": ssm_norm_g[0].reshape(1, -1),
        "ssm_w_out": ssm_w_out[0].astype(BF16),
        "wg": ffn_w_gate.astype(BF16),
        "wu": ffn_w_up.astype(BF16),
        "wd": ffn_w_down.astype(BF16),
    }
    wts["g"] = [[norm_g[l, j].reshape(1, -1) for j in range(4)] for l in range(norm_g.shape[0])]
    g = wts["g"]
    bp = x_prompt.shape[0]
    bs, ls = x_sample.shape[0], x_sample.shape[1]
    t_ffn = 512

    mix_p, (kp, vp, scp) = _layer0_mixers(x_prompt, 0, jnp.zeros((bp, B_CONV - 1, B_WIDTH), F32), None, None,
                                          wts, (1024, 512, CHUNK))
    mix_s, (ks, vs, scs) = _layer0_mixers(x_sample, PAST_LEN, state_sconv[0],
                                          cache_attn_k[0].reshape(bs, -1, KV_WIDTH),
                                          cache_attn_v[0].reshape(bs, -1, KV_WIDTH), wts, (ls, ls, ls))
    xp, xs = _proj_ffn(mix_p, mix_s, None, wts["ab_w_out"], x_prompt, x_sample, g[0][1], g[0][2],
                       wts["wg"], wts["wu"], wts["wd"], 0, g[0][3], t_ffn)

    mix_p, (ccp, ssp) = _layer1_mixer(xp, jnp.zeros((bp, SSM_CONV - 1, SSM_CONV_CH), F32),
                                      jnp.zeros((bp, SSM_HEADS, SSM_HEADDIM, SSM_STATE), F32), wts, (512, 512, CHUNK))
    mix_s, (ccs, sss) = _layer1_mixer(xs, state_ssm_conv[0], state_ssm[0], wts, (ls, ls, ls))
    yp, ys = _proj_ffn(mix_p, mix_s, wts["ssm_norm_g"], wts["ssm_w_out"], xp, xs, g[1][1], g[1][2],
                       wts["wg"], wts["wu"], wts["wd"], 1, g[1][3], t_ffn)

    lead = lambda a: a[None]
    return (yp, ys, lead(kp), lead(vp), lead(scp), lead(ccp), lead(ssp),
            lead(ks), lead(vs), lead(scs), lead(ccs), lead(sss))
```

```python
import functools

import jax
import jax.numpy as jnp
from jax import lax
from jax.experimental import pallas as pl
from jax.experimental.pallas import tpu as pltpu

F32 = jnp.float32
BF16 = jnp.bfloat16

D_MODEL = 1024
CHUNK = 64
EPS = 1e-6
PAST_LEN = 4096

A_HEADS = 8
A_KV_HEADS = 2
A_GROUP = A_HEADS // A_KV_HEADS
HEAD_DIM = 64
A_WIDTH = A_HEADS * HEAD_DIM
KV_WIDTH = A_KV_HEADS * HEAD_DIM
WINDOW = 128
ROPE_DIM = HEAD_DIM // 4
ROPE_HALF = ROPE_DIM // 2
ROPE_THETA = 500000.0
ATTN_SCALE = HEAD_DIM ** -0.5

B_WIDTH = D_MODEL // 2
B_CONV = 3
AB_IN = A_WIDTH + 2 * KV_WIDTH + 3 * B_WIDTH

SSM_INNER = 2 * D_MODEL
SSM_HEADDIM = 64
SSM_HEADS = SSM_INNER // SSM_HEADDIM
SSM_GROUPS = 4
SSM_GROUP_W = SSM_INNER // SSM_GROUPS
SSM_STATE = 128
SSM_CONV = 4
SSM_GN = SSM_GROUPS * SSM_STATE
SSM_CONV_CH = SSM_INNER + 2 * SSM_GN

D_FF = -(-8 * D_MODEL // (3 * 256)) * 256

LANES = 128
SUBLANES = 8
MXU_DIM = 256
DT_REP = LANES // SSM_HEADS
VMEM_LIMIT = 56 * 1024 * 1024
COL_CHUNK = MXU_DIM
FF_CHUNKS = (6 * MXU_DIM, 5 * MXU_DIM)
assert sum(FF_CHUNKS) == D_FF


def _rms(x, g):
    return x * lax.rsqrt(jnp.mean(x * x, axis=-1, keepdims=True) + EPS) * g


def _dot(a, b):
    return jnp.dot(a, b, preferred_element_type=F32)


def _dot_nt(a, b):
    return lax.dot_general(a, b, (((1,), (1,)), ((), ())), preferred_element_type=F32)


def _silu(x):
    return x * jax.nn.sigmoid(x)


def _const_spec(shape):
    return pl.BlockSpec(shape, lambda *_: (0,) * len(shape), pipeline_mode=pl.Buffered(1))


def _layer_spec(shape, layer):
    return pl.BlockSpec((None,) + shape, lambda *_: (layer,) + (0,) * len(shape), pipeline_mode=pl.Buffered(1))


def _params():
    return pltpu.CompilerParams(dimension_semantics=("arbitrary", "arbitrary"),
                                vmem_limit_bytes=VMEM_LIMIT)


def _causal_conv(carry_ref, nh_ref, u, cw, cols):
    width = cw.shape[0]
    rows, ch = u.shape
    nseq = carry_ref.shape[0]
    u4 = u.reshape(nseq, rows // (nseq * SUBLANES), SUBLANES, ch)
    ext4 = jnp.concatenate([carry_ref[:, :, cols].reshape(nseq, 1, SUBLANES, ch), u4], axis=1)
    sub = lax.broadcasted_iota(jnp.int32, (1, 1, SUBLANES, ch), 2)

    def shift(above, here, k):
        return pltpu.roll(jnp.where(sub >= SUBLANES - k, above, here), k, 2)

    if width == 4:
        s1 = shift(jnp.concatenate([ext4[:, :1], ext4[:, :-1]], axis=1), ext4, 1)
        b = s1 * cw[0:1] + ext4 * cw[1:2]
        y4 = shift(b[:, :-1], b[:, 1:], 2) + (s1[:, 1:] * cw[2:3] + u4 * cw[3:4])
    else:
        y4 = None
        for j in range(width - 1):
            term = shift(ext4[:, :-1], u4, width - 1 - j) * cw[j:j + 1]
            y4 = term if y4 is None else y4 + term
        y4 = y4 + u4 * cw[width - 1:width]
    carry_ref[:, :, cols] = u4[:, -1]
    nh_ref[:, :, cols] = u4[:, -1, SUBLANES - (width - 1):]
    return y4.reshape(rows, ch)


def _rope_freq(shape):
    lane = lax.broadcasted_iota(jnp.int32, shape, 1)
    d = lane % HEAD_DIM
    expo = -((d % ROPE_HALF).astype(F32)) / ROPE_HALF
    freq = jnp.power(jnp.full(shape, ROPE_THETA, F32), expo)
    return jnp.where(d < ROPE_DIM, freq, 0.0), d


def _attn_in_kernel(pos0, tm, seq_rows, x_ref, g_ref, w_ref, hist_ref, cw_ref,
                    q_ref, k_ref, v_ref, gated_ref, nh_ref,
                    cr_ref, sr_ref, carry_ref, p_ref):
    b = pl.program_id(0)
    i = pl.program_id(1)

    @pl.when((b == 0) & (i == 0))
    def _():
        freq, _ = _rope_freq((tm, LANES))
        ang = (lax.broadcasted_iota(jnp.int32, (tm, LANES), 0) % seq_rows).astype(F32) * freq
        cr_ref[...] = jnp.cos(ang)
        sr_ref[...] = jnp.sin(ang)

    @pl.when(i == 0)
    def _():
        carry_ref[:, SUBLANES - (B_CONV - 1):, :] = hist_ref[...]

    freq1, d1 = _rope_freq((1, LANES))
    base = (pos0 + i * tm).astype(F32) * freq1
    cb = jnp.cos(base)
    sb = jnp.sin(base)
    cr = cr_ref[...]
    sr = sr_ref[...]
    cos_t = cr * cb - sr * sb
    sin_t = sr * cb + cr * sb
    m_lo = jnp.where(d1 < ROPE_HALF, -1.0, 0.0)
    m_hi = jnp.where((d1 >= ROPE_HALF) & (d1 < ROPE_DIM), 1.0, 0.0)

    def rope(t):
        partner = pltpu.roll(t, LANES - ROPE_HALF, 1) * m_lo + pltpu.roll(t, ROPE_HALF, 1) * m_hi
        return t * cos_t + partner * sin_t

    h = _rms(x_ref[...], g_ref[...]).astype(BF16)
    o1 = A_WIDTH
    o2 = o1 + KV_WIDTH
    o3 = o2 + KV_WIDTH
    o4 = o3 + B_WIDTH
    o5 = o4 + B_WIDTH

    for n in range(AB_IN // MXU_DIM):
        cols = slice(n * MXU_DIM, (n + 1) * MXU_DIM)
        p_ref[:, cols] = _dot(h, w_ref[:, cols])

    u = p_ref[:, o4:o5] * p_ref[:, o5:]
    y = _causal_conv(carry_ref, nh_ref, u, cw_ref[...], slice(None))
    gated_ref[...] = (p_ref[:, o3:o4] * y).astype(BF16)
    k_ref[...] = rope(p_ref[:, o1:o2])
    v_ref[...] = p_ref[:, o2:o3]
    for s in range(A_WIDTH // LANES):
        q_ref[:, s * LANES:(s + 1) * LANES] = (rope(p_ref[:, s * LANES:(s + 1) * LANES]) * ATTN_SCALE).astype(BF16)


def _tile_rows(x, tm):
    bsz, seq, d = x.shape
    nseq = max(tm // seq, 1)
    return x.reshape(bsz // nseq, nseq * seq, d), nseq


def _attn_in(x, g, w, hist, cw, pos0, tm):
    bsz, seq, _ = x.shape
    xt, nseq = _tile_rows(x, tm)
    groups, rows, _ = xt.shape
    tok = lambda width: pl.BlockSpec((None, tm, width), lambda b, i: (b, i, 0))
    hist_spec = pl.BlockSpec((nseq, B_CONV - 1, B_WIDTH), lambda b, i: (b, 0, 0))
    outs = pl.pallas_call(
        functools.partial(_attn_in_kernel, pos0, tm, tm // nseq),
        grid=(groups, rows // tm),
        in_specs=[tok(D_MODEL), _const_spec((1, D_MODEL)), _const_spec((D_MODEL, AB_IN)), hist_spec,
                  _const_spec((B_CONV, B_WIDTH))],
        out_specs=[tok(A_WIDTH), tok(KV_WIDTH), tok(KV_WIDTH), tok(B_WIDTH), hist_spec],
        out_shape=[jax.ShapeDtypeStruct((groups, rows, A_WIDTH), BF16),
                   jax.ShapeDtypeStruct((groups, rows, KV_WIDTH), F32),
                   jax.ShapeDtypeStruct((groups, rows, KV_WIDTH), F32),
                   jax.ShapeDtypeStruct((groups, rows, B_WIDTH), BF16),
                   jax.ShapeDtypeStruct((bsz, B_CONV - 1, B_WIDTH), F32)],
        scratch_shapes=[pltpu.VMEM((tm, LANES), F32), pltpu.VMEM((tm, LANES), F32),
                        pltpu.VMEM((nseq, SUBLANES, B_WIDTH), F32), pltpu.VMEM((tm, AB_IN), F32)],
        compiler_params=_params(),
        name="attn_in_proj",
    )(xt, g, w, hist, cw)
    return [o.reshape(bsz, seq, o.shape[-1]) for o in outs[:-1]] + [outs[-1]]


def _attn_kernel(tq, cq, masked, sinks_ref, q_ref, kc_ref, vc_ref, kp_ref, vp_ref, att_ref,
                 kk_ref, vx_ref, s_ref, e_ref):
    i = pl.program_id(1)
    nk = WINDOW + cq
    nkp = MXU_DIM
    rows_kv = WINDOW + tq
    hd = HEAD_DIM

    kk_ref[0:WINDOW, :] = kp_ref[...].astype(BF16)
    kk_ref[WINDOW:rows_kv, :] = kc_ref[...].astype(BF16)
    kk_ref[rows_kv:, :] = jnp.zeros((kk_ref.shape[0] - rows_kv, KV_WIDTH), BF16)
    vall = jnp.concatenate([vp_ref[...], vc_ref[...]], axis=0)
    vx_ref[0:rows_kv, :] = jnp.concatenate([vall[:, :hd], vall[:, :hd], vall[:, hd:], vall[:, hd:]], axis=1)
    vx_ref[rows_kv:, :] = jnp.zeros((vx_ref.shape[0] - rows_kv, 2 * LANES), F32)

    col = lax.broadcasted_iota(jnp.int32, (A_GROUP * cq, nkp), 1)
    krow = lax.broadcasted_iota(jnp.int32, (nkp, LANES), 0)
    ones = jnp.ones((nkp, LANES), BF16)
    bodies = [(c, kh) for c in range(tq // cq) for kh in range(A_KV_HEADS)]
    fills = []
    for kh in range(A_KV_HEADS):
        sink = jnp.concatenate(
            [jnp.full((cq, nkp), sinks_ref[kh * A_GROUP + g], F32) for g in range(A_GROUP)], axis=0)
        fills.append(jnp.where(col == nk, sink, -jnp.inf))

    for n, (c, kh) in enumerate(bodies):
        r0 = c * cq
        heads = [kh * A_GROUP + g for g in range(A_GROUP)]
        kt = kk_ref[r0:r0 + nkp, kh * hd:(kh + 1) * hd]
        qs = jnp.concatenate([q_ref[r0:r0 + cq, hh * hd:(hh + 1) * hd] for hh in heads], axis=0)
        s = _dot_nt(qs, kt)
        valid = col < nk
        if masked and r0 < WINDOW:
            valid = valid & ((col >= WINDOW - r0) | (i > 0))
        s_ref[n] = jnp.where(valid, s, fills[kh])

    for n in range(len(bodies)):
        s = s_ref[n]
        m = jnp.max(s, axis=-1, keepdims=True)
        e_ref[n] = jnp.exp(s - m).astype(BF16)

    for n, (c, kh) in enumerate(bodies):
        r0 = c * cq
        vt = jnp.where(krow == nk, 0.0, vx_ref[r0:r0 + nkp, kh * LANES:(kh + 1) * LANES]).astype(BF16)
        e = e_ref[n]
        o = _dot(e, vt) / _dot(e, ones)
        for g in range(A_GROUP):
            hh = kh * A_GROUP + g
            lanes = slice((hh % 2) * hd, (hh % 2 + 1) * hd)
            att_ref[r0:r0 + cq, hh * hd:(hh + 1) * hd] = o[g * cq:(g + 1) * cq, lanes].astype(BF16)


def _attn(sinks, q, k, v, k_prev, v_prev, tq, cq, masked):
    bsz, seq, _ = q.shape
    grid = (bsz, seq // tq)
    tok = lambda width: pl.BlockSpec((None, tq, width), lambda b, i: (b, i, 0))
    if masked:
        blocks_per_tile = tq // WINDOW
        prev = pl.BlockSpec((None, WINDOW, KV_WIDTH),
                            lambda b, i: (b, jnp.maximum(i * blocks_per_tile - 1, 0), 0))
    else:
        prev = pl.BlockSpec((None, WINDOW, KV_WIDTH), lambda b, i: (b, 0, 0))
    n_bodies = (tq // cq) * A_KV_HEADS
    kv_rows = (tq // cq - 1) * cq + MXU_DIM
    return pl.pallas_call(
        functools.partial(_attn_kernel, tq, cq, masked),
        grid=grid,
        in_specs=[pl.BlockSpec(memory_space=pltpu.SMEM),
                  tok(A_WIDTH), tok(KV_WIDTH), tok(KV_WIDTH), prev, prev],
        out_specs=tok(A_WIDTH),
        out_shape=jax.ShapeDtypeStruct((bsz, seq, A_WIDTH), BF16),
        scratch_shapes=[pltpu.VMEM((kv_rows, KV_WIDTH), BF16), pltpu.VMEM((kv_rows, 2 * LANES), F32),
                        pltpu.VMEM((n_bodies, A_GROUP * cq, MXU_DIM), F32),
                        pltpu.VMEM((n_bodies, A_GROUP * cq, MXU_DIM), BF16)],
        compiler_params=_params(),
        name="band_attention",
    )(sinks, q, k, v, k_prev, v_prev)


def _proj_ffn_kernel(n_in, gate, n_tiles, *refs):
    long_in, short_in = refs[:n_in], refs[n_in:2 * n_in]
    refs = refs[2 * n_in:]
    ng_ref = None
    if gate:
        ng_ref, refs = refs[0], refs[1:]
    wo_ref, xl_ref, xs_ref, g1_ref, g2_ref, wg_ref, wu_ref, wd_ref, g3_ref, ol_ref, os_ref = refs
    t = pl.program_id(0)

    def tile(ins, x_ref, o_ref):
        acc = None
        if gate:
            y_ref, z_ref = ins
            for gi in range(SSM_GROUPS):
                gcols = slice(gi * SSM_GROUP_W, (gi + 1) * SSM_GROUP_W)
                gg = y_ref[:, gcols].astype(F32) * z_ref[:, gcols].astype(F32)
                gg = gg * lax.rsqrt(jnp.mean(gg * gg, axis=-1, keepdims=True) + EPS)
                part = _dot((gg * ng_ref[:, gcols]).astype(BF16), wo_ref[gcols, :])
                acc = part if acc is None else acc + part
        else:
            off = 0
            for a_ref in ins:
                kdim = a_ref.shape[-1]
                part = _dot(a_ref[...], wo_ref[off:off + kdim, :])
                acc = part if acc is None else acc + part
                off += kdim

        x = x_ref[...] + _rms(acc, g1_ref[...])
        h = _rms(x, g2_ref[...]).astype(BF16)
        f = None
        off = 0
        for width in FF_CHUNKS:
            fc = slice(off, off + width)
            m = (_silu(_dot(h, wg_ref[:, fc])) * _dot(h, wu_ref[:, fc])).astype(BF16)
            part = _dot(m, wd_ref[fc, :])
            f = part if f is None else f + part
            off += width
        o_ref[...] = x + _rms(f, g3_ref[...])

    pl.when(t < n_tiles)(lambda: tile(long_in, xl_ref, ol_ref))
    pl.when(t == n_tiles)(lambda: tile(short_in, xs_ref, os_ref))


def _proj_ffn(mix_long, mix_short, ng, wo, x_long, x_short, g1, g2, wg, wu, wd, layer, g3, tm):
    bsz, seq, _ = x_long.shape
    tiles_per_seq = seq // tm
    n_tiles = bsz * tiles_per_seq
    n_short = x_short.shape[0] * x_short.shape[1]
    flat = lambda a: a.reshape(1, n_short, a.shape[-1])
    tile_idx = lambda t: jnp.minimum(t, n_tiles - 1)
    ltok = lambda width: pl.BlockSpec((None, tm, width),
                                      lambda t: (tile_idx(t) // tiles_per_seq, tile_idx(t) % tiles_per_seq, 0))
    stok = lambda width: pl.BlockSpec((None, n_short, width), lambda t: (0, 0, 0))
    gate = ng is not None
    ng_args, ng_specs = ([ng], [_const_spec((1, SSM_INNER))]) if gate else ([], [])
    out_long, out_short = pl.pallas_call(
        functools.partial(_proj_ffn_kernel, len(mix_long), gate, n_tiles),
        grid=(n_tiles + 1,),
        in_specs=[ltok(a.shape[-1]) for a in mix_long] + [stok(a.shape[-1]) for a in mix_short] + ng_specs + [
            _const_spec(wo.shape), ltok(D_MODEL), stok(D_MODEL), _const_spec((1, D_MODEL)), _const_spec((1, D_MODEL)),
            _layer_spec((D_MODEL, D_FF), layer), _layer_spec((D_MODEL, D_FF), layer),
            _layer_spec((D_FF, D_MODEL), layer), _const_spec((1, D_MODEL))],
        out_specs=[ltok(D_MODEL), stok(D_MODEL)],
        out_shape=[jax.ShapeDtypeStruct(x_long.shape, F32), jax.ShapeDtypeStruct((1, n_short, D_MODEL), F32)],
        compiler_params=pltpu.CompilerParams(dimension_semantics=("arbitrary",), vmem_limit_bytes=VMEM_LIMIT),
        name="out_proj_ffn",
    )(*mix_long, *[flat(a) for a in mix_short], *ng_args, wo, x_long, flat(x_short), g1, g2, wg, wu, wd, g3)
    return out_long, out_short.reshape(x_short.shape)


def _split3(v):
    t1 = v.astype(BF16)
    r = v - t1.astype(F32)
    t2 = r.astype(BF16)
    t3 = (r - t2.astype(F32)).astype(BF16)
    return t1, t2, t3


def _pack3(v):
    t1, t2, t3 = _split3(v)
    lane = lax.broadcasted_iota(jnp.int32, v.shape, 1)
    return jnp.where(lane < SSM_HEADS, t1,
                     jnp.where(lane < 2 * SSM_HEADS, t2,
                               jnp.where(lane < 3 * SSM_HEADS, t3, jnp.zeros_like(t3))))


def _ssm_in_kernel(tm, cl, x_ref, g_ref, w_ref, wdt_ref, hist_ref, cw_ref, cb_ref, dtb_ref, alog_ref,
                   z_ref, xs_ref, bm_ref, cm_ref, ta_ref, tdt_ref, nh_ref, carry_ref):
    i = pl.program_id(1)

    @pl.when(i == 0)
    def _():
        carry_ref[:, SUBLANES - (SSM_CONV - 1):, :] = hist_ref[...]

    h = _rms(x_ref[...], g_ref[...]).astype(BF16)

    dt = jax.nn.softplus(_dot(h, wdt_ref[...]) + dtb_ref[...])
    tdt_ref[...] = _pack3(dt)
    d1, d2, d3 = _split3(dt * (-jnp.exp(alog_ref[...])))
    tcol = lax.broadcasted_iota(jnp.int32, (cl, 3 * cl), 1) % cl
    trow = lax.broadcasted_iota(jnp.int32, (cl, 3 * cl), 0)
    tril = jnp.where(tcol <= trow, 1.0, 0.0).astype(BF16)
    acum = []
    for c in range(tm // cl):
        rows = slice(c * cl, (c + 1) * cl)
        acum.append(_dot(tril, jnp.concatenate([d1[rows], d2[rows], d3[rows]], axis=0)))
    ta_ref[...] = _pack3(jnp.concatenate(acum, axis=0))

    for n in range(SSM_INNER // COL_CHUNK):
        cols = slice(n * COL_CHUNK, (n + 1) * COL_CHUNK)
        z_ref[:, cols] = _silu(_dot(h, w_ref[:, cols])).astype(BF16)

    for n in range(SSM_CONV_CH // COL_CHUNK):
        cols = slice(n * COL_CHUNK, (n + 1) * COL_CHUNK)
        raw = _dot(h, w_ref[:, SSM_INNER + cols.start:SSM_INNER + cols.stop])
        y = _silu(_causal_conv(carry_ref, nh_ref, raw, cw_ref[:, cols], cols) + cb_ref[:, cols])
        if cols.stop <= SSM_INNER:
            xs_ref[:, cols] = y
        elif cols.stop <= SSM_INNER + SSM_GN:
            bm_ref[:, cols.start - SSM_INNER:cols.stop - SSM_INNER] = y.astype(BF16)
        else:
            lo = SSM_INNER + SSM_GN
            cm_ref[:, cols.start - lo:cols.stop - lo] = y.astype(BF16)


def _ssm_in(x, g, w, wdt, hist, cw, cb, dtb, alog, tm, cl):
    bsz, seq, _ = x.shape
    xt, nseq = _tile_rows(x, tm)
    groups, rows, _ = xt.shape
    tok = lambda width: pl.BlockSpec((None, tm, width), lambda b, i: (b, i, 0))
    hist_spec = pl.BlockSpec((nseq, SSM_CONV - 1, SSM_CONV_CH), lambda b, i: (b, 0, 0))
    outs = pl.pallas_call(
        functools.partial(_ssm_in_kernel, tm, cl),
        grid=(groups, rows // tm),
        in_specs=[tok(D_MODEL), _const_spec((1, D_MODEL)), _const_spec(w.shape), _const_spec((D_MODEL, LANES)),
                  hist_spec,
                  _const_spec((SSM_CONV, SSM_CONV_CH)), _const_spec((1, SSM_CONV_CH)), _const_spec((1, LANES)),
                  _const_spec((1, LANES))],
        out_specs=[tok(SSM_INNER), tok(SSM_INNER), tok(SSM_GN), tok(SSM_GN), tok(LANES), tok(LANES), hist_spec],
        out_shape=[jax.ShapeDtypeStruct((groups, rows, SSM_INNER), BF16),
                   jax.ShapeDtypeStruct((groups, rows, SSM_INNER), F32),
                   jax.ShapeDtypeStruct((groups, rows, SSM_GN), BF16),
                   jax.ShapeDtypeStruct((groups, rows, SSM_GN), BF16),
                   jax.ShapeDtypeStruct((groups, rows, LANES), BF16),
                   jax.ShapeDtypeStruct((groups, rows, LANES), BF16),
                   jax.ShapeDtypeStruct((bsz, SSM_CONV - 1, SSM_CONV_CH), F32)],
        scratch_shapes=[pltpu.VMEM((nseq, SUBLANES, SSM_CONV_CH), F32)],
        compiler_params=_params(),
        name="ssm_in_proj",
    )(xt, g, w, wdt, hist, cw, cb, dtb, alog)
    return [o.reshape(bsz, seq, o.shape[-1]) for o in outs[:-1]] + [outs[-1]]


def _pad_rows(a, rows):
    if a.shape[0] == rows:
        return a
    return jnp.concatenate([a, jnp.zeros((rows - a.shape[0], a.shape[1]), a.dtype)], axis=0)


def _ssd_kernel(ts, cl, xs_ref, bm_ref, cm_ref, ta_ref, tdt_ref, h0_ref, dskip_ref,
                yo_ref, hout_ref, ht_ref, y_ref, e3_ref, aexp_ref, xdt_ref, st_ref, cb_ref, bmt_ref, w2_ref, rhs_ref):
    b = pl.program_id(0)
    i = pl.program_id(1)
    p = SSM_HEADDIM
    nc = ts // cl
    pairs = SSM_GROUP_W // LANES

    @pl.when((b == 0) & (i == 0))
    def _():
        kk = lax.broadcasted_iota(jnp.int32, (LANES, SSM_INNER), 0)
        cc = lax.broadcasted_iota(jnp.int32, (LANES, SSM_INNER), 1)
        hit = (kk % SSM_HEADS == cc // p) & (kk < 3 * SSM_HEADS)
        e3_ref[...] = jnp.where(hit, 1.0, 0.0).astype(BF16)

    @pl.when(i == 0)
    def _():
        ht_ref[...] = h0_ref[...].T

    lane = lax.broadcasted_iota(jnp.int32, (cl, LANES), 1)
    row = lax.broadcasted_iota(jnp.int32, (cl, LANES), 0)
    causal2 = (lane % p) <= row
    left = lane.astype(F32).astype(BF16) < p
    diag2 = (lane % p) == row
    groups = [(gi, slice(gi * SSM_GROUP_W, (gi + 1) * SSM_GROUP_W), slice(gi * SSM_STATE, (gi + 1) * SSM_STATE))
              for gi in range(SSM_GROUPS)]
    chunks = [(c, slice(c * cl, (c + 1) * cl)) for c in range(nc)]

    for gi, gcols, _ in groups:
        e3g = e3_ref[:, gcols]
        aexp_ref[:, gcols] = _dot(ta_ref[...], e3g)
        xdt_ref[:, gcols] = (xs_ref[:, gcols] * _dot(tdt_ref[...], e3g)).astype(BF16)

    for c, rows in chunks:
        for gi, _, scol in groups:
            bmp = _pad_rows(bm_ref[rows, scol], p)
            cbm = _dot_nt(cm_ref[rows, scol], bmp)
            cb_ref[c * SSM_GROUPS + gi] = jnp.concatenate([cbm, cbm], axis=1)
            bmt_ref[c * SSM_GROUPS + gi] = bmp.astype(F32).T.astype(BF16)

    for c, rows in chunks:
        for gi, _, _ in groups:
            for jj in range(pairs):
                idx = (c * SSM_GROUPS + gi) * pairs + jj
                cols = slice(gi * SSM_GROUP_W + jj * LANES, gi * SSM_GROUP_W + (jj + 1) * LANES)
                a_pair = aexp_ref[rows, cols]
                a_src = jnp.sum(jnp.where(diag2, a_pair, 0.0), axis=0, keepdims=True)
                seg = a_pair - a_src
                dec = jnp.exp(jnp.where(causal2, seg, -jnp.inf))
                w2_ref[idx] = (dec * cb_ref[c * SSM_GROUPS + gi]).astype(BF16)
                x2 = xdt_ref[rows, cols]
                top = _pad_rows(jnp.where(left, x2, jnp.zeros_like(x2)), p)
                bot = _pad_rows(jnp.where(left, jnp.zeros_like(x2), x2), p)
                rhs_ref[idx] = jnp.concatenate([top, bot], axis=0)

    for c, rows in chunks:
        for gi, _, _ in groups:
            for jj in range(pairs):
                idx = (c * SSM_GROUPS + gi) * pairs + jj
                cols = slice(gi * SSM_GROUP_W + jj * LANES, gi * SSM_GROUP_W + (jj + 1) * LANES)
                y_ref[rows, cols] = _dot(w2_ref[idx], rhs_ref[idx])

    for c, rows in chunks:
        for gi, gcols, _ in groups:
            a_g = aexp_ref[rows, gcols]
            xw = xdt_ref[rows, gcols] * jnp.exp(a_g[cl - 1:cl, :] - a_g).astype(BF16)
            st_ref[c, :, gcols] = _dot(bmt_ref[c * SSM_GROUPS + gi], _pad_rows(xw, p))

    for c, rows in chunks:
        for gi, gcols, scol in groups:
            hprev = ht_ref[:, gcols]
            a_g = aexp_ref[rows, gcols]
            y_off = _dot(cm_ref[rows, scol], hprev.astype(BF16)) * jnp.exp(a_g)
            y_ref[rows, gcols] = y_ref[rows, gcols] + y_off
            ht_ref[:, gcols] = hprev * jnp.exp(a_g[cl - 1:cl, :]) + st_ref[c, :, gcols]

    yo_ref[...] = (y_ref[...] + dskip_ref[...] * xs_ref[...]).astype(BF16)

    @pl.when(i == pl.num_programs(1) - 1)
    def _():
        hout_ref[...] = ht_ref[...].T


def _ssd(xs, bm, cm, ta, tdt, h0, dskip, ts, cl):
    bsz, seq, _ = xs.shape
    tok = lambda width: pl.BlockSpec((None, ts, width), lambda b, i: (b, i, 0))
    st_spec = pl.BlockSpec((None, SSM_INNER, SSM_STATE), lambda b, i: (b, 0, 0))
    nc = ts // cl
    n_cg = nc * SSM_GROUPS
    n_pairs = n_cg * (SSM_GROUP_W // LANES)
    return pl.pallas_call(
        functools.partial(_ssd_kernel, ts, cl),
        grid=(bsz, seq // ts),
        in_specs=[tok(SSM_INNER), tok(SSM_GN), tok(SSM_GN), tok(LANES), tok(LANES), st_spec,
                  _const_spec((1, SSM_INNER))],
        out_specs=[tok(SSM_INNER), st_spec],
        out_shape=[jax.ShapeDtypeStruct((bsz, seq, SSM_INNER), BF16),
                   jax.ShapeDtypeStruct((bsz, SSM_INNER, SSM_STATE), F32)],
        scratch_shapes=[pltpu.VMEM((SSM_STATE, SSM_INNER), F32), pltpu.VMEM((ts, SSM_INNER), F32),
                        pltpu.VMEM((LANES, SSM_INNER), BF16), pltpu.VMEM((ts, SSM_INNER), F32),
                        pltpu.VMEM((ts, SSM_INNER), BF16), pltpu.VMEM((nc, SSM_STATE, SSM_INNER), F32),
                        pltpu.VMEM((n_cg, cl, LANES), F32), pltpu.VMEM((n_cg, SSM_STATE, SSM_HEADDIM), BF16),
                        pltpu.VMEM((n_pairs, cl, LANES), BF16), pltpu.VMEM((n_pairs, LANES, LANES), BF16)],
        compiler_params=_params(),
        name="ssd_scan",
    )(xs, bm, cm, ta, tdt, h0, dskip)


def _layer0_mixers(x, pos0, sconv_hist, k_cache, v_cache, wts, tiles):
    t_in, t_att, c_att = tiles
    q, k, v, gated, new_sconv = _attn_in(x, wts["g"][0][0], wts["ab_w_in"], sconv_hist, wts["sconv_w"], pos0, t_in)
    masked = k_cache is None
    k_prev, v_prev = (k, v) if masked else (k_cache, v_cache)
    att = _attn(wts["sinks"], q, k, v, k_prev, v_prev, t_att, c_att, masked)
    bsz, seq, _ = x.shape
    new_k = k[:, seq - min(WINDOW, seq):].reshape(bsz, -1, A_KV_HEADS, HEAD_DIM)
    new_v = v[:, seq - min(WINDOW, seq):].reshape(bsz, -1, A_KV_HEADS, HEAD_DIM)
    return [att, gated], (new_k, new_v, new_sconv)


def _layer1_mixer(x, conv_hist, ssm_state, wts, tiles):
    t_sin, t_ssd, c_ssd = tiles
    bsz = x.shape[0]
    z, xs, bm, cm, ta, tdt, new_conv = _ssm_in(
        x, wts["g"][1][0], wts["ssm_w_in"], wts["w_dt"], conv_hist, wts["ssm_conv_w"],
        wts["ssm_conv_b"], wts["dt_bias"], wts["a_log"], t_sin, c_ssd)
    y, new_state = _ssd(xs, bm, cm, ta, tdt, ssm_state.reshape(bsz, SSM_INNER, SSM_STATE),
                        wts["d_skip"], t_ssd, c_ssd)
    return [y, z], (new_conv, new_state.reshape(bsz, SSM_HEADS, SSM_HEADDIM, SSM_STATE))


def kernel(x_prompt, x_sample, cache_attn_k, cache_attn_v, state_sconv, state_ssm_conv, state_ssm, norm_g,
           ab_w_in, ab_w_out, attn_sinks, sconv_w, ssm_w_in, ssm_conv_w, ssm_conv_b, ssm_dt_bias, ssm_a_log,
           ssm_d, ssm_norm_g, ssm_w_out, ffn_w_gate, ffn_w_up, ffn_w_down):
    w_in1 = ssm_w_in[0]
    wts = {
        "ab_w_in": ab_w_in[0].astype(BF16),
        "ab_w_out": ab_w_out[0].astype(BF16),
        "sinks": attn_sinks[0],
        "sconv_w": sconv_w[0],
        "ssm_w_in": w_in1.astype(BF16),
        "w_dt": jnp.tile(w_in1[:, SSM_INNER + SSM_CONV_CH:], (1, DT_REP)).astype(BF16),
        "ssm_conv_w": ssm_conv_w[0],
        "ssm_conv_b": ssm_conv_b[0].reshape(1, -1),
        "dt_bias": jnp.tile(ssm_dt_bias[0], DT_REP).reshape(1, -1),
        "a_log": jnp.tile(ssm_a_log[0], DT_REP).reshape(1, -1),
        "d_skip": jnp.repeat(ssm_d[0], SSM_HEADDIM).reshape(1, -1),
        "ssm_norm_g": ssm_norm_g[0].reshape(1, -1),
        "ssm_w_out": ssm_w_out[0].astype(BF16),
        "wg": ffn_w_gate.astype(BF16),
        "wu": ffn_w_up.astype(BF16),
        "wd": ffn_w_down.astype(BF16),
    }
    wts["g"] = [[norm_g[l, j].reshape(1, -1) for j in range(4)] for l in range(norm_g.shape[0])]
    g = wts["g"]
    bp = x_prompt.shape[0]
    bs, ls = x_sample.shape[0], x_sample.shape[1]
    t_ffn = 512

    mix_p, (kp, vp, scp) = _layer0_mixers(x_prompt, 0, jnp.zeros((bp, B_CONV - 1, B_WIDTH), F32), None, None,
                                          wts, (1024, 512, CHUNK))
    mix_s, (ks, vs, scs) = _layer0_mixers(x_sample, PAST_LEN, state_sconv[0],
                                          cache_attn_k[0].reshape(bs, -1, KV_WIDTH),
                                          cache_attn_v[0].reshape(bs, -1, KV_WIDTH), wts, (bs * ls, ls, ls))
    xp, xs = _proj_ffn(mix_p, mix_s, None, wts["ab_w_out"], x_prompt, x_sample, g[0][1], g[0][2],
                       wts["wg"], wts["wu"], wts["wd"], 0, g[0][3], t_ffn)

    mix_p, (ccp, ssp) = _layer1_mixer(xp, jnp.zeros((bp, SSM_CONV - 1, SSM_CONV_CH), F32),
                                      jnp.zeros((bp, SSM_HEADS, SSM_HEADDIM, SSM_STATE), F32), wts, (512, 512, CHUNK))
    mix_s, (ccs, sss) = _layer1_mixer(xs, state_ssm_conv[0], state_ssm[0], wts, (bs * ls, ls, ls))
    yp, ys = _proj_ffn(mix_p, mix_s, wts["ssm_norm_g"], wts["ssm_w_out"], xp, xs, g[1][1], g[1][2],
                       wts["wg"], wts["wu"], wts["wd"], 1, g[1][3], t_ffn)

    lead = lambda a: a[None]
    return (yp, ys, lead(kp), lead(vp), lead(scp), lead(ccp), lead(ssp),
            lead(ks), lead(vs), lead(scs), lead(ccs), lead(sss))
```

```python
import functools

import jax
import jax.numpy as jnp
from jax import lax
from jax.experimental import pallas as pl
from jax.experimental.pallas import tpu as pltpu

F32 = jnp.float32
BF16 = jnp.bfloat16

D_MODEL = 1024
CHUNK = 64
EPS = 1e-6
PAST_LEN = 4096

A_HEADS = 8
A_KV_HEADS = 2
A_GROUP = A_HEADS // A_KV_HEADS
HEAD_DIM = 64
A_WIDTH = A_HEADS * HEAD_DIM
KV_WIDTH = A_KV_HEADS * HEAD_DIM
WINDOW = 128
ROPE_DIM = HEAD_DIM // 4
ROPE_HALF = ROPE_DIM // 2
ROPE_THETA = 500000.0
ATTN_SCALE = HEAD_DIM ** -0.5

B_WIDTH = D_MODEL // 2
B_CONV = 3
AB_IN = A_WIDTH + 2 * KV_WIDTH + 3 * B_WIDTH

SSM_INNER = 2 * D_MODEL
SSM_HEADDIM = 64
SSM_HEADS = SSM_INNER // SSM_HEADDIM
SSM_GROUPS = 4
SSM_GROUP_W = SSM_INNER // SSM_GROUPS
SSM_STATE = 128
SSM_CONV = 4
SSM_GN = SSM_GROUPS * SSM_STATE
SSM_CONV_CH = SSM_INNER + 2 * SSM_GN

D_FF = -(-8 * D_MODEL // (3 * 256)) * 256

LANES = 128
SUBLANES = 8
MXU_DIM = 256
DT_REP = LANES // SSM_HEADS
VMEM_LIMIT = 56 * 1024 * 1024
COL_CHUNK = MXU_DIM
FF_CHUNKS = (6 * MXU_DIM, 5 * MXU_DIM)
assert sum(FF_CHUNKS) == D_FF
WEIGHT_CHUNKS = 16


def _rms(x, g):
    return x * lax.rsqrt(jnp.mean(x * x, axis=-1, keepdims=True) + EPS) * g


def _dot(a, b):
    return jnp.dot(a, b, preferred_element_type=F32)


def _dot_nt(a, b):
    return lax.dot_general(a, b, (((1,), (1,)), ((), ())), preferred_element_type=F32)


def _silu(x):
    return x * jax.nn.sigmoid(x)


def _const_spec(shape):
    return pl.BlockSpec(shape, lambda *_: (0,) * len(shape), pipeline_mode=pl.Buffered(1))


def _load_as_bf16(src, dst, stage, sem):
    rows = stage.shape[1]
    n = src.shape[0] // rows
    copies = [pltpu.make_async_copy(src.at[pl.ds(k * rows, rows)], stage.at[k % 2], sem.at[k % 2])
              for k in range(n)]
    copies[0].start()
    for k in range(n):
        if k + 1 < n:
            copies[k + 1].start()
        copies[k].wait()
        dst[pl.ds(k * rows, rows), :] = stage[k % 2].astype(BF16)


def _params():
    return pltpu.CompilerParams(dimension_semantics=("arbitrary", "arbitrary"),
                                vmem_limit_bytes=VMEM_LIMIT)


def _causal_conv(carry_ref, nh_ref, u, cw, cols):
    width = cw.shape[0]
    rows, ch = u.shape
    nseq = carry_ref.shape[0]
    u4 = u.reshape(nseq, rows // (nseq * SUBLANES), SUBLANES, ch)
    ext4 = jnp.concatenate([carry_ref[:, :, cols].reshape(nseq, 1, SUBLANES, ch), u4], axis=1)
    sub = lax.broadcasted_iota(jnp.int32, (1, 1, SUBLANES, ch), 2)

    def shift(above, here, k):
        return pltpu.roll(jnp.where(sub >= SUBLANES - k, above, here), k, 2)

    if width == 4:
        s1 = shift(jnp.concatenate([ext4[:, :1], ext4[:, :-1]], axis=1), ext4, 1)
        b = s1 * cw[0:1] + ext4 * cw[1:2]
        y4 = shift(b[:, :-1], b[:, 1:], 2) + (s1[:, 1:] * cw[2:3] + u4 * cw[3:4])
    else:
        y4 = None
        for j in range(width - 1):
            term = shift(ext4[:, :-1], u4, width - 1 - j) * cw[j:j + 1]
            y4 = term if y4 is None else y4 + term
        y4 = y4 + u4 * cw[width - 1:width]
    carry_ref[:, :, cols] = u4[:, -1]
    nh_ref[:, :, cols] = u4[:, -1, SUBLANES - (width - 1):]
    return y4.reshape(rows, ch)


def _rope_freq(shape):
    lane = lax.broadcasted_iota(jnp.int32, shape, 1)
    d = lane % HEAD_DIM
    expo = -((d % ROPE_HALF).astype(F32)) / ROPE_HALF
    freq = jnp.power(jnp.full(shape, ROPE_THETA, F32), expo)
    return jnp.where(d < ROPE_DIM, freq, 0.0), d


def _attn_in_kernel(pos0, tm, seq_rows, x_ref, g_ref, w_ref, hist_ref, cw_ref,
                    q_ref, k_ref, v_ref, gated_ref, nh_ref,
                    cr_ref, sr_ref, carry_ref, p_ref):
    b = pl.program_id(0)
    i = pl.program_id(1)

    @pl.when((b == 0) & (i == 0))
    def _():
        freq, _ = _rope_freq((tm, LANES))
        ang = (lax.broadcasted_iota(jnp.int32, (tm, LANES), 0) % seq_rows).astype(F32) * freq
        cr_ref[...] = jnp.cos(ang)
        sr_ref[...] = jnp.sin(ang)

    @pl.when(i == 0)
    def _():
        carry_ref[:, SUBLANES - (B_CONV - 1):, :] = hist_ref[...]

    freq1, d1 = _rope_freq((1, LANES))
    base = (pos0 + i * tm).astype(F32) * freq1
    cb = jnp.cos(base)
    sb = jnp.sin(base)
    cr = cr_ref[...]
    sr = sr_ref[...]
    cos_t = cr * cb - sr * sb
    sin_t = sr * cb + cr * sb
    m_lo = jnp.where(d1 < ROPE_HALF, -1.0, 0.0)
    m_hi = jnp.where((d1 >= ROPE_HALF) & (d1 < ROPE_DIM), 1.0, 0.0)

    def rope(t):
        partner = pltpu.roll(t, LANES - ROPE_HALF, 1) * m_lo + pltpu.roll(t, ROPE_HALF, 1) * m_hi
        return t * cos_t + partner * sin_t

    h = _rms(x_ref[...], g_ref[...]).astype(BF16)
    o1 = A_WIDTH
    o2 = o1 + KV_WIDTH
    o3 = o2 + KV_WIDTH
    o4 = o3 + B_WIDTH
    o5 = o4 + B_WIDTH

    for n in range(AB_IN // MXU_DIM):
        cols = slice(n * MXU_DIM, (n + 1) * MXU_DIM)
        p_ref[:, cols] = _dot(h, w_ref[:, cols])

    u = p_ref[:, o4:o5] * p_ref[:, o5:]
    y = _causal_conv(carry_ref, nh_ref, u, cw_ref[...], slice(None))
    gated_ref[...] = (p_ref[:, o3:o4] * y).astype(BF16)
    k_ref[...] = rope(p_ref[:, o1:o2])
    v_ref[...] = p_ref[:, o2:o3]
    for s in range(A_WIDTH // LANES):
        q_ref[:, s * LANES:(s + 1) * LANES] = (rope(p_ref[:, s * LANES:(s + 1) * LANES]) * ATTN_SCALE).astype(BF16)


def _tile_rows(x, tm):
    bsz, seq, d = x.shape
    nseq = max(tm // seq, 1)
    return x.reshape(bsz // nseq, nseq * seq, d), nseq


def _attn_in(x, g, w, hist, cw, pos0, tm):
    bsz, seq, _ = x.shape
    xt, nseq = _tile_rows(x, tm)
    groups, rows, _ = xt.shape
    tok = lambda width: pl.BlockSpec((None, tm, width), lambda b, i: (b, i, 0))
    hist_spec = pl.BlockSpec((nseq, B_CONV - 1, B_WIDTH), lambda b, i: (b, 0, 0))
    outs = pl.pallas_call(
        functools.partial(_attn_in_kernel, pos0, tm, tm // nseq),
        grid=(groups, rows // tm),
        in_specs=[tok(D_MODEL), _const_spec((1, D_MODEL)), _const_spec((D_MODEL, AB_IN)), hist_spec,
                  _const_spec((B_CONV, B_WIDTH))],
        out_specs=[tok(A_WIDTH), tok(KV_WIDTH), tok(KV_WIDTH), tok(B_WIDTH), hist_spec],
        out_shape=[jax.ShapeDtypeStruct((groups, rows, A_WIDTH), BF16),
                   jax.ShapeDtypeStruct((groups, rows, KV_WIDTH), F32),
                   jax.ShapeDtypeStruct((groups, rows, KV_WIDTH), F32),
                   jax.ShapeDtypeStruct((groups, rows, B_WIDTH), BF16),
                   jax.ShapeDtypeStruct((bsz, B_CONV - 1, B_WIDTH), F32)],
        scratch_shapes=[pltpu.VMEM((tm, LANES), F32), pltpu.VMEM((tm, LANES), F32),
                        pltpu.VMEM((nseq, SUBLANES, B_WIDTH), F32), pltpu.VMEM((tm, AB_IN), F32)],
        compiler_params=_params(),
        name="attn_in_proj",
    )(xt, g, w, hist, cw)
    return [o.reshape(bsz, seq, o.shape[-1]) for o in outs[:-1]] + [outs[-1]]


def _attn_kernel(tq, cq, masked, sinks_ref, q_ref, kc_ref, vc_ref, kp_ref, vp_ref, att_ref,
                 kk_ref, vx_ref, s_ref, e_ref):
    i = pl.program_id(1)
    nk = WINDOW + cq
    nkp = MXU_DIM
    rows_kv = WINDOW + tq
    hd = HEAD_DIM

    kk_ref[0:WINDOW, :] = kp_ref[...].astype(BF16)
    kk_ref[WINDOW:rows_kv, :] = kc_ref[...].astype(BF16)
    kk_ref[rows_kv:, :] = jnp.zeros((kk_ref.shape[0] - rows_kv, KV_WIDTH), BF16)
    vall = jnp.concatenate([vp_ref[...], vc_ref[...]], axis=0)
    vx_ref[0:rows_kv, :] = jnp.concatenate([vall[:, :hd], vall[:, :hd], vall[:, hd:], vall[:, hd:]], axis=1)
    vx_ref[rows_kv:, :] = jnp.zeros((vx_ref.shape[0] - rows_kv, 2 * LANES), F32)

    col = lax.broadcasted_iota(jnp.int32, (A_GROUP * cq, nkp), 1)
    krow = lax.broadcasted_iota(jnp.int32, (nkp, LANES), 0)
    ones = jnp.ones((nkp, LANES), BF16)
    bodies = [(c, kh) for c in range(tq // cq) for kh in range(A_KV_HEADS)]
    fills = []
    for kh in range(A_KV_HEADS):
        sink = jnp.concatenate(
            [jnp.full((cq, nkp), sinks_ref[kh * A_GROUP + g], F32) for g in range(A_GROUP)], axis=0)
        fills.append(jnp.where(col == nk, sink, -jnp.inf))

    for n, (c, kh) in enumerate(bodies):
        r0 = c * cq
        heads = [kh * A_GROUP + g for g in range(A_GROUP)]
        kt = kk_ref[r0:r0 + nkp, kh * hd:(kh + 1) * hd]
        qs = jnp.concatenate([q_ref[r0:r0 + cq, hh * hd:(hh + 1) * hd] for hh in heads], axis=0)
        s = _dot_nt(qs, kt)
        valid = col < nk
        if masked and r0 < WINDOW:
            valid = valid & ((col >= WINDOW - r0) | (i > 0))
        s_ref[n] = jnp.where(valid, s, fills[kh])

    for n in range(len(bodies)):
        s = s_ref[n]
        m = jnp.max(s, axis=-1, keepdims=True)
        e_ref[n] = jnp.exp(s - m).astype(BF16)

    for n, (c, kh) in enumerate(bodies):
        r0 = c * cq
        vt = jnp.where(krow == nk, 0.0, vx_ref[r0:r0 + nkp, kh * LANES:(kh + 1) * LANES]).astype(BF16)
        e = e_ref[n]
        o = _dot(e, vt) / _dot(e, ones)
        for g in range(A_GROUP):
            hh = kh * A_GROUP + g
            lanes = slice((hh % 2) * hd, (hh % 2 + 1) * hd)
            att_ref[r0:r0 + cq, hh * hd:(hh + 1) * hd] = o[g * cq:(g + 1) * cq, lanes].astype(BF16)


def _attn(sinks, q, k, v, k_prev, v_prev, tq, cq, masked):
    bsz, seq, _ = q.shape
    grid = (bsz, seq // tq)
    tok = lambda width: pl.BlockSpec((None, tq, width), lambda b, i: (b, i, 0))
    if masked:
        blocks_per_tile = tq // WINDOW
        prev = pl.BlockSpec((None, WINDOW, KV_WIDTH),
                            lambda b, i: (b, jnp.maximum(i * blocks_per_tile - 1, 0), 0))
    else:
        prev = pl.BlockSpec((None, WINDOW, KV_WIDTH), lambda b, i: (b, 0, 0))
    n_bodies = (tq // cq) * A_KV_HEADS
    kv_rows = (tq // cq - 1) * cq + MXU_DIM
    return pl.pallas_call(
        functools.partial(_attn_kernel, tq, cq, masked),
        grid=grid,
        in_specs=[pl.BlockSpec(memory_space=pltpu.SMEM),
                  tok(A_WIDTH), tok(KV_WIDTH), tok(KV_WIDTH), prev, prev],
        out_specs=tok(A_WIDTH),
        out_shape=jax.ShapeDtypeStruct((bsz, seq, A_WIDTH), BF16),
        scratch_shapes=[pltpu.VMEM((kv_rows, KV_WIDTH), BF16), pltpu.VMEM((kv_rows, 2 * LANES), F32),
                        pltpu.VMEM((n_bodies, A_GROUP * cq, MXU_DIM), F32),
                        pltpu.VMEM((n_bodies, A_GROUP * cq, MXU_DIM), BF16)],
        compiler_params=_params(),
        name="band_attention",
    )(sinks, q, k, v, k_prev, v_prev)


def _proj_ffn_kernel(n_in, gate, n_tiles, layer, *refs):
    long_in, short_in = refs[:n_in], refs[n_in:2 * n_in]
    refs = refs[2 * n_in:]
    ng_ref = None
    if gate:
        ng_ref, refs = refs[0], refs[1:]
    (wo_hbm, xl_ref, xs_ref, g1_ref, g2_ref, wg_hbm, wu_hbm, wd_hbm, g3_ref, ol_ref, os_ref,
     wo_ref, wg_ref, wu_ref, wd_ref, stage_o, stage_gu, stage_d, sem) = refs
    t = pl.program_id(0)

    @pl.when(t == 0)
    def _():
        _load_as_bf16(wo_hbm, wo_ref, stage_o, sem)
        _load_as_bf16(wg_hbm.at[layer], wg_ref, stage_gu, sem)
        _load_as_bf16(wu_hbm.at[layer], wu_ref, stage_gu, sem)
        _load_as_bf16(wd_hbm.at[layer], wd_ref, stage_d, sem)

    def tile(ins, x_ref, o_ref):
        acc = None
        if gate:
            y_ref, z_ref = ins
            for gi in range(SSM_GROUPS):
                gcols = slice(gi * SSM_GROUP_W, (gi + 1) * SSM_GROUP_W)
                gg = y_ref[:, gcols].astype(F32) * z_ref[:, gcols].astype(F32)
                gg = gg * lax.rsqrt(jnp.mean(gg * gg, axis=-1, keepdims=True) + EPS)
                part = _dot((gg * ng_ref[:, gcols]).astype(BF16), wo_ref[gcols, :])
                acc = part if acc is None else acc + part
        else:
            off = 0
            for a_ref in ins:
                kdim = a_ref.shape[-1]
                part = _dot(a_ref[...], wo_ref[off:off + kdim, :])
                acc = part if acc is None else acc + part
                off += kdim

        x = x_ref[...] + _rms(acc, g1_ref[...])
        h = _rms(x, g2_ref[...]).astype(BF16)
        f = None
        off = 0
        for width in FF_CHUNKS:
            fc = slice(off, off + width)
            m = (_silu(_dot(h, wg_ref[:, fc])) * _dot(h, wu_ref[:, fc])).astype(BF16)
            part = _dot(m, wd_ref[fc, :])
            f = part if f is None else f + part
            off += width
        o_ref[...] = x + _rms(f, g3_ref[...])

    pl.when(t < n_tiles)(lambda: tile(long_in, xl_ref, ol_ref))
    pl.when(t == n_tiles)(lambda: tile(short_in, xs_ref, os_ref))


def _proj_ffn(mix_long, mix_short, ng, wo, x_long, x_short, g1, g2, wg, wu, wd, layer, g3, tm):
    bsz, seq, _ = x_long.shape
    tiles_per_seq = seq // tm
    n_tiles = bsz * tiles_per_seq
    n_short = x_short.shape[0] * x_short.shape[1]
    flat = lambda a: a.reshape(1, n_short, a.shape[-1])
    tile_idx = lambda t: jnp.minimum(t, n_tiles - 1)
    ltok = lambda width: pl.BlockSpec((None, tm, width),
                                      lambda t: (tile_idx(t) // tiles_per_seq, tile_idx(t) % tiles_per_seq, 0))
    stok = lambda width: pl.BlockSpec((None, n_short, width), lambda t: (0, 0, 0))
    gate = ng is not None
    ng_args, ng_specs = ([ng], [_const_spec((1, SSM_INNER))]) if gate else ([], [])
    hbm = pl.BlockSpec(memory_space=pl.ANY)
    out_long, out_short = pl.pallas_call(
        functools.partial(_proj_ffn_kernel, len(mix_long), gate, n_tiles, layer),
        grid=(n_tiles + 1,),
        in_specs=[ltok(a.shape[-1]) for a in mix_long] + [stok(a.shape[-1]) for a in mix_short] + ng_specs + [
            hbm, ltok(D_MODEL), stok(D_MODEL), _const_spec((1, D_MODEL)), _const_spec((1, D_MODEL)),
            hbm, hbm, hbm, _const_spec((1, D_MODEL))],
        out_specs=[ltok(D_MODEL), stok(D_MODEL)],
        out_shape=[jax.ShapeDtypeStruct(x_long.shape, F32), jax.ShapeDtypeStruct((1, n_short, D_MODEL), F32)],
        scratch_shapes=[pltpu.VMEM(wo.shape, BF16), pltpu.VMEM((D_MODEL, D_FF), BF16), pltpu.VMEM((D_MODEL, D_FF), BF16),
                        pltpu.VMEM((D_FF, D_MODEL), BF16),
                        pltpu.VMEM((2, wo.shape[0] // WEIGHT_CHUNKS, D_MODEL), F32),
                        pltpu.VMEM((2, D_MODEL // WEIGHT_CHUNKS, D_FF), F32),
                        pltpu.VMEM((2, D_FF // WEIGHT_CHUNKS, D_MODEL), F32),
                        pltpu.SemaphoreType.DMA((2,))],
        compiler_params=pltpu.CompilerParams(dimension_semantics=("arbitrary",), vmem_limit_bytes=VMEM_LIMIT),
        name="out_proj_ffn",
    )(*mix_long, *[flat(a) for a in mix_short], *ng_args, wo, x_long, flat(x_short), g1, g2, wg, wu, wd, g3)
    return out_long, out_short.reshape(x_short.shape)


def _split3(v):
    t1 = v.astype(BF16)
    r = v - t1.astype(F32)
    t2 = r.astype(BF16)
    t3 = (r - t2.astype(F32)).astype(BF16)
    return t1, t2, t3


def _pack3(v):
    t1, t2, t3 = _split3(v)
    lane = lax.broadcasted_iota(jnp.int32, v.shape, 1)
    return jnp.where(lane < SSM_HEADS, t1,
                     jnp.where(lane < 2 * SSM_HEADS, t2,
                               jnp.where(lane < 3 * SSM_HEADS, t3, jnp.zeros_like(t3))))


def _ssm_in_kernel(tm, cl, x_ref, g_ref, w_ref, wdt_ref, hist_ref, cw_ref, cb_ref, dtb_ref, alog_ref,
                   z_ref, xs_ref, bm_ref, cm_ref, ta_ref, tdt_ref, nh_ref, carry_ref):
    i = pl.program_id(1)

    @pl.when(i == 0)
    def _():
        carry_ref[:, SUBLANES - (SSM_CONV - 1):, :] = hist_ref[...]

    h = _rms(x_ref[...], g_ref[...]).astype(BF16)

    dt = jax.nn.softplus(_dot(h, wdt_ref[...]) + dtb_ref[...])
    tdt_ref[...] = _pack3(dt)
    d1, d2, d3 = _split3(dt * (-jnp.exp(alog_ref[...])))
    tcol = lax.broadcasted_iota(jnp.int32, (cl, 3 * cl), 1) % cl
    trow = lax.broadcasted_iota(jnp.int32, (cl, 3 * cl), 0)
    tril = jnp.where(tcol <= trow, 1.0, 0.0).astype(BF16)
    acum = []
    for c in range(tm // cl):
        rows = slice(c * cl, (c + 1) * cl)
        acum.append(_dot(tril, jnp.concatenate([d1[rows], d2[rows], d3[rows]], axis=0)))
    ta_ref[...] = _pack3(jnp.concatenate(acum, axis=0))

    for n in range(SSM_INNER // COL_CHUNK):
        cols = slice(n * COL_CHUNK, (n + 1) * COL_CHUNK)
        z_ref[:, cols] = _silu(_dot(h, w_ref[:, cols])).astype(BF16)

    for n in range(SSM_CONV_CH // COL_CHUNK):
        cols = slice(n * COL_CHUNK, (n + 1) * COL_CHUNK)
        raw = _dot(h, w_ref[:, SSM_INNER + cols.start:SSM_INNER + cols.stop])
        y = _silu(_causal_conv(carry_ref, nh_ref, raw, cw_ref[:, cols], cols) + cb_ref[:, cols])
        if cols.stop <= SSM_INNER:
            xs_ref[:, cols] = y
        elif cols.stop <= SSM_INNER + SSM_GN:
            bm_ref[:, cols.start - SSM_INNER:cols.stop - SSM_INNER] = y.astype(BF16)
        else:
            lo = SSM_INNER + SSM_GN
            cm_ref[:, cols.start - lo:cols.stop - lo] = y.astype(BF16)


def _ssm_in(x, g, w, wdt, hist, cw, cb, dtb, alog, tm, cl):
    bsz, seq, _ = x.shape
    xt, nseq = _tile_rows(x, tm)
    groups, rows, _ = xt.shape
    tok = lambda width: pl.BlockSpec((None, tm, width), lambda b, i: (b, i, 0))
    hist_spec = pl.BlockSpec((nseq, SSM_CONV - 1, SSM_CONV_CH), lambda b, i: (b, 0, 0))
    outs = pl.pallas_call(
        functools.partial(_ssm_in_kernel, tm, cl),
        grid=(groups, rows // tm),
        in_specs=[tok(D_MODEL), _const_spec((1, D_MODEL)), _const_spec(w.shape), _const_spec((D_MODEL, LANES)),
                  hist_spec,
                  _const_spec((SSM_CONV, SSM_CONV_CH)), _const_spec((1, SSM_CONV_CH)), _const_spec((1, LANES)),
                  _const_spec((1, LANES))],
        out_specs=[tok(SSM_INNER), tok(SSM_INNER), tok(SSM_GN), tok(SSM_GN), tok(LANES), tok(LANES), hist_spec],
        out_shape=[jax.ShapeDtypeStruct((groups, rows, SSM_INNER), BF16),
                   jax.ShapeDtypeStruct((groups, rows, SSM_INNER), F32),
                   jax.ShapeDtypeStruct((groups, rows, SSM_GN), BF16),
                   jax.ShapeDtypeStruct((groups, rows, SSM_GN), BF16),
                   jax.ShapeDtypeStruct((groups, rows, LANES), BF16),
                   jax.ShapeDtypeStruct((groups, rows, LANES), BF16),
                   jax.ShapeDtypeStruct((bsz, SSM_CONV - 1, SSM_CONV_CH), F32)],
        scratch_shapes=[pltpu.VMEM((nseq, SUBLANES, SSM_CONV_CH), F32)],
        compiler_params=_params(),
        name="ssm_in_proj",
    )(xt, g, w, wdt, hist, cw, cb, dtb, alog)
    return [o.reshape(bsz, seq, o.shape[-1]) for o in outs[:-1]] + [outs[-1]]


def _pad_rows(a, rows):
    if a.shape[0] == rows:
        return a
    return jnp.concatenate([a, jnp.zeros((rows - a.shape[0], a.shape[1]), a.dtype)], axis=0)


def _ssd_kernel(ts, cl, xs_ref, bm_ref, cm_ref, ta_ref, tdt_ref, h0_ref, dskip_ref,
                yo_ref, hout_ref, ht_ref, y_ref, e3_ref, aexp_ref, xdt_ref, st_ref, cb_ref, bmt_ref, w2_ref, rhs_ref):
    b = pl.program_id(0)
    i = pl.program_id(1)
    p = SSM_HEADDIM
    nc = ts // cl
    pairs = SSM_GROUP_W // LANES

    @pl.when((b == 0) & (i == 0))
    def _():
        kk = lax.broadcasted_iota(jnp.int32, (LANES, SSM_INNER), 0)
        cc = lax.broadcasted_iota(jnp.int32, (LANES, SSM_INNER), 1)
        hit = (kk % SSM_HEADS == cc // p) & (kk < 3 * SSM_HEADS)
        e3_ref[...] = jnp.where(hit, 1.0, 0.0).astype(BF16)

    @pl.when(i == 0)
    def _():
        ht_ref[...] = h0_ref[...].T

    lane = lax.broadcasted_iota(jnp.int32, (cl, LANES), 1)
    row = lax.broadcasted_iota(jnp.int32, (cl, LANES), 0)
    causal2 = (lane % p) <= row
    left = lane.astype(F32).astype(BF16) < p
    diag2 = (lane % p) == row
    groups = [(gi, slice(gi * SSM_GROUP_W, (gi + 1) * SSM_GROUP_W), slice(gi * SSM_STATE, (gi + 1) * SSM_STATE))
              for gi in range(SSM_GROUPS)]
    chunks = [(c, slice(c * cl, (c + 1) * cl)) for c in range(nc)]

    for gi, gcols, _ in groups:
        e3g = e3_ref[:, gcols]
        aexp_ref[:, gcols] = _dot(ta_ref[...], e3g)
        xdt_ref[:, gcols] = (xs_ref[:, gcols] * _dot(tdt_ref[...], e3g)).astype(BF16)

    for c, rows in chunks:
        for gi, _, scol in groups:
            bmp = _pad_rows(bm_ref[rows, scol], p)
            cbm = _dot_nt(cm_ref[rows, scol], bmp)
            cb_ref[c * SSM_GROUPS + gi] = jnp.concatenate([cbm, cbm], axis=1)
            bmt_ref[c * SSM_GROUPS + gi] = bmp.astype(F32).T.astype(BF16)

    for c, rows in chunks:
        for gi, _, _ in groups:
            for jj in range(pairs):
                idx = (c * SSM_GROUPS + gi) * pairs + jj
                cols = slice(gi * SSM_GROUP_W + jj * LANES, gi * SSM_GROUP_W + (jj + 1) * LANES)
                a_pair = aexp_ref[rows, cols]
                a_src = jnp.sum(jnp.where(diag2, a_pair, 0.0), axis=0, keepdims=True)
                seg = a_pair - a_src
                dec = jnp.exp(jnp.where(causal2, seg, -jnp.inf))
                w2_ref[idx] = (dec * cb_ref[c * SSM_GROUPS + gi]).astype(BF16)
                x2 = xdt_ref[rows, cols]
                top = _pad_rows(jnp.where(left, x2, jnp.zeros_like(x2)), p)
                bot = _pad_rows(jnp.where(left, jnp.zeros_like(x2), x2), p)
                rhs_ref[idx] = jnp.concatenate([top, bot], axis=0)

    for c, rows in chunks:
        for gi, _, _ in groups:
            for jj in range(pairs):
                idx = (c * SSM_GROUPS + gi) * pairs + jj
                cols = slice(gi * SSM_GROUP_W + jj * LANES, gi * SSM_GROUP_W + (jj + 1) * LANES)
                y_ref[rows, cols] = _dot(w2_ref[idx], rhs_ref[idx])

    for c, rows in chunks:
        for gi, gcols, _ in groups:
            a_g = aexp_ref[rows, gcols]
            xw = xdt_ref[rows, gcols] * jnp.exp(a_g[cl - 1:cl, :] - a_g).astype(BF16)
            st_ref[c, :, gcols] = _dot(bmt_ref[c * SSM_GROUPS + gi], _pad_rows(xw, p))

    for c, rows in chunks:
        for gi, gcols, scol in groups:
            hprev = ht_ref[:, gcols]
            a_g = aexp_ref[rows, gcols]
            y_off = _dot(cm_ref[rows, scol], hprev.astype(BF16)) * jnp.exp(a_g)
            y_ref[rows, gcols] = y_ref[rows, gcols] + y_off
            ht_ref[:, gcols] = hprev * jnp.exp(a_g[cl - 1:cl, :]) + st_ref[c, :, gcols]

    yo_ref[...] = (y_ref[...] + dskip_ref[...] * xs_ref[...]).astype(BF16)

    @pl.when(i == pl.num_programs(1) - 1)
    def _():
        hout_ref[...] = ht_ref[...].T


def _ssd(xs, bm, cm, ta, tdt, h0, dskip, ts, cl):
    bsz, seq, _ = xs.shape
    tok = lambda width: pl.BlockSpec((None, ts, width), lambda b, i: (b, i, 0))
    st_spec = pl.BlockSpec((None, SSM_INNER, SSM_STATE), lambda b, i: (b, 0, 0))
    nc = ts // cl
    n_cg = nc * SSM_GROUPS
    n_pairs = n_cg * (SSM_GROUP_W // LANES)
    return pl.pallas_call(
        functools.partial(_ssd_kernel, ts, cl),
        grid=(bsz, seq // ts),
        in_specs=[tok(SSM_INNER), tok(SSM_GN), tok(SSM_GN), tok(LANES), tok(LANES), st_spec,
                  _const_spec((1, SSM_INNER))],
        out_specs=[tok(SSM_INNER), st_spec],
        out_shape=[jax.ShapeDtypeStruct((bsz, seq, SSM_INNER), BF16),
                   jax.ShapeDtypeStruct((bsz, SSM_INNER, SSM_STATE), F32)],
        scratch_shapes=[pltpu.VMEM((SSM_STATE, SSM_INNER), F32), pltpu.VMEM((ts, SSM_INNER), F32),
                        pltpu.VMEM((LANES, SSM_INNER), BF16), pltpu.VMEM((ts, SSM_INNER), F32),
                        pltpu.VMEM((ts, SSM_INNER), BF16), pltpu.VMEM((nc, SSM_STATE, SSM_INNER), F32),
                        pltpu.VMEM((n_cg, cl, LANES), F32), pltpu.VMEM((n_cg, SSM_STATE, SSM_HEADDIM), BF16),
                        pltpu.VMEM((n_pairs, cl, LANES), BF16), pltpu.VMEM((n_pairs, LANES, LANES), BF16)],
        compiler_params=_params(),
        name="ssd_scan",
    )(xs, bm, cm, ta, tdt, h0, dskip)


def _layer0_mixers(x, pos0, sconv_hist, k_cache, v_cache, wts, tiles):
    t_in, t_att, c_att = tiles
    q, k, v, gated, new_sconv = _attn_in(x, wts["g"][0][0], wts["ab_w_in"], sconv_hist, wts["sconv_w"], pos0, t_in)
    masked = k_cache is None
    k_prev, v_prev = (k, v) if masked else (k_cache, v_cache)
    att = _attn(wts["sinks"], q, k, v, k_prev, v_prev, t_att, c_att, masked)
    bsz, seq, _ = x.shape
    new_k = k[:, seq - min(WINDOW, seq):].reshape(bsz, -1, A_KV_HEADS, HEAD_DIM)
    new_v = v[:, seq - min(WINDOW, seq):].reshape(bsz, -1, A_KV_HEADS, HEAD_DIM)
    return [att, gated], (new_k, new_v, new_sconv)


def _layer1_mixer(x, conv_hist, ssm_state, wts, tiles):
    t_sin, t_ssd, c_ssd = tiles
    bsz = x.shape[0]
    z, xs, bm, cm, ta, tdt, new_conv = _ssm_in(
        x, wts["g"][1][0], wts["ssm_w_in"], wts["w_dt"], conv_hist, wts["ssm_conv_w"],
        wts["ssm_conv_b"], wts["dt_bias"], wts["a_log"], t_sin, c_ssd)
    y, new_state = _ssd(xs, bm, cm, ta, tdt, ssm_state.reshape(bsz, SSM_INNER, SSM_STATE),
                        wts["d_skip"], t_ssd, c_ssd)
    return [y, z], (new_conv, new_state.reshape(bsz, SSM_HEADS, SSM_HEADDIM, SSM_STATE))


def kernel(x_prompt, x_sample, cache_attn_k, cache_attn_v, state_sconv, state_ssm_conv, state_ssm, norm_g,
           ab_w_in, ab_w_out, attn_sinks, sconv_w, ssm_w_in, ssm_conv_w, ssm_conv_b, ssm_dt_bias, ssm_a_log,
           ssm_d, ssm_norm_g, ssm_w_out, ffn_w_gate, ffn_w_up, ffn_w_down):
    w_in1 = ssm_w_in[0]
    wts = {
        "ab_w_in": ab_w_in[0].astype(BF16),
        "ab_w_out": ab_w_out[0],
        "sinks": attn_sinks[0],
        "sconv_w": sconv_w[0],
        "ssm_w_in": w_in1.astype(BF16),
        "w_dt": jnp.tile(w_in1[:, SSM_INNER + SSM_CONV_CH:], (1, DT_REP)).astype(BF16),
        "ssm_conv_w": ssm_conv_w[0],
        "ssm_conv_b": ssm_conv_b[0].reshape(1, -1),
        "dt_bias": jnp.tile(ssm_dt_bias[0], DT_REP).reshape(1, -1),
        "a_log": jnp.tile(ssm_a_log[0], DT_REP).reshape(1, -1),
        "d_skip": jnp.repeat(ssm_d[0], SSM_HEADDIM).reshape(1, -1),
        "ssm_norm_g": ssm_norm_g[0].reshape(1, -1),
        "ssm_w_out": ssm_w_out[0],
        "wg": ffn_w_gate,
        "wu": ffn_w_up,
        "wd": ffn_w_down,
    }
    wts["g"] = [[norm_g[l, j].reshape(1, -1) for j in range(4)] for l in range(norm_g.shape[0])]
    g = wts["g"]
    bp = x_prompt.shape[0]
    bs, ls = x_sample.shape[0], x_sample.shape[1]
    t_ffn = 512

    mix_p, (kp, vp, scp) = _layer0_mixers(x_prompt, 0, jnp.zeros((bp, B_CONV - 1, B_WIDTH), F32), None, None,
                                          wts, (1024, 512, CHUNK))
    mix_s, (ks, vs, scs) = _layer0_mixers(x_sample, PAST_LEN, state_sconv[0],
                                          cache_attn_k[0].reshape(bs, -1, KV_WIDTH),
                                          cache_attn_v[0].reshape(bs, -1, KV_WIDTH), wts, (bs * ls, ls, ls))
    xp, xs = _proj_ffn(mix_p, mix_s, None, wts["ab_w_out"], x_prompt, x_sample, g[0][1], g[0][2],
                       wts["wg"], wts["wu"], wts["wd"], 0, g[0][3], t_ffn)

    mix_p, (ccp, ssp) = _layer1_mixer(xp, jnp.zeros((bp, SSM_CONV - 1, SSM_CONV_CH), F32),
                                      jnp.zeros((bp, SSM_HEADS, SSM_HEADDIM, SSM_STATE), F32), wts, (512, 512, CHUNK))
    mix_s, (ccs, sss) = _layer1_mixer(xs, state_ssm_conv[0], state_ssm[0], wts, (bs * ls, ls, ls))
    yp, ys = _proj_ffn(mix_p, mix_s, wts["ssm_norm_g"], wts["ssm_w_out"], xp, xs, g[1][1], g[1][2],
                       wts["wg"], wts["wu"], wts["wd"], 1, g[1][3], t_ffn)

    lead = lambda a: a[None]
    return (yp, ys, lead(kp), lead(vp), lead(scp), lead(ccp), lead(ssp),
            lead(ks), lead(vs), lead(scs), lead(ccs), lead(sss))
```

```python
import functools

import jax
import jax.numpy as jnp
from jax import lax
from jax.experimental import pallas as pl
from jax.experimental.pallas import tpu as pltpu

F32 = jnp.float32
BF16 = jnp.bfloat16

D_MODEL = 1024
CHUNK = 64
EPS = 1e-6
PAST_LEN = 4096

A_HEADS = 8
A_KV_HEADS = 2
A_GROUP = A_HEADS // A_KV_HEADS
HEAD_DIM = 64
A_WIDTH = A_HEADS * HEAD_DIM
KV_WIDTH = A_KV_HEADS * HEAD_DIM
WINDOW = 128
ROPE_DIM = HEAD_DIM // 4
ROPE_HALF = ROPE_DIM // 2
ROPE_THETA = 500000.0
ATTN_SCALE = HEAD_DIM ** -0.5

B_WIDTH = D_MODEL // 2
B_CONV = 3
AB_IN = A_WIDTH + 2 * KV_WIDTH + 3 * B_WIDTH

SSM_INNER = 2 * D_MODEL
SSM_HEADDIM = 64
SSM_HEADS = SSM_INNER // SSM_HEADDIM
SSM_GROUPS = 4
SSM_GROUP_W = SSM_INNER // SSM_GROUPS
SSM_STATE = 128
SSM_CONV = 4
SSM_GN = SSM_GROUPS * SSM_STATE
SSM_CONV_CH = SSM_INNER + 2 * SSM_GN

D_FF = -(-8 * D_MODEL // (3 * 256)) * 256

LANES = 128
SUBLANES = 8
MXU_DIM = 256
DT_REP = LANES // SSM_HEADS
VMEM_LIMIT = 56 * 1024 * 1024
COL_CHUNK = MXU_DIM
FF_CHUNKS = (6 * MXU_DIM, 5 * MXU_DIM)
assert sum(FF_CHUNKS) == D_FF
WEIGHT_CHUNKS = 32
STAGE_SLOTS = 4


def _rms(x, g):
    return x * lax.rsqrt(jnp.mean(x * x, axis=-1, keepdims=True) + EPS) * g


def _dot(a, b):
    return jnp.dot(a, b, preferred_element_type=F32)


def _dot_nt(a, b):
    return lax.dot_general(a, b, (((1,), (1,)), ((), ())), preferred_element_type=F32)


def _silu(x):
    return x * jax.nn.sigmoid(x)


def _const_spec(shape):
    return pl.BlockSpec(shape, lambda *_: (0,) * len(shape), pipeline_mode=pl.Buffered(1))


def _load_as_bf16(src, dst, stage, sem):
    slots, rows = stage.shape[0], stage.shape[1]
    n = src.shape[0] // rows
    copies = [pltpu.make_async_copy(src.at[pl.ds(k * rows, rows)], stage.at[k % slots], sem.at[k % slots])
              for k in range(n)]
    for k in range(min(slots - 1, n)):
        copies[k].start()
    for k in range(n):
        if k + slots - 1 < n:
            copies[k + slots - 1].start()
        copies[k].wait()
        dst[pl.ds(k * rows, rows), :] = stage[k % slots].astype(BF16)


def _params():
    return pltpu.CompilerParams(dimension_semantics=("arbitrary", "arbitrary"),
                                vmem_limit_bytes=VMEM_LIMIT)


def _causal_conv(carry_ref, nh_ref, u, cw, cols):
    width = cw.shape[0]
    rows, ch = u.shape
    nseq = carry_ref.shape[0]
    u4 = u.reshape(nseq, rows // (nseq * SUBLANES), SUBLANES, ch)
    ext4 = jnp.concatenate([carry_ref[:, :, cols].reshape(nseq, 1, SUBLANES, ch), u4], axis=1)
    sub = lax.broadcasted_iota(jnp.int32, (1, 1, SUBLANES, ch), 2)

    def shift(above, here, k):
        return pltpu.roll(jnp.where(sub >= SUBLANES - k, above, here), k, 2)

    if width == 4:
        s1 = shift(jnp.concatenate([ext4[:, :1], ext4[:, :-1]], axis=1), ext4, 1)
        b = s1 * cw[0:1] + ext4 * cw[1:2]
        y4 = shift(b[:, :-1], b[:, 1:], 2) + (s1[:, 1:] * cw[2:3] + u4 * cw[3:4])
    else:
        y4 = None
        for j in range(width - 1):
            term = shift(ext4[:, :-1], u4, width - 1 - j) * cw[j:j + 1]
            y4 = term if y4 is None else y4 + term
        y4 = y4 + u4 * cw[width - 1:width]
    carry_ref[:, :, cols] = u4[:, -1]
    nh_ref[:, :, cols] = u4[:, -1, SUBLANES - (width - 1):]
    return y4.reshape(rows, ch)


def _rope_freq(shape):
    lane = lax.broadcasted_iota(jnp.int32, shape, 1)
    d = lane % HEAD_DIM
    expo = -((d % ROPE_HALF).astype(F32)) / ROPE_HALF
    freq = jnp.power(jnp.full(shape, ROPE_THETA, F32), expo)
    return jnp.where(d < ROPE_DIM, freq, 0.0), d


def _attn_in_kernel(pos0, tm, seq_rows, x_ref, g_ref, w_ref, hist_ref, cw_ref,
                    q_ref, k_ref, v_ref, gated_ref, nh_ref,
                    cr_ref, sr_ref, carry_ref, p_ref):
    b = pl.program_id(0)
    i = pl.program_id(1)

    @pl.when((b == 0) & (i == 0))
    def _():
        freq, _ = _rope_freq((tm, LANES))
        ang = (lax.broadcasted_iota(jnp.int32, (tm, LANES), 0) % seq_rows).astype(F32) * freq
        cr_ref[...] = jnp.cos(ang)
        sr_ref[...] = jnp.sin(ang)

    @pl.when(i == 0)
    def _():
        carry_ref[:, SUBLANES - (B_CONV - 1):, :] = hist_ref[...]

    freq1, d1 = _rope_freq((1, LANES))
    base = (pos0 + i * tm).astype(F32) * freq1
    cb = jnp.cos(base)
    sb = jnp.sin(base)
    cr = cr_ref[...]
    sr = sr_ref[...]
    cos_t = cr * cb - sr * sb
    sin_t = sr * cb + cr * sb
    m_lo = jnp.where(d1 < ROPE_HALF, -1.0, 0.0)
    m_hi = jnp.where((d1 >= ROPE_HALF) & (d1 < ROPE_DIM), 1.0, 0.0)

    def rope(t):
        partner = pltpu.roll(t, LANES - ROPE_HALF, 1) * m_lo + pltpu.roll(t, ROPE_HALF, 1) * m_hi
        return t * cos_t + partner * sin_t

    h = _rms(x_ref[...], g_ref[...]).astype(BF16)
    o1 = A_WIDTH
    o2 = o1 + KV_WIDTH
    o3 = o2 + KV_WIDTH
    o4 = o3 + B_WIDTH
    o5 = o4 + B_WIDTH

    for n in range(AB_IN // MXU_DIM):
        cols = slice(n * MXU_DIM, (n + 1) * MXU_DIM)
        p_ref[:, cols] = _dot(h, w_ref[:, cols])

    u = p_ref[:, o4:o5] * p_ref[:, o5:]
    y = _causal_conv(carry_ref, nh_ref, u, cw_ref[...], slice(None))
    gated_ref[...] = (p_ref[:, o3:o4] * y).astype(BF16)
    k_ref[...] = rope(p_ref[:, o1:o2])
    v_ref[...] = p_ref[:, o2:o3]
    for s in range(A_WIDTH // LANES):
        q_ref[:, s * LANES:(s + 1) * LANES] = (rope(p_ref[:, s * LANES:(s + 1) * LANES]) * ATTN_SCALE).astype(BF16)


def _tile_rows(x, tm):
    bsz, seq, d = x.shape
    nseq = max(tm // seq, 1)
    return x.reshape(bsz // nseq, nseq * seq, d), nseq


def _attn_in(x, g, w, hist, cw, pos0, tm):
    bsz, seq, _ = x.shape
    xt, nseq = _tile_rows(x, tm)
    groups, rows, _ = xt.shape
    tok = lambda width: pl.BlockSpec((None, tm, width), lambda b, i: (b, i, 0))
    hist_spec = pl.BlockSpec((nseq, B_CONV - 1, B_WIDTH), lambda b, i: (b, 0, 0))
    outs = pl.pallas_call(
        functools.partial(_attn_in_kernel, pos0, tm, tm // nseq),
        grid=(groups, rows // tm),
        in_specs=[tok(D_MODEL), _const_spec((1, D_MODEL)), _const_spec((D_MODEL, AB_IN)), hist_spec,
                  _const_spec((B_CONV, B_WIDTH))],
        out_specs=[tok(A_WIDTH), tok(KV_WIDTH), tok(KV_WIDTH), tok(B_WIDTH), hist_spec],
        out_shape=[jax.ShapeDtypeStruct((groups, rows, A_WIDTH), BF16),
                   jax.ShapeDtypeStruct((groups, rows, KV_WIDTH), F32),
                   jax.ShapeDtypeStruct((groups, rows, KV_WIDTH), F32),
                   jax.ShapeDtypeStruct((groups, rows, B_WIDTH), BF16),
                   jax.ShapeDtypeStruct((bsz, B_CONV - 1, B_WIDTH), F32)],
        scratch_shapes=[pltpu.VMEM((tm, LANES), F32), pltpu.VMEM((tm, LANES), F32),
                        pltpu.VMEM((nseq, SUBLANES, B_WIDTH), F32), pltpu.VMEM((tm, AB_IN), F32)],
        compiler_params=_params(),
        name="attn_in_proj",
    )(xt, g, w, hist, cw)
    return [o.reshape(bsz, seq, o.shape[-1]) for o in outs[:-1]] + [outs[-1]]


def _attn_kernel(tq, cq, masked, sinks_ref, q_ref, kc_ref, vc_ref, kp_ref, vp_ref, att_ref,
                 kk_ref, vx_ref, s_ref, e_ref):
    i = pl.program_id(1)
    nk = WINDOW + cq
    nkp = MXU_DIM
    rows_kv = WINDOW + tq
    hd = HEAD_DIM

    kk_ref[0:WINDOW, :] = kp_ref[...].astype(BF16)
    kk_ref[WINDOW:rows_kv, :] = kc_ref[...].astype(BF16)
    kk_ref[rows_kv:, :] = jnp.zeros((kk_ref.shape[0] - rows_kv, KV_WIDTH), BF16)
    vall = jnp.concatenate([vp_ref[...], vc_ref[...]], axis=0)
    vx_ref[0:rows_kv, :] = jnp.concatenate([vall[:, :hd], vall[:, :hd], vall[:, hd:], vall[:, hd:]], axis=1)
    vx_ref[rows_kv:, :] = jnp.zeros((vx_ref.shape[0] - rows_kv, 2 * LANES), F32)

    col = lax.broadcasted_iota(jnp.int32, (A_GROUP * cq, nkp), 1)
    krow = lax.broadcasted_iota(jnp.int32, (nkp, LANES), 0)
    ones = jnp.ones((nkp, LANES), BF16)
    bodies = [(c, kh) for c in range(tq // cq) for kh in range(A_KV_HEADS)]
    fills = []
    for kh in range(A_KV_HEADS):
        sink = jnp.concatenate(
            [jnp.full((cq, nkp), sinks_ref[kh * A_GROUP + g], F32) for g in range(A_GROUP)], axis=0)
        fills.append(jnp.where(col == nk, sink, -jnp.inf))

    for n, (c, kh) in enumerate(bodies):
        r0 = c * cq
        heads = [kh * A_GROUP + g for g in range(A_GROUP)]
        kt = kk_ref[r0:r0 + nkp, kh * hd:(kh + 1) * hd]
        qs = jnp.concatenate([q_ref[r0:r0 + cq, hh * hd:(hh + 1) * hd] for hh in heads], axis=0)
        s = _dot_nt(qs, kt)
        valid = col < nk
        if masked and r0 < WINDOW:
            valid = valid & ((col >= WINDOW - r0) | (i > 0))
        s_ref[n] = jnp.where(valid, s, fills[kh])

    for n in range(len(bodies)):
        s = s_ref[n]
        m = jnp.max(s, axis=-1, keepdims=True)
        e_ref[n] = jnp.exp(s - m).astype(BF16)

    for n, (c, kh) in enumerate(bodies):
        r0 = c * cq
        vt = jnp.where(krow == nk, 0.0, vx_ref[r0:r0 + nkp, kh * LANES:(kh + 1) * LANES]).astype(BF16)
        e = e_ref[n]
        o = _dot(e, vt) / _dot(e, ones)
        for g in range(A_GROUP):
            hh = kh * A_GROUP + g
            lanes = slice((hh % 2) * hd, (hh % 2 + 1) * hd)
            att_ref[r0:r0 + cq, hh * hd:(hh + 1) * hd] = o[g * cq:(g + 1) * cq, lanes].astype(BF16)


def _attn(sinks, q, k, v, k_prev, v_prev, tq, cq, masked):
    bsz, seq, _ = q.shape
    grid = (bsz, seq // tq)
    tok = lambda width: pl.BlockSpec((None, tq, width), lambda b, i: (b, i, 0))
    if masked:
        blocks_per_tile = tq // WINDOW
        prev = pl.BlockSpec((None, WINDOW, KV_WIDTH),
                            lambda b, i: (b, jnp.maximum(i * blocks_per_tile - 1, 0), 0))
    else:
        prev = pl.BlockSpec((None, WINDOW, KV_WIDTH), lambda b, i: (b, 0, 0))
    n_bodies = (tq // cq) * A_KV_HEADS
    kv_rows = (tq // cq - 1) * cq + MXU_DIM
    return pl.pallas_call(
        functools.partial(_attn_kernel, tq, cq, masked),
        grid=grid,
        in_specs=[pl.BlockSpec(memory_space=pltpu.SMEM),
                  tok(A_WIDTH), tok(KV_WIDTH), tok(KV_WIDTH), prev, prev],
        out_specs=tok(A_WIDTH),
        out_shape=jax.ShapeDtypeStruct((bsz, seq, A_WIDTH), BF16),
        scratch_shapes=[pltpu.VMEM((kv_rows, KV_WIDTH), BF16), pltpu.VMEM((kv_rows, 2 * LANES), F32),
                        pltpu.VMEM((n_bodies, A_GROUP * cq, MXU_DIM), F32),
                        pltpu.VMEM((n_bodies, A_GROUP * cq, MXU_DIM), BF16)],
        compiler_params=_params(),
        name="band_attention",
    )(sinks, q, k, v, k_prev, v_prev)


def _proj_ffn_kernel(n_in, gate, n_tiles, layer, *refs):
    long_in, short_in = refs[:n_in], refs[n_in:2 * n_in]
    refs = refs[2 * n_in:]
    ng_ref = None
    if gate:
        ng_ref, refs = refs[0], refs[1:]
    (wo_hbm, xl_ref, xs_ref, g1_ref, g2_ref, wg_hbm, wu_hbm, wd_hbm, g3_ref, ol_ref, os_ref,
     wo_ref, wg_ref, wu_ref, wd_ref, stage_o, stage_gu, stage_d, sem) = refs
    t = pl.program_id(0)

    @pl.when(t == 0)
    def _():
        _load_as_bf16(wo_hbm, wo_ref, stage_o, sem)
        _load_as_bf16(wg_hbm.at[layer], wg_ref, stage_gu, sem)
        _load_as_bf16(wu_hbm.at[layer], wu_ref, stage_gu, sem)
        _load_as_bf16(wd_hbm.at[layer], wd_ref, stage_d, sem)

    def tile(ins, x_ref, o_ref):
        acc = None
        if gate:
            y_ref, z_ref = ins
            for gi in range(SSM_GROUPS):
                gcols = slice(gi * SSM_GROUP_W, (gi + 1) * SSM_GROUP_W)
                gg = y_ref[:, gcols].astype(F32) * z_ref[:, gcols].astype(F32)
                gg = gg * lax.rsqrt(jnp.mean(gg * gg, axis=-1, keepdims=True) + EPS)
                part = _dot((gg * ng_ref[:, gcols]).astype(BF16), wo_ref[gcols, :])
                acc = part if acc is None else acc + part
        else:
            off = 0
            for a_ref in ins:
                kdim = a_ref.shape[-1]
                part = _dot(a_ref[...], wo_ref[off:off + kdim, :])
                acc = part if acc is None else acc + part
                off += kdim

        x = x_ref[...] + _rms(acc, g1_ref[...])
        h = _rms(x, g2_ref[...]).astype(BF16)
        f = None
        off = 0
        for width in FF_CHUNKS:
            fc = slice(off, off + width)
            m = (_silu(_dot(h, wg_ref[:, fc])) * _dot(h, wu_ref[:, fc])).astype(BF16)
            part = _dot(m, wd_ref[fc, :])
            f = part if f is None else f + part
            off += width
        o_ref[...] = x + _rms(f, g3_ref[...])

    pl.when(t < n_tiles)(lambda: tile(long_in, xl_ref, ol_ref))
    pl.when(t == n_tiles)(lambda: tile(short_in, xs_ref, os_ref))


def _proj_ffn(mix_long, mix_short, ng, wo, x_long, x_short, g1, g2, wg, wu, wd, layer, g3, tm):
    bsz, seq, _ = x_long.shape
    tiles_per_seq = seq // tm
    n_tiles = bsz * tiles_per_seq
    n_short = x_short.shape[0] * x_short.shape[1]
    flat = lambda a: a.reshape(1, n_short, a.shape[-1])
    tile_idx = lambda t: jnp.minimum(t, n_tiles - 1)
    ltok = lambda width: pl.BlockSpec((None, tm, width),
                                      lambda t: (tile_idx(t) // tiles_per_seq, tile_idx(t) % tiles_per_seq, 0))
    stok = lambda width: pl.BlockSpec((None, n_short, width), lambda t: (0, 0, 0))
    gate = ng is not None
    ng_args, ng_specs = ([ng], [_const_spec((1, SSM_INNER))]) if gate else ([], [])
    hbm = pl.BlockSpec(memory_space=pl.ANY)
    out_long, out_short = pl.pallas_call(
        functools.partial(_proj_ffn_kernel, len(mix_long), gate, n_tiles, layer),
        grid=(n_tiles + 1,),
        in_specs=[ltok(a.shape[-1]) for a in mix_long] + [stok(a.shape[-1]) for a in mix_short] + ng_specs + [
            hbm, ltok(D_MODEL), stok(D_MODEL), _const_spec((1, D_MODEL)), _const_spec((1, D_MODEL)),
            hbm, hbm, hbm, _const_spec((1, D_MODEL))],
        out_specs=[ltok(D_MODEL), stok(D_MODEL)],
        out_shape=[jax.ShapeDtypeStruct(x_long.shape, F32), jax.ShapeDtypeStruct((1, n_short, D_MODEL), F32)],
        scratch_shapes=[pltpu.VMEM(wo.shape, BF16), pltpu.VMEM((D_MODEL, D_FF), BF16), pltpu.VMEM((D_MODEL, D_FF), BF16),
                        pltpu.VMEM((D_FF, D_MODEL), BF16),
                        pltpu.VMEM((STAGE_SLOTS, wo.shape[0] // WEIGHT_CHUNKS, D_MODEL), F32),
                        pltpu.VMEM((STAGE_SLOTS, D_MODEL // WEIGHT_CHUNKS, D_FF), F32),
                        pltpu.VMEM((STAGE_SLOTS, D_FF // WEIGHT_CHUNKS, D_MODEL), F32),
                        pltpu.SemaphoreType.DMA((STAGE_SLOTS,))],
        compiler_params=pltpu.CompilerParams(dimension_semantics=("arbitrary",), vmem_limit_bytes=VMEM_LIMIT),
        name="out_proj_ffn",
    )(*mix_long, *[flat(a) for a in mix_short], *ng_args, wo, x_long, flat(x_short), g1, g2, wg, wu, wd, g3)
    return out_long, out_short.reshape(x_short.shape)


def _split3(v):
    t1 = v.astype(BF16)
    r = v - t1.astype(F32)
    t2 = r.astype(BF16)
    t3 = (r - t2.astype(F32)).astype(BF16)
    return t1, t2, t3


def _pack3(v):
    t1, t2, t3 = _split3(v)
    lane = lax.broadcasted_iota(jnp.int32, v.shape, 1)
    return jnp.where(lane < SSM_HEADS, t1,
                     jnp.where(lane < 2 * SSM_HEADS, t2,
                               jnp.where(lane < 3 * SSM_HEADS, t3, jnp.zeros_like(t3))))


def _ssm_in_kernel(tm, cl, x_ref, g_ref, w_ref, wdt_ref, hist_ref, cw_ref, cb_ref, dtb_ref, alog_ref,
                   z_ref, xs_ref, bm_ref, cm_ref, ta_ref, tdt_ref, nh_ref, carry_ref):
    i = pl.program_id(1)

    @pl.when(i == 0)
    def _():
        carry_ref[:, SUBLANES - (SSM_CONV - 1):, :] = hist_ref[...]

    h = _rms(x_ref[...], g_ref[...]).astype(BF16)

    dt = jax.nn.softplus(_dot(h, wdt_ref[...]) + dtb_ref[...])
    tdt_ref[...] = _pack3(dt)
    d1, d2, d3 = _split3(dt * (-jnp.exp(alog_ref[...])))
    tcol = lax.broadcasted_iota(jnp.int32, (cl, 3 * cl), 1) % cl
    trow = lax.broadcasted_iota(jnp.int32, (cl, 3 * cl), 0)
    tril = jnp.where(tcol <= trow, 1.0, 0.0).astype(BF16)
    acum = []
    for c in range(tm // cl):
        rows = slice(c * cl, (c + 1) * cl)
        acum.append(_dot(tril, jnp.concatenate([d1[rows], d2[rows], d3[rows]], axis=0)))
    ta_ref[...] = _pack3(jnp.concatenate(acum, axis=0))

    for n in range(SSM_INNER // COL_CHUNK):
        cols = slice(n * COL_CHUNK, (n + 1) * COL_CHUNK)
        z_ref[:, cols] = _silu(_dot(h, w_ref[:, cols])).astype(BF16)

    for n in range(SSM_CONV_CH // COL_CHUNK):
        cols = slice(n * COL_CHUNK, (n + 1) * COL_CHUNK)
        raw = _dot(h, w_ref[:, SSM_INNER + cols.start:SSM_INNER + cols.stop])
        y = _silu(_causal_conv(carry_ref, nh_ref, raw, cw_ref[:, cols], cols) + cb_ref[:, cols])
        if cols.stop <= SSM_INNER:
            xs_ref[:, cols] = y
        elif cols.stop <= SSM_INNER + SSM_GN:
            bm_ref[:, cols.start - SSM_INNER:cols.stop - SSM_INNER] = y.astype(BF16)
        else:
            lo = SSM_INNER + SSM_GN
            cm_ref[:, cols.start - lo:cols.stop - lo] = y.astype(BF16)


def _ssm_in(x, g, w, wdt, hist, cw, cb, dtb, alog, tm, cl):
    bsz, seq, _ = x.shape
    xt, nseq = _tile_rows(x, tm)
    groups, rows, _ = xt.shape
    tok = lambda width: pl.BlockSpec((None, tm, width), lambda b, i: (b, i, 0))
    hist_spec = pl.BlockSpec((nseq, SSM_CONV - 1, SSM_CONV_CH), lambda b, i: (b, 0, 0))
    outs = pl.pallas_call(
        functools.partial(_ssm_in_kernel, tm, cl),
        grid=(groups, rows // tm),
        in_specs=[tok(D_MODEL), _const_spec((1, D_MODEL)), _const_spec(w.shape), _const_spec((D_MODEL, LANES)),
                  hist_spec,
                  _const_spec((SSM_CONV, SSM_CONV_CH)), _const_spec((1, SSM_CONV_CH)), _const_spec((1, LANES)),
                  _const_spec((1, LANES))],
        out_specs=[tok(SSM_INNER), tok(SSM_INNER), tok(SSM_GN), tok(SSM_GN), tok(LANES), tok(LANES), hist_spec],
        out_shape=[jax.ShapeDtypeStruct((groups, rows, SSM_INNER), BF16),
                   jax.ShapeDtypeStruct((groups, rows, SSM_INNER), F32),
                   jax.ShapeDtypeStruct((groups, rows, SSM_GN), BF16),
                   jax.ShapeDtypeStruct((groups, rows, SSM_GN), BF16),
                   jax.ShapeDtypeStruct((groups, rows, LANES), BF16),
                   jax.ShapeDtypeStruct((groups, rows, LANES), BF16),
                   jax.ShapeDtypeStruct((bsz, SSM_CONV - 1, SSM_CONV_CH), F32)],
        scratch_shapes=[pltpu.VMEM((nseq, SUBLANES, SSM_CONV_CH), F32)],
        compiler_params=_params(),
        name="ssm_in_proj",
    )(xt, g, w, wdt, hist, cw, cb, dtb, alog)
    return [o.reshape(bsz, seq, o.shape[-1]) for o in outs[:-1]] + [outs[-1]]


def _pad_rows(a, rows):
    if a.shape[0] == rows:
        return a
    return jnp.concatenate([a, jnp.zeros((rows - a.shape[0], a.shape[1]), a.dtype)], axis=0)


def _ssd_kernel(ts, cl, xs_ref, bm_ref, cm_ref, ta_ref, tdt_ref, h0_ref, dskip_ref,
                yo_ref, hout_ref, ht_ref, y_ref, e3_ref, aexp_ref, xdt_ref, st_ref, cb_ref, bmt_ref, w2_ref, rhs_ref):
    b = pl.program_id(0)
    i = pl.program_id(1)
    p = SSM_HEADDIM
    nc = ts // cl
    pairs = SSM_GROUP_W // LANES

    @pl.when((b == 0) & (i == 0))
    def _():
        kk = lax.broadcasted_iota(jnp.int32, (LANES, SSM_INNER), 0)
        cc = lax.broadcasted_iota(jnp.int32, (LANES, SSM_INNER), 1)
        hit = (kk % SSM_HEADS == cc // p) & (kk < 3 * SSM_HEADS)
        e3_ref[...] = jnp.where(hit, 1.0, 0.0).astype(BF16)

    @pl.when(i == 0)
    def _():
        ht_ref[...] = h0_ref[...].T

    lane = lax.broadcasted_iota(jnp.int32, (cl, LANES), 1)
    row = lax.broadcasted_iota(jnp.int32, (cl, LANES), 0)
    causal2 = (lane % p) <= row
    left = lane.astype(F32).astype(BF16) < p
    diag2 = (lane % p) == row
    groups = [(gi, slice(gi * SSM_GROUP_W, (gi + 1) * SSM_GROUP_W), slice(gi * SSM_STATE, (gi + 1) * SSM_STATE))
              for gi in range(SSM_GROUPS)]
    chunks = [(c, slice(c * cl, (c + 1) * cl)) for c in range(nc)]

    for gi, gcols, _ in groups:
        e3g = e3_ref[:, gcols]
        aexp_ref[:, gcols] = _dot(ta_ref[...], e3g)
        xdt_ref[:, gcols] = (xs_ref[:, gcols] * _dot(tdt_ref[...], e3g)).astype(BF16)

    for c, rows in chunks:
        for gi, _, scol in groups:
            bmp = _pad_rows(bm_ref[rows, scol], p)
            cbm = _dot_nt(cm_ref[rows, scol], bmp)
            cb_ref[c * SSM_GROUPS + gi] = jnp.concatenate([cbm, cbm], axis=1)
            bmt_ref[c * SSM_GROUPS + gi] = bmp.astype(F32).T.astype(BF16)

    for c, rows in chunks:
        for gi, _, _ in groups:
            for jj in range(pairs):
                idx = (c * SSM_GROUPS + gi) * pairs + jj
                cols = slice(gi * SSM_GROUP_W + jj * LANES, gi * SSM_GROUP_W + (jj + 1) * LANES)
                a_pair = aexp_ref[rows, cols]
                a_src = jnp.sum(jnp.where(diag2, a_pair, 0.0), axis=0, keepdims=True)
                seg = a_pair - a_src
                dec = jnp.exp(jnp.where(causal2, seg, -jnp.inf))
                w2_ref[idx] = (dec * cb_ref[c * SSM_GROUPS + gi]).astype(BF16)
                x2 = xdt_ref[rows, cols]
                top = _pad_rows(jnp.where(left, x2, jnp.zeros_like(x2)), p)
                bot = _pad_rows(jnp.where(left, jnp.zeros_like(x2), x2), p)
                rhs_ref[idx] = jnp.concatenate([top, bot], axis=0)

    for c, rows in chunks:
        for gi, _, _ in groups:
            for jj in range(pairs):
                idx = (c * SSM_GROUPS + gi) * pairs + jj
                cols = slice(gi * SSM_GROUP_W + jj * LANES, gi * SSM_GROUP_W + (jj + 1) * LANES)
                y_ref[rows, cols] = _dot(w2_ref[idx], rhs_ref[idx])

    for c, rows in chunks:
        for gi, gcols, _ in groups:
            a_g = aexp_ref[rows, gcols]
            xw = xdt_ref[rows, gcols] * jnp.exp(a_g[cl - 1:cl, :] - a_g).astype(BF16)
            st_ref[c, :, gcols] = _dot(bmt_ref[c * SSM_GROUPS + gi], _pad_rows(xw, p))

    for c, rows in chunks:
        for gi, gcols, scol in groups:
            hprev = ht_ref[:, gcols]
            a_g = aexp_ref[rows, gcols]
            y_off = _dot(cm_ref[rows, scol], hprev.astype(BF16)) * jnp.exp(a_g)
            y_ref[rows, gcols] = y_ref[rows, gcols] + y_off
            ht_ref[:, gcols] = hprev * jnp.exp(a_g[cl - 1:cl, :]) + st_ref[c, :, gcols]

    yo_ref[...] = (y_ref[...] + dskip_ref[...] * xs_ref[...]).astype(BF16)

    @pl.when(i == pl.num_programs(1) - 1)
    def _():
        hout_ref[...] = ht_ref[...].T


def _ssd(xs, bm, cm, ta, tdt, h0, dskip, ts, cl):
    bsz, seq, _ = xs.shape
    tok = lambda width: pl.BlockSpec((None, ts, width), lambda b, i: (b, i, 0))
    st_spec = pl.BlockSpec((None, SSM_INNER, SSM_STATE), lambda b, i: (b, 0, 0))
    nc = ts // cl
    n_cg = nc * SSM_GROUPS
    n_pairs = n_cg * (SSM_GROUP_W // LANES)
    return pl.pallas_call(
        functools.partial(_ssd_kernel, ts, cl),
        grid=(bsz, seq // ts),
        in_specs=[tok(SSM_INNER), tok(SSM_GN), tok(SSM_GN), tok(LANES), tok(LANES), st_spec,
                  _const_spec((1, SSM_INNER))],
        out_specs=[tok(SSM_INNER), st_spec],
        out_shape=[jax.ShapeDtypeStruct((bsz, seq, SSM_INNER), BF16),
                   jax.ShapeDtypeStruct((bsz, SSM_INNER, SSM_STATE), F32)],
        scratch_shapes=[pltpu.VMEM((SSM_STATE, SSM_INNER), F32), pltpu.VMEM((ts, SSM_INNER), F32),
                        pltpu.VMEM((LANES, SSM_INNER), BF16), pltpu.VMEM((ts, SSM_INNER), F32),
                        pltpu.VMEM((ts, SSM_INNER), BF16), pltpu.VMEM((nc, SSM_STATE, SSM_INNER), F32),
                        pltpu.VMEM((n_cg, cl, LANES), F32), pltpu.VMEM((n_cg, SSM_STATE, SSM_HEADDIM), BF16),
                        pltpu.VMEM((n_pairs, cl, LANES), BF16), pltpu.VMEM((n_pairs, LANES, LANES), BF16)],
        compiler_params=_params(),
        name="ssd_scan",
    )(xs, bm, cm, ta, tdt, h0, dskip)


def _layer0_mixers(x, pos0, sconv_hist, k_cache, v_cache, wts, tiles):
    t_in, t_att, c_att = tiles
    q, k, v, gated, new_sconv = _attn_in(x, wts["g"][0][0], wts["ab_w_in"], sconv_hist, wts["sconv_w"], pos0, t_in)
    masked = k_cache is None
    k_prev, v_prev = (k, v) if masked else (k_cache, v_cache)
    att = _attn(wts["sinks"], q, k, v, k_prev, v_prev, t_att, c_att, masked)
    bsz, seq, _ = x.shape
    new_k = k[:, seq - min(WINDOW, seq):].reshape(bsz, -1, A_KV_HEADS, HEAD_DIM)
    new_v = v[:, seq - min(WINDOW, seq):].reshape(bsz, -1, A_KV_HEADS, HEAD_DIM)
    return [att, gated], (new_k, new_v, new_sconv)


def _layer1_mixer(x, conv_hist, ssm_state, wts, tiles):
    t_sin, t_ssd, c_ssd = tiles
    bsz = x.shape[0]
    z, xs, bm, cm, ta, tdt, new_conv = _ssm_in(
        x, wts["g"][1][0], wts["ssm_w_in"], wts["w_dt"], conv_hist, wts["ssm_conv_w"],
        wts["ssm_conv_b"], wts["dt_bias"], wts["a_log"], t_sin, c_ssd)
    y, new_state = _ssd(xs, bm, cm, ta, tdt, ssm_state.reshape(bsz, SSM_INNER, SSM_STATE),
                        wts["d_skip"], t_ssd, c_ssd)
    return [y, z], (new_conv, new_state.reshape(bsz, SSM_HEADS, SSM_HEADDIM, SSM_STATE))


def kernel(x_prompt, x_sample, cache_attn_k, cache_attn_v, state_sconv, state_ssm_conv, state_ssm, norm_g,
           ab_w_in, ab_w_out, attn_sinks, sconv_w, ssm_w_in, ssm_conv_w, ssm_conv_b, ssm_dt_bias, ssm_a_log,
           ssm_d, ssm_norm_g, ssm_w_out, ffn_w_gate, ffn_w_up, ffn_w_down):
    w_in1 = ssm_w_in[0]
    wts = {
        "ab_w_in": ab_w_in[0].astype(BF16),
        "ab_w_out": ab_w_out[0],
        "sinks": attn_sinks[0],
        "sconv_w": sconv_w[0],
        "ssm_w_in": w_in1.astype(BF16),
        "w_dt": jnp.tile(w_in1[:, SSM_INNER + SSM_CONV_CH:], (1, DT_REP)).astype(BF16),
        "ssm_conv_w": ssm_conv_w[0],
        "ssm_conv_b": ssm_conv_b[0].reshape(1, -1),
        "dt_bias": jnp.tile(ssm_dt_bias[0], DT_REP).reshape(1, -1),
        "a_log": jnp.tile(ssm_a_log[0], DT_REP).reshape(1, -1),
        "d_skip": jnp.repeat(ssm_d[0], SSM_HEADDIM).reshape(1, -1),
        "ssm_norm_g": ssm_norm_g[0].reshape(1, -1),
        "ssm_w_out": ssm_w_out[0],
        "wg": ffn_w_gate,
        "wu": ffn_w_up,
        "wd": ffn_w_down,
    }
    wts["g"] = [[norm_g[l, j].reshape(1, -1) for j in range(4)] for l in range(norm_g.shape[0])]
    g = wts["g"]
    bp = x_prompt.shape[0]
    bs, ls = x_sample.shape[0], x_sample.shape[1]
    t_ffn = 512

    mix_p, (kp, vp, scp) = _layer0_mixers(x_prompt, 0, jnp.zeros((bp, B_CONV - 1, B_WIDTH), F32), None, None,
                                          wts, (1024, 512, CHUNK))
    mix_s, (ks, vs, scs) = _layer0_mixers(x_sample, PAST_LEN, state_sconv[0],
                                          cache_attn_k[0].reshape(bs, -1, KV_WIDTH),
                                          cache_attn_v[0].reshape(bs, -1, KV_WIDTH), wts, (bs * ls, ls, ls))
    xp, xs = _proj_ffn(mix_p, mix_s, None, wts["ab_w_out"], x_prompt, x_sample, g[0][1], g[0][2],
                       wts["wg"], wts["wu"], wts["wd"], 0, g[0][3], t_ffn)

    mix_p, (ccp, ssp) = _layer1_mixer(xp, jnp.zeros((bp, SSM_CONV - 1, SSM_CONV_CH), F32),
                                      jnp.zeros((bp, SSM_HEADS, SSM_HEADDIM, SSM_STATE), F32), wts, (512, 512, CHUNK))
    mix_s, (ccs, sss) = _layer1_mixer(xs, state_ssm_conv[0], state_ssm[0], wts, (bs * ls, ls, ls))
    yp, ys = _proj_ffn(mix_p, mix_s, wts["ssm_norm_g"], wts["ssm_w_out"], xp, xs, g[1][1], g[1][2],
                       wts["wg"], wts["wu"], wts["wd"], 1, g[1][3], t_ffn)

    lead = lambda a: a[None]
    return (yp, ys, lead(kp), lead(vp), lead(scp), lead(ccp), lead(ssp),
            lead(ks), lead(vs), lead(scs), lead(ccs), lead(sss))
```

```python
import functools

import jax
import jax.numpy as jnp
from jax import lax
from jax.experimental import pallas as pl
from jax.experimental.pallas import tpu as pltpu

F32 = jnp.float32
BF16 = jnp.bfloat16

D_MODEL = 1024
CHUNK = 64
EPS = 1e-6
PAST_LEN = 4096

A_HEADS = 8
A_KV_HEADS = 2
A_GROUP = A_HEADS // A_KV_HEADS
HEAD_DIM = 64
A_WIDTH = A_HEADS * HEAD_DIM
KV_WIDTH = A_KV_HEADS * HEAD_DIM
WINDOW = 128
ROPE_DIM = HEAD_DIM // 4
ROPE_HALF = ROPE_DIM // 2
ROPE_THETA = 500000.0
ATTN_SCALE = HEAD_DIM ** -0.5

B_WIDTH = D_MODEL // 2
B_CONV = 3
AB_IN = A_WIDTH + 2 * KV_WIDTH + 3 * B_WIDTH

SSM_INNER = 2 * D_MODEL
SSM_HEADDIM = 64
SSM_HEADS = SSM_INNER // SSM_HEADDIM
SSM_GROUPS = 4
SSM_GROUP_W = SSM_INNER // SSM_GROUPS
SSM_STATE = 128
SSM_CONV = 4
SSM_GN = SSM_GROUPS * SSM_STATE
SSM_CONV_CH = SSM_INNER + 2 * SSM_GN

D_FF = -(-8 * D_MODEL // (3 * 256)) * 256

LANES = 128
SUBLANES = 8
MXU_DIM = 256
DT_REP = LANES // SSM_HEADS
VMEM_LIMIT = 56 * 1024 * 1024
COL_CHUNK = MXU_DIM
FF_CHUNKS = (6 * MXU_DIM, 5 * MXU_DIM)
assert sum(FF_CHUNKS) == D_FF


def _rms(x, g):
    return x * lax.rsqrt(jnp.mean(x * x, axis=-1, keepdims=True) + EPS) * g


def _dot(a, b):
    return jnp.dot(a, b, preferred_element_type=F32)


def _dot_nt(a, b):
    return lax.dot_general(a, b, (((1,), (1,)), ((), ())), preferred_element_type=F32)


def _silu(x):
    return x * jax.nn.sigmoid(x)


def _const_spec(shape):
    return pl.BlockSpec(shape, lambda *_: (0,) * len(shape), pipeline_mode=pl.Buffered(1))


def _layer_spec(shape, layer):
    return pl.BlockSpec((None,) + shape, lambda *_: (layer,) + (0,) * len(shape), pipeline_mode=pl.Buffered(1))


def _params():
    return pltpu.CompilerParams(dimension_semantics=("arbitrary", "arbitrary"),
                                vmem_limit_bytes=VMEM_LIMIT)


def _causal_conv(carry_ref, nh_ref, u, cw, cols):
    width = cw.shape[0]
    rows, ch = u.shape
    nseq = carry_ref.shape[0]
    u4 = u.reshape(nseq, rows // (nseq * SUBLANES), SUBLANES, ch)
    ext4 = jnp.concatenate([carry_ref[:, :, cols].reshape(nseq, 1, SUBLANES, ch), u4], axis=1)
    sub = lax.broadcasted_iota(jnp.int32, (1, 1, SUBLANES, ch), 2)

    def shift(above, here, k):
        return pltpu.roll(jnp.where(sub >= SUBLANES - k, above, here), k, 2)

    if width == 4:
        s1 = shift(jnp.concatenate([ext4[:, :1], ext4[:, :-1]], axis=1), ext4, 1)
        b = s1 * cw[0:1] + ext4 * cw[1:2]
        y4 = shift(b[:, :-1], b[:, 1:], 2) + (s1[:, 1:] * cw[2:3] + u4 * cw[3:4])
    else:
        y4 = None
        for j in range(width - 1):
            term = shift(ext4[:, :-1], u4, width - 1 - j) * cw[j:j + 1]
            y4 = term if y4 is None else y4 + term
        y4 = y4 + u4 * cw[width - 1:width]
    carry_ref[:, :, cols] = u4[:, -1]
    nh_ref[:, :, cols] = u4[:, -1, SUBLANES - (width - 1):]
    return y4.reshape(rows, ch)


def _rope_freq(shape):
    lane = lax.broadcasted_iota(jnp.int32, shape, 1)
    d = lane % HEAD_DIM
    expo = -((d % ROPE_HALF).astype(F32)) / ROPE_HALF
    freq = jnp.power(jnp.full(shape, ROPE_THETA, F32), expo)
    return jnp.where(d < ROPE_DIM, freq, 0.0), d


def _attn_in_kernel(pos0, tm, seq_rows, x_ref, g_ref, w_ref, hist_ref, cw_ref,
                    q_ref, k_ref, v_ref, gated_ref, nh_ref,
                    cr_ref, sr_ref, carry_ref, p_ref):
    b = pl.program_id(0)
    i = pl.program_id(1)

    @pl.when((b == 0) & (i == 0))
    def _():
        freq, _ = _rope_freq((tm, LANES))
        ang = (lax.broadcasted_iota(jnp.int32, (tm, LANES), 0) % seq_rows).astype(F32) * freq
        cr_ref[...] = jnp.cos(ang)
        sr_ref[...] = jnp.sin(ang)

    @pl.when(i == 0)
    def _():
        carry_ref[:, SUBLANES - (B_CONV - 1):, :] = hist_ref[...]

    freq1, d1 = _rope_freq((1, LANES))
    base = (pos0 + i * tm).astype(F32) * freq1
    cb = jnp.cos(base)
    sb = jnp.sin(base)
    cr = cr_ref[...]
    sr = sr_ref[...]
    cos_t = cr * cb - sr * sb
    sin_t = sr * cb + cr * sb
    m_lo = jnp.where(d1 < ROPE_HALF, -1.0, 0.0)
    m_hi = jnp.where((d1 >= ROPE_HALF) & (d1 < ROPE_DIM), 1.0, 0.0)

    def rope(t):
        partner = pltpu.roll(t, LANES - ROPE_HALF, 1) * m_lo + pltpu.roll(t, ROPE_HALF, 1) * m_hi
        return t * cos_t + partner * sin_t

    h = _rms(x_ref[...], g_ref[...]).astype(BF16)
    o1 = A_WIDTH
    o2 = o1 + KV_WIDTH
    o3 = o2 + KV_WIDTH
    o4 = o3 + B_WIDTH
    o5 = o4 + B_WIDTH

    for n in range(AB_IN // MXU_DIM):
        cols = slice(n * MXU_DIM, (n + 1) * MXU_DIM)
        p_ref[:, cols] = _dot(h, w_ref[:, cols])

    u = p_ref[:, o4:o5] * p_ref[:, o5:]
    y = _causal_conv(carry_ref, nh_ref, u, cw_ref[...], slice(None))
    gated_ref[...] = (p_ref[:, o3:o4] * y).astype(BF16)
    k_ref[...] = rope(p_ref[:, o1:o2])
    v_ref[...] = p_ref[:, o2:o3]
    for s in range(A_WIDTH // LANES):
        q_ref[:, s * LANES:(s + 1) * LANES] = (rope(p_ref[:, s * LANES:(s + 1) * LANES]) * ATTN_SCALE).astype(BF16)


def _tile_rows(x, tm):
    bsz, seq, d = x.shape
    nseq = max(tm // seq, 1)
    return x.reshape(bsz // nseq, nseq * seq, d), nseq


def _attn_in(x, g, w, hist, cw, pos0, tm):
    bsz, seq, _ = x.shape
    xt, nseq = _tile_rows(x, tm)
    groups, rows, _ = xt.shape
    tok = lambda width: pl.BlockSpec((None, tm, width), lambda b, i: (b, i, 0))
    hist_spec = pl.BlockSpec((nseq, B_CONV - 1, B_WIDTH), lambda b, i: (b, 0, 0))
    outs = pl.pallas_call(
        functools.partial(_attn_in_kernel, pos0, tm, tm // nseq),
        grid=(groups, rows // tm),
        in_specs=[tok(D_MODEL), _const_spec((1, D_MODEL)), _const_spec((D_MODEL, AB_IN)), hist_spec,
                  _const_spec((B_CONV, B_WIDTH))],
        out_specs=[tok(A_WIDTH), tok(KV_WIDTH), tok(KV_WIDTH), tok(B_WIDTH), hist_spec],
        out_shape=[jax.ShapeDtypeStruct((groups, rows, A_WIDTH), BF16),
                   jax.ShapeDtypeStruct((groups, rows, KV_WIDTH), F32),
                   jax.ShapeDtypeStruct((groups, rows, KV_WIDTH), F32),
                   jax.ShapeDtypeStruct((groups, rows, B_WIDTH), BF16),
                   jax.ShapeDtypeStruct((bsz, B_CONV - 1, B_WIDTH), F32)],
        scratch_shapes=[pltpu.VMEM((tm, LANES), F32), pltpu.VMEM((tm, LANES), F32),
                        pltpu.VMEM((nseq, SUBLANES, B_WIDTH), F32), pltpu.VMEM((tm, AB_IN), F32)],
        compiler_params=_params(),
        name="attn_in_proj",
    )(xt, g, w, hist, cw)
    return [o.reshape(bsz, seq, o.shape[-1]) for o in outs[:-1]] + [outs[-1]]


def _attn_kernel(tq, cq, masked, sinks_ref, q_ref, kc_ref, vc_ref, kp_ref, vp_ref, att_ref,
                 kk_ref, vx_ref, s_ref, e_ref):
    i = pl.program_id(1)
    nk = WINDOW + cq
    nkp = MXU_DIM
    rows_kv = WINDOW + tq
    hd = HEAD_DIM

    kk_ref[0:WINDOW, :] = kp_ref[...].astype(BF16)
    kk_ref[WINDOW:rows_kv, :] = kc_ref[...].astype(BF16)
    kk_ref[rows_kv:, :] = jnp.zeros((kk_ref.shape[0] - rows_kv, KV_WIDTH), BF16)
    vall = jnp.concatenate([vp_ref[...], vc_ref[...]], axis=0)
    vx_ref[0:rows_kv, :] = jnp.concatenate([vall[:, :hd], vall[:, :hd], vall[:, hd:], vall[:, hd:]], axis=1)
    vx_ref[rows_kv:, :] = jnp.zeros((vx_ref.shape[0] - rows_kv, 2 * LANES), F32)

    col = lax.broadcasted_iota(jnp.int32, (A_GROUP * cq, nkp), 1)
    krow = lax.broadcasted_iota(jnp.int32, (nkp, LANES), 0)
    ones = jnp.ones((nkp, LANES), BF16)
    bodies = [(c, kh) for c in range(tq // cq) for kh in range(A_KV_HEADS)]
    fills = []
    for kh in range(A_KV_HEADS):
        sink = jnp.concatenate(
            [jnp.full((cq, nkp), sinks_ref[kh * A_GROUP + g], F32) for g in range(A_GROUP)], axis=0)
        fills.append(jnp.where(col == nk, sink, -jnp.inf))

    for n, (c, kh) in enumerate(bodies):
        r0 = c * cq
        heads = [kh * A_GROUP + g for g in range(A_GROUP)]
        kt = kk_ref[r0:r0 + nkp, kh * hd:(kh + 1) * hd]
        qs = jnp.concatenate([q_ref[r0:r0 + cq, hh * hd:(hh + 1) * hd] for hh in heads], axis=0)
        s = _dot_nt(qs, kt)
        valid = col < nk
        if masked and r0 < WINDOW:
            valid = valid & ((col >= WINDOW - r0) | (i > 0))
        s_ref[n] = jnp.where(valid, s, fills[kh])

    for n in range(len(bodies)):
        s = s_ref[n]
        m = jnp.max(s, axis=-1, keepdims=True)
        e_ref[n] = jnp.exp(s - m).astype(BF16)

    for n, (c, kh) in enumerate(bodies):
        r0 = c * cq
        vt = jnp.where(krow == nk, 0.0, vx_ref[r0:r0 + nkp, kh * LANES:(kh + 1) * LANES]).astype(BF16)
        e = e_ref[n]
        o = _dot(e, vt) / _dot(e, ones)
        for g in range(A_GROUP):
            hh = kh * A_GROUP + g
            lanes = slice((hh % 2) * hd, (hh % 2 + 1) * hd)
            att_ref[r0:r0 + cq, hh * hd:(hh + 1) * hd] = o[g * cq:(g + 1) * cq, lanes].astype(BF16)


def _attn(sinks, q, k, v, k_prev, v_prev, tq, cq, masked):
    bsz, seq, _ = q.shape
    grid = (bsz, seq // tq)
    tok = lambda width: pl.BlockSpec((None, tq, width), lambda b, i: (b, i, 0))
    if masked:
        blocks_per_tile = tq // WINDOW
        prev = pl.BlockSpec((None, WINDOW, KV_WIDTH),
                            lambda b, i: (b, jnp.maximum(i * blocks_per_tile - 1, 0), 0))
    else:
        prev = pl.BlockSpec((None, WINDOW, KV_WIDTH), lambda b, i: (b, 0, 0))
    n_bodies = (tq // cq) * A_KV_HEADS
    kv_rows = (tq // cq - 1) * cq + MXU_DIM
    return pl.pallas_call(
        functools.partial(_attn_kernel, tq, cq, masked),
        grid=grid,
        in_specs=[pl.BlockSpec(memory_space=pltpu.SMEM),
                  tok(A_WIDTH), tok(KV_WIDTH), tok(KV_WIDTH), prev, prev],
        out_specs=tok(A_WIDTH),
        out_shape=jax.ShapeDtypeStruct((bsz, seq, A_WIDTH), BF16),
        scratch_shapes=[pltpu.VMEM((kv_rows, KV_WIDTH), BF16), pltpu.VMEM((kv_rows, 2 * LANES), F32),
                        pltpu.VMEM((n_bodies, A_GROUP * cq, MXU_DIM), F32),
                        pltpu.VMEM((n_bodies, A_GROUP * cq, MXU_DIM), BF16)],
        compiler_params=_params(),
        name="band_attention",
    )(sinks, q, k, v, k_prev, v_prev)


def _proj_ffn_kernel(n_in, gate, n_tiles, *refs):
    long_in, short_in = refs[:n_in], refs[n_in:2 * n_in]
    refs = refs[2 * n_in:]
    ng_ref = None
    if gate:
        ng_ref, refs = refs[0], refs[1:]
    wo_ref, xl_ref, xs_ref, g1_ref, g2_ref, wg_ref, wu_ref, wd_ref, g3_ref, ol_ref, os_ref = refs
    t = pl.program_id(0)

    def tile(ins, x_ref, o_ref):
        acc = None
        if gate:
            y_ref, z_ref = ins
            for gi in range(SSM_GROUPS):
                gcols = slice(gi * SSM_GROUP_W, (gi + 1) * SSM_GROUP_W)
                gg = y_ref[:, gcols].astype(F32) * z_ref[:, gcols].astype(F32)
                gg = gg * lax.rsqrt(jnp.mean(gg * gg, axis=-1, keepdims=True) + EPS)
                part = _dot((gg * ng_ref[:, gcols]).astype(BF16), wo_ref[gcols, :])
                acc = part if acc is None else acc + part
        else:
            off = 0
            for a_ref in ins:
                kdim = a_ref.shape[-1]
                part = _dot(a_ref[...], wo_ref[off:off + kdim, :])
                acc = part if acc is None else acc + part
                off += kdim

        x = x_ref[...] + _rms(acc, g1_ref[...])
        h = _rms(x, g2_ref[...]).astype(BF16)
        f = None
        off = 0
        for width in FF_CHUNKS:
            fc = slice(off, off + width)
            m = (_silu(_dot(h, wg_ref[:, fc])) * _dot(h, wu_ref[:, fc])).astype(BF16)
            part = _dot(m, wd_ref[fc, :])
            f = part if f is None else f + part
            off += width
        o_ref[...] = x + _rms(f, g3_ref[...])

    pl.when(t < n_tiles)(lambda: tile(long_in, xl_ref, ol_ref))
    pl.when(t == n_tiles)(lambda: tile(short_in, xs_ref, os_ref))


def _proj_ffn(mix_long, mix_short, ng, wo, x_long, x_short, g1, g2, wg, wu, wd, layer, g3, tm):
    bsz, seq, _ = x_long.shape
    tiles_per_seq = seq // tm
    n_tiles = bsz * tiles_per_seq
    n_short = x_short.shape[0] * x_short.shape[1]
    flat = lambda a: a.reshape(1, n_short, a.shape[-1])
    tile_idx = lambda t: jnp.minimum(t, n_tiles - 1)
    ltok = lambda width: pl.BlockSpec((None, tm, width),
                                      lambda t: (tile_idx(t) // tiles_per_seq, tile_idx(t) % tiles_per_seq, 0))
    stok = lambda width: pl.BlockSpec((None, n_short, width), lambda t: (0, 0, 0))
    gate = ng is not None
    ng_args, ng_specs = ([ng], [_const_spec((1, SSM_INNER))]) if gate else ([], [])
    out_long, out_short = pl.pallas_call(
        functools.partial(_proj_ffn_kernel, len(mix_long), gate, n_tiles),
        grid=(n_tiles + 1,),
        in_specs=[ltok(a.shape[-1]) for a in mix_long] + [stok(a.shape[-1]) for a in mix_short] + ng_specs + [
            _const_spec(wo.shape), ltok(D_MODEL), stok(D_MODEL), _const_spec((1, D_MODEL)), _const_spec((1, D_MODEL)),
            _layer_spec((D_MODEL, D_FF), layer), _layer_spec((D_MODEL, D_FF), layer),
            _layer_spec((D_FF, D_MODEL), layer), _const_spec((1, D_MODEL))],
        out_specs=[ltok(D_MODEL), stok(D_MODEL)],
        out_shape=[jax.ShapeDtypeStruct(x_long.shape, F32), jax.ShapeDtypeStruct((1, n_short, D_MODEL), F32)],
        compiler_params=pltpu.CompilerParams(dimension_semantics=("arbitrary",), vmem_limit_bytes=VMEM_LIMIT),
        name="out_proj_ffn",
    )(*mix_long, *[flat(a) for a in mix_short], *ng_args, wo, x_long, flat(x_short), g1, g2, wg, wu, wd, g3)
    return out_long, out_short.reshape(x_short.shape)


def _split3(v):
    t1 = v.astype(BF16)
    r = v - t1.astype(F32)
    t2 = r.astype(BF16)
    t3 = (r - t2.astype(F32)).astype(BF16)
    return t1, t2, t3


def _pack3(v):
    t1, t2, t3 = _split3(v)
    lane = lax.broadcasted_iota(jnp.int32, v.shape, 1)
    return jnp.where(lane < SSM_HEADS, t1,
                     jnp.where(lane < 2 * SSM_HEADS, t2,
                               jnp.where(lane < 3 * SSM_HEADS, t3, jnp.zeros_like(t3))))


def _ssm_in_kernel(tm, cl, x_ref, g_ref, w_ref, wdt_ref, hist_ref, cw_ref, cb_ref, dtb_ref, alog_ref,
                   z_ref, xs_ref, bm_ref, cm_ref, ta_ref, tdt_ref, nh_ref, carry_ref):
    i = pl.program_id(1)

    @pl.when(i == 0)
    def _():
        carry_ref[:, SUBLANES - (SSM_CONV - 1):, :] = hist_ref[...]

    h = _rms(x_ref[...], g_ref[...]).astype(BF16)

    dt = jax.nn.softplus(_dot(h, wdt_ref[...]) + dtb_ref[...])
    tdt_ref[...] = _pack3(dt)
    d1, d2, d3 = _split3(dt * (-jnp.exp(alog_ref[...])))
    tcol = lax.broadcasted_iota(jnp.int32, (cl, 3 * cl), 1) % cl
    trow = lax.broadcasted_iota(jnp.int32, (cl, 3 * cl), 0)
    tril = jnp.where(tcol <= trow, 1.0, 0.0).astype(BF16)
    acum = []
    for c in range(tm // cl):
        rows = slice(c * cl, (c + 1) * cl)
        acum.append(_dot(tril, jnp.concatenate([d1[rows], d2[rows], d3[rows]], axis=0)))
    ta_ref[...] = _pack3(jnp.concatenate(acum, axis=0))

    for n in range(SSM_INNER // COL_CHUNK):
        cols = slice(n * COL_CHUNK, (n + 1) * COL_CHUNK)
        z_ref[:, cols] = _silu(_dot(h, w_ref[:, cols])).astype(BF16)

    for n in range(SSM_CONV_CH // COL_CHUNK):
        cols = slice(n * COL_CHUNK, (n + 1) * COL_CHUNK)
        raw = _dot(h, w_ref[:, SSM_INNER + cols.start:SSM_INNER + cols.stop])
        y = _silu(_causal_conv(carry_ref, nh_ref, raw, cw_ref[:, cols], cols) + cb_ref[:, cols])
        if cols.stop <= SSM_INNER:
            xs_ref[:, cols] = y
        elif cols.stop <= SSM_INNER + SSM_GN:
            bm_ref[:, cols.start - SSM_INNER:cols.stop - SSM_INNER] = y.astype(BF16)
        else:
            lo = SSM_INNER + SSM_GN
            cm_ref[:, cols.start - lo:cols.stop - lo] = y.astype(BF16)


def _ssm_in(x, g, w, wdt, hist, cw, cb, dtb, alog, tm, cl):
    bsz, seq, _ = x.shape
    xt, nseq = _tile_rows(x, tm)
    groups, rows, _ = xt.shape
    tok = lambda width: pl.BlockSpec((None, tm, width), lambda b, i: (b, i, 0))
    hist_spec = pl.BlockSpec((nseq, SSM_CONV - 1, SSM_CONV_CH), lambda b, i: (b, 0, 0))
    outs = pl.pallas_call(
        functools.partial(_ssm_in_kernel, tm, cl),
        grid=(groups, rows // tm),
        in_specs=[tok(D_MODEL), _const_spec((1, D_MODEL)), _const_spec(w.shape), _const_spec((D_MODEL, LANES)),
                  hist_spec,
                  _const_spec((SSM_CONV, SSM_CONV_CH)), _const_spec((1, SSM_CONV_CH)), _const_spec((1, LANES)),
                  _const_spec((1, LANES))],
        out_specs=[tok(SSM_INNER), tok(SSM_INNER), tok(SSM_GN), tok(SSM_GN), tok(LANES), tok(LANES), hist_spec],
        out_shape=[jax.ShapeDtypeStruct((groups, rows, SSM_INNER), BF16),
                   jax.ShapeDtypeStruct((groups, rows, SSM_INNER), F32),
                   jax.ShapeDtypeStruct((groups, rows, SSM_GN), BF16),
                   jax.ShapeDtypeStruct((groups, rows, SSM_GN), BF16),
                   jax.ShapeDtypeStruct((groups, rows, LANES), BF16),
                   jax.ShapeDtypeStruct((groups, rows, LANES), BF16),
                   jax.ShapeDtypeStruct((bsz, SSM_CONV - 1, SSM_CONV_CH), F32)],
        scratch_shapes=[pltpu.VMEM((nseq, SUBLANES, SSM_CONV_CH), F32)],
        compiler_params=_params(),
        name="ssm_in_proj",
    )(xt, g, w, wdt, hist, cw, cb, dtb, alog)
    return [o.reshape(bsz, seq, o.shape[-1]) for o in outs[:-1]] + [outs[-1]]


def _pad_rows(a, rows):
    if a.shape[0] == rows:
        return a
    return jnp.concatenate([a, jnp.zeros((rows - a.shape[0], a.shape[1]), a.dtype)], axis=0)


def _ssd_kernel(ts, cl, xs_ref, bm_ref, cm_ref, ta_ref, tdt_ref, h0_ref, dskip_ref,
                yo_ref, hout_ref, ht_ref, y_ref, e3_ref, aexp_ref, xdt_ref, st_ref, cb_ref, bmt_ref, w2_ref, rhs_ref):
    b = pl.program_id(0)
    i = pl.program_id(1)
    p = SSM_HEADDIM
    nc = ts // cl
    pairs = SSM_GROUP_W // LANES

    @pl.when((b == 0) & (i == 0))
    def _():
        kk = lax.broadcasted_iota(jnp.int32, (LANES, SSM_INNER), 0)
        cc = lax.broadcasted_iota(jnp.int32, (LANES, SSM_INNER), 1)
        hit = (kk % SSM_HEADS == cc // p) & (kk < 3 * SSM_HEADS)
        e3_ref[...] = jnp.where(hit, 1.0, 0.0).astype(BF16)

    @pl.when(i == 0)
    def _():
        ht_ref[...] = h0_ref[...].T

    lane = lax.broadcasted_iota(jnp.int32, (cl, LANES), 1)
    row = lax.broadcasted_iota(jnp.int32, (cl, LANES), 0)
    causal2 = (lane % p) <= row
    left = lane.astype(F32).astype(BF16) < p
    diag2 = (lane % p) == row
    groups = [(gi, slice(gi * SSM_GROUP_W, (gi + 1) * SSM_GROUP_W), slice(gi * SSM_STATE, (gi + 1) * SSM_STATE))
              for gi in range(SSM_GROUPS)]
    chunks = [(c, slice(c * cl, (c + 1) * cl)) for c in range(nc)]

    for gi, gcols, _ in groups:
        e3g = e3_ref[:, gcols]
        aexp_ref[:, gcols] = _dot(ta_ref[...], e3g)
        xdt_ref[:, gcols] = (xs_ref[:, gcols] * _dot(tdt_ref[...], e3g)).astype(BF16)

    for c, rows in chunks:
        for gi, _, scol in groups:
            bmp = _pad_rows(bm_ref[rows, scol], p)
            cbm = _dot_nt(cm_ref[rows, scol], bmp)
            cb_ref[c * SSM_GROUPS + gi] = jnp.concatenate([cbm, cbm], axis=1)
            bmt_ref[c * SSM_GROUPS + gi] = bmp.astype(F32).T.astype(BF16)

    for c, rows in chunks:
        for gi, _, _ in groups:
            for jj in range(pairs):
                idx = (c * SSM_GROUPS + gi) * pairs + jj
                cols = slice(gi * SSM_GROUP_W + jj * LANES, gi * SSM_GROUP_W + (jj + 1) * LANES)
                a_pair = aexp_ref[rows, cols]
                a_src = jnp.sum(jnp.where(diag2, a_pair, 0.0), axis=0, keepdims=True)
                seg = a_pair - a_src
                dec = jnp.exp(jnp.where(causal2, seg, -jnp.inf))
                w2_ref[idx] = (dec * cb_ref[c * SSM_GROUPS + gi]).astype(BF16)
                x2 = xdt_ref[rows, cols]
                top = _pad_rows(jnp.where(left, x2, jnp.zeros_like(x2)), p)
                bot = _pad_rows(jnp.where(left, jnp.zeros_like(x2), x2), p)
                rhs_ref[idx] = jnp.concatenate([top, bot], axis=0)

    for c, rows in chunks:
        for gi, _, _ in groups:
            for jj in range(pairs):
                idx = (c * SSM_GROUPS + gi) * pairs + jj
                cols = slice(gi * SSM_GROUP_W + jj * LANES, gi * SSM_GROUP_W + (jj + 1) * LANES)
                y_ref[rows, cols] = _dot(w2_ref[idx], rhs_ref[idx])

    for c, rows in chunks:
        for gi, gcols, _ in groups:
            a_g = aexp_ref[rows, gcols]
            xw = xdt_ref[rows, gcols] * jnp.exp(a_g[cl - 1:cl, :] - a_g).astype(BF16)
            st_ref[c, :, gcols] = _dot(bmt_ref[c * SSM_GROUPS + gi], _pad_rows(xw, p))

    for c, rows in chunks:
        for gi, gcols, scol in groups:
            hprev = ht_ref[:, gcols]
            a_g = aexp_ref[rows, gcols]
            y_off = _dot(cm_ref[rows, scol], hprev.astype(BF16)) * jnp.exp(a_g)
            y_ref[rows, gcols] = y_ref[rows, gcols] + y_off
            ht_ref[:, gcols] = hprev * jnp.exp(a_g[cl - 1:cl, :]) + st_ref[c, :, gcols]

    yo_ref[...] = (y_ref[...] + dskip_ref[...] * xs_ref[...]).astype(BF16)

    @pl.when(i == pl.num_programs(1) - 1)
    def _():
        hout_ref[...] = ht_ref[...].T


def _ssd(xs, bm, cm, ta, tdt, h0, dskip, ts, cl):
    bsz, seq, _ = xs.shape
    tok = lambda width: pl.BlockSpec((None, ts, width), lambda b, i: (b, i, 0))
    st_spec = pl.BlockSpec((None, SSM_INNER, SSM_STATE), lambda b, i: (b, 0, 0))
    nc = ts // cl
    n_cg = nc * SSM_GROUPS
    n_pairs = n_cg * (SSM_GROUP_W // LANES)
    return pl.pallas_call(
        functools.partial(_ssd_kernel, ts, cl),
        grid=(bsz, seq // ts),
        in_specs=[tok(SSM_INNER), tok(SSM_GN), tok(SSM_GN), tok(LANES), tok(LANES), st_spec,
                  _const_spec((1, SSM_INNER))],
        out_specs=[tok(SSM_INNER), st_spec],
        out_shape=[jax.ShapeDtypeStruct((bsz, seq, SSM_INNER), BF16),
                   jax.ShapeDtypeStruct((bsz, SSM_INNER, SSM_STATE), F32)],
        scratch_shapes=[pltpu.VMEM((SSM_STATE, SSM_INNER), F32), pltpu.VMEM((ts, SSM_INNER), F32),
                        pltpu.VMEM((LANES, SSM_INNER), BF16), pltpu.VMEM((ts, SSM_INNER), F32),
                        pltpu.VMEM((ts, SSM_INNER), BF16), pltpu.VMEM((nc, SSM_STATE, SSM_INNER), F32),
                        pltpu.VMEM((n_cg, cl, LANES), F32), pltpu.VMEM((n_cg, SSM_STATE, SSM_HEADDIM), BF16),
                        pltpu.VMEM((n_pairs, cl, LANES), BF16), pltpu.VMEM((n_pairs, LANES, LANES), BF16)],
        compiler_params=_params(),
        name="ssd_scan",
    )(xs, bm, cm, ta, tdt, h0, dskip)


def _layer0_mixers(x, pos0, sconv_hist, k_cache, v_cache, wts, tiles):
    t_in, t_att, c_att = tiles
    q, k, v, gated, new_sconv = _attn_in(x, wts["g"][0][0], wts["ab_w_in"], sconv_hist, wts["sconv_w"], pos0, t_in)
    masked = k_cache is None
    k_prev, v_prev = (k, v) if masked else (k_cache, v_cache)
    att = _attn(wts["sinks"], q, k, v, k_prev, v_prev, t_att, c_att, masked)
    bsz, seq, _ = x.shape
    new_k = k[:, seq - min(WINDOW, seq):].reshape(bsz, -1, A_KV_HEADS, HEAD_DIM)
    new_v = v[:, seq - min(WINDOW, seq):].reshape(bsz, -1, A_KV_HEADS, HEAD_DIM)
    return [att, gated], (new_k, new_v, new_sconv)


def _layer1_mixer(x, conv_hist, ssm_state, wts, tiles):
    t_sin, t_ssd, c_ssd = tiles
    bsz = x.shape[0]
    z, xs, bm, cm, ta, tdt, new_conv = _ssm_in(
        x, wts["g"][1][0], wts["ssm_w_in"], wts["w_dt"], conv_hist, wts["ssm_conv_w"],
        wts["ssm_conv_b"], wts["dt_bias"], wts["a_log"], t_sin, c_ssd)
    y, new_state = _ssd(xs, bm, cm, ta, tdt, ssm_state.reshape(bsz, SSM_INNER, SSM_STATE),
                        wts["d_skip"], t_ssd, c_ssd)
    return [y, z], (new_conv, new_state.reshape(bsz, SSM_HEADS, SSM_HEADDIM, SSM_STATE))


def kernel(x_prompt, x_sample, cache_attn_k, cache_attn_v, state_sconv, state_ssm_conv, state_ssm, norm_g,
           ab_w_in, ab_w_out, attn_sinks, sconv_w, ssm_w_in, ssm_conv_w, ssm_conv_b, ssm_dt_bias, ssm_a_log,
           ssm_d, ssm_norm_g, ssm_w_out, ffn_w_gate, ffn_w_up, ffn_w_down):
    w_in1 = ssm_w_in[0]
    wts = {
        "ab_w_in": ab_w_in[0].astype(BF16),
        "ab_w_out": ab_w_out[0].astype(BF16),
        "sinks": attn_sinks[0],
        "sconv_w": sconv_w[0],
        "ssm_w_in": w_in1.astype(BF16),
        "w_dt": jnp.tile(w_in1[:, SSM_INNER + SSM_CONV_CH:], (1, DT_REP)).astype(BF16),
        "ssm_conv_w": ssm_conv_w[0],
        "ssm_conv_b": ssm_conv_b[0].reshape(1, -1),
        "dt_bias": jnp.tile(ssm_dt_bias[0], DT_REP).reshape(1, -1),
        "a_log": jnp.tile(ssm_a_log[0], DT_REP).reshape(1, -1),
        "d_skip": jnp.repeat(ssm_d[0], SSM_HEADDIM).reshape(1, -1),
        "ssm_norm_g": ssm_norm_g[0].reshape(1, -1),
        "ssm_w_out": ssm_w_out[0].astype(BF16),
        "wg": ffn_w_gate.astype(BF16),
        "wu": ffn_w_up.astype(BF16),
        "wd": ffn_w_down.astype(BF16),
    }
    wts["g"] = [[norm_g[l, j].reshape(1, -1) for j in range(4)] for l in range(norm_g.shape[0])]
    g = wts["g"]
    bp = x_prompt.shape[0]
    bs, ls = x_sample.shape[0], x_sample.shape[1]
    t_ffn = 512

    mix_p, (kp, vp, scp) = _layer0_mixers(x_prompt, 0, jnp.zeros((bp, B_CONV - 1, B_WIDTH), F32), None, None,
                                          wts, (1024, 1024, CHUNK))
    mix_s, (ks, vs, scs) = _layer0_mixers(x_sample, PAST_LEN, state_sconv[0],
                                          cache_attn_k[0].reshape(bs, -1, KV_WIDTH),
                                          cache_attn_v[0].reshape(bs, -1, KV_WIDTH), wts, (bs * ls, ls, ls))
    xp, xs = _proj_ffn(mix_p, mix_s, None, wts["ab_w_out"], x_prompt, x_sample, g[0][1], g[0][2],
                       wts["wg"], wts["wu"], wts["wd"], 0, g[0][3], t_ffn)

    mix_p, (ccp, ssp) = _layer1_mixer(xp, jnp.zeros((bp, SSM_CONV - 1, SSM_CONV_CH), F32),
                                      jnp.zeros((bp, SSM_HEADS, SSM_HEADDIM, SSM_STATE), F32), wts, (512, 512, CHUNK))
    mix_s, (ccs, sss) = _layer1_mixer(xs, state_ssm_conv[0], state_ssm[0], wts, (bs * ls, ls, ls))
    yp, ys = _proj_ffn(mix_p, mix_s, wts["ssm_norm_g"], wts["ssm_w_out"], xp, xs, g[1][1], g[1][2],
                       wts["wg"], wts["wu"], wts["wd"], 1, g[1][3], t_ffn)

    lead = lambda a: a[None]
    return (yp, ys, lead(kp), lead(vp), lead(scp), lead(ccp), lead(ssp),
            lead(ks), lead(vs), lead(scs), lead(ccs), lead(sss))
```

```python
import functools

import jax
import jax.numpy as jnp
from jax import lax
from jax.experimental import pallas as pl
from jax.experimental.pallas import tpu as pltpu

F32 = jnp.float32
BF16 = jnp.bfloat16

D_MODEL = 1024
CHUNK = 64
EPS = 1e-6
PAST_LEN = 4096

A_HEADS = 8
A_KV_HEADS = 2
A_GROUP = A_HEADS // A_KV_HEADS
HEAD_DIM = 64
A_WIDTH = A_HEADS * HEAD_DIM
KV_WIDTH = A_KV_HEADS * HEAD_DIM
WINDOW = 128
ROPE_DIM = HEAD_DIM // 4
ROPE_HALF = ROPE_DIM // 2
ROPE_THETA = 500000.0
ATTN_SCALE = HEAD_DIM ** -0.5

B_WIDTH = D_MODEL // 2
B_CONV = 3
AB_IN = A_WIDTH + 2 * KV_WIDTH + 3 * B_WIDTH

SSM_INNER = 2 * D_MODEL
SSM_HEADDIM = 64
SSM_HEADS = SSM_INNER // SSM_HEADDIM
SSM_GROUPS = 4
SSM_GROUP_W = SSM_INNER // SSM_GROUPS
SSM_STATE = 128
SSM_CONV = 4
SSM_GN = SSM_GROUPS * SSM_STATE
SSM_CONV_CH = SSM_INNER + 2 * SSM_GN

D_FF = -(-8 * D_MODEL // (3 * 256)) * 256

LANES = 128
SUBLANES = 8
MXU_DIM = 256
DT_REP = LANES // SSM_HEADS
VMEM_LIMIT = 56 * 1024 * 1024
COL_CHUNK = MXU_DIM
FF_CHUNKS = (6 * MXU_DIM, 5 * MXU_DIM)
assert sum(FF_CHUNKS) == D_FF


def _rms(x, g):
    return x * lax.rsqrt(jnp.mean(x * x, axis=-1, keepdims=True) + EPS) * g


def _dot(a, b):
    return jnp.dot(a, b, preferred_element_type=F32)


def _dot_nt(a, b):
    return lax.dot_general(a, b, (((1,), (1,)), ((), ())), preferred_element_type=F32)


def _silu(x):
    return x * jax.nn.sigmoid(x)


def _const_spec(shape):
    return pl.BlockSpec(shape, lambda *_: (0,) * len(shape), pipeline_mode=pl.Buffered(1))


def _layer_spec(shape, layer):
    return pl.BlockSpec((None,) + shape, lambda *_: (layer,) + (0,) * len(shape), pipeline_mode=pl.Buffered(1))


def _params():
    return pltpu.CompilerParams(dimension_semantics=("arbitrary", "arbitrary"),
                                vmem_limit_bytes=VMEM_LIMIT)


def _causal_conv(carry_ref, nh_ref, u, cw, cols):
    width = cw.shape[0]
    rows, ch = u.shape
    nseq = carry_ref.shape[0]
    u4 = u.reshape(nseq, rows // (nseq * SUBLANES), SUBLANES, ch)
    ext4 = jnp.concatenate([carry_ref[:, :, cols].reshape(nseq, 1, SUBLANES, ch), u4], axis=1)
    sub = lax.broadcasted_iota(jnp.int32, (1, 1, SUBLANES, ch), 2)

    def shift(above, here, k):
        return pltpu.roll(jnp.where(sub >= SUBLANES - k, above, here), k, 2)

    if width == 4:
        s1 = shift(jnp.concatenate([ext4[:, :1], ext4[:, :-1]], axis=1), ext4, 1)
        b = s1 * cw[0:1] + ext4 * cw[1:2]
        y4 = shift(b[:, :-1], b[:, 1:], 2) + (s1[:, 1:] * cw[2:3] + u4 * cw[3:4])
    else:
        y4 = None
        for j in range(width - 1):
            term = shift(ext4[:, :-1], u4, width - 1 - j) * cw[j:j + 1]
            y4 = term if y4 is None else y4 + term
        y4 = y4 + u4 * cw[width - 1:width]
    carry_ref[:, :, cols] = u4[:, -1]
    nh_ref[:, :, cols] = u4[:, -1, SUBLANES - (width - 1):]
    return y4.reshape(rows, ch)


def _rope_freq(shape):
    lane = lax.broadcasted_iota(jnp.int32, shape, 1)
    d = lane % HEAD_DIM
    expo = -((d % ROPE_HALF).astype(F32)) / ROPE_HALF
    freq = jnp.power(jnp.full(shape, ROPE_THETA, F32), expo)
    return jnp.where(d < ROPE_DIM, freq, 0.0), d


def _attn_in_kernel(pos0, tm, seq_rows, x_ref, g_ref, w_ref, hist_ref, cw_ref,
                    q_ref, k_ref, v_ref, gated_ref, nh_ref,
                    cr_ref, sr_ref, carry_ref, p_ref):
    b = pl.program_id(0)
    i = pl.program_id(1)

    @pl.when((b == 0) & (i == 0))
    def _():
        freq, _ = _rope_freq((tm, LANES))
        ang = (lax.broadcasted_iota(jnp.int32, (tm, LANES), 0) % seq_rows).astype(F32) * freq
        cr_ref[...] = jnp.cos(ang)
        sr_ref[...] = jnp.sin(ang)

    @pl.when(i == 0)
    def _():
        carry_ref[:, SUBLANES - (B_CONV - 1):, :] = hist_ref[...]

    freq1, d1 = _rope_freq((1, LANES))
    base = (pos0 + i * tm).astype(F32) * freq1
    cb = jnp.cos(base)
    sb = jnp.sin(base)
    cr = cr_ref[...]
    sr = sr_ref[...]
    cos_t = cr * cb - sr * sb
    sin_t = sr * cb + cr * sb
    m_lo = jnp.where(d1 < ROPE_HALF, -1.0, 0.0)
    m_hi = jnp.where((d1 >= ROPE_HALF) & (d1 < ROPE_DIM), 1.0, 0.0)

    def rope(t):
        partner = pltpu.roll(t, LANES - ROPE_HALF, 1) * m_lo + pltpu.roll(t, ROPE_HALF, 1) * m_hi
        return t * cos_t + partner * sin_t

    h = _rms(x_ref[...], g_ref[...]).astype(BF16)
    o1 = A_WIDTH
    o2 = o1 + KV_WIDTH
    o3 = o2 + KV_WIDTH
    o4 = o3 + B_WIDTH
    o5 = o4 + B_WIDTH

    for n in range(AB_IN // MXU_DIM):
        cols = slice(n * MXU_DIM, (n + 1) * MXU_DIM)
        p_ref[:, cols] = _dot(h, w_ref[:, cols])

    u = p_ref[:, o4:o5] * p_ref[:, o5:]
    y = _causal_conv(carry_ref, nh_ref, u, cw_ref[...], slice(None))
    gated_ref[...] = (p_ref[:, o3:o4] * y).astype(BF16)
    k_ref[...] = rope(p_ref[:, o1:o2])
    v_ref[...] = p_ref[:, o2:o3]
    for s in range(A_WIDTH // LANES):
        q_ref[:, s * LANES:(s + 1) * LANES] = (rope(p_ref[:, s * LANES:(s + 1) * LANES]) * ATTN_SCALE).astype(BF16)


def _tile_rows(x, tm):
    bsz, seq, d = x.shape
    nseq = max(tm // seq, 1)
    return x.reshape(bsz // nseq, nseq * seq, d), nseq


def _attn_in(x, g, w, hist, cw, pos0, tm):
    bsz, seq, _ = x.shape
    xt, nseq = _tile_rows(x, tm)
    groups, rows, _ = xt.shape
    tok = lambda width: pl.BlockSpec((None, tm, width), lambda b, i: (b, i, 0))
    hist_spec = pl.BlockSpec((nseq, B_CONV - 1, B_WIDTH), lambda b, i: (b, 0, 0))
    outs = pl.pallas_call(
        functools.partial(_attn_in_kernel, pos0, tm, tm // nseq),
        grid=(groups, rows // tm),
        in_specs=[tok(D_MODEL), _const_spec((1, D_MODEL)), _const_spec((D_MODEL, AB_IN)), hist_spec,
                  _const_spec((B_CONV, B_WIDTH))],
        out_specs=[tok(A_WIDTH), tok(KV_WIDTH), tok(KV_WIDTH), tok(B_WIDTH), hist_spec],
        out_shape=[jax.ShapeDtypeStruct((groups, rows, A_WIDTH), BF16),
                   jax.ShapeDtypeStruct((groups, rows, KV_WIDTH), F32),
                   jax.ShapeDtypeStruct((groups, rows, KV_WIDTH), F32),
                   jax.ShapeDtypeStruct((groups, rows, B_WIDTH), BF16),
                   jax.ShapeDtypeStruct((bsz, B_CONV - 1, B_WIDTH), F32)],
        scratch_shapes=[pltpu.VMEM((tm, LANES), F32), pltpu.VMEM((tm, LANES), F32),
                        pltpu.VMEM((nseq, SUBLANES, B_WIDTH), F32), pltpu.VMEM((tm, AB_IN), F32)],
        compiler_params=_params(),
        name="attn_in_proj",
    )(xt, g, w, hist, cw)
    return [o.reshape(bsz, seq, o.shape[-1]) for o in outs[:-1]] + [outs[-1]]


def _attn_kernel(tq, cq, masked, sinks_ref, q_ref, kc_ref, vc_ref, kp_ref, vp_ref, att_ref,
                 kk_ref, vx_ref, s_ref, e_ref):
    i = pl.program_id(1)
    nk = WINDOW + cq
    nkp = MXU_DIM
    rows_kv = WINDOW + tq
    hd = HEAD_DIM

    kk_ref[0:WINDOW, :] = kp_ref[...].astype(BF16)
    kk_ref[WINDOW:rows_kv, :] = kc_ref[...].astype(BF16)
    kk_ref[rows_kv:, :] = jnp.zeros((kk_ref.shape[0] - rows_kv, KV_WIDTH), BF16)
    vall = jnp.concatenate([vp_ref[...], vc_ref[...]], axis=0)
    vx_ref[0:rows_kv, :] = jnp.concatenate([vall[:, :hd], vall[:, :hd], vall[:, hd:], vall[:, hd:]], axis=1)
    vx_ref[rows_kv:, :] = jnp.zeros((vx_ref.shape[0] - rows_kv, 2 * LANES), F32)

    col = lax.broadcasted_iota(jnp.int32, (A_GROUP * cq, nkp), 1)
    krow = lax.broadcasted_iota(jnp.int32, (nkp, LANES), 0)
    ones = jnp.ones((nkp, LANES), BF16)
    bodies = [(c, kh) for c in range(tq // cq) for kh in range(A_KV_HEADS)]
    fills = []
    for kh in range(A_KV_HEADS):
        sink = jnp.concatenate(
            [jnp.full((cq, nkp), sinks_ref[kh * A_GROUP + g], F32) for g in range(A_GROUP)], axis=0)
        fills.append(jnp.where(col == nk, sink, -jnp.inf))

    for n, (c, kh) in enumerate(bodies):
        r0 = c * cq
        heads = [kh * A_GROUP + g for g in range(A_GROUP)]
        kt = kk_ref[r0:r0 + nkp, kh * hd:(kh + 1) * hd]
        qs = jnp.concatenate([q_ref[r0:r0 + cq, hh * hd:(hh + 1) * hd] for hh in heads], axis=0)
        s = _dot_nt(qs, kt)
        valid = col < nk
        if masked and r0 < WINDOW:
            valid = valid & ((col >= WINDOW - r0) | (i > 0))
        s_ref[n] = jnp.where(valid, s, fills[kh])

    for n in range(len(bodies)):
        s = s_ref[n]
        m = jnp.max(s, axis=-1, keepdims=True)
        e_ref[n] = jnp.exp(s - m).astype(BF16)

    for n, (c, kh) in enumerate(bodies):
        r0 = c * cq
        vt = jnp.where(krow == nk, 0.0, vx_ref[r0:r0 + nkp, kh * LANES:(kh + 1) * LANES]).astype(BF16)
        ox = _dot(e_ref[n], jnp.concatenate([vt, ones], axis=1))
        o = ox[:, :LANES] / ox[:, LANES:]
        for g in range(A_GROUP):
            hh = kh * A_GROUP + g
            lanes = slice((hh % 2) * hd, (hh % 2 + 1) * hd)
            att_ref[r0:r0 + cq, hh * hd:(hh + 1) * hd] = o[g * cq:(g + 1) * cq, lanes].astype(BF16)


def _attn(sinks, q, k, v, k_prev, v_prev, tq, cq, masked):
    bsz, seq, _ = q.shape
    grid = (bsz, seq // tq)
    tok = lambda width: pl.BlockSpec((None, tq, width), lambda b, i: (b, i, 0))
    if masked:
        blocks_per_tile = tq // WINDOW
        prev = pl.BlockSpec((None, WINDOW, KV_WIDTH),
                            lambda b, i: (b, jnp.maximum(i * blocks_per_tile - 1, 0), 0))
    else:
        prev = pl.BlockSpec((None, WINDOW, KV_WIDTH), lambda b, i: (b, 0, 0))
    n_bodies = (tq // cq) * A_KV_HEADS
    kv_rows = (tq // cq - 1) * cq + MXU_DIM
    return pl.pallas_call(
        functools.partial(_attn_kernel, tq, cq, masked),
        grid=grid,
        in_specs=[pl.BlockSpec(memory_space=pltpu.SMEM),
                  tok(A_WIDTH), tok(KV_WIDTH), tok(KV_WIDTH), prev, prev],
        out_specs=tok(A_WIDTH),
        out_shape=jax.ShapeDtypeStruct((bsz, seq, A_WIDTH), BF16),
        scratch_shapes=[pltpu.VMEM((kv_rows, KV_WIDTH), BF16), pltpu.VMEM((kv_rows, 2 * LANES), F32),
                        pltpu.VMEM((n_bodies, A_GROUP * cq, MXU_DIM), F32),
                        pltpu.VMEM((n_bodies, A_GROUP * cq, MXU_DIM), BF16)],
        compiler_params=_params(),
        name="band_attention",
    )(sinks, q, k, v, k_prev, v_prev)


def _proj_ffn_kernel(n_in, gate, n_tiles, *refs):
    long_in, short_in = refs[:n_in], refs[n_in:2 * n_in]
    refs = refs[2 * n_in:]
    ng_ref = None
    if gate:
        ng_ref, refs = refs[0], refs[1:]
    wo_ref, xl_ref, xs_ref, g1_ref, g2_ref, wg_ref, wu_ref, wd_ref, g3_ref, ol_ref, os_ref = refs
    t = pl.program_id(0)

    def tile(ins, x_ref, o_ref):
        acc = None
        if gate:
            y_ref, z_ref = ins
            for gi in range(SSM_GROUPS):
                gcols = slice(gi * SSM_GROUP_W, (gi + 1) * SSM_GROUP_W)
                gg = y_ref[:, gcols].astype(F32) * z_ref[:, gcols].astype(F32)
                gg = gg * lax.rsqrt(jnp.mean(gg * gg, axis=-1, keepdims=True) + EPS)
                part = _dot((gg * ng_ref[:, gcols]).astype(BF16), wo_ref[gcols, :])
                acc = part if acc is None else acc + part
        else:
            off = 0
            for a_ref in ins:
                kdim = a_ref.shape[-1]
                part = _dot(a_ref[...], wo_ref[off:off + kdim, :])
                acc = part if acc is None else acc + part
                off += kdim

        x = x_ref[...] + _rms(acc, g1_ref[...])
        h = _rms(x, g2_ref[...]).astype(BF16)
        f = None
        off = 0
        for width in FF_CHUNKS:
            fc = slice(off, off + width)
            m = (_silu(_dot(h, wg_ref[:, fc])) * _dot(h, wu_ref[:, fc])).astype(BF16)
            part = _dot(m, wd_ref[fc, :])
            f = part if f is None else f + part
            off += width
        o_ref[...] = x + _rms(f, g3_ref[...])

    pl.when(t < n_tiles)(lambda: tile(long_in, xl_ref, ol_ref))
    pl.when(t == n_tiles)(lambda: tile(short_in, xs_ref, os_ref))


def _proj_ffn(mix_long, mix_short, ng, wo, x_long, x_short, g1, g2, wg, wu, wd, layer, g3, tm):
    bsz, seq, _ = x_long.shape
    tiles_per_seq = seq // tm
    n_tiles = bsz * tiles_per_seq
    n_short = x_short.shape[0] * x_short.shape[1]
    flat = lambda a: a.reshape(1, n_short, a.shape[-1])
    tile_idx = lambda t: jnp.minimum(t, n_tiles - 1)
    ltok = lambda width: pl.BlockSpec((None, tm, width),
                                      lambda t: (tile_idx(t) // tiles_per_seq, tile_idx(t) % tiles_per_seq, 0))
    stok = lambda width: pl.BlockSpec((None, n_short, width), lambda t: (0, 0, 0))
    gate = ng is not None
    ng_args, ng_specs = ([ng], [_const_spec((1, SSM_INNER))]) if gate else ([], [])
    out_long, out_short = pl.pallas_call(
        functools.partial(_proj_ffn_kernel, len(mix_long), gate, n_tiles),
        grid=(n_tiles + 1,),
        in_specs=[ltok(a.shape[-1]) for a in mix_long] + [stok(a.shape[-1]) for a in mix_short] + ng_specs + [
            _const_spec(wo.shape), ltok(D_MODEL), stok(D_MODEL), _const_spec((1, D_MODEL)), _const_spec((1, D_MODEL)),
            _layer_spec((D_MODEL, D_FF), layer), _layer_spec((D_MODEL, D_FF), layer),
            _layer_spec((D_FF, D_MODEL), layer), _const_spec((1, D_MODEL))],
        out_specs=[ltok(D_MODEL), stok(D_MODEL)],
        out_shape=[jax.ShapeDtypeStruct(x_long.shape, F32), jax.ShapeDtypeStruct((1, n_short, D_MODEL), F32)],
        compiler_params=pltpu.CompilerParams(dimension_semantics=("arbitrary",), vmem_limit_bytes=VMEM_LIMIT),
        name="out_proj_ffn",
    )(*mix_long, *[flat(a) for a in mix_short], *ng_args, wo, x_long, flat(x_short), g1, g2, wg, wu, wd, g3)
    return out_long, out_short.reshape(x_short.shape)


def _split3(v):
    t1 = v.astype(BF16)
    r = v - t1.astype(F32)
    t2 = r.astype(BF16)
    t3 = (r - t2.astype(F32)).astype(BF16)
    return t1, t2, t3


def _pack3(v):
    t1, t2, t3 = _split3(v)
    lane = lax.broadcasted_iota(jnp.int32, v.shape, 1)
    return jnp.where(lane < SSM_HEADS, t1,
                     jnp.where(lane < 2 * SSM_HEADS, t2,
                               jnp.where(lane < 3 * SSM_HEADS, t3, jnp.zeros_like(t3))))


def _ssm_in_kernel(tm, cl, x_ref, g_ref, w_ref, wdt_ref, hist_ref, cw_ref, cb_ref, dtb_ref, alog_ref,
                   z_ref, xs_ref, bm_ref, cm_ref, ta_ref, tdt_ref, nh_ref, carry_ref):
    i = pl.program_id(1)

    @pl.when(i == 0)
    def _():
        carry_ref[:, SUBLANES - (SSM_CONV - 1):, :] = hist_ref[...]

    h = _rms(x_ref[...], g_ref[...]).astype(BF16)

    dt = jax.nn.softplus(_dot(h, wdt_ref[...]) + dtb_ref[...])
    tdt_ref[...] = _pack3(dt)
    d1, d2, d3 = _split3(dt * (-jnp.exp(alog_ref[...])))
    tcol = lax.broadcasted_iota(jnp.int32, (cl, 3 * cl), 1) % cl
    trow = lax.broadcasted_iota(jnp.int32, (cl, 3 * cl), 0)
    tril = jnp.where(tcol <= trow, 1.0, 0.0).astype(BF16)
    acum = []
    for c in range(tm // cl):
        rows = slice(c * cl, (c + 1) * cl)
        acum.append(_dot(tril, jnp.concatenate([d1[rows], d2[rows], d3[rows]], axis=0)))
    ta_ref[...] = _pack3(jnp.concatenate(acum, axis=0))

    for n in range(SSM_INNER // COL_CHUNK):
        cols = slice(n * COL_CHUNK, (n + 1) * COL_CHUNK)
        z_ref[:, cols] = _silu(_dot(h, w_ref[:, cols])).astype(BF16)

    for n in range(SSM_CONV_CH // COL_CHUNK):
        cols = slice(n * COL_CHUNK, (n + 1) * COL_CHUNK)
        raw = _dot(h, w_ref[:, SSM_INNER + cols.start:SSM_INNER + cols.stop])
        y = _silu(_causal_conv(carry_ref, nh_ref, raw, cw_ref[:, cols], cols) + cb_ref[:, cols])
        if cols.stop <= SSM_INNER:
            xs_ref[:, cols] = y
        elif cols.stop <= SSM_INNER + SSM_GN:
            bm_ref[:, cols.start - SSM_INNER:cols.stop - SSM_INNER] = y.astype(BF16)
        else:
            lo = SSM_INNER + SSM_GN
            cm_ref[:, cols.start - lo:cols.stop - lo] = y.astype(BF16)


def _ssm_in(x, g, w, wdt, hist, cw, cb, dtb, alog, tm, cl):
    bsz, seq, _ = x.shape
    xt, nseq = _tile_rows(x, tm)
    groups, rows, _ = xt.shape
    tok = lambda width: pl.BlockSpec((None, tm, width), lambda b, i: (b, i, 0))
    hist_spec = pl.BlockSpec((nseq, SSM_CONV - 1, SSM_CONV_CH), lambda b, i: (b, 0, 0))
    outs = pl.pallas_call(
        functools.partial(_ssm_in_kernel, tm, cl),
        grid=(groups, rows // tm),
        in_specs=[tok(D_MODEL), _const_spec((1, D_MODEL)), _const_spec(w.shape), _const_spec((D_MODEL, LANES)),
                  hist_spec,
                  _const_spec((SSM_CONV, SSM_CONV_CH)), _const_spec((1, SSM_CONV_CH)), _const_spec((1, LANES)),
                  _const_spec((1, LANES))],
        out_specs=[tok(SSM_INNER), tok(SSM_INNER), tok(SSM_GN), tok(SSM_GN), tok(LANES), tok(LANES), hist_spec],
        out_shape=[jax.ShapeDtypeStruct((groups, rows, SSM_INNER), BF16),
                   jax.ShapeDtypeStruct((groups, rows, SSM_INNER), F32),
                   jax.ShapeDtypeStruct((groups, rows, SSM_GN), BF16),
                   jax.ShapeDtypeStruct((groups, rows, SSM_GN), BF16),
                   jax.ShapeDtypeStruct((groups, rows, LANES), BF16),
                   jax.ShapeDtypeStruct((groups, rows, LANES), BF16),
                   jax.ShapeDtypeStruct((bsz, SSM_CONV - 1, SSM_CONV_CH), F32)],
        scratch_shapes=[pltpu.VMEM((nseq, SUBLANES, SSM_CONV_CH), F32)],
        compiler_params=_params(),
        name="ssm_in_proj",
    )(xt, g, w, wdt, hist, cw, cb, dtb, alog)
    return [o.reshape(bsz, seq, o.shape[-1]) for o in outs[:-1]] + [outs[-1]]


def _pad_rows(a, rows):
    if a.shape[0] == rows:
        return a
    return jnp.concatenate([a, jnp.zeros((rows - a.shape[0], a.shape[1]), a.dtype)], axis=0)


def _ssd_kernel(ts, cl, xs_ref, bm_ref, cm_ref, ta_ref, tdt_ref, h0_ref, dskip_ref,
                yo_ref, hout_ref, ht_ref, y_ref, e3_ref, aexp_ref, xdt_ref, st_ref, cb_ref, bmt_ref, w2_ref, rhs_ref):
    b = pl.program_id(0)
    i = pl.program_id(1)
    p = SSM_HEADDIM
    nc = ts // cl
    pairs = SSM_GROUP_W // LANES

    @pl.when((b == 0) & (i == 0))
    def _():
        kk = lax.broadcasted_iota(jnp.int32, (LANES, SSM_INNER), 0)
        cc = lax.broadcasted_iota(jnp.int32, (LANES, SSM_INNER), 1)
        hit = (kk % SSM_HEADS == cc // p) & (kk < 3 * SSM_HEADS)
        e3_ref[...] = jnp.where(hit, 1.0, 0.0).astype(BF16)

    @pl.when(i == 0)
    def _():
        ht_ref[...] = h0_ref[...].T

    lane = lax.broadcasted_iota(jnp.int32, (cl, LANES), 1)
    row = lax.broadcasted_iota(jnp.int32, (cl, LANES), 0)
    causal2 = (lane % p) <= row
    left = lane.astype(F32).astype(BF16) < p
    diag2 = (lane % p) == row
    groups = [(gi, slice(gi * SSM_GROUP_W, (gi + 1) * SSM_GROUP_W), slice(gi * SSM_STATE, (gi + 1) * SSM_STATE))
              for gi in range(SSM_GROUPS)]
    chunks = [(c, slice(c * cl, (c + 1) * cl)) for c in range(nc)]

    for gi, gcols, _ in groups:
        e3g = e3_ref[:, gcols]
        aexp_ref[:, gcols] = _dot(ta_ref[...], e3g)
        xdt_ref[:, gcols] = (xs_ref[:, gcols] * _dot(tdt_ref[...], e3g)).astype(BF16)

    for c, rows in chunks:
        for gi, _, scol in groups:
            bmp = _pad_rows(bm_ref[rows, scol], p)
            cbm = _dot_nt(cm_ref[rows, scol], bmp)
            cb_ref[c * SSM_GROUPS + gi] = jnp.concatenate([cbm, cbm], axis=1)
            bmt_ref[c * SSM_GROUPS + gi] = bmp.astype(F32).T.astype(BF16)

    for c, rows in chunks:
        for gi, _, _ in groups:
            for jj in range(pairs):
                idx = (c * SSM_GROUPS + gi) * pairs + jj
                cols = slice(gi * SSM_GROUP_W + jj * LANES, gi * SSM_GROUP_W + (jj + 1) * LANES)
                a_pair = aexp_ref[rows, cols]
                a_src = jnp.sum(jnp.where(diag2, a_pair, 0.0), axis=0, keepdims=True)
                seg = a_pair - a_src
                dec = jnp.exp(jnp.where(causal2, seg, -jnp.inf))
                w2_ref[idx] = (dec * cb_ref[c * SSM_GROUPS + gi]).astype(BF16)
                x2 = xdt_ref[rows, cols]
                top = _pad_rows(jnp.where(left, x2, jnp.zeros_like(x2)), p)
                bot = _pad_rows(jnp.where(left, jnp.zeros_like(x2), x2), p)
                rhs_ref[idx] = jnp.concatenate([top, bot], axis=0)

    for c, rows in chunks:
        for gi, _, _ in groups:
            for jj in range(pairs):
                idx = (c * SSM_GROUPS + gi) * pairs + jj
                cols = slice(gi * SSM_GROUP_W + jj * LANES, gi * SSM_GROUP_W + (jj + 1) * LANES)
                y_ref[rows, cols] = _dot(w2_ref[idx], rhs_ref[idx])

    for c, rows in chunks:
        for gi, gcols, _ in groups:
            a_g = aexp_ref[rows, gcols]
            xw = xdt_ref[rows, gcols] * jnp.exp(a_g[cl - 1:cl, :] - a_g).astype(BF16)
            st_ref[c, :, gcols] = _dot(bmt_ref[c * SSM_GROUPS + gi], _pad_rows(xw, p))

    for c, rows in chunks:
        for gi, gcols, scol in groups:
            hprev = ht_ref[:, gcols]
            a_g = aexp_ref[rows, gcols]
            y_off = _dot(cm_ref[rows, scol], hprev.astype(BF16)) * jnp.exp(a_g)
            y_ref[rows, gcols] = y_ref[rows, gcols] + y_off
            ht_ref[:, gcols] = hprev * jnp.exp(a_g[cl - 1:cl, :]) + st_ref[c, :, gcols]

    yo_ref[...] = (y_ref[...] + dskip_ref[...] * xs_ref[...]).astype(BF16)

    @pl.when(i == pl.num_programs(1) - 1)
    def _():
        hout_ref[...] = ht_ref[...].T


def _ssd(xs, bm, cm, ta, tdt, h0, dskip, ts, cl):
    bsz, seq, _ = xs.shape
    tok = lambda width: pl.BlockSpec((None, ts, width), lambda b, i: (b, i, 0))
    st_spec = pl.BlockSpec((None, SSM_INNER, SSM_STATE), lambda b, i: (b, 0, 0))
    nc = ts // cl
    n_cg = nc * SSM_GROUPS
    n_pairs = n_cg * (SSM_GROUP_W // LANES)
    return pl.pallas_call(
        functools.partial(_ssd_kernel, ts, cl),
        grid=(bsz, seq // ts),
        in_specs=[tok(SSM_INNER), tok(SSM_GN), tok(SSM_GN), tok(LANES), tok(LANES), st_spec,
                  _const_spec((1, SSM_INNER))],
        out_specs=[tok(SSM_INNER), st_spec],
        out_shape=[jax.ShapeDtypeStruct((bsz, seq, SSM_INNER), BF16),
                   jax.ShapeDtypeStruct((bsz, SSM_INNER, SSM_STATE), F32)],
        scratch_shapes=[pltpu.VMEM((SSM_STATE, SSM_INNER), F32), pltpu.VMEM((ts, SSM_INNER), F32),
                        pltpu.VMEM((LANES, SSM_INNER), BF16), pltpu.VMEM((ts, SSM_INNER), F32),
                        pltpu.VMEM((ts, SSM_INNER), BF16), pltpu.VMEM((nc, SSM_STATE, SSM_INNER), F32),
                        pltpu.VMEM((n_cg, cl, LANES), F32), pltpu.VMEM((n_cg, SSM_STATE, SSM_HEADDIM), BF16),
                        pltpu.VMEM((n_pairs, cl, LANES), BF16), pltpu.VMEM((n_pairs, LANES, LANES), BF16)],
        compiler_params=_params(),
        name="ssd_scan",
    )(xs, bm, cm, ta, tdt, h0, dskip)


def _layer0_mixers(x, pos0, sconv_hist, k_cache, v_cache, wts, tiles):
    t_in, t_att, c_att = tiles
    q, k, v, gated, new_sconv = _attn_in(x, wts["g"][0][0], wts["ab_w_in"], sconv_hist, wts["sconv_w"], pos0, t_in)
    masked = k_cache is None
    k_prev, v_prev = (k, v) if masked else (k_cache, v_cache)
    att = _attn(wts["sinks"], q, k, v, k_prev, v_prev, t_att, c_att, masked)
    bsz, seq, _ = x.shape
    new_k = k[:, seq - min(WINDOW, seq):].reshape(bsz, -1, A_KV_HEADS, HEAD_DIM)
    new_v = v[:, seq - min(WINDOW, seq):].reshape(bsz, -1, A_KV_HEADS, HEAD_DIM)
    return [att, gated], (new_k, new_v, new_sconv)


def _layer1_mixer(x, conv_hist, ssm_state, wts, tiles):
    t_sin, t_ssd, c_ssd = tiles
    bsz = x.shape[0]
    z, xs, bm, cm, ta, tdt, new_conv = _ssm_in(
        x, wts["g"][1][0], wts["ssm_w_in"], wts["w_dt"], conv_hist, wts["ssm_conv_w"],
        wts["ssm_conv_b"], wts["dt_bias"], wts["a_log"], t_sin, c_ssd)
    y, new_state = _ssd(xs, bm, cm, ta, tdt, ssm_state.reshape(bsz, SSM_INNER, SSM_STATE),
                        wts["d_skip"], t_ssd, c_ssd)
    return [y, z], (new_conv, new_state.reshape(bsz, SSM_HEADS, SSM_HEADDIM, SSM_STATE))


def kernel(x_prompt, x_sample, cache_attn_k, cache_attn_v, state_sconv, state_ssm_conv, state_ssm, norm_g,
           ab_w_in, ab_w_out, attn_sinks, sconv_w, ssm_w_in, ssm_conv_w, ssm_conv_b, ssm_dt_bias, ssm_a_log,
           ssm_d, ssm_norm_g, ssm_w_out, ffn_w_gate, ffn_w_up, ffn_w_down):
    w_in1 = ssm_w_in[0]
    wts = {
        "ab_w_in": ab_w_in[0].astype(BF16),
        "ab_w_out": ab_w_out[0].astype(BF16),
        "sinks": attn_sinks[0],
        "sconv_w": sconv_w[0],
        "ssm_w_in": w_in1.astype(BF16),
        "w_dt": jnp.tile(w_in1[:, SSM_INNER + SSM_CONV_CH:], (1, DT_REP)).astype(BF16),
        "ssm_conv_w": ssm_conv_w[0],
        "ssm_conv_b": ssm_conv_b[0].reshape(1, -1),
        "dt_bias": jnp.tile(ssm_dt_bias[0], DT_REP).reshape(1, -1),
        "a_log": jnp.tile(ssm_a_log[0], DT_REP).reshape(1, -1),
        "d_skip": jnp.repeat(ssm_d[0], SSM_HEADDIM).reshape(1, -1),
        "ssm_norm_g": ssm_norm_g[0].reshape(1, -1),
        "ssm_w_out": ssm_w_out[0].astype(BF16),
        "wg": ffn_w_gate.astype(BF16),
        "wu": ffn_w_up.astype(BF16),
        "wd": ffn_w_down.astype(BF16),
    }
    wts["g"] = [[norm_g[l, j].reshape(1, -1) for j in range(4)] for l in range(norm_g.shape[0])]
    g = wts["g"]
    bp = x_prompt.shape[0]
    bs, ls = x_sample.shape[0], x_sample.shape[1]
    t_ffn = 512

    mix_p, (kp, vp, scp) = _layer0_mixers(x_prompt, 0, jnp.zeros((bp, B_CONV - 1, B_WIDTH), F32), None, None,
                                          wts, (1024, 1024, CHUNK))
    mix_s, (ks, vs, scs) = _layer0_mixers(x_sample, PAST_LEN, state_sconv[0],
                                          cache_attn_k[0].reshape(bs, -1, KV_WIDTH),
                                          cache_attn_v[0].reshape(bs, -1, KV_WIDTH), wts, (bs * ls, ls, ls))
    xp, xs = _proj_ffn(mix_p, mix_s, None, wts["ab_w_out"], x_prompt, x_sample, g[0][1], g[0][2],
                       wts["wg"], wts["wu"], wts["wd"], 0, g[0][3], t_ffn)

    mix_p, (ccp, ssp) = _layer1_mixer(xp, jnp.zeros((bp, SSM_CONV - 1, SSM_CONV_CH), F32),
                                      jnp.zeros((bp, SSM_HEADS, SSM_HEADDIM, SSM_STATE), F32), wts, (512, 512, CHUNK))
    mix_s, (ccs, sss) = _layer1_mixer(xs, state_ssm_conv[0], state_ssm[0], wts, (bs * ls, ls, ls))
    yp, ys = _proj_ffn(mix_p, mix_s, wts["ssm_norm_g"], wts["ssm_w_out"], xp, xs, g[1][1], g[1][2],
                       wts["wg"], wts["wu"], wts["wd"], 1, g[1][3], t_ffn)

    lead = lambda a: a[None]
    return (yp, ys, lead(kp), lead(vp), lead(scp), lead(ccp), lead(ssp),
            lead(ks), lead(vs), lead(scs), lead(ccs), lead(sss))
```

```python
import functools

import jax
import jax.numpy as jnp
from jax import lax
from jax.experimental import pallas as pl
from jax.experimental.pallas import tpu as pltpu

F32 = jnp.float32
BF16 = jnp.bfloat16

D_MODEL = 1024
CHUNK = 64
EPS = 1e-6
LOG2E = 1.4426950408889634
PAST_LEN = 4096

A_HEADS = 8
A_KV_HEADS = 2
A_GROUP = A_HEADS // A_KV_HEADS
HEAD_DIM = 64
A_WIDTH = A_HEADS * HEAD_DIM
KV_WIDTH = A_KV_HEADS * HEAD_DIM
WINDOW = 128
ROPE_DIM = HEAD_DIM // 4
ROPE_HALF = ROPE_DIM // 2
ROPE_THETA = 500000.0
ATTN_SCALE = HEAD_DIM ** -0.5

B_WIDTH = D_MODEL // 2
B_CONV = 3
AB_IN = A_WIDTH + 2 * KV_WIDTH + 3 * B_WIDTH

SSM_INNER = 2 * D_MODEL
SSM_HEADDIM = 64
SSM_HEADS = SSM_INNER // SSM_HEADDIM
SSM_GROUPS = 4
SSM_GROUP_W = SSM_INNER // SSM_GROUPS
SSM_STATE = 128
SSM_CONV = 4
SSM_GN = SSM_GROUPS * SSM_STATE
SSM_CONV_CH = SSM_INNER + 2 * SSM_GN

D_FF = -(-8 * D_MODEL // (3 * 256)) * 256

LANES = 128
SUBLANES = 8
MXU_DIM = 256
DT_REP = LANES // SSM_HEADS
VMEM_LIMIT = 56 * 1024 * 1024
COL_CHUNK = MXU_DIM
FF_CHUNKS = (6 * MXU_DIM, 5 * MXU_DIM)
assert sum(FF_CHUNKS) == D_FF


def _rms(x, g):
    return x * lax.rsqrt(jnp.mean(x * x, axis=-1, keepdims=True) + EPS) * g


def _dot(a, b):
    return jnp.dot(a, b, preferred_element_type=F32)


def _dot_nt(a, b):
    return lax.dot_general(a, b, (((1,), (1,)), ((), ())), preferred_element_type=F32)


def _silu(x):
    return x * jax.nn.sigmoid(x)


def _const_spec(shape):
    return pl.BlockSpec(shape, lambda *_: (0,) * len(shape), pipeline_mode=pl.Buffered(1))


def _layer_spec(shape, layer):
    return pl.BlockSpec((None,) + shape, lambda *_: (layer,) + (0,) * len(shape), pipeline_mode=pl.Buffered(1))


def _params():
    return pltpu.CompilerParams(dimension_semantics=("arbitrary", "arbitrary"),
                                vmem_limit_bytes=VMEM_LIMIT)


def _causal_conv(carry_ref, nh_ref, u, cw, cols):
    width = cw.shape[0]
    rows, ch = u.shape
    nseq = carry_ref.shape[0]
    u4 = u.reshape(nseq, rows // (nseq * SUBLANES), SUBLANES, ch)
    ext4 = jnp.concatenate([carry_ref[:, :, cols].reshape(nseq, 1, SUBLANES, ch), u4], axis=1)
    sub = lax.broadcasted_iota(jnp.int32, (1, 1, SUBLANES, ch), 2)

    def shift(above, here, k):
        return pltpu.roll(jnp.where(sub >= SUBLANES - k, above, here), k, 2)

    if width == 4:
        s1 = shift(jnp.concatenate([ext4[:, :1], ext4[:, :-1]], axis=1), ext4, 1)
        b = s1 * cw[0:1] + ext4 * cw[1:2]
        y4 = shift(b[:, :-1], b[:, 1:], 2) + (s1[:, 1:] * cw[2:3] + u4 * cw[3:4])
    else:
        y4 = None
        for j in range(width - 1):
            term = shift(ext4[:, :-1], u4, width - 1 - j) * cw[j:j + 1]
            y4 = term if y4 is None else y4 + term
        y4 = y4 + u4 * cw[width - 1:width]
    carry_ref[:, :, cols] = u4[:, -1]
    nh_ref[:, :, cols] = u4[:, -1, SUBLANES - (width - 1):]
    return y4.reshape(rows, ch)


def _rope_freq(shape):
    lane = lax.broadcasted_iota(jnp.int32, shape, 1)
    d = lane % HEAD_DIM
    expo = -((d % ROPE_HALF).astype(F32)) / ROPE_HALF
    freq = jnp.power(jnp.full(shape, ROPE_THETA, F32), expo)
    return jnp.where(d < ROPE_DIM, freq, 0.0), d


def _attn_in_kernel(pos0, tm, seq_rows, x_ref, g_ref, w_ref, hist_ref, cw_ref,
                    q_ref, k_ref, v_ref, gated_ref, nh_ref,
                    cr_ref, sr_ref, carry_ref, p_ref):
    b = pl.program_id(0)
    i = pl.program_id(1)

    @pl.when((b == 0) & (i == 0))
    def _():
        freq, _ = _rope_freq((tm, LANES))
        ang = (lax.broadcasted_iota(jnp.int32, (tm, LANES), 0) % seq_rows).astype(F32) * freq
        cr_ref[...] = jnp.cos(ang)
        sr_ref[...] = jnp.sin(ang)

    @pl.when(i == 0)
    def _():
        carry_ref[:, SUBLANES - (B_CONV - 1):, :] = hist_ref[...]

    freq1, d1 = _rope_freq((1, LANES))
    base = (pos0 + i * tm).astype(F32) * freq1
    cb = jnp.cos(base)
    sb = jnp.sin(base)
    cr = cr_ref[...]
    sr = sr_ref[...]
    cos_t = cr * cb - sr * sb
    sin_t = sr * cb + cr * sb
    m_lo = jnp.where(d1 < ROPE_HALF, -1.0, 0.0)
    m_hi = jnp.where((d1 >= ROPE_HALF) & (d1 < ROPE_DIM), 1.0, 0.0)

    def rope(t):
        partner = pltpu.roll(t, LANES - ROPE_HALF, 1) * m_lo + pltpu.roll(t, ROPE_HALF, 1) * m_hi
        return t * cos_t + partner * sin_t

    h = _rms(x_ref[...], g_ref[...]).astype(BF16)
    o1 = A_WIDTH
    o2 = o1 + KV_WIDTH
    o3 = o2 + KV_WIDTH
    o4 = o3 + B_WIDTH
    o5 = o4 + B_WIDTH

    for n in range(AB_IN // MXU_DIM):
        cols = slice(n * MXU_DIM, (n + 1) * MXU_DIM)
        p_ref[:, cols] = _dot(h, w_ref[:, cols])

    u = p_ref[:, o4:o5] * p_ref[:, o5:]
    y = _causal_conv(carry_ref, nh_ref, u, cw_ref[...], slice(None))
    gated_ref[...] = (p_ref[:, o3:o4] * y).astype(BF16)
    k_ref[...] = rope(p_ref[:, o1:o2])
    v_ref[...] = p_ref[:, o2:o3]
    for s in range(A_WIDTH // LANES):
        q_ref[:, s * LANES:(s + 1) * LANES] = (rope(p_ref[:, s * LANES:(s + 1) * LANES]) * ATTN_SCALE).astype(BF16)


def _tile_rows(x, tm):
    bsz, seq, d = x.shape
    nseq = max(tm // seq, 1)
    return x.reshape(bsz // nseq, nseq * seq, d), nseq


def _attn_in(x, g, w, hist, cw, pos0, tm):
    bsz, seq, _ = x.shape
    xt, nseq = _tile_rows(x, tm)
    groups, rows, _ = xt.shape
    tok = lambda width: pl.BlockSpec((None, tm, width), lambda b, i: (b, i, 0))
    hist_spec = pl.BlockSpec((nseq, B_CONV - 1, B_WIDTH), lambda b, i: (b, 0, 0))
    outs = pl.pallas_call(
        functools.partial(_attn_in_kernel, pos0, tm, tm // nseq),
        grid=(groups, rows // tm),
        in_specs=[tok(D_MODEL), _const_spec((1, D_MODEL)), _const_spec((D_MODEL, AB_IN)), hist_spec,
                  _const_spec((B_CONV, B_WIDTH))],
        out_specs=[tok(A_WIDTH), tok(KV_WIDTH), tok(KV_WIDTH), tok(B_WIDTH), hist_spec],
        out_shape=[jax.ShapeDtypeStruct((groups, rows, A_WIDTH), BF16),
                   jax.ShapeDtypeStruct((groups, rows, KV_WIDTH), F32),
                   jax.ShapeDtypeStruct((groups, rows, KV_WIDTH), F32),
                   jax.ShapeDtypeStruct((groups, rows, B_WIDTH), BF16),
                   jax.ShapeDtypeStruct((bsz, B_CONV - 1, B_WIDTH), F32)],
        scratch_shapes=[pltpu.VMEM((tm, LANES), F32), pltpu.VMEM((tm, LANES), F32),
                        pltpu.VMEM((nseq, SUBLANES, B_WIDTH), F32), pltpu.VMEM((tm, AB_IN), F32)],
        compiler_params=_params(),
        name="attn_in_proj",
    )(xt, g, w, hist, cw)
    return [o.reshape(bsz, seq, o.shape[-1]) for o in outs[:-1]] + [outs[-1]]


def _attn_kernel(tq, cq, masked, sinks_ref, q_ref, kc_ref, vc_ref, kp_ref, vp_ref, att_ref,
                 kk_ref, vx_ref, s_ref, e_ref):
    i = pl.program_id(1)
    nk = WINDOW + cq
    nkp = MXU_DIM
    rows_kv = WINDOW + tq
    hd = HEAD_DIM

    kk_ref[0:WINDOW, :] = kp_ref[...].astype(BF16)
    kk_ref[WINDOW:rows_kv, :] = kc_ref[...].astype(BF16)
    kk_ref[rows_kv:, :] = jnp.zeros((kk_ref.shape[0] - rows_kv, KV_WIDTH), BF16)
    vall = jnp.concatenate([vp_ref[...], vc_ref[...]], axis=0)
    vx_ref[0:rows_kv, :] = jnp.concatenate([vall[:, :hd], vall[:, :hd], vall[:, hd:], vall[:, hd:]], axis=1)
    vx_ref[rows_kv:, :] = jnp.zeros((vx_ref.shape[0] - rows_kv, 2 * LANES), F32)

    col = lax.broadcasted_iota(jnp.int32, (A_GROUP * cq, nkp), 1)
    krow = lax.broadcasted_iota(jnp.int32, (nkp, LANES), 0)
    ones = jnp.ones((nkp, LANES), BF16)
    bodies = [(c, kh) for c in range(tq // cq) for kh in range(A_KV_HEADS)]
    fills = []
    for kh in range(A_KV_HEADS):
        sink = jnp.concatenate(
            [jnp.full((cq, nkp), sinks_ref[kh * A_GROUP + g], F32) for g in range(A_GROUP)], axis=0)
        fills.append(jnp.where(col == nk, sink, -jnp.inf))

    for n, (c, kh) in enumerate(bodies):
        r0 = c * cq
        heads = [kh * A_GROUP + g for g in range(A_GROUP)]
        kt = kk_ref[r0:r0 + nkp, kh * hd:(kh + 1) * hd]
        qs = jnp.concatenate([q_ref[r0:r0 + cq, hh * hd:(hh + 1) * hd] for hh in heads], axis=0)
        s = _dot_nt(qs, kt)
        valid = col < nk
        if masked and r0 < WINDOW:
            valid = valid & ((col >= WINDOW - r0) | (i > 0))
        s_ref[n] = jnp.where(valid, s, fills[kh])

    for n in range(len(bodies)):
        s = s_ref[n]
        m = jnp.max(s, axis=-1, keepdims=True)
        e_ref[n] = jnp.exp(s - m).astype(BF16)

    for n, (c, kh) in enumerate(bodies):
        r0 = c * cq
        vt = jnp.where(krow == nk, 0.0, vx_ref[r0:r0 + nkp, kh * LANES:(kh + 1) * LANES]).astype(BF16)
        ox = _dot(e_ref[n], jnp.concatenate([vt, ones], axis=1))
        o = ox[:, :LANES] / ox[:, LANES:]
        for g in range(A_GROUP):
            hh = kh * A_GROUP + g
            lanes = slice((hh % 2) * hd, (hh % 2 + 1) * hd)
            att_ref[r0:r0 + cq, hh * hd:(hh + 1) * hd] = o[g * cq:(g + 1) * cq, lanes].astype(BF16)


def _attn(sinks, q, k, v, k_prev, v_prev, tq, cq, masked):
    bsz, seq, _ = q.shape
    grid = (bsz, seq // tq)
    tok = lambda width: pl.BlockSpec((None, tq, width), lambda b, i: (b, i, 0))
    if masked:
        blocks_per_tile = tq // WINDOW
        prev = pl.BlockSpec((None, WINDOW, KV_WIDTH),
                            lambda b, i: (b, jnp.maximum(i * blocks_per_tile - 1, 0), 0))
    else:
        prev = pl.BlockSpec((None, WINDOW, KV_WIDTH), lambda b, i: (b, 0, 0))
    n_bodies = (tq // cq) * A_KV_HEADS
    kv_rows = (tq // cq - 1) * cq + MXU_DIM
    return pl.pallas_call(
        functools.partial(_attn_kernel, tq, cq, masked),
        grid=grid,
        in_specs=[pl.BlockSpec(memory_space=pltpu.SMEM),
                  tok(A_WIDTH), tok(KV_WIDTH), tok(KV_WIDTH), prev, prev],
        out_specs=tok(A_WIDTH),
        out_shape=jax.ShapeDtypeStruct((bsz, seq, A_WIDTH), BF16),
        scratch_shapes=[pltpu.VMEM((kv_rows, KV_WIDTH), BF16), pltpu.VMEM((kv_rows, 2 * LANES), F32),
                        pltpu.VMEM((n_bodies, A_GROUP * cq, MXU_DIM), F32),
                        pltpu.VMEM((n_bodies, A_GROUP * cq, MXU_DIM), BF16)],
        compiler_params=_params(),
        name="band_attention",
    )(sinks, q, k, v, k_prev, v_prev)


def _proj_ffn_kernel(n_in, gate, n_tiles, *refs):
    long_in, short_in = refs[:n_in], refs[n_in:2 * n_in]
    refs = refs[2 * n_in:]
    ng_ref = None
    if gate:
        ng_ref, refs = refs[0], refs[1:]
    wo_ref, xl_ref, xs_ref, g1_ref, g2_ref, wg_ref, wu_ref, wd_ref, g3_ref, ol_ref, os_ref = refs
    t = pl.program_id(0)

    def tile(ins, x_ref, o_ref):
        acc = None
        if gate:
            y_ref, z_ref = ins
            for gi in range(SSM_GROUPS):
                gcols = slice(gi * SSM_GROUP_W, (gi + 1) * SSM_GROUP_W)
                gg = y_ref[:, gcols].astype(F32) * z_ref[:, gcols].astype(F32)
                gg = gg * lax.rsqrt(jnp.mean(gg * gg, axis=-1, keepdims=True) + EPS)
                part = _dot((gg * ng_ref[:, gcols]).astype(BF16), wo_ref[gcols, :])
                acc = part if acc is None else acc + part
        else:
            off = 0
            for a_ref in ins:
                kdim = a_ref.shape[-1]
                part = _dot(a_ref[...], wo_ref[off:off + kdim, :])
                acc = part if acc is None else acc + part
                off += kdim

        x = x_ref[...] + _rms(acc, g1_ref[...])
        h = _rms(x, g2_ref[...]).astype(BF16)
        f = None
        off = 0
        for width in FF_CHUNKS:
            fc = slice(off, off + width)
            m = (_silu(_dot(h, wg_ref[:, fc])) * _dot(h, wu_ref[:, fc])).astype(BF16)
            part = _dot(m, wd_ref[fc, :])
            f = part if f is None else f + part
            off += width
        o_ref[...] = x + _rms(f, g3_ref[...])

    pl.when(t < n_tiles)(lambda: tile(long_in, xl_ref, ol_ref))
    pl.when(t == n_tiles)(lambda: tile(short_in, xs_ref, os_ref))


def _proj_ffn(mix_long, mix_short, ng, wo, x_long, x_short, g1, g2, wg, wu, wd, layer, g3, tm):
    bsz, seq, _ = x_long.shape
    tiles_per_seq = seq // tm
    n_tiles = bsz * tiles_per_seq
    n_short = x_short.shape[0] * x_short.shape[1]
    flat = lambda a: a.reshape(1, n_short, a.shape[-1])
    tile_idx = lambda t: jnp.minimum(t, n_tiles - 1)
    ltok = lambda width: pl.BlockSpec((None, tm, width),
                                      lambda t: (tile_idx(t) // tiles_per_seq, tile_idx(t) % tiles_per_seq, 0))
    stok = lambda width: pl.BlockSpec((None, n_short, width), lambda t: (0, 0, 0))
    gate = ng is not None
    ng_args, ng_specs = ([ng], [_const_spec((1, SSM_INNER))]) if gate else ([], [])
    out_long, out_short = pl.pallas_call(
        functools.partial(_proj_ffn_kernel, len(mix_long), gate, n_tiles),
        grid=(n_tiles + 1,),
        in_specs=[ltok(a.shape[-1]) for a in mix_long] + [stok(a.shape[-1]) for a in mix_short] + ng_specs + [
            _const_spec(wo.shape), ltok(D_MODEL), stok(D_MODEL), _const_spec((1, D_MODEL)), _const_spec((1, D_MODEL)),
            _layer_spec((D_MODEL, D_FF), layer), _layer_spec((D_MODEL, D_FF), layer),
            _layer_spec((D_FF, D_MODEL), layer), _const_spec((1, D_MODEL))],
        out_specs=[ltok(D_MODEL), stok(D_MODEL)],
        out_shape=[jax.ShapeDtypeStruct(x_long.shape, F32), jax.ShapeDtypeStruct((1, n_short, D_MODEL), F32)],
        compiler_params=pltpu.CompilerParams(dimension_semantics=("arbitrary",), vmem_limit_bytes=VMEM_LIMIT),
        name="out_proj_ffn",
    )(*mix_long, *[flat(a) for a in mix_short], *ng_args, wo, x_long, flat(x_short), g1, g2, wg, wu, wd, g3)
    return out_long, out_short.reshape(x_short.shape)


def _split3(v):
    t1 = v.astype(BF16)
    r = v - t1.astype(F32)
    t2 = r.astype(BF16)
    t3 = (r - t2.astype(F32)).astype(BF16)
    return t1, t2, t3


def _pack3(v):
    t1, t2, t3 = _split3(v)
    lane = lax.broadcasted_iota(jnp.int32, v.shape, 1)
    return jnp.where(lane < SSM_HEADS, t1,
                     jnp.where(lane < 2 * SSM_HEADS, t2,
                               jnp.where(lane < 3 * SSM_HEADS, t3, jnp.zeros_like(t3))))


def _ssm_in_kernel(tm, cl, x_ref, g_ref, w_ref, wdt_ref, hist_ref, cw_ref, cb_ref, dtb_ref, alog_ref,
                   z_ref, xs_ref, bm_ref, cm_ref, ta_ref, tdt_ref, nh_ref, carry_ref):
    i = pl.program_id(1)

    @pl.when(i == 0)
    def _():
        carry_ref[:, SUBLANES - (SSM_CONV - 1):, :] = hist_ref[...]

    h = _rms(x_ref[...], g_ref[...]).astype(BF16)

    dt = jax.nn.softplus(_dot(h, wdt_ref[...]) + dtb_ref[...])
    tdt_ref[...] = _pack3(dt)
    d1, d2, d3 = _split3(dt * (-jnp.exp(alog_ref[...]) * LOG2E))
    tcol = lax.broadcasted_iota(jnp.int32, (cl, 3 * cl), 1) % cl
    trow = lax.broadcasted_iota(jnp.int32, (cl, 3 * cl), 0)
    tril = jnp.where(tcol <= trow, 1.0, 0.0).astype(BF16)
    acum = []
    for c in range(tm // cl):
        rows = slice(c * cl, (c + 1) * cl)
        acum.append(_dot(tril, jnp.concatenate([d1[rows], d2[rows], d3[rows]], axis=0)))
    ta_ref[...] = _pack3(jnp.concatenate(acum, axis=0))

    for n in range(SSM_INNER // COL_CHUNK):
        cols = slice(n * COL_CHUNK, (n + 1) * COL_CHUNK)
        z_ref[:, cols] = _silu(_dot(h, w_ref[:, cols])).astype(BF16)

    for n in range(SSM_CONV_CH // COL_CHUNK):
        cols = slice(n * COL_CHUNK, (n + 1) * COL_CHUNK)
        raw = _dot(h, w_ref[:, SSM_INNER + cols.start:SSM_INNER + cols.stop])
        y = _silu(_causal_conv(carry_ref, nh_ref, raw, cw_ref[:, cols], cols) + cb_ref[:, cols])
        if cols.stop <= SSM_INNER:
            xs_ref[:, cols] = y
        elif cols.stop <= SSM_INNER + SSM_GN:
            bm_ref[:, cols.start - SSM_INNER:cols.stop - SSM_INNER] = y.astype(BF16)
        else:
            lo = SSM_INNER + SSM_GN
            cm_ref[:, cols.start - lo:cols.stop - lo] = y.astype(BF16)


def _ssm_in(x, g, w, wdt, hist, cw, cb, dtb, alog, tm, cl):
    bsz, seq, _ = x.shape
    xt, nseq = _tile_rows(x, tm)
    groups, rows, _ = xt.shape
    tok = lambda width: pl.BlockSpec((None, tm, width), lambda b, i: (b, i, 0))
    hist_spec = pl.BlockSpec((nseq, SSM_CONV - 1, SSM_CONV_CH), lambda b, i: (b, 0, 0))
    outs = pl.pallas_call(
        functools.partial(_ssm_in_kernel, tm, cl),
        grid=(groups, rows // tm),
        in_specs=[tok(D_MODEL), _const_spec((1, D_MODEL)), _const_spec(w.shape), _const_spec((D_MODEL, LANES)),
                  hist_spec,
                  _const_spec((SSM_CONV, SSM_CONV_CH)), _const_spec((1, SSM_CONV_CH)), _const_spec((1, LANES)),
                  _const_spec((1, LANES))],
        out_specs=[tok(SSM_INNER), tok(SSM_INNER), tok(SSM_GN), tok(SSM_GN), tok(LANES), tok(LANES), hist_spec],
        out_shape=[jax.ShapeDtypeStruct((groups, rows, SSM_INNER), BF16),
                   jax.ShapeDtypeStruct((groups, rows, SSM_INNER), F32),
                   jax.ShapeDtypeStruct((groups, rows, SSM_GN), BF16),
                   jax.ShapeDtypeStruct((groups, rows, SSM_GN), BF16),
                   jax.ShapeDtypeStruct((groups, rows, LANES), BF16),
                   jax.ShapeDtypeStruct((groups, rows, LANES), BF16),
                   jax.ShapeDtypeStruct((bsz, SSM_CONV - 1, SSM_CONV_CH), F32)],
        scratch_shapes=[pltpu.VMEM((nseq, SUBLANES, SSM_CONV_CH), F32)],
        compiler_params=_params(),
        name="ssm_in_proj",
    )(xt, g, w, wdt, hist, cw, cb, dtb, alog)
    return [o.reshape(bsz, seq, o.shape[-1]) for o in outs[:-1]] + [outs[-1]]


def _pad_rows(a, rows):
    if a.shape[0] == rows:
        return a
    return jnp.concatenate([a, jnp.zeros((rows - a.shape[0], a.shape[1]), a.dtype)], axis=0)


def _ssd_kernel(ts, cl, xs_ref, bm_ref, cm_ref, ta_ref, tdt_ref, h0_ref, dskip_ref,
                yo_ref, hout_ref, ht_ref, y_ref, e3_ref, aexp_ref, xdt_ref, st_ref, cb_ref, bmt_ref, w2_ref, rhs_ref):
    b = pl.program_id(0)
    i = pl.program_id(1)
    p = SSM_HEADDIM
    nc = ts // cl
    pairs = SSM_GROUP_W // LANES

    @pl.when((b == 0) & (i == 0))
    def _():
        kk = lax.broadcasted_iota(jnp.int32, (LANES, SSM_INNER), 0)
        cc = lax.broadcasted_iota(jnp.int32, (LANES, SSM_INNER), 1)
        hit = (kk % SSM_HEADS == cc // p) & (kk < 3 * SSM_HEADS)
        e3_ref[...] = jnp.where(hit, 1.0, 0.0).astype(BF16)

    @pl.when(i == 0)
    def _():
        ht_ref[...] = h0_ref[...].T

    lane = lax.broadcasted_iota(jnp.int32, (cl, LANES), 1)
    row = lax.broadcasted_iota(jnp.int32, (cl, LANES), 0)
    causal2 = (lane % p) <= row
    left = lane.astype(F32).astype(BF16) < p
    diag2 = (lane % p) == row
    groups = [(gi, slice(gi * SSM_GROUP_W, (gi + 1) * SSM_GROUP_W), slice(gi * SSM_STATE, (gi + 1) * SSM_STATE))
              for gi in range(SSM_GROUPS)]
    chunks = [(c, slice(c * cl, (c + 1) * cl)) for c in range(nc)]

    for gi, gcols, _ in groups:
        e3g = e3_ref[:, gcols]
        aexp_ref[:, gcols] = _dot(ta_ref[...], e3g)
        xdt_ref[:, gcols] = (xs_ref[:, gcols] * _dot(tdt_ref[...], e3g)).astype(BF16)

    for c, rows in chunks:
        for gi, _, scol in groups:
            bmp = _pad_rows(bm_ref[rows, scol], p)
            cbm = _dot_nt(cm_ref[rows, scol], bmp)
            cb_ref[c * SSM_GROUPS + gi] = jnp.concatenate([cbm, cbm], axis=1)
            bmt_ref[c * SSM_GROUPS + gi] = bmp.astype(F32).T.astype(BF16)

    for c, rows in chunks:
        for gi, _, _ in groups:
            for jj in range(pairs):
                idx = (c * SSM_GROUPS + gi) * pairs + jj
                cols = slice(gi * SSM_GROUP_W + jj * LANES, gi * SSM_GROUP_W + (jj + 1) * LANES)
                a_pair = aexp_ref[rows, cols]
                a_src = jnp.sum(jnp.where(diag2, a_pair, 0.0), axis=0, keepdims=True)
                seg = a_pair - a_src
                dec = jnp.exp2(jnp.where(causal2, seg, -jnp.inf))
                w2_ref[idx] = (dec * cb_ref[c * SSM_GROUPS + gi]).astype(BF16)
                x2 = xdt_ref[rows, cols]
                top = _pad_rows(jnp.where(left, x2, jnp.zeros_like(x2)), p)
                bot = _pad_rows(jnp.where(left, jnp.zeros_like(x2), x2), p)
                rhs_ref[idx] = jnp.concatenate([top, bot], axis=0)

    for c, rows in chunks:
        for gi, _, _ in groups:
            for jj in range(pairs):
                idx = (c * SSM_GROUPS + gi) * pairs + jj
                cols = slice(gi * SSM_GROUP_W + jj * LANES, gi * SSM_GROUP_W + (jj + 1) * LANES)
                y_ref[rows, cols] = _dot(w2_ref[idx], rhs_ref[idx])

    for c, rows in chunks:
        for gi, gcols, _ in groups:
            a_g = aexp_ref[rows, gcols]
            xw = xdt_ref[rows, gcols] * jnp.exp2(a_g[cl - 1:cl, :] - a_g).astype(BF16)
            st_ref[c, :, gcols] = _dot(bmt_ref[c * SSM_GROUPS + gi], _pad_rows(xw, p))

    for c, rows in chunks:
        for gi, gcols, scol in groups:
            hprev = ht_ref[:, gcols]
            a_g = aexp_ref[rows, gcols]
            y_off = _dot(cm_ref[rows, scol], hprev.astype(BF16)) * jnp.exp2(a_g)
            y_ref[rows, gcols] = y_ref[rows, gcols] + y_off
            ht_ref[:, gcols] = hprev * jnp.exp2(a_g[cl - 1:cl, :]) + st_ref[c, :, gcols]

    yo_ref[...] = (y_ref[...] + dskip_ref[...] * xs_ref[...]).astype(BF16)

    @pl.when(i == pl.num_programs(1) - 1)
    def _():
        hout_ref[...] = ht_ref[...].T


def _ssd(xs, bm, cm, ta, tdt, h0, dskip, ts, cl):
    bsz, seq, _ = xs.shape
    tok = lambda width: pl.BlockSpec((None, ts, width), lambda b, i: (b, i, 0))
    st_spec = pl.BlockSpec((None, SSM_INNER, SSM_STATE), lambda b, i: (b, 0, 0))
    nc = ts // cl
    n_cg = nc * SSM_GROUPS
    n_pairs = n_cg * (SSM_GROUP_W // LANES)
    return pl.pallas_call(
        functools.partial(_ssd_kernel, ts, cl),
        grid=(bsz, seq // ts),
        in_specs=[tok(SSM_INNER), tok(SSM_GN), tok(SSM_GN), tok(LANES), tok(LANES), st_spec,
                  _const_spec((1, SSM_INNER))],
        out_specs=[tok(SSM_INNER), st_spec],
        out_shape=[jax.ShapeDtypeStruct((bsz, seq, SSM_INNER), BF16),
                   jax.ShapeDtypeStruct((bsz, SSM_INNER, SSM_STATE), F32)],
        scratch_shapes=[pltpu.VMEM((SSM_STATE, SSM_INNER), F32), pltpu.VMEM((ts, SSM_INNER), F32),
                        pltpu.VMEM((LANES, SSM_INNER), BF16), pltpu.VMEM((ts, SSM_INNER), F32),
                        pltpu.VMEM((ts, SSM_INNER), BF16), pltpu.VMEM((nc, SSM_STATE, SSM_INNER), F32),
                        pltpu.VMEM((n_cg, cl, LANES), F32), pltpu.VMEM((n_cg, SSM_STATE, SSM_HEADDIM), BF16),
                        pltpu.VMEM((n_pairs, cl, LANES), BF16), pltpu.VMEM((n_pairs, LANES, LANES), BF16)],
        compiler_params=_params(),
        name="ssd_scan",
    )(xs, bm, cm, ta, tdt, h0, dskip)


def _layer0_mixers(x, pos0, sconv_hist, k_cache, v_cache, wts, tiles):
    t_in, t_att, c_att = tiles
    q, k, v, gated, new_sconv = _attn_in(x, wts["g"][0][0], wts["ab_w_in"], sconv_hist, wts["sconv_w"], pos0, t_in)
    masked = k_cache is None
    k_prev, v_prev = (k, v) if masked else (k_cache, v_cache)
    att = _attn(wts["sinks"], q, k, v, k_prev, v_prev, t_att, c_att, masked)
    bsz, seq, _ = x.shape
    new_k = k[:, seq - min(WINDOW, seq):].reshape(bsz, -1, A_KV_HEADS, HEAD_DIM)
    new_v = v[:, seq - min(WINDOW, seq):].reshape(bsz, -1, A_KV_HEADS, HEAD_DIM)
    return [att, gated], (new_k, new_v, new_sconv)


def _layer1_mixer(x, conv_hist, ssm_state, wts, tiles):
    t_sin, t_ssd, c_ssd = tiles
    bsz = x.shape[0]
    z, xs, bm, cm, ta, tdt, new_conv = _ssm_in(
        x, wts["g"][1][0], wts["ssm_w_in"], wts["w_dt"], conv_hist, wts["ssm_conv_w"],
        wts["ssm_conv_b"], wts["dt_bias"], wts["a_log"], t_sin, c_ssd)
    y, new_state = _ssd(xs, bm, cm, ta, tdt, ssm_state.reshape(bsz, SSM_INNER, SSM_STATE),
                        wts["d_skip"], t_ssd, c_ssd)
    return [y, z], (new_conv, new_state.reshape(bsz, SSM_HEADS, SSM_HEADDIM, SSM_STATE))


def kernel(x_prompt, x_sample, cache_attn_k, cache_attn_v, state_sconv, state_ssm_conv, state_ssm, norm_g,
           ab_w_in, ab_w_out, attn_sinks, sconv_w, ssm_w_in, ssm_conv_w, ssm_conv_b, ssm_dt_bias, ssm_a_log,
           ssm_d, ssm_norm_g, ssm_w_out, ffn_w_gate, ffn_w_up, ffn_w_down):
    w_in1 = ssm_w_in[0]
    wts = {
        "ab_w_in": ab_w_in[0].astype(BF16),
        "ab_w_out": ab_w_out[0].astype(BF16),
        "sinks": attn_sinks[0],
        "sconv_w": sconv_w[0],
        "ssm_w_in": w_in1.astype(BF16),
        "w_dt": jnp.tile(w_in1[:, SSM_INNER + SSM_CONV_CH:], (1, DT_REP)).astype(BF16),
        "ssm_conv_w": ssm_conv_w[0],
        "ssm_conv_b": ssm_conv_b[0].reshape(1, -1),
        "dt_bias": jnp.tile(ssm_dt_bias[0], DT_REP).reshape(1, -1),
        "a_log": jnp.tile(ssm_a_log[0], DT_REP).reshape(1, -1),
        "d_skip": jnp.repeat(ssm_d[0], SSM_HEADDIM).reshape(1, -1),
        "ssm_norm_g": ssm_norm_g[0].reshape(1, -1),
        "ssm_w_out": ssm_w_out[0].astype(BF16),
        "wg": ffn_w_gate.astype(BF16),
        "wu": ffn_w_up.astype(BF16),
        "wd": ffn_w_down.astype(BF16),
    }
    wts["g"] = [[norm_g[l, j].reshape(1, -1) for j in range(4)] for l in range(norm_g.shape[0])]
    g = wts["g"]
    bp = x_prompt.shape[0]
    bs, ls = x_sample.shape[0], x_sample.shape[1]
    t_ffn = 512

    mix_p, (kp, vp, scp) = _layer0_mixers(x_prompt, 0, jnp.zeros((bp, B_CONV - 1, B_WIDTH), F32), None, None,
                                          wts, (1024, 1024, CHUNK))
    mix_s, (ks, vs, scs) = _layer0_mixers(x_sample, PAST_LEN, state_sconv[0],
                                          cache_attn_k[0].reshape(bs, -1, KV_WIDTH),
                                          cache_attn_v[0].reshape(bs, -1, KV_WIDTH), wts, (bs * ls, ls, ls))
    xp, xs = _proj_ffn(mix_p, mix_s, None, wts["ab_w_out"], x_prompt, x_sample, g[0][1], g[0][2],
                       wts["wg"], wts["wu"], wts["wd"], 0, g[0][3], t_ffn)

    mix_p, (ccp, ssp) = _layer1_mixer(xp, jnp.zeros((bp, SSM_CONV - 1, SSM_CONV_CH), F32),
                                      jnp.zeros((bp, SSM_HEADS, SSM_HEADDIM, SSM_STATE), F32), wts, (512, 512, CHUNK))
    mix_s, (ccs, sss) = _layer1_mixer(xs, state_ssm_conv[0], state_ssm[0], wts, (bs * ls, ls, ls))
    yp, ys = _proj_ffn(mix_p, mix_s, wts["ssm_norm_g"], wts["ssm_w_out"], xp, xs, g[1][1], g[1][2],
                       wts["wg"], wts["wu"], wts["wd"], 1, g[1][3], t_ffn)

    lead = lambda a: a[None]
    return (yp, ys, lead(kp), lead(vp), lead(scp), lead(ccp), lead(ssp),
            lead(ks), lead(vs), lead(scs), lead(ccs), lead(sss))
```

```python
import functools

import jax
import jax.numpy as jnp
from jax import lax
from jax.experimental import pallas as pl
from jax.experimental.pallas import tpu as pltpu

F32 = jnp.float32
BF16 = jnp.bfloat16

D_MODEL = 1024
CHUNK = 64
EPS = 1e-6
LOG2E = 1.4426950408889634
PAST_LEN = 4096

A_HEADS = 8
A_KV_HEADS = 2
A_GROUP = A_HEADS // A_KV_HEADS
HEAD_DIM = 64
A_WIDTH = A_HEADS * HEAD_DIM
KV_WIDTH = A_KV_HEADS * HEAD_DIM
WINDOW = 128
ROPE_DIM = HEAD_DIM // 4
ROPE_HALF = ROPE_DIM // 2
ROPE_THETA = 500000.0
ATTN_SCALE = HEAD_DIM ** -0.5

B_WIDTH = D_MODEL // 2
B_CONV = 3
AB_IN = A_WIDTH + 2 * KV_WIDTH + 3 * B_WIDTH

SSM_INNER = 2 * D_MODEL
SSM_HEADDIM = 64
SSM_HEADS = SSM_INNER // SSM_HEADDIM
SSM_GROUPS = 4
SSM_GROUP_W = SSM_INNER // SSM_GROUPS
SSM_STATE = 128
SSM_CONV = 4
SSM_GN = SSM_GROUPS * SSM_STATE
SSM_CONV_CH = SSM_INNER + 2 * SSM_GN

D_FF = -(-8 * D_MODEL // (3 * 256)) * 256

LANES = 128
SUBLANES = 8
MXU_DIM = 256
DT_REP = LANES // SSM_HEADS
VMEM_LIMIT = 56 * 1024 * 1024
COL_CHUNK = MXU_DIM
FF_CHUNKS = (6 * MXU_DIM, 5 * MXU_DIM)
assert sum(FF_CHUNKS) == D_FF


def _rms(x, g):
    return x * lax.rsqrt(jnp.mean(x * x, axis=-1, keepdims=True) + EPS) * g


def _dot(a, b):
    return jnp.dot(a, b, preferred_element_type=F32)


def _dot_nt(a, b):
    return lax.dot_general(a, b, (((1,), (1,)), ((), ())), preferred_element_type=F32)


def _silu(x):
    half = 0.5 * x
    return half + half * jnp.tanh(half)


def _const_spec(shape):
    return pl.BlockSpec(shape, lambda *_: (0,) * len(shape), pipeline_mode=pl.Buffered(1))


def _layer_spec(shape, layer):
    return pl.BlockSpec((None,) + shape, lambda *_: (layer,) + (0,) * len(shape), pipeline_mode=pl.Buffered(1))


def _params():
    return pltpu.CompilerParams(dimension_semantics=("arbitrary", "arbitrary"),
                                vmem_limit_bytes=VMEM_LIMIT)


def _causal_conv(carry_ref, nh_ref, u, cw, cols):
    width = cw.shape[0]
    rows, ch = u.shape
    nseq = carry_ref.shape[0]
    u4 = u.reshape(nseq, rows // (nseq * SUBLANES), SUBLANES, ch)
    ext4 = jnp.concatenate([carry_ref[:, :, cols].reshape(nseq, 1, SUBLANES, ch), u4], axis=1)
    sub = lax.broadcasted_iota(jnp.int32, (1, 1, SUBLANES, ch), 2)

    def shift(above, here, k):
        return pltpu.roll(jnp.where(sub >= SUBLANES - k, above, here), k, 2)

    if width == 4:
        s1 = shift(jnp.concatenate([ext4[:, :1], ext4[:, :-1]], axis=1), ext4, 1)
        b = s1 * cw[0:1] + ext4 * cw[1:2]
        y4 = shift(b[:, :-1], b[:, 1:], 2) + (s1[:, 1:] * cw[2:3] + u4 * cw[3:4])
    else:
        y4 = None
        for j in range(width - 1):
            term = shift(ext4[:, :-1], u4, width - 1 - j) * cw[j:j + 1]
            y4 = term if y4 is None else y4 + term
        y4 = y4 + u4 * cw[width - 1:width]
    carry_ref[:, :, cols] = u4[:, -1]
    nh_ref[:, :, cols] = u4[:, -1, SUBLANES - (width - 1):]
    return y4.reshape(rows, ch)


def _rope_freq(shape):
    lane = lax.broadcasted_iota(jnp.int32, shape, 1)
    d = lane % HEAD_DIM
    expo = -((d % ROPE_HALF).astype(F32)) / ROPE_HALF
    freq = jnp.power(jnp.full(shape, ROPE_THETA, F32), expo)
    return jnp.where(d < ROPE_DIM, freq, 0.0), d


def _attn_in_kernel(pos0, tm, seq_rows, x_ref, g_ref, w_ref, hist_ref, cw_ref,
                    q_ref, k_ref, v_ref, gated_ref, nh_ref,
                    cr_ref, sr_ref, carry_ref, p_ref):
    b = pl.program_id(0)
    i = pl.program_id(1)

    @pl.when((b == 0) & (i == 0))
    def _():
        freq, _ = _rope_freq((tm, LANES))
        ang = (lax.broadcasted_iota(jnp.int32, (tm, LANES), 0) % seq_rows).astype(F32) * freq
        cr_ref[...] = jnp.cos(ang)
        sr_ref[...] = jnp.sin(ang)

    @pl.when(i == 0)
    def _():
        carry_ref[:, SUBLANES - (B_CONV - 1):, :] = hist_ref[...]

    freq1, d1 = _rope_freq((1, LANES))
    base = (pos0 + i * tm).astype(F32) * freq1
    cb = jnp.cos(base)
    sb = jnp.sin(base)
    cr = cr_ref[...]
    sr = sr_ref[...]
    cos_t = cr * cb - sr * sb
    sin_t = sr * cb + cr * sb
    m_lo = jnp.where(d1 < ROPE_HALF, -1.0, 0.0)
    m_hi = jnp.where((d1 >= ROPE_HALF) & (d1 < ROPE_DIM), 1.0, 0.0)

    def rope(t):
        partner = pltpu.roll(t, LANES - ROPE_HALF, 1) * m_lo + pltpu.roll(t, ROPE_HALF, 1) * m_hi
        return t * cos_t + partner * sin_t

    h = _rms(x_ref[...], g_ref[...]).astype(BF16)
    o1 = A_WIDTH
    o2 = o1 + KV_WIDTH
    o3 = o2 + KV_WIDTH
    o4 = o3 + B_WIDTH
    o5 = o4 + B_WIDTH

    for n in range(AB_IN // MXU_DIM):
        cols = slice(n * MXU_DIM, (n + 1) * MXU_DIM)
        p_ref[:, cols] = _dot(h, w_ref[:, cols])

    u = p_ref[:, o4:o5] * p_ref[:, o5:]
    y = _causal_conv(carry_ref, nh_ref, u, cw_ref[...], slice(None))
    gated_ref[...] = (p_ref[:, o3:o4] * y).astype(BF16)
    k_ref[...] = rope(p_ref[:, o1:o2])
    v_ref[...] = p_ref[:, o2:o3]
    for s in range(A_WIDTH // LANES):
        q_ref[:, s * LANES:(s + 1) * LANES] = (rope(p_ref[:, s * LANES:(s + 1) * LANES]) * ATTN_SCALE).astype(BF16)


def _tile_rows(x, tm):
    bsz, seq, d = x.shape
    nseq = max(tm // seq, 1)
    return x.reshape(bsz // nseq, nseq * seq, d), nseq


def _attn_in(x, g, w, hist, cw, pos0, tm):
    bsz, seq, _ = x.shape
    xt, nseq = _tile_rows(x, tm)
    groups, rows, _ = xt.shape
    tok = lambda width: pl.BlockSpec((None, tm, width), lambda b, i: (b, i, 0))
    hist_spec = pl.BlockSpec((nseq, B_CONV - 1, B_WIDTH), lambda b, i: (b, 0, 0))
    outs = pl.pallas_call(
        functools.partial(_attn_in_kernel, pos0, tm, tm // nseq),
        grid=(groups, rows // tm),
        in_specs=[tok(D_MODEL), _const_spec((1, D_MODEL)), _const_spec((D_MODEL, AB_IN)), hist_spec,
                  _const_spec((B_CONV, B_WIDTH))],
        out_specs=[tok(A_WIDTH), tok(KV_WIDTH), tok(KV_WIDTH), tok(B_WIDTH), hist_spec],
        out_shape=[jax.ShapeDtypeStruct((groups, rows, A_WIDTH), BF16),
                   jax.ShapeDtypeStruct((groups, rows, KV_WIDTH), F32),
                   jax.ShapeDtypeStruct((groups, rows, KV_WIDTH), F32),
                   jax.ShapeDtypeStruct((groups, rows, B_WIDTH), BF16),
                   jax.ShapeDtypeStruct((bsz, B_CONV - 1, B_WIDTH), F32)],
        scratch_shapes=[pltpu.VMEM((tm, LANES), F32), pltpu.VMEM((tm, LANES), F32),
                        pltpu.VMEM((nseq, SUBLANES, B_WIDTH), F32), pltpu.VMEM((tm, AB_IN), F32)],
        compiler_params=_params(),
        name="attn_in_proj",
    )(xt, g, w, hist, cw)
    return [o.reshape(bsz, seq, o.shape[-1]) for o in outs[:-1]] + [outs[-1]]


def _attn_kernel(tq, cq, masked, sinks_ref, q_ref, kc_ref, vc_ref, kp_ref, vp_ref, att_ref,
                 kk_ref, vx_ref, s_ref, e_ref):
    i = pl.program_id(1)
    nk = WINDOW + cq
    nkp = MXU_DIM
    rows_kv = WINDOW + tq
    hd = HEAD_DIM

    kk_ref[0:WINDOW, :] = kp_ref[...].astype(BF16)
    kk_ref[WINDOW:rows_kv, :] = kc_ref[...].astype(BF16)
    kk_ref[rows_kv:, :] = jnp.zeros((kk_ref.shape[0] - rows_kv, KV_WIDTH), BF16)
    vall = jnp.concatenate([vp_ref[...], vc_ref[...]], axis=0)
    vx_ref[0:rows_kv, :] = jnp.concatenate([vall[:, :hd], vall[:, :hd], vall[:, hd:], vall[:, hd:]], axis=1)
    vx_ref[rows_kv:, :] = jnp.zeros((vx_ref.shape[0] - rows_kv, 2 * LANES), F32)

    col = lax.broadcasted_iota(jnp.int32, (A_GROUP * cq, nkp), 1)
    krow = lax.broadcasted_iota(jnp.int32, (nkp, LANES), 0)
    ones = jnp.ones((nkp, LANES), BF16)
    bodies = [(c, kh) for c in range(tq // cq) for kh in range(A_KV_HEADS)]
    fills = []
    for kh in range(A_KV_HEADS):
        sink = jnp.concatenate(
            [jnp.full((cq, nkp), sinks_ref[kh * A_GROUP + g], F32) for g in range(A_GROUP)], axis=0)
        fills.append(jnp.where(col == nk, sink, -jnp.inf))

    for n, (c, kh) in enumerate(bodies):
        r0 = c * cq
        heads = [kh * A_GROUP + g for g in range(A_GROUP)]
        kt = kk_ref[r0:r0 + nkp, kh * hd:(kh + 1) * hd]
        qs = jnp.concatenate([q_ref[r0:r0 + cq, hh * hd:(hh + 1) * hd] for hh in heads], axis=0)
        s = _dot_nt(qs, kt)
        valid = col < nk
        if masked and r0 < WINDOW:
            valid = valid & ((col >= WINDOW - r0) | (i > 0))
        s_ref[n] = jnp.where(valid, s, fills[kh])

    for n in range(len(bodies)):
        s = s_ref[n]
        m = jnp.max(s, axis=-1, keepdims=True)
        e_ref[n] = jnp.exp(s - m).astype(BF16)

    for n, (c, kh) in enumerate(bodies):
        r0 = c * cq
        vt = jnp.where(krow == nk, 0.0, vx_ref[r0:r0 + nkp, kh * LANES:(kh + 1) * LANES]).astype(BF16)
        ox = _dot(e_ref[n], jnp.concatenate([vt, ones], axis=1))
        o = ox[:, :LANES] / ox[:, LANES:]
        for g in range(A_GROUP):
            hh = kh * A_GROUP + g
            lanes = slice((hh % 2) * hd, (hh % 2 + 1) * hd)
            att_ref[r0:r0 + cq, hh * hd:(hh + 1) * hd] = o[g * cq:(g + 1) * cq, lanes].astype(BF16)


def _attn(sinks, q, k, v, k_prev, v_prev, tq, cq, masked):
    bsz, seq, _ = q.shape
    grid = (bsz, seq // tq)
    tok = lambda width: pl.BlockSpec((None, tq, width), lambda b, i: (b, i, 0))
    if masked:
        blocks_per_tile = tq // WINDOW
        prev = pl.BlockSpec((None, WINDOW, KV_WIDTH),
                            lambda b, i: (b, jnp.maximum(i * blocks_per_tile - 1, 0), 0))
    else:
        prev = pl.BlockSpec((None, WINDOW, KV_WIDTH), lambda b, i: (b, 0, 0))
    n_bodies = (tq // cq) * A_KV_HEADS
    kv_rows = (tq // cq - 1) * cq + MXU_DIM
    return pl.pallas_call(
        functools.partial(_attn_kernel, tq, cq, masked),
        grid=grid,
        in_specs=[pl.BlockSpec(memory_space=pltpu.SMEM),
                  tok(A_WIDTH), tok(KV_WIDTH), tok(KV_WIDTH), prev, prev],
        out_specs=tok(A_WIDTH),
        out_shape=jax.ShapeDtypeStruct((bsz, seq, A_WIDTH), BF16),
        scratch_shapes=[pltpu.VMEM((kv_rows, KV_WIDTH), BF16), pltpu.VMEM((kv_rows, 2 * LANES), F32),
                        pltpu.VMEM((n_bodies, A_GROUP * cq, MXU_DIM), F32),
                        pltpu.VMEM((n_bodies, A_GROUP * cq, MXU_DIM), BF16)],
        compiler_params=_params(),
        name="band_attention",
    )(sinks, q, k, v, k_prev, v_prev)


def _proj_ffn_kernel(n_in, gate, n_tiles, *refs):
    long_in, short_in = refs[:n_in], refs[n_in:2 * n_in]
    refs = refs[2 * n_in:]
    ng_ref = None
    if gate:
        ng_ref, refs = refs[0], refs[1:]
    wo_ref, xl_ref, xs_ref, g1_ref, g2_ref, wg_ref, wu_ref, wd_ref, g3_ref, ol_ref, os_ref = refs
    t = pl.program_id(0)

    def tile(ins, x_ref, o_ref):
        acc = None
        if gate:
            y_ref, z_ref = ins
            for gi in range(SSM_GROUPS):
                gcols = slice(gi * SSM_GROUP_W, (gi + 1) * SSM_GROUP_W)
                gg = y_ref[:, gcols].astype(F32) * z_ref[:, gcols].astype(F32)
                gg = gg * lax.rsqrt(jnp.mean(gg * gg, axis=-1, keepdims=True) + EPS)
                part = _dot((gg * ng_ref[:, gcols]).astype(BF16), wo_ref[gcols, :])
                acc = part if acc is None else acc + part
        else:
            off = 0
            for a_ref in ins:
                kdim = a_ref.shape[-1]
                part = _dot(a_ref[...], wo_ref[off:off + kdim, :])
                acc = part if acc is None else acc + part
                off += kdim

        x = x_ref[...] + _rms(acc, g1_ref[...])
        h = _rms(x, g2_ref[...]).astype(BF16)
        f = None
        off = 0
        for width in FF_CHUNKS:
            fc = slice(off, off + width)
            m = (_silu(_dot(h, wg_ref[:, fc])) * _dot(h, wu_ref[:, fc])).astype(BF16)
            part = _dot(m, wd_ref[fc, :])
            f = part if f is None else f + part
            off += width
        o_ref[...] = x + _rms(f, g3_ref[...])

    pl.when(t < n_tiles)(lambda: tile(long_in, xl_ref, ol_ref))
    pl.when(t == n_tiles)(lambda: tile(short_in, xs_ref, os_ref))


def _proj_ffn(mix_long, mix_short, ng, wo, x_long, x_short, g1, g2, wg, wu, wd, layer, g3, tm):
    bsz, seq, _ = x_long.shape
    tiles_per_seq = seq // tm
    n_tiles = bsz * tiles_per_seq
    n_short = x_short.shape[0] * x_short.shape[1]
    flat = lambda a: a.reshape(1, n_short, a.shape[-1])
    tile_idx = lambda t: jnp.minimum(t, n_tiles - 1)
    ltok = lambda width: pl.BlockSpec((None, tm, width),
                                      lambda t: (tile_idx(t) // tiles_per_seq, tile_idx(t) % tiles_per_seq, 0))
    stok = lambda width: pl.BlockSpec((None, n_short, width), lambda t: (0, 0, 0))
    gate = ng is not None
    ng_args, ng_specs = ([ng], [_const_spec((1, SSM_INNER))]) if gate else ([], [])
    out_long, out_short = pl.pallas_call(
        functools.partial(_proj_ffn_kernel, len(mix_long), gate, n_tiles),
        grid=(n_tiles + 1,),
        in_specs=[ltok(a.shape[-1]) for a in mix_long] + [stok(a.shape[-1]) for a in mix_short] + ng_specs + [
            _const_spec(wo.shape), ltok(D_MODEL), stok(D_MODEL), _const_spec((1, D_MODEL)), _const_spec((1, D_MODEL)),
            _layer_spec((D_MODEL, D_FF), layer), _layer_spec((D_MODEL, D_FF), layer),
            _layer_spec((D_FF, D_MODEL), layer), _const_spec((1, D_MODEL))],
        out_specs=[ltok(D_MODEL), stok(D_MODEL)],
        out_shape=[jax.ShapeDtypeStruct(x_long.shape, F32), jax.ShapeDtypeStruct((1, n_short, D_MODEL), F32)],
        compiler_params=pltpu.CompilerParams(dimension_semantics=("arbitrary",), vmem_limit_bytes=VMEM_LIMIT),
        name="out_proj_ffn",
    )(*mix_long, *[flat(a) for a in mix_short], *ng_args, wo, x_long, flat(x_short), g1, g2, wg, wu, wd, g3)
    return out_long, out_short.reshape(x_short.shape)


def _split3(v):
    t1 = v.astype(BF16)
    r = v - t1.astype(F32)
    t2 = r.astype(BF16)
    t3 = (r - t2.astype(F32)).astype(BF16)
    return t1, t2, t3


def _pack3(v):
    t1, t2, t3 = _split3(v)
    lane = lax.broadcasted_iota(jnp.int32, v.shape, 1)
    return jnp.where(lane < SSM_HEADS, t1,
                     jnp.where(lane < 2 * SSM_HEADS, t2,
                               jnp.where(lane < 3 * SSM_HEADS, t3, jnp.zeros_like(t3))))


def _ssm_in_kernel(tm, cl, x_ref, g_ref, w_ref, wdt_ref, hist_ref, cw_ref, cb_ref, dtb_ref, alog_ref,
                   z_ref, xs_ref, bm_ref, cm_ref, ta_ref, tdt_ref, nh_ref, carry_ref):
    i = pl.program_id(1)

    @pl.when(i == 0)
    def _():
        carry_ref[:, SUBLANES - (SSM_CONV - 1):, :] = hist_ref[...]

    h = _rms(x_ref[...], g_ref[...]).astype(BF16)

    dt = jax.nn.softplus(_dot(h, wdt_ref[...]) + dtb_ref[...])
    tdt_ref[...] = _pack3(dt)
    d1, d2, d3 = _split3(dt * (-jnp.exp(alog_ref[...]) * LOG2E))
    tcol = lax.broadcasted_iota(jnp.int32, (cl, 3 * cl), 1) % cl
    trow = lax.broadcasted_iota(jnp.int32, (cl, 3 * cl), 0)
    tril = jnp.where(tcol <= trow, 1.0, 0.0).astype(BF16)
    acum = []
    for c in range(tm // cl):
        rows = slice(c * cl, (c + 1) * cl)
        acum.append(_dot(tril, jnp.concatenate([d1[rows], d2[rows], d3[rows]], axis=0)))
    ta_ref[...] = _pack3(jnp.concatenate(acum, axis=0))

    for n in range(SSM_INNER // COL_CHUNK):
        cols = slice(n * COL_CHUNK, (n + 1) * COL_CHUNK)
        z_ref[:, cols] = _silu(_dot(h, w_ref[:, cols])).astype(BF16)

    for n in range(SSM_CONV_CH // COL_CHUNK):
        cols = slice(n * COL_CHUNK, (n + 1) * COL_CHUNK)
        raw = _dot(h, w_ref[:, SSM_INNER + cols.start:SSM_INNER + cols.stop])
        y = _silu(_causal_conv(carry_ref, nh_ref, raw, cw_ref[:, cols], cols) + cb_ref[:, cols])
        if cols.stop <= SSM_INNER:
            xs_ref[:, cols] = y
        elif cols.stop <= SSM_INNER + SSM_GN:
            bm_ref[:, cols.start - SSM_INNER:cols.stop - SSM_INNER] = y.astype(BF16)
        else:
            lo = SSM_INNER + SSM_GN
            cm_ref[:, cols.start - lo:cols.stop - lo] = y.astype(BF16)


def _ssm_in(x, g, w, wdt, hist, cw, cb, dtb, alog, tm, cl):
    bsz, seq, _ = x.shape
    xt, nseq = _tile_rows(x, tm)
    groups, rows, _ = xt.shape
    tok = lambda width: pl.BlockSpec((None, tm, width), lambda b, i: (b, i, 0))
    hist_spec = pl.BlockSpec((nseq, SSM_CONV - 1, SSM_CONV_CH), lambda b, i: (b, 0, 0))
    outs = pl.pallas_call(
        functools.partial(_ssm_in_kernel, tm, cl),
        grid=(groups, rows // tm),
        in_specs=[tok(D_MODEL), _const_spec((1, D_MODEL)), _const_spec(w.shape), _const_spec((D_MODEL, LANES)),
                  hist_spec,
                  _const_spec((SSM_CONV, SSM_CONV_CH)), _const_spec((1, SSM_CONV_CH)), _const_spec((1, LANES)),
                  _const_spec((1, LANES))],
        out_specs=[tok(SSM_INNER), tok(SSM_INNER), tok(SSM_GN), tok(SSM_GN), tok(LANES), tok(LANES), hist_spec],
        out_shape=[jax.ShapeDtypeStruct((groups, rows, SSM_INNER), BF16),
                   jax.ShapeDtypeStruct((groups, rows, SSM_INNER), F32),
                   jax.ShapeDtypeStruct((groups, rows, SSM_GN), BF16),
                   jax.ShapeDtypeStruct((groups, rows, SSM_GN), BF16),
                   jax.ShapeDtypeStruct((groups, rows, LANES), BF16),
                   jax.ShapeDtypeStruct((groups, rows, LANES), BF16),
                   jax.ShapeDtypeStruct((bsz, SSM_CONV - 1, SSM_CONV_CH), F32)],
        scratch_shapes=[pltpu.VMEM((nseq, SUBLANES, SSM_CONV_CH), F32)],
        compiler_params=_params(),
        name="ssm_in_proj",
    )(xt, g, w, wdt, hist, cw, cb, dtb, alog)
    return [o.reshape(bsz, seq, o.shape[-1]) for o in outs[:-1]] + [outs[-1]]


def _pad_rows(a, rows):
    if a.shape[0] == rows:
        return a
    return jnp.concatenate([a, jnp.zeros((rows - a.shape[0], a.shape[1]), a.dtype)], axis=0)


def _ssd_kernel(ts, cl, xs_ref, bm_ref, cm_ref, ta_ref, tdt_ref, h0_ref, dskip_ref,
                yo_ref, hout_ref, ht_ref, y_ref, e3_ref, aexp_ref, xdt_ref, st_ref, cb_ref, bmt_ref, w2_ref, rhs_ref):
    b = pl.program_id(0)
    i = pl.program_id(1)
    p = SSM_HEADDIM
    nc = ts // cl
    pairs = SSM_GROUP_W // LANES

    @pl.when((b == 0) & (i == 0))
    def _():
        kk = lax.broadcasted_iota(jnp.int32, (LANES, SSM_INNER), 0)
        cc = lax.broadcasted_iota(jnp.int32, (LANES, SSM_INNER), 1)
        hit = (kk % SSM_HEADS == cc // p) & (kk < 3 * SSM_HEADS)
        e3_ref[...] = jnp.where(hit, 1.0, 0.0).astype(BF16)

    @pl.when(i == 0)
    def _():
        ht_ref[...] = h0_ref[...].T

    lane = lax.broadcasted_iota(jnp.int32, (cl, LANES), 1)
    row = lax.broadcasted_iota(jnp.int32, (cl, LANES), 0)
    causal2 = (lane % p) <= row
    left = lane.astype(F32).astype(BF16) < p
    diag2 = (lane % p) == row
    groups = [(gi, slice(gi * SSM_GROUP_W, (gi + 1) * SSM_GROUP_W), slice(gi * SSM_STATE, (gi + 1) * SSM_STATE))
              for gi in range(SSM_GROUPS)]
    chunks = [(c, slice(c * cl, (c + 1) * cl)) for c in range(nc)]

    for gi, gcols, _ in groups:
        e3g = e3_ref[:, gcols]
        aexp_ref[:, gcols] = _dot(ta_ref[...], e3g)
        xdt_ref[:, gcols] = (xs_ref[:, gcols] * _dot(tdt_ref[...], e3g)).astype(BF16)

    for c, rows in chunks:
        for gi, _, scol in groups:
            bmp = _pad_rows(bm_ref[rows, scol], p)
            cbm = _dot_nt(cm_ref[rows, scol], bmp)
            cb_ref[c * SSM_GROUPS + gi] = jnp.concatenate([cbm, cbm], axis=1)
            bmt_ref[c * SSM_GROUPS + gi] = bmp.astype(F32).T.astype(BF16)

    for c, rows in chunks:
        for gi, _, _ in groups:
            for jj in range(pairs):
                idx = (c * SSM_GROUPS + gi) * pairs + jj
                cols = slice(gi * SSM_GROUP_W + jj * LANES, gi * SSM_GROUP_W + (jj + 1) * LANES)
                a_pair = aexp_ref[rows, cols]
                a_src = jnp.sum(jnp.where(diag2, a_pair, 0.0), axis=0, keepdims=True)
                seg = a_pair - a_src
                dec = jnp.exp2(jnp.where(causal2, seg, -jnp.inf))
                w2_ref[idx] = (dec * cb_ref[c * SSM_GROUPS + gi]).astype(BF16)
                x2 = xdt_ref[rows, cols]
                top = _pad_rows(jnp.where(left, x2, jnp.zeros_like(x2)), p)
                bot = _pad_rows(jnp.where(left, jnp.zeros_like(x2), x2), p)
                rhs_ref[idx] = jnp.concatenate([top, bot], axis=0)

    for c, rows in chunks:
        for gi, _, _ in groups:
            for jj in range(pairs):
                idx = (c * SSM_GROUPS + gi) * pairs + jj
                cols = slice(gi * SSM_GROUP_W + jj * LANES, gi * SSM_GROUP_W + (jj + 1) * LANES)
                y_ref[rows, cols] = _dot(w2_ref[idx], rhs_ref[idx])

    for c, rows in chunks:
        for gi, gcols, _ in groups:
            a_g = aexp_ref[rows, gcols]
            xw = xdt_ref[rows, gcols] * jnp.exp2(a_g[cl - 1:cl, :] - a_g).astype(BF16)
            st_ref[c, :, gcols] = _dot(bmt_ref[c * SSM_GROUPS + gi], _pad_rows(xw, p))

    for c, rows in chunks:
        for gi, gcols, scol in groups:
            hprev = ht_ref[:, gcols]
            a_g = aexp_ref[rows, gcols]
            y_off = _dot(cm_ref[rows, scol], hprev.astype(BF16)) * jnp.exp2(a_g)
            y_ref[rows, gcols] = y_ref[rows, gcols] + y_off
            ht_ref[:, gcols] = hprev * jnp.exp2(a_g[cl - 1:cl, :]) + st_ref[c, :, gcols]

    yo_ref[...] = (y_ref[...] + dskip_ref[...] * xs_ref[...]).astype(BF16)

    @pl.when(i == pl.num_programs(1) - 1)
    def _():
        hout_ref[...] = ht_ref[...].T


def _ssd(xs, bm, cm, ta, tdt, h0, dskip, ts, cl):
    bsz, seq, _ = xs.shape
    tok = lambda width: pl.BlockSpec((None, ts, width), lambda b, i: (b, i, 0))
    st_spec = pl.BlockSpec((None, SSM_INNER, SSM_STATE), lambda b, i: (b, 0, 0))
    nc = ts // cl
    n_cg = nc * SSM_GROUPS
    n_pairs = n_cg * (SSM_GROUP_W // LANES)
    return pl.pallas_call(
        functools.partial(_ssd_kernel, ts, cl),
        grid=(bsz, seq // ts),
        in_specs=[tok(SSM_INNER), tok(SSM_GN), tok(SSM_GN), tok(LANES), tok(LANES), st_spec,
                  _const_spec((1, SSM_INNER))],
        out_specs=[tok(SSM_INNER), st_spec],
        out_shape=[jax.ShapeDtypeStruct((bsz, seq, SSM_INNER), BF16),
                   jax.ShapeDtypeStruct((bsz, SSM_INNER, SSM_STATE), F32)],
        scratch_shapes=[pltpu.VMEM((SSM_STATE, SSM_INNER), F32), pltpu.VMEM((ts, SSM_INNER), F32),
                        pltpu.VMEM((LANES, SSM_INNER), BF16), pltpu.VMEM((ts, SSM_INNER), F32),
                        pltpu.VMEM((ts, SSM_INNER), BF16), pltpu.VMEM((nc, SSM_STATE, SSM_INNER), F32),
                        pltpu.VMEM((n_cg, cl, LANES), F32), pltpu.VMEM((n_cg, SSM_STATE, SSM_HEADDIM), BF16),
                        pltpu.VMEM((n_pairs, cl, LANES), BF16), pltpu.VMEM((n_pairs, LANES, LANES), BF16)],
        compiler_params=_params(),
        name="ssd_scan",
    )(xs, bm, cm, ta, tdt, h0, dskip)


def _layer0_mixers(x, pos0, sconv_hist, k_cache, v_cache, wts, tiles):
    t_in, t_att, c_att = tiles
    q, k, v, gated, new_sconv = _attn_in(x, wts["g"][0][0], wts["ab_w_in"], sconv_hist, wts["sconv_w"], pos0, t_in)
    masked = k_cache is None
    k_prev, v_prev = (k, v) if masked else (k_cache, v_cache)
    att = _attn(wts["sinks"], q, k, v, k_prev, v_prev, t_att, c_att, masked)
    bsz, seq, _ = x.shape
    new_k = k[:, seq - min(WINDOW, seq):].reshape(bsz, -1, A_KV_HEADS, HEAD_DIM)
    new_v = v[:, seq - min(WINDOW, seq):].reshape(bsz, -1, A_KV_HEADS, HEAD_DIM)
    return [att, gated], (new_k, new_v, new_sconv)


def _layer1_mixer(x, conv_hist, ssm_state, wts, tiles):
    t_sin, t_ssd, c_ssd = tiles
    bsz = x.shape[0]
    z, xs, bm, cm, ta, tdt, new_conv = _ssm_in(
        x, wts["g"][1][0], wts["ssm_w_in"], wts["w_dt"], conv_hist, wts["ssm_conv_w"],
        wts["ssm_conv_b"], wts["dt_bias"], wts["a_log"], t_sin, c_ssd)
    y, new_state = _ssd(xs, bm, cm, ta, tdt, ssm_state.reshape(bsz, SSM_INNER, SSM_STATE),
                        wts["d_skip"], t_ssd, c_ssd)
    return [y, z], (new_conv, new_state.reshape(bsz, SSM_HEADS, SSM_HEADDIM, SSM_STATE))


def kernel(x_prompt, x_sample, cache_attn_k, cache_attn_v, state_sconv, state_ssm_conv, state_ssm, norm_g,
           ab_w_in, ab_w_out, attn_sinks, sconv_w, ssm_w_in, ssm_conv_w, ssm_conv_b, ssm_dt_bias, ssm_a_log,
           ssm_d, ssm_norm_g, ssm_w_out, ffn_w_gate, ffn_w_up, ffn_w_down):
    w_in1 = ssm_w_in[0]
    wts = {
        "ab_w_in": ab_w_in[0].astype(BF16),
        "ab_w_out": ab_w_out[0].astype(BF16),
        "sinks": attn_sinks[0],
        "sconv_w": sconv_w[0],
        "ssm_w_in": w_in1.astype(BF16),
        "w_dt": jnp.tile(w_in1[:, SSM_INNER + SSM_CONV_CH:], (1, DT_REP)).astype(BF16),
        "ssm_conv_w": ssm_conv_w[0],
        "ssm_conv_b": ssm_conv_b[0].reshape(1, -1),
        "dt_bias": jnp.tile(ssm_dt_bias[0], DT_REP).reshape(1, -1),
        "a_log": jnp.tile(ssm_a_log[0], DT_REP).reshape(1, -1),
        "d_skip": jnp.repeat(ssm_d[0], SSM_HEADDIM).reshape(1, -1),
        "ssm_norm_g": ssm_norm_g[0].reshape(1, -1),
        "ssm_w_out": ssm_w_out[0].astype(BF16),
        "wg": ffn_w_gate.astype(BF16),
        "wu": ffn_w_up.astype(BF16),
        "wd": ffn_w_down.astype(BF16),
    }
    wts["g"] = [[norm_g[l, j].reshape(1, -1) for j in range(4)] for l in range(norm_g.shape[0])]
    g = wts["g"]
    bp = x_prompt.shape[0]
    bs, ls = x_sample.shape[0], x_sample.shape[1]
    t_ffn = 512

    mix_p, (kp, vp, scp) = _layer0_mixers(x_prompt, 0, jnp.zeros((bp, B_CONV - 1, B_WIDTH), F32), None, None,
                                          wts, (1024, 1024, CHUNK))
    mix_s, (ks, vs, scs) = _layer0_mixers(x_sample, PAST_LEN, state_sconv[0],
                                          cache_attn_k[0].reshape(bs, -1, KV_WIDTH),
                                          cache_attn_v[0].reshape(bs, -1, KV_WIDTH), wts, (bs * ls, ls, ls))
    xp, xs = _proj_ffn(mix_p, mix_s, None, wts["ab_w_out"], x_prompt, x_sample, g[0][1], g[0][2],
                       wts["wg"], wts["wu"], wts["wd"], 0, g[0][3], t_ffn)

    mix_p, (ccp, ssp) = _layer1_mixer(xp, jnp.zeros((bp, SSM_CONV - 1, SSM_CONV_CH), F32),
                                      jnp.zeros((bp, SSM_HEADS, SSM_HEADDIM, SSM_STATE), F32), wts, (512, 512, CHUNK))
    mix_s, (ccs, sss) = _layer1_mixer(xs, state_ssm_conv[0], state_ssm[0], wts, (bs * ls, ls, ls))
    yp, ys = _proj_ffn(mix_p, mix_s, wts["ssm_norm_g"], wts["ssm_w_out"], xp, xs, g[1][1], g[1][2],
                       wts["wg"], wts["wu"], wts["wd"], 1, g[1][3], t_ffn)

    lead = lambda a: a[None]
    return (yp, ys, lead(kp), lead(vp), lead(scp), lead(ccp), lead(ssp),
            lead(ks), lead(vs), lead(scs), lead(ccs), lead(sss))
```

```python
import functools

import jax
import jax.numpy as jnp
from jax import lax
from jax.experimental import pallas as pl
from jax.experimental.pallas import tpu as pltpu

F32 = jnp.float32
BF16 = jnp.bfloat16

D_MODEL = 1024
CHUNK = 64
EPS = 1e-6
LOG2E = 1.4426950408889634
PAST_LEN = 4096

A_HEADS = 8
A_KV_HEADS = 2
A_GROUP = A_HEADS // A_KV_HEADS
HEAD_DIM = 64
A_WIDTH = A_HEADS * HEAD_DIM
KV_WIDTH = A_KV_HEADS * HEAD_DIM
WINDOW = 128
ROPE_DIM = HEAD_DIM // 4
ROPE_HALF = ROPE_DIM // 2
ROPE_THETA = 500000.0
ATTN_SCALE = HEAD_DIM ** -0.5

B_WIDTH = D_MODEL // 2
B_CONV = 3
AB_IN = A_WIDTH + 2 * KV_WIDTH + 3 * B_WIDTH

SSM_INNER = 2 * D_MODEL
SSM_HEADDIM = 64
SSM_HEADS = SSM_INNER // SSM_HEADDIM
SSM_GROUPS = 4
SSM_GROUP_W = SSM_INNER // SSM_GROUPS
SSM_STATE = 128
SSM_CONV = 4
SSM_GN = SSM_GROUPS * SSM_STATE
SSM_CONV_CH = SSM_INNER + 2 * SSM_GN

D_FF = -(-8 * D_MODEL // (3 * 256)) * 256

LANES = 128
SUBLANES = 8
MXU_DIM = 256
DT_REP = LANES // SSM_HEADS
VMEM_LIMIT = 56 * 1024 * 1024
COL_CHUNK = MXU_DIM
FF_CHUNKS = (6 * MXU_DIM, 5 * MXU_DIM)
assert sum(FF_CHUNKS) == D_FF


def _rms(x, g):
    return x * lax.rsqrt(jnp.mean(x * x, axis=-1, keepdims=True) + EPS) * g


def _dot(a, b):
    return jnp.dot(a, b, preferred_element_type=F32)


def _dot_nt(a, b):
    return lax.dot_general(a, b, (((1,), (1,)), ((), ())), preferred_element_type=F32)


def _silu_of_twice(half):
    return half + half * jnp.tanh(half)


def _const_spec(shape):
    return pl.BlockSpec(shape, lambda *_: (0,) * len(shape), pipeline_mode=pl.Buffered(1))


def _layer_spec(shape, layer):
    return pl.BlockSpec((None,) + shape, lambda *_: (layer,) + (0,) * len(shape), pipeline_mode=pl.Buffered(1))


def _params():
    return pltpu.CompilerParams(dimension_semantics=("arbitrary", "arbitrary"),
                                vmem_limit_bytes=VMEM_LIMIT)


def _causal_conv(carry_ref, nh_ref, u, cw, cols):
    width = cw.shape[0]
    rows, ch = u.shape
    nseq = carry_ref.shape[0]
    u4 = u.reshape(nseq, rows // (nseq * SUBLANES), SUBLANES, ch)
    ext4 = jnp.concatenate([carry_ref[:, :, cols].reshape(nseq, 1, SUBLANES, ch), u4], axis=1)
    sub = lax.broadcasted_iota(jnp.int32, (1, 1, SUBLANES, ch), 2)

    def shift(above, here, k):
        return pltpu.roll(jnp.where(sub >= SUBLANES - k, above, here), k, 2)

    if width == 4:
        s1 = shift(jnp.concatenate([ext4[:, :1], ext4[:, :-1]], axis=1), ext4, 1)
        b = s1 * cw[0:1] + ext4 * cw[1:2]
        y4 = shift(b[:, :-1], b[:, 1:], 2) + (s1[:, 1:] * cw[2:3] + u4 * cw[3:4])
    else:
        y4 = None
        for j in range(width - 1):
            term = shift(ext4[:, :-1], u4, width - 1 - j) * cw[j:j + 1]
            y4 = term if y4 is None else y4 + term
        y4 = y4 + u4 * cw[width - 1:width]
    carry_ref[:, :, cols] = u4[:, -1]
    nh_ref[:, :, cols] = u4[:, -1, SUBLANES - (width - 1):]
    return y4.reshape(rows, ch)


def _rope_freq(shape):
    lane = lax.broadcasted_iota(jnp.int32, shape, 1)
    d = lane % HEAD_DIM
    expo = -((d % ROPE_HALF).astype(F32)) / ROPE_HALF
    freq = jnp.power(jnp.full(shape, ROPE_THETA, F32), expo)
    return jnp.where(d < ROPE_DIM, freq, 0.0), d


def _attn_in_kernel(pos0, tm, seq_rows, x_ref, g_ref, w_ref, hist_ref, cw_ref,
                    q_ref, k_ref, v_ref, gated_ref, nh_ref,
                    cr_ref, sr_ref, carry_ref, p_ref):
    b = pl.program_id(0)
    i = pl.program_id(1)

    @pl.when((b == 0) & (i == 0))
    def _():
        freq, _ = _rope_freq((tm, LANES))
        ang = (lax.broadcasted_iota(jnp.int32, (tm, LANES), 0) % seq_rows).astype(F32) * freq
        cr_ref[...] = jnp.cos(ang)
        sr_ref[...] = jnp.sin(ang)

    @pl.when(i == 0)
    def _():
        carry_ref[:, SUBLANES - (B_CONV - 1):, :] = hist_ref[...]

    freq1, d1 = _rope_freq((1, LANES))
    base = (pos0 + i * tm).astype(F32) * freq1
    cb = jnp.cos(base)
    sb = jnp.sin(base)
    cr = cr_ref[...]
    sr = sr_ref[...]
    cos_t = cr * cb - sr * sb
    sin_t = sr * cb + cr * sb
    m_lo = jnp.where(d1 < ROPE_HALF, -1.0, 0.0)
    m_hi = jnp.where((d1 >= ROPE_HALF) & (d1 < ROPE_DIM), 1.0, 0.0)

    def rope(t):
        partner = pltpu.roll(t, LANES - ROPE_HALF, 1) * m_lo + pltpu.roll(t, ROPE_HALF, 1) * m_hi
        return t * cos_t + partner * sin_t

    h = _rms(x_ref[...], g_ref[...]).astype(BF16)
    o1 = A_WIDTH
    o2 = o1 + KV_WIDTH
    o3 = o2 + KV_WIDTH
    o4 = o3 + B_WIDTH
    o5 = o4 + B_WIDTH

    for n in range(AB_IN // MXU_DIM):
        cols = slice(n * MXU_DIM, (n + 1) * MXU_DIM)
        p_ref[:, cols] = _dot(h, w_ref[:, cols])

    u = p_ref[:, o4:o5] * p_ref[:, o5:]
    y = _causal_conv(carry_ref, nh_ref, u, cw_ref[...], slice(None))
    gated_ref[...] = (p_ref[:, o3:o4] * y).astype(BF16)
    k_ref[...] = rope(p_ref[:, o1:o2])
    v_ref[...] = p_ref[:, o2:o3]
    for s in range(A_WIDTH // LANES):
        q_ref[:, s * LANES:(s + 1) * LANES] = (rope(p_ref[:, s * LANES:(s + 1) * LANES]) * ATTN_SCALE).astype(BF16)


def _tile_rows(x, tm):
    bsz, seq, d = x.shape
    nseq = max(tm // seq, 1)
    return x.reshape(bsz // nseq, nseq * seq, d), nseq


def _attn_in(x, g, w, hist, cw, pos0, tm):
    bsz, seq, _ = x.shape
    xt, nseq = _tile_rows(x, tm)
    groups, rows, _ = xt.shape
    tok = lambda width: pl.BlockSpec((None, tm, width), lambda b, i: (b, i, 0))
    hist_spec = pl.BlockSpec((nseq, B_CONV - 1, B_WIDTH), lambda b, i: (b, 0, 0))
    outs = pl.pallas_call(
        functools.partial(_attn_in_kernel, pos0, tm, tm // nseq),
        grid=(groups, rows // tm),
        in_specs=[tok(D_MODEL), _const_spec((1, D_MODEL)), _const_spec((D_MODEL, AB_IN)), hist_spec,
                  _const_spec((B_CONV, B_WIDTH))],
        out_specs=[tok(A_WIDTH), tok(KV_WIDTH), tok(KV_WIDTH), tok(B_WIDTH), hist_spec],
        out_shape=[jax.ShapeDtypeStruct((groups, rows, A_WIDTH), BF16),
                   jax.ShapeDtypeStruct((groups, rows, KV_WIDTH), F32),
                   jax.ShapeDtypeStruct((groups, rows, KV_WIDTH), F32),
                   jax.ShapeDtypeStruct((groups, rows, B_WIDTH), BF16),
                   jax.ShapeDtypeStruct((bsz, B_CONV - 1, B_WIDTH), F32)],
        scratch_shapes=[pltpu.VMEM((tm, LANES), F32), pltpu.VMEM((tm, LANES), F32),
                        pltpu.VMEM((nseq, SUBLANES, B_WIDTH), F32), pltpu.VMEM((tm, AB_IN), F32)],
        compiler_params=_params(),
        name="attn_in_proj",
    )(xt, g, w, hist, cw)
    return [o.reshape(bsz, seq, o.shape[-1]) for o in outs[:-1]] + [outs[-1]]


def _attn_kernel(tq, cq, masked, sinks_ref, q_ref, kc_ref, vc_ref, kp_ref, vp_ref, att_ref,
                 kk_ref, vx_ref, s_ref, e_ref):
    i = pl.program_id(1)
    nk = WINDOW + cq
    nkp = MXU_DIM
    rows_kv = WINDOW + tq
    hd = HEAD_DIM

    kk_ref[0:WINDOW, :] = kp_ref[...].astype(BF16)
    kk_ref[WINDOW:rows_kv, :] = kc_ref[...].astype(BF16)
    kk_ref[rows_kv:, :] = jnp.zeros((kk_ref.shape[0] - rows_kv, KV_WIDTH), BF16)
    vall = jnp.concatenate([vp_ref[...], vc_ref[...]], axis=0)
    vx_ref[0:rows_kv, :] = jnp.concatenate([vall[:, :hd], vall[:, :hd], vall[:, hd:], vall[:, hd:]], axis=1)
    vx_ref[rows_kv:, :] = jnp.zeros((vx_ref.shape[0] - rows_kv, 2 * LANES), F32)

    col = lax.broadcasted_iota(jnp.int32, (A_GROUP * cq, nkp), 1)
    krow = lax.broadcasted_iota(jnp.int32, (nkp, LANES), 0)
    ones = jnp.ones((nkp, LANES), BF16)
    bodies = [(c, kh) for c in range(tq // cq) for kh in range(A_KV_HEADS)]
    fills = []
    for kh in range(A_KV_HEADS):
        sink = jnp.concatenate(
            [jnp.full((cq, nkp), sinks_ref[kh * A_GROUP + g], F32) for g in range(A_GROUP)], axis=0)
        fills.append(jnp.where(col == nk, sink, -jnp.inf))

    for n, (c, kh) in enumerate(bodies):
        r0 = c * cq
        heads = [kh * A_GROUP + g for g in range(A_GROUP)]
        kt = kk_ref[r0:r0 + nkp, kh * hd:(kh + 1) * hd]
        qs = jnp.concatenate([q_ref[r0:r0 + cq, hh * hd:(hh + 1) * hd] for hh in heads], axis=0)
        s = _dot_nt(qs, kt)
        valid = col < nk
        if masked and r0 < WINDOW:
            valid = valid & ((col >= WINDOW - r0) | (i > 0))
        s_ref[n] = jnp.where(valid, s, fills[kh])

    for n in range(len(bodies)):
        s = s_ref[n]
        m = jnp.max(s, axis=-1, keepdims=True)
        e_ref[n] = jnp.exp(s - m).astype(BF16)

    for n, (c, kh) in enumerate(bodies):
        r0 = c * cq
        vt = jnp.where(krow == nk, 0.0, vx_ref[r0:r0 + nkp, kh * LANES:(kh + 1) * LANES]).astype(BF16)
        ox = _dot(e_ref[n], jnp.concatenate([vt, ones], axis=1))
        o = ox[:, :LANES] / ox[:, LANES:]
        for g in range(A_GROUP):
            hh = kh * A_GROUP + g
            lanes = slice((hh % 2) * hd, (hh % 2 + 1) * hd)
            att_ref[r0:r0 + cq, hh * hd:(hh + 1) * hd] = o[g * cq:(g + 1) * cq, lanes].astype(BF16)


def _attn(sinks, q, k, v, k_prev, v_prev, tq, cq, masked):
    bsz, seq, _ = q.shape
    grid = (bsz, seq // tq)
    tok = lambda width: pl.BlockSpec((None, tq, width), lambda b, i: (b, i, 0))
    if masked:
        blocks_per_tile = tq // WINDOW
        prev = pl.BlockSpec((None, WINDOW, KV_WIDTH),
                            lambda b, i: (b, jnp.maximum(i * blocks_per_tile - 1, 0), 0))
    else:
        prev = pl.BlockSpec((None, WINDOW, KV_WIDTH), lambda b, i: (b, 0, 0))
    n_bodies = (tq // cq) * A_KV_HEADS
    kv_rows = (tq // cq - 1) * cq + MXU_DIM
    return pl.pallas_call(
        functools.partial(_attn_kernel, tq, cq, masked),
        grid=grid,
        in_specs=[pl.BlockSpec(memory_space=pltpu.SMEM),
                  tok(A_WIDTH), tok(KV_WIDTH), tok(KV_WIDTH), prev, prev],
        out_specs=tok(A_WIDTH),
        out_shape=jax.ShapeDtypeStruct((bsz, seq, A_WIDTH), BF16),
        scratch_shapes=[pltpu.VMEM((kv_rows, KV_WIDTH), BF16), pltpu.VMEM((kv_rows, 2 * LANES), F32),
                        pltpu.VMEM((n_bodies, A_GROUP * cq, MXU_DIM), F32),
                        pltpu.VMEM((n_bodies, A_GROUP * cq, MXU_DIM), BF16)],
        compiler_params=_params(),
        name="band_attention",
    )(sinks, q, k, v, k_prev, v_prev)


def _proj_ffn_kernel(n_in, gate, n_tiles, *refs):
    long_in, short_in = refs[:n_in], refs[n_in:2 * n_in]
    refs = refs[2 * n_in:]
    ng_ref = None
    if gate:
        ng_ref, refs = refs[0], refs[1:]
    wo_ref, xl_ref, xs_ref, g1_ref, g2_ref, wg_ref, wu_ref, wd_ref, g3_ref, ol_ref, os_ref = refs
    t = pl.program_id(0)

    def tile(ins, x_ref, o_ref):
        acc = None
        if gate:
            y_ref, z_ref = ins
            for gi in range(SSM_GROUPS):
                gcols = slice(gi * SSM_GROUP_W, (gi + 1) * SSM_GROUP_W)
                gg = y_ref[:, gcols].astype(F32) * z_ref[:, gcols].astype(F32)
                gg = gg * lax.rsqrt(jnp.mean(gg * gg, axis=-1, keepdims=True) + EPS)
                part = _dot((gg * ng_ref[:, gcols]).astype(BF16), wo_ref[gcols, :])
                acc = part if acc is None else acc + part
        else:
            off = 0
            for a_ref in ins:
                kdim = a_ref.shape[-1]
                part = _dot(a_ref[...], wo_ref[off:off + kdim, :])
                acc = part if acc is None else acc + part
                off += kdim

        x = x_ref[...] + _rms(acc, g1_ref[...])
        h = _rms(x, g2_ref[...]).astype(BF16)
        h_half = h * 0.5
        f = None
        off = 0
        for width in FF_CHUNKS:
            fc = slice(off, off + width)
            m = (_silu_of_twice(_dot(h_half, wg_ref[:, fc])) * _dot(h, wu_ref[:, fc])).astype(BF16)
            part = _dot(m, wd_ref[fc, :])
            f = part if f is None else f + part
            off += width
        o_ref[...] = x + _rms(f, g3_ref[...])

    pl.when(t < n_tiles)(lambda: tile(long_in, xl_ref, ol_ref))
    pl.when(t == n_tiles)(lambda: tile(short_in, xs_ref, os_ref))


def _proj_ffn(mix_long, mix_short, ng, wo, x_long, x_short, g1, g2, wg, wu, wd, layer, g3, tm):
    bsz, seq, _ = x_long.shape
    tiles_per_seq = seq // tm
    n_tiles = bsz * tiles_per_seq
    n_short = x_short.shape[0] * x_short.shape[1]
    flat = lambda a: a.reshape(1, n_short, a.shape[-1])
    tile_idx = lambda t: jnp.minimum(t, n_tiles - 1)
    ltok = lambda width: pl.BlockSpec((None, tm, width),
                                      lambda t: (tile_idx(t) // tiles_per_seq, tile_idx(t) % tiles_per_seq, 0))
    stok = lambda width: pl.BlockSpec((None, n_short, width), lambda t: (0, 0, 0))
    gate = ng is not None
    ng_args, ng_specs = ([ng], [_const_spec((1, SSM_INNER))]) if gate else ([], [])
    out_long, out_short = pl.pallas_call(
        functools.partial(_proj_ffn_kernel, len(mix_long), gate, n_tiles),
        grid=(n_tiles + 1,),
        in_specs=[ltok(a.shape[-1]) for a in mix_long] + [stok(a.shape[-1]) for a in mix_short] + ng_specs + [
            _const_spec(wo.shape), ltok(D_MODEL), stok(D_MODEL), _const_spec((1, D_MODEL)), _const_spec((1, D_MODEL)),
            _layer_spec((D_MODEL, D_FF), layer), _layer_spec((D_MODEL, D_FF), layer),
            _layer_spec((D_FF, D_MODEL), layer), _const_spec((1, D_MODEL))],
        out_specs=[ltok(D_MODEL), stok(D_MODEL)],
        out_shape=[jax.ShapeDtypeStruct(x_long.shape, F32), jax.ShapeDtypeStruct((1, n_short, D_MODEL), F32)],
        compiler_params=pltpu.CompilerParams(dimension_semantics=("arbitrary",), vmem_limit_bytes=VMEM_LIMIT),
        name="out_proj_ffn",
    )(*mix_long, *[flat(a) for a in mix_short], *ng_args, wo, x_long, flat(x_short), g1, g2, wg, wu, wd, g3)
    return out_long, out_short.reshape(x_short.shape)


def _split3(v):
    t1 = v.astype(BF16)
    r = v - t1.astype(F32)
    t2 = r.astype(BF16)
    t3 = (r - t2.astype(F32)).astype(BF16)
    return t1, t2, t3


def _pack3(v):
    t1, t2, t3 = _split3(v)
    lane = lax.broadcasted_iota(jnp.int32, v.shape, 1)
    return jnp.where(lane < SSM_HEADS, t1,
                     jnp.where(lane < 2 * SSM_HEADS, t2,
                               jnp.where(lane < 3 * SSM_HEADS, t3, jnp.zeros_like(t3))))


def _ssm_in_kernel(tm, cl, x_ref, g_ref, w_ref, wdt_ref, hist_ref, cw_ref, cb_ref, dtb_ref, alog_ref,
                   z_ref, xs_ref, bm_ref, cm_ref, ta_ref, tdt_ref, nh_ref, carry_ref):
    i = pl.program_id(1)

    @pl.when(i == 0)
    def _():
        carry_ref[:, SUBLANES - (SSM_CONV - 1):, :] = hist_ref[...]

    h = _rms(x_ref[...], g_ref[...]).astype(BF16)

    dt = jax.nn.softplus(_dot(h, wdt_ref[...]) + dtb_ref[...])
    tdt_ref[...] = _pack3(dt)
    d1, d2, d3 = _split3(dt * (-jnp.exp(alog_ref[...]) * LOG2E))
    tcol = lax.broadcasted_iota(jnp.int32, (cl, 3 * cl), 1) % cl
    trow = lax.broadcasted_iota(jnp.int32, (cl, 3 * cl), 0)
    tril = jnp.where(tcol <= trow, 1.0, 0.0).astype(BF16)
    acum = []
    for c in range(tm // cl):
        rows = slice(c * cl, (c + 1) * cl)
        acum.append(_dot(tril, jnp.concatenate([d1[rows], d2[rows], d3[rows]], axis=0)))
    ta_ref[...] = _pack3(jnp.concatenate(acum, axis=0))

    h_half = h * 0.5
    for n in range(SSM_INNER // COL_CHUNK):
        cols = slice(n * COL_CHUNK, (n + 1) * COL_CHUNK)
        z_ref[:, cols] = _silu_of_twice(_dot(h_half, w_ref[:, cols])).astype(BF16)

    for n in range(SSM_CONV_CH // COL_CHUNK):
        cols = slice(n * COL_CHUNK, (n + 1) * COL_CHUNK)
        raw = _dot(h, w_ref[:, SSM_INNER + cols.start:SSM_INNER + cols.stop])
        y = _silu_of_twice(_causal_conv(carry_ref, nh_ref, raw, 0.5 * cw_ref[:, cols], cols) + 0.5 * cb_ref[:, cols])
        if cols.stop <= SSM_INNER:
            xs_ref[:, cols] = y
        elif cols.stop <= SSM_INNER + SSM_GN:
            bm_ref[:, cols.start - SSM_INNER:cols.stop - SSM_INNER] = y.astype(BF16)
        else:
            lo = SSM_INNER + SSM_GN
            cm_ref[:, cols.start - lo:cols.stop - lo] = y.astype(BF16)


def _ssm_in(x, g, w, wdt, hist, cw, cb, dtb, alog, tm, cl):
    bsz, seq, _ = x.shape
    xt, nseq = _tile_rows(x, tm)
    groups, rows, _ = xt.shape
    tok = lambda width: pl.BlockSpec((None, tm, width), lambda b, i: (b, i, 0))
    hist_spec = pl.BlockSpec((nseq, SSM_CONV - 1, SSM_CONV_CH), lambda b, i: (b, 0, 0))
    outs = pl.pallas_call(
        functools.partial(_ssm_in_kernel, tm, cl),
        grid=(groups, rows // tm),
        in_specs=[tok(D_MODEL), _const_spec((1, D_MODEL)), _const_spec(w.shape), _const_spec((D_MODEL, LANES)),
                  hist_spec,
                  _const_spec((SSM_CONV, SSM_CONV_CH)), _const_spec((1, SSM_CONV_CH)), _const_spec((1, LANES)),
                  _const_spec((1, LANES))],
        out_specs=[tok(SSM_INNER), tok(SSM_INNER), tok(SSM_GN), tok(SSM_GN), tok(LANES), tok(LANES), hist_spec],
        out_shape=[jax.ShapeDtypeStruct((groups, rows, SSM_INNER), BF16),
                   jax.ShapeDtypeStruct((groups, rows, SSM_INNER), F32),
                   jax.ShapeDtypeStruct((groups, rows, SSM_GN), BF16),
                   jax.ShapeDtypeStruct((groups, rows, SSM_GN), BF16),
                   jax.ShapeDtypeStruct((groups, rows, LANES), BF16),
                   jax.ShapeDtypeStruct((groups, rows, LANES), BF16),
                   jax.ShapeDtypeStruct((bsz, SSM_CONV - 1, SSM_CONV_CH), F32)],
        scratch_shapes=[pltpu.VMEM((nseq, SUBLANES, SSM_CONV_CH), F32)],
        compiler_params=_params(),
        name="ssm_in_proj",
    )(xt, g, w, wdt, hist, cw, cb, dtb, alog)
    return [o.reshape(bsz, seq, o.shape[-1]) for o in outs[:-1]] + [outs[-1]]


def _pad_rows(a, rows):
    if a.shape[0] == rows:
        return a
    return jnp.concatenate([a, jnp.zeros((rows - a.shape[0], a.shape[1]), a.dtype)], axis=0)


def _ssd_kernel(ts, cl, xs_ref, bm_ref, cm_ref, ta_ref, tdt_ref, h0_ref, dskip_ref,
                yo_ref, hout_ref, ht_ref, y_ref, e3_ref, aexp_ref, xdt_ref, st_ref, cb_ref, bmt_ref, w2_ref, rhs_ref):
    b = pl.program_id(0)
    i = pl.program_id(1)
    p = SSM_HEADDIM
    nc = ts // cl
    pairs = SSM_GROUP_W // LANES

    @pl.when((b == 0) & (i == 0))
    def _():
        kk = lax.broadcasted_iota(jnp.int32, (LANES, SSM_INNER), 0)
        cc = lax.broadcasted_iota(jnp.int32, (LANES, SSM_INNER), 1)
        hit = (kk % SSM_HEADS == cc // p) & (kk < 3 * SSM_HEADS)
        e3_ref[...] = jnp.where(hit, 1.0, 0.0).astype(BF16)

    @pl.when(i == 0)
    def _():
        ht_ref[...] = h0_ref[...].T

    lane = lax.broadcasted_iota(jnp.int32, (cl, LANES), 1)
    row = lax.broadcasted_iota(jnp.int32, (cl, LANES), 0)
    causal2 = (lane % p) <= row
    left = lane.astype(F32).astype(BF16) < p
    diag2 = (lane % p) == row
    groups = [(gi, slice(gi * SSM_GROUP_W, (gi + 1) * SSM_GROUP_W), slice(gi * SSM_STATE, (gi + 1) * SSM_STATE))
              for gi in range(SSM_GROUPS)]
    chunks = [(c, slice(c * cl, (c + 1) * cl)) for c in range(nc)]

    for gi, gcols, _ in groups:
        e3g = e3_ref[:, gcols]
        aexp_ref[:, gcols] = _dot(ta_ref[...], e3g)
        xdt_ref[:, gcols] = (xs_ref[:, gcols] * _dot(tdt_ref[...], e3g)).astype(BF16)

    for c, rows in chunks:
        for gi, _, scol in groups:
            bmp = _pad_rows(bm_ref[rows, scol], p)
            cbm = _dot_nt(cm_ref[rows, scol], bmp)
            cb_ref[c * SSM_GROUPS + gi] = jnp.concatenate([cbm, cbm], axis=1)
            bmt_ref[c * SSM_GROUPS + gi] = bmp.astype(F32).T.astype(BF16)

    for c, rows in chunks:
        for gi, _, _ in groups:
            for jj in range(pairs):
                idx = (c * SSM_GROUPS + gi) * pairs + jj
                cols = slice(gi * SSM_GROUP_W + jj * LANES, gi * SSM_GROUP_W + (jj + 1) * LANES)
                a_pair = aexp_ref[rows, cols]
                a_src = jnp.sum(jnp.where(diag2, a_pair, 0.0), axis=0, keepdims=True)
                seg = a_pair - a_src
                dec = jnp.exp2(jnp.where(causal2, seg, -jnp.inf))
                w2_ref[idx] = (dec * cb_ref[c * SSM_GROUPS + gi]).astype(BF16)
                x2 = xdt_ref[rows, cols]
                top = _pad_rows(jnp.where(left, x2, jnp.zeros_like(x2)), p)
                bot = _pad_rows(jnp.where(left, jnp.zeros_like(x2), x2), p)
                rhs_ref[idx] = jnp.concatenate([top, bot], axis=0)

    for c, rows in chunks:
        for gi, _, _ in groups:
            for jj in range(pairs):
                idx = (c * SSM_GROUPS + gi) * pairs + jj
                cols = slice(gi * SSM_GROUP_W + jj * LANES, gi * SSM_GROUP_W + (jj + 1) * LANES)
                y_ref[rows, cols] = _dot(w2_ref[idx], rhs_ref[idx])

    for c, rows in chunks:
        for gi, gcols, _ in groups:
            a_g = aexp_ref[rows, gcols]
            xw = xdt_ref[rows, gcols] * jnp.exp2(a_g[cl - 1:cl, :] - a_g).astype(BF16)
            st_ref[c, :, gcols] = _dot(bmt_ref[c * SSM_GROUPS + gi], _pad_rows(xw, p))

    for c, rows in chunks:
        for gi, gcols, scol in groups:
            hprev = ht_ref[:, gcols]
            a_g = aexp_ref[rows, gcols]
            y_off = _dot(cm_ref[rows, scol], hprev.astype(BF16)) * jnp.exp2(a_g)
            y_ref[rows, gcols] = y_ref[rows, gcols] + y_off
            ht_ref[:, gcols] = hprev * jnp.exp2(a_g[cl - 1:cl, :]) + st_ref[c, :, gcols]

    yo_ref[...] = (y_ref[...] + dskip_ref[...] * xs_ref[...]).astype(BF16)

    @pl.when(i == pl.num_programs(1) - 1)
    def _():
        hout_ref[...] = ht_ref[...].T


def _ssd(xs, bm, cm, ta, tdt, h0, dskip, ts, cl):
    bsz, seq, _ = xs.shape
    tok = lambda width: pl.BlockSpec((None, ts, width), lambda b, i: (b, i, 0))
    st_spec = pl.BlockSpec((None, SSM_INNER, SSM_STATE), lambda b, i: (b, 0, 0))
    nc = ts // cl
    n_cg = nc * SSM_GROUPS
    n_pairs = n_cg * (SSM_GROUP_W // LANES)
    return pl.pallas_call(
        functools.partial(_ssd_kernel, ts, cl),
        grid=(bsz, seq // ts),
        in_specs=[tok(SSM_INNER), tok(SSM_GN), tok(SSM_GN), tok(LANES), tok(LANES), st_spec,
                  _const_spec((1, SSM_INNER))],
        out_specs=[tok(SSM_INNER), st_spec],
        out_shape=[jax.ShapeDtypeStruct((bsz, seq, SSM_INNER), BF16),
                   jax.ShapeDtypeStruct((bsz, SSM_INNER, SSM_STATE), F32)],
        scratch_shapes=[pltpu.VMEM((SSM_STATE, SSM_INNER), F32), pltpu.VMEM((ts, SSM_INNER), F32),
                        pltpu.VMEM((LANES, SSM_INNER), BF16), pltpu.VMEM((ts, SSM_INNER), F32),
                        pltpu.VMEM((ts, SSM_INNER), BF16), pltpu.VMEM((nc, SSM_STATE, SSM_INNER), F32),
                        pltpu.VMEM((n_cg, cl, LANES), F32), pltpu.VMEM((n_cg, SSM_STATE, SSM_HEADDIM), BF16),
                        pltpu.VMEM((n_pairs, cl, LANES), BF16), pltpu.VMEM((n_pairs, LANES, LANES), BF16)],
        compiler_params=_params(),
        name="ssd_scan",
    )(xs, bm, cm, ta, tdt, h0, dskip)


def _layer0_mixers(x, pos0, sconv_hist, k_cache, v_cache, wts, tiles):
    t_in, t_att, c_att = tiles
    q, k, v, gated, new_sconv = _attn_in(x, wts["g"][0][0], wts["ab_w_in"], sconv_hist, wts["sconv_w"], pos0, t_in)
    masked = k_cache is None
    k_prev, v_prev = (k, v) if masked else (k_cache, v_cache)
    att = _attn(wts["sinks"], q, k, v, k_prev, v_prev, t_att, c_att, masked)
    bsz, seq, _ = x.shape
    new_k = k[:, seq - min(WINDOW, seq):].reshape(bsz, -1, A_KV_HEADS, HEAD_DIM)
    new_v = v[:, seq - min(WINDOW, seq):].reshape(bsz, -1, A_KV_HEADS, HEAD_DIM)
    return [att, gated], (new_k, new_v, new_sconv)


def _layer1_mixer(x, conv_hist, ssm_state, wts, tiles):
    t_sin, t_ssd, c_ssd = tiles
    bsz = x.shape[0]
    z, xs, bm, cm, ta, tdt, new_conv = _ssm_in(
        x, wts["g"][1][0], wts["ssm_w_in"], wts["w_dt"], conv_hist, wts["ssm_conv_w"],
        wts["ssm_conv_b"], wts["dt_bias"], wts["a_log"], t_sin, c_ssd)
    y, new_state = _ssd(xs, bm, cm, ta, tdt, ssm_state.reshape(bsz, SSM_INNER, SSM_STATE),
                        wts["d_skip"], t_ssd, c_ssd)
    return [y, z], (new_conv, new_state.reshape(bsz, SSM_HEADS, SSM_HEADDIM, SSM_STATE))


def kernel(x_prompt, x_sample, cache_attn_k, cache_attn_v, state_sconv, state_ssm_conv, state_ssm, norm_g,
           ab_w_in, ab_w_out, attn_sinks, sconv_w, ssm_w_in, ssm_conv_w, ssm_conv_b, ssm_dt_bias, ssm_a_log,
           ssm_d, ssm_norm_g, ssm_w_out, ffn_w_gate, ffn_w_up, ffn_w_down):
    w_in1 = ssm_w_in[0]
    wts = {
        "ab_w_in": ab_w_in[0].astype(BF16),
        "ab_w_out": ab_w_out[0].astype(BF16),
        "sinks": attn_sinks[0],
        "sconv_w": sconv_w[0],
        "ssm_w_in": w_in1.astype(BF16),
        "w_dt": jnp.tile(w_in1[:, SSM_INNER + SSM_CONV_CH:], (1, DT_REP)).astype(BF16),
        "ssm_conv_w": ssm_conv_w[0],
        "ssm_conv_b": ssm_conv_b[0].reshape(1, -1),
        "dt_bias": jnp.tile(ssm_dt_bias[0], DT_REP).reshape(1, -1),
        "a_log": jnp.tile(ssm_a_log[0], DT_REP).reshape(1, -1),
        "d_skip": jnp.repeat(ssm_d[0], SSM_HEADDIM).reshape(1, -1),
        "ssm_norm_g": ssm_norm_g[0].reshape(1, -1),
        "ssm_w_out": ssm_w_out[0].astype(BF16),
        "wg": ffn_w_gate.astype(BF16),
        "wu": ffn_w_up.astype(BF16),
        "wd": ffn_w_down.astype(BF16),
    }
    wts["g"] = [[norm_g[l, j].reshape(1, -1) for j in range(4)] for l in range(norm_g.shape[0])]
    g = wts["g"]
    bp = x_prompt.shape[0]
    bs, ls = x_sample.shape[0], x_sample.shape[1]
    t_ffn = 512

    mix_p, (kp, vp, scp) = _layer0_mixers(x_prompt, 0, jnp.zeros((bp, B_CONV - 1, B_WIDTH), F32), None, None,
                                          wts, (1024, 1024, CHUNK))
    mix_s, (ks, vs, scs) = _layer0_mixers(x_sample, PAST_LEN, state_sconv[0],
                                          cache_attn_k[0].reshape(bs, -1, KV_WIDTH),
                                          cache_attn_v[0].reshape(bs, -1, KV_WIDTH), wts, (bs * ls, ls, ls))
    xp, xs = _proj_ffn(mix_p, mix_s, None, wts["ab_w_out"], x_prompt, x_sample, g[0][1], g[0][2],
                       wts["wg"], wts["wu"], wts["wd"], 0, g[0][3], t_ffn)

    mix_p, (ccp, ssp) = _layer1_mixer(xp, jnp.zeros((bp, SSM_CONV - 1, SSM_CONV_CH), F32),
                                      jnp.zeros((bp, SSM_HEADS, SSM_HEADDIM, SSM_STATE), F32), wts, (512, 512, CHUNK))
    mix_s, (ccs, sss) = _layer1_mixer(xs, state_ssm_conv[0], state_ssm[0], wts, (bs * ls, ls, ls))
    yp, ys = _proj_ffn(mix_p, mix_s, wts["ssm_norm_g"], wts["ssm_w_out"], xp, xs, g[1][1], g[1][2],
                       wts["wg"], wts["wu"], wts["wd"], 1, g[1][3], t_ffn)

    lead = lambda a: a[None]
    return (yp, ys, lead(kp), lead(vp), lead(scp), lead(ccp), lead(ssp),
            lead(ks), lead(vs), lead(scs), lead(ccs), lead(sss))
```

```python
import functools

import jax
import jax.numpy as jnp
from jax import lax
from jax.experimental import pallas as pl
from jax.experimental.pallas import tpu as pltpu

F32 = jnp.float32
BF16 = jnp.bfloat16

D_MODEL = 1024
CHUNK = 64
EPS = 1e-6
LOG2E = 1.4426950408889634
PAST_LEN = 4096

A_HEADS = 8
A_KV_HEADS = 2
A_GROUP = A_HEADS // A_KV_HEADS
HEAD_DIM = 64
A_WIDTH = A_HEADS * HEAD_DIM
KV_WIDTH = A_KV_HEADS * HEAD_DIM
WINDOW = 128
ROPE_DIM = HEAD_DIM // 4
ROPE_HALF = ROPE_DIM // 2
ROPE_THETA = 500000.0
ATTN_SCALE = HEAD_DIM ** -0.5

B_WIDTH = D_MODEL // 2
B_CONV = 3
AB_IN = A_WIDTH + 2 * KV_WIDTH + 3 * B_WIDTH

SSM_INNER = 2 * D_MODEL
SSM_HEADDIM = 64
SSM_HEADS = SSM_INNER // SSM_HEADDIM
SSM_GROUPS = 4
SSM_GROUP_W = SSM_INNER // SSM_GROUPS
SSM_STATE = 128
SSM_CONV = 4
SSM_GN = SSM_GROUPS * SSM_STATE
SSM_CONV_CH = SSM_INNER + 2 * SSM_GN

D_FF = -(-8 * D_MODEL // (3 * 256)) * 256

LANES = 128
SUBLANES = 8
MXU_DIM = 256
DT_REP = LANES // SSM_HEADS
VMEM_LIMIT = 56 * 1024 * 1024
COL_CHUNK = MXU_DIM
FF_CHUNKS = (6 * MXU_DIM, 5 * MXU_DIM)
assert sum(FF_CHUNKS) == D_FF


def _rms(x, g):
    return x * lax.rsqrt(jnp.mean(x * x, axis=-1, keepdims=True) + EPS) * g


def _dot(a, b):
    return jnp.dot(a, b, preferred_element_type=F32)


def _dot_nt(a, b):
    return lax.dot_general(a, b, (((1,), (1,)), ((), ())), preferred_element_type=F32)


def _silu_of_twice(half):
    return half + half * jnp.tanh(half)


def _const_spec(shape):
    return pl.BlockSpec(shape, lambda *_: (0,) * len(shape), pipeline_mode=pl.Buffered(1))


def _layer_spec(shape, layer):
    return pl.BlockSpec((None,) + shape, lambda *_: (layer,) + (0,) * len(shape), pipeline_mode=pl.Buffered(1))


def _params():
    return pltpu.CompilerParams(dimension_semantics=("arbitrary", "arbitrary"),
                                vmem_limit_bytes=VMEM_LIMIT)


def _causal_conv(carry_ref, nh_ref, u, cw, cols):
    width = cw.shape[0]
    rows, ch = u.shape
    nseq = carry_ref.shape[0]
    u4 = u.reshape(nseq, rows // (nseq * SUBLANES), SUBLANES, ch)
    ext4 = jnp.concatenate([carry_ref[:, :, cols].reshape(nseq, 1, SUBLANES, ch), u4], axis=1)
    sub = lax.broadcasted_iota(jnp.int32, (1, 1, SUBLANES, ch), 2)

    def shift(above, here, k):
        return pltpu.roll(jnp.where(sub >= SUBLANES - k, above, here), k, 2)

    if width == 4:
        s1 = shift(jnp.concatenate([ext4[:, :1], ext4[:, :-1]], axis=1), ext4, 1)
        b = s1 * cw[0:1] + ext4 * cw[1:2]
        y4 = shift(b[:, :-1], b[:, 1:], 2) + (s1[:, 1:] * cw[2:3] + u4 * cw[3:4])
    else:
        y4 = None
        for j in range(width - 1):
            term = shift(ext4[:, :-1], u4, width - 1 - j) * cw[j:j + 1]
            y4 = term if y4 is None else y4 + term
        y4 = y4 + u4 * cw[width - 1:width]
    carry_ref[:, :, cols] = u4[:, -1]
    nh_ref[:, :, cols] = u4[:, -1, SUBLANES - (width - 1):]
    return y4.reshape(rows, ch)


def _rope_freq(shape):
    lane = lax.broadcasted_iota(jnp.int32, shape, 1)
    d = lane % HEAD_DIM
    expo = -((d % ROPE_HALF).astype(F32)) / ROPE_HALF
    freq = jnp.power(jnp.full(shape, ROPE_THETA, F32), expo)
    return jnp.where(d < ROPE_DIM, freq, 0.0), d


def _attn_in_kernel(pos0, tm, seq_rows, x_ref, g_ref, w_ref, hist_ref, cw_ref,
                    q_ref, k_ref, v_ref, gated_ref, nh_ref,
                    cr_ref, sr_ref, carry_ref, p_ref):
    b = pl.program_id(0)
    i = pl.program_id(1)

    @pl.when((b == 0) & (i == 0))
    def _():
        freq, _ = _rope_freq((tm, LANES))
        ang = (lax.broadcasted_iota(jnp.int32, (tm, LANES), 0) % seq_rows).astype(F32) * freq
        cr_ref[...] = jnp.cos(ang)
        sr_ref[...] = jnp.sin(ang)

    @pl.when(i == 0)
    def _():
        carry_ref[:, SUBLANES - (B_CONV - 1):, :] = hist_ref[...]

    freq1, d1 = _rope_freq((1, LANES))
    base = (pos0 + i * tm).astype(F32) * freq1
    cb = jnp.cos(base)
    sb = jnp.sin(base)
    cr = cr_ref[...]
    sr = sr_ref[...]
    cos_t = cr * cb - sr * sb
    sin_t = sr * cb + cr * sb
    m_lo = jnp.where(d1 < ROPE_HALF, -1.0, 0.0)
    m_hi = jnp.where((d1 >= ROPE_HALF) & (d1 < ROPE_DIM), 1.0, 0.0)

    def rope(t):
        partner = pltpu.roll(t, LANES - ROPE_HALF, 1) * m_lo + pltpu.roll(t, ROPE_HALF, 1) * m_hi
        return t * cos_t + partner * sin_t

    h = _rms(x_ref[...], g_ref[...]).astype(BF16)
    o1 = A_WIDTH
    o2 = o1 + KV_WIDTH
    o3 = o2 + KV_WIDTH
    o4 = o3 + B_WIDTH
    o5 = o4 + B_WIDTH

    for n in range(AB_IN // MXU_DIM):
        cols = slice(n * MXU_DIM, (n + 1) * MXU_DIM)
        p_ref[:, cols] = _dot(h, w_ref[:, cols])

    u = p_ref[:, o4:o5] * p_ref[:, o5:]
    y = _causal_conv(carry_ref, nh_ref, u, cw_ref[...], slice(None))
    gated_ref[...] = (p_ref[:, o3:o4] * y).astype(BF16)
    k_ref[...] = rope(p_ref[:, o1:o2])
    v_ref[...] = p_ref[:, o2:o3]
    for s in range(A_WIDTH // LANES):
        q_ref[:, s * LANES:(s + 1) * LANES] = (rope(p_ref[:, s * LANES:(s + 1) * LANES]) * ATTN_SCALE).astype(BF16)


def _tile_rows(x, tm):
    bsz, seq, d = x.shape
    nseq = max(tm // seq, 1)
    return x.reshape(bsz // nseq, nseq * seq, d), nseq


def _attn_in(x, g, w, hist, cw, pos0, tm):
    bsz, seq, _ = x.shape
    xt, nseq = _tile_rows(x, tm)
    groups, rows, _ = xt.shape
    tok = lambda width: pl.BlockSpec((None, tm, width), lambda b, i: (b, i, 0))
    hist_spec = pl.BlockSpec((nseq, B_CONV - 1, B_WIDTH), lambda b, i: (b, 0, 0))
    outs = pl.pallas_call(
        functools.partial(_attn_in_kernel, pos0, tm, tm // nseq),
        grid=(groups, rows // tm),
        in_specs=[tok(D_MODEL), _const_spec((1, D_MODEL)), _const_spec((D_MODEL, AB_IN)), hist_spec,
                  _const_spec((B_CONV, B_WIDTH))],
        out_specs=[tok(A_WIDTH), tok(KV_WIDTH), tok(KV_WIDTH), tok(B_WIDTH), hist_spec],
        out_shape=[jax.ShapeDtypeStruct((groups, rows, A_WIDTH), BF16),
                   jax.ShapeDtypeStruct((groups, rows, KV_WIDTH), F32),
                   jax.ShapeDtypeStruct((groups, rows, KV_WIDTH), F32),
                   jax.ShapeDtypeStruct((groups, rows, B_WIDTH), BF16),
                   jax.ShapeDtypeStruct((bsz, B_CONV - 1, B_WIDTH), F32)],
        scratch_shapes=[pltpu.VMEM((tm, LANES), F32), pltpu.VMEM((tm, LANES), F32),
                        pltpu.VMEM((nseq, SUBLANES, B_WIDTH), F32), pltpu.VMEM((tm, AB_IN), F32)],
        compiler_params=_params(),
        name="attn_in_proj",
    )(xt, g, w, hist, cw)
    return [o.reshape(bsz, seq, o.shape[-1]) for o in outs[:-1]] + [outs[-1]]


def _attn_kernel(tq, cq, masked, sinks_ref, q_ref, kc_ref, vc_ref, kp_ref, vp_ref, att_ref,
                 kk_ref, vx_ref, s_ref, e_ref):
    i = pl.program_id(1)
    nk = WINDOW + cq
    nkp = MXU_DIM
    rows_kv = WINDOW + tq
    hd = HEAD_DIM

    kk_ref[0:WINDOW, :] = kp_ref[...].astype(BF16)
    kk_ref[WINDOW:rows_kv, :] = kc_ref[...].astype(BF16)
    kk_ref[rows_kv:, :] = jnp.zeros((kk_ref.shape[0] - rows_kv, KV_WIDTH), BF16)
    vall = jnp.concatenate([vp_ref[...], vc_ref[...]], axis=0)
    vx_ref[0:rows_kv, :] = jnp.concatenate([vall[:, :hd], vall[:, :hd], vall[:, hd:], vall[:, hd:]], axis=1)
    vx_ref[rows_kv:, :] = jnp.zeros((vx_ref.shape[0] - rows_kv, 2 * LANES), F32)

    col = lax.broadcasted_iota(jnp.int32, (A_GROUP * cq, nkp), 1)
    krow = lax.broadcasted_iota(jnp.int32, (nkp, LANES), 0)
    ones = jnp.ones((nkp, LANES), BF16)
    bodies = [(c, kh) for c in range(tq // cq) for kh in range(A_KV_HEADS)]
    fills = []
    for kh in range(A_KV_HEADS):
        sink = jnp.concatenate(
            [jnp.full((cq, nkp), sinks_ref[kh * A_GROUP + g], F32) for g in range(A_GROUP)], axis=0)
        fills.append(jnp.where(col == nk, sink, -jnp.inf))

    for n, (c, kh) in enumerate(bodies):
        r0 = c * cq
        heads = [kh * A_GROUP + g for g in range(A_GROUP)]
        kt = kk_ref[r0:r0 + nkp, kh * hd:(kh + 1) * hd]
        qs = jnp.concatenate([q_ref[r0:r0 + cq, hh * hd:(hh + 1) * hd] for hh in heads], axis=0)
        s = _dot_nt(qs, kt)
        valid = col < nk
        if masked and r0 < WINDOW:
            valid = valid & ((col >= WINDOW - r0) | (i > 0))
        s_ref[n] = jnp.where(valid, s, fills[kh])

    for n in range(len(bodies)):
        s = s_ref[n]
        m = jnp.max(s, axis=-1, keepdims=True)
        e_ref[n] = jnp.exp(s - m).astype(BF16)

    for n, (c, kh) in enumerate(bodies):
        r0 = c * cq
        vt = jnp.where(krow == nk, 0.0, vx_ref[r0:r0 + nkp, kh * LANES:(kh + 1) * LANES]).astype(BF16)
        ox = _dot(e_ref[n], jnp.concatenate([vt, ones], axis=1))
        o = ox[:, :LANES] / ox[:, LANES:]
        for g in range(A_GROUP):
            hh = kh * A_GROUP + g
            lanes = slice((hh % 2) * hd, (hh % 2 + 1) * hd)
            att_ref[r0:r0 + cq, hh * hd:(hh + 1) * hd] = o[g * cq:(g + 1) * cq, lanes].astype(BF16)


def _attn(sinks, q, k, v, k_prev, v_prev, tq, cq, masked):
    bsz, seq, _ = q.shape
    grid = (bsz, seq // tq)
    tok = lambda width: pl.BlockSpec((None, tq, width), lambda b, i: (b, i, 0))
    if masked:
        blocks_per_tile = tq // WINDOW
        prev = pl.BlockSpec((None, WINDOW, KV_WIDTH),
                            lambda b, i: (b, jnp.maximum(i * blocks_per_tile - 1, 0), 0))
    else:
        prev = pl.BlockSpec((None, WINDOW, KV_WIDTH), lambda b, i: (b, 0, 0))
    n_bodies = (tq // cq) * A_KV_HEADS
    kv_rows = (tq // cq - 1) * cq + MXU_DIM
    return pl.pallas_call(
        functools.partial(_attn_kernel, tq, cq, masked),
        grid=grid,
        in_specs=[pl.BlockSpec(memory_space=pltpu.SMEM),
                  tok(A_WIDTH), tok(KV_WIDTH), tok(KV_WIDTH), prev, prev],
        out_specs=tok(A_WIDTH),
        out_shape=jax.ShapeDtypeStruct((bsz, seq, A_WIDTH), BF16),
        scratch_shapes=[pltpu.VMEM((kv_rows, KV_WIDTH), BF16), pltpu.VMEM((kv_rows, 2 * LANES), F32),
                        pltpu.VMEM((n_bodies, A_GROUP * cq, MXU_DIM), F32),
                        pltpu.VMEM((n_bodies, A_GROUP * cq, MXU_DIM), BF16)],
        compiler_params=_params(),
        name="band_attention",
    )(sinks, q, k, v, k_prev, v_prev)


def _proj_ffn_kernel(n_in, gate, n_tiles, *refs):
    long_in, short_in = refs[:n_in], refs[n_in:2 * n_in]
    refs = refs[2 * n_in:]
    ng_ref = None
    if gate:
        ng_ref, refs = refs[0], refs[1:]
    wo_ref, xl_ref, xs_ref, g1_ref, g2_ref, wg_ref, wu_ref, wd_ref, g3_ref, ol_ref, os_ref = refs
    t = pl.program_id(0)

    def tile(ins, x_ref, o_ref):
        acc = None
        if gate:
            y_ref, z_ref = ins
            for gi in range(SSM_GROUPS):
                gcols = slice(gi * SSM_GROUP_W, (gi + 1) * SSM_GROUP_W)
                gg = y_ref[:, gcols].astype(F32) * z_ref[:, gcols].astype(F32)
                gg = gg * lax.rsqrt(jnp.mean(gg * gg, axis=-1, keepdims=True) + EPS)
                part = _dot((gg * ng_ref[:, gcols]).astype(BF16), wo_ref[gcols, :])
                acc = part if acc is None else acc + part
        else:
            off = 0
            for a_ref in ins:
                kdim = a_ref.shape[-1]
                part = _dot(a_ref[...], wo_ref[off:off + kdim, :])
                acc = part if acc is None else acc + part
                off += kdim

        x = x_ref[...] + _rms(acc, g1_ref[...])
        h = _rms(x, g2_ref[...]).astype(BF16)
        h_half = h * 0.5
        f = None
        off = 0
        for width in FF_CHUNKS:
            fc = slice(off, off + width)
            m = (_silu_of_twice(_dot(h_half, wg_ref[:, fc])) * _dot(h, wu_ref[:, fc])).astype(BF16)
            part = _dot(m, wd_ref[fc, :])
            f = part if f is None else f + part
            off += width
        o_ref[...] = x + _rms(f, g3_ref[...])

    pl.when(t < n_tiles)(lambda: tile(long_in, xl_ref, ol_ref))
    pl.when(t == n_tiles)(lambda: tile(short_in, xs_ref, os_ref))


def _proj_ffn(mix_long, mix_short, ng, wo, x_long, x_short, g1, g2, wg, wu, wd, layer, g3, tm):
    bsz, seq, _ = x_long.shape
    tiles_per_seq = seq // tm
    n_tiles = bsz * tiles_per_seq
    n_short = x_short.shape[0] * x_short.shape[1]
    flat = lambda a: a.reshape(1, n_short, a.shape[-1])
    tile_idx = lambda t: jnp.minimum(t, n_tiles - 1)
    ltok = lambda width: pl.BlockSpec((None, tm, width),
                                      lambda t: (tile_idx(t) // tiles_per_seq, tile_idx(t) % tiles_per_seq, 0))
    stok = lambda width: pl.BlockSpec((None, n_short, width), lambda t: (0, 0, 0))
    gate = ng is not None
    ng_args, ng_specs = ([ng], [_const_spec((1, SSM_INNER))]) if gate else ([], [])
    out_long, out_short = pl.pallas_call(
        functools.partial(_proj_ffn_kernel, len(mix_long), gate, n_tiles),
        grid=(n_tiles + 1,),
        in_specs=[ltok(a.shape[-1]) for a in mix_long] + [stok(a.shape[-1]) for a in mix_short] + ng_specs + [
            _const_spec(wo.shape), ltok(D_MODEL), stok(D_MODEL), _const_spec((1, D_MODEL)), _const_spec((1, D_MODEL)),
            _layer_spec((D_MODEL, D_FF), layer), _layer_spec((D_MODEL, D_FF), layer),
            _layer_spec((D_FF, D_MODEL), layer), _const_spec((1, D_MODEL))],
        out_specs=[ltok(D_MODEL), stok(D_MODEL)],
        out_shape=[jax.ShapeDtypeStruct(x_long.shape, F32), jax.ShapeDtypeStruct((1, n_short, D_MODEL), F32)],
        compiler_params=pltpu.CompilerParams(dimension_semantics=("arbitrary",), vmem_limit_bytes=VMEM_LIMIT),
        name="out_proj_ffn",
    )(*mix_long, *[flat(a) for a in mix_short], *ng_args, wo, x_long, flat(x_short), g1, g2, wg, wu, wd, g3)
    return out_long, out_short.reshape(x_short.shape)


def _split3(v):
    t1 = v.astype(BF16)
    r = v - t1.astype(F32)
    t2 = r.astype(BF16)
    t3 = (r - t2.astype(F32)).astype(BF16)
    return t1, t2, t3


def _pack3(v):
    t1, t2, t3 = _split3(v)
    lane = lax.broadcasted_iota(jnp.int32, v.shape, 1)
    return jnp.where(lane < SSM_HEADS, t1,
                     jnp.where(lane < 2 * SSM_HEADS, t2,
                               jnp.where(lane < 3 * SSM_HEADS, t3, jnp.zeros_like(t3))))


def _ssm_in_kernel(tm, cl, x_ref, g_ref, w_ref, wdt_ref, hist_ref, cw_ref, cb_ref, dtb_ref, alog_ref,
                   z_ref, xs_ref, bc_ref, dec_ref, nh_ref, carry_ref):
    i = pl.program_id(1)

    @pl.when(i == 0)
    def _():
        carry_ref[:, SUBLANES - (SSM_CONV - 1):, :] = hist_ref[...]

    h = _rms(x_ref[...], g_ref[...]).astype(BF16)

    dt = jax.nn.softplus(_dot(h, wdt_ref[...]) + dtb_ref[...])
    dec_ref[:, LANES:] = _pack3(dt)
    d1, d2, d3 = _split3(dt * (-jnp.exp(alog_ref[...]) * LOG2E))
    tcol = lax.broadcasted_iota(jnp.int32, (cl, 3 * cl), 1) % cl
    trow = lax.broadcasted_iota(jnp.int32, (cl, 3 * cl), 0)
    tril = jnp.where(tcol <= trow, 1.0, 0.0).astype(BF16)
    acum = []
    for c in range(tm // cl):
        rows = slice(c * cl, (c + 1) * cl)
        acum.append(_dot(tril, jnp.concatenate([d1[rows], d2[rows], d3[rows]], axis=0)))
    dec_ref[:, :LANES] = _pack3(jnp.concatenate(acum, axis=0))

    h_half = h * 0.5
    for n in range(SSM_INNER // COL_CHUNK):
        cols = slice(n * COL_CHUNK, (n + 1) * COL_CHUNK)
        z_ref[:, cols] = _silu_of_twice(_dot(h_half, w_ref[:, cols])).astype(BF16)

    for n in range(SSM_CONV_CH // COL_CHUNK):
        cols = slice(n * COL_CHUNK, (n + 1) * COL_CHUNK)
        raw = _dot(h, w_ref[:, SSM_INNER + cols.start:SSM_INNER + cols.stop])
        y = _silu_of_twice(_causal_conv(carry_ref, nh_ref, raw, 0.5 * cw_ref[:, cols], cols) + 0.5 * cb_ref[:, cols])
        if cols.stop <= SSM_INNER:
            xs_ref[:, cols] = y
        else:
            bc_ref[:, cols.start - SSM_INNER:cols.stop - SSM_INNER] = y.astype(BF16)


def _ssm_in(x, g, w, wdt, hist, cw, cb, dtb, alog, tm, cl):
    bsz, seq, _ = x.shape
    xt, nseq = _tile_rows(x, tm)
    groups, rows, _ = xt.shape
    tok = lambda width: pl.BlockSpec((None, tm, width), lambda b, i: (b, i, 0))
    hist_spec = pl.BlockSpec((nseq, SSM_CONV - 1, SSM_CONV_CH), lambda b, i: (b, 0, 0))
    outs = pl.pallas_call(
        functools.partial(_ssm_in_kernel, tm, cl),
        grid=(groups, rows // tm),
        in_specs=[tok(D_MODEL), _const_spec((1, D_MODEL)), _const_spec(w.shape), _const_spec((D_MODEL, LANES)),
                  hist_spec,
                  _const_spec((SSM_CONV, SSM_CONV_CH)), _const_spec((1, SSM_CONV_CH)), _const_spec((1, LANES)),
                  _const_spec((1, LANES))],
        out_specs=[tok(SSM_INNER), tok(SSM_INNER), tok(2 * SSM_GN), tok(2 * LANES), hist_spec],
        out_shape=[jax.ShapeDtypeStruct((groups, rows, SSM_INNER), BF16),
                   jax.ShapeDtypeStruct((groups, rows, SSM_INNER), F32),
                   jax.ShapeDtypeStruct((groups, rows, 2 * SSM_GN), BF16),
                   jax.ShapeDtypeStruct((groups, rows, 2 * LANES), BF16),
                   jax.ShapeDtypeStruct((bsz, SSM_CONV - 1, SSM_CONV_CH), F32)],
        scratch_shapes=[pltpu.VMEM((nseq, SUBLANES, SSM_CONV_CH), F32)],
        compiler_params=_params(),
        name="ssm_in_proj",
    )(xt, g, w, wdt, hist, cw, cb, dtb, alog)
    return [o.reshape(bsz, seq, o.shape[-1]) for o in outs[:-1]] + [outs[-1]]


def _pad_rows(a, rows):
    if a.shape[0] == rows:
        return a
    return jnp.concatenate([a, jnp.zeros((rows - a.shape[0], a.shape[1]), a.dtype)], axis=0)


def _ssd_kernel(ts, cl, xs_ref, bc_ref, dec_ref, h0_ref, dskip_ref,
                yo_ref, hout_ref, ht_ref, y_ref, e3_ref, aexp_ref, xdt_ref, st_ref, cb_ref, bmt_ref, w2_ref, rhs_ref):
    b = pl.program_id(0)
    i = pl.program_id(1)
    p = SSM_HEADDIM
    nc = ts // cl
    pairs = SSM_GROUP_W // LANES

    @pl.when((b == 0) & (i == 0))
    def _():
        kk = lax.broadcasted_iota(jnp.int32, (LANES, SSM_INNER), 0)
        cc = lax.broadcasted_iota(jnp.int32, (LANES, SSM_INNER), 1)
        hit = (kk % SSM_HEADS == cc // p) & (kk < 3 * SSM_HEADS)
        e3_ref[...] = jnp.where(hit, 1.0, 0.0).astype(BF16)

    @pl.when(i == 0)
    def _():
        ht_ref[...] = h0_ref[...].T

    lane = lax.broadcasted_iota(jnp.int32, (cl, LANES), 1)
    row = lax.broadcasted_iota(jnp.int32, (cl, LANES), 0)
    causal2 = (lane % p) <= row
    left = lane.astype(F32).astype(BF16) < p
    diag2 = (lane % p) == row
    groups = [(gi, slice(gi * SSM_GROUP_W, (gi + 1) * SSM_GROUP_W), slice(gi * SSM_STATE, (gi + 1) * SSM_STATE),
               slice(SSM_GN + gi * SSM_STATE, SSM_GN + (gi + 1) * SSM_STATE))
              for gi in range(SSM_GROUPS)]
    chunks = [(c, slice(c * cl, (c + 1) * cl)) for c in range(nc)]

    for gi, gcols, _, _ in groups:
        e3g = e3_ref[:, gcols]
        aexp_ref[:, gcols] = _dot(dec_ref[:, :LANES], e3g)
        xdt_ref[:, gcols] = (xs_ref[:, gcols] * _dot(dec_ref[:, LANES:], e3g)).astype(BF16)

    for c, rows in chunks:
        for gi, _, scol, ccol in groups:
            bmp = _pad_rows(bc_ref[rows, scol], p)
            cbm = _dot_nt(bc_ref[rows, ccol], bmp)
            cb_ref[c * SSM_GROUPS + gi] = jnp.concatenate([cbm, cbm], axis=1)
            bmt_ref[c * SSM_GROUPS + gi] = bmp.astype(F32).T.astype(BF16)

    for c, rows in chunks:
        for gi, _, _, _ in groups:
            for jj in range(pairs):
                idx = (c * SSM_GROUPS + gi) * pairs + jj
                cols = slice(gi * SSM_GROUP_W + jj * LANES, gi * SSM_GROUP_W + (jj + 1) * LANES)
                a_pair = aexp_ref[rows, cols]
                a_src = jnp.sum(jnp.where(diag2, a_pair, 0.0), axis=0, keepdims=True)
                seg = a_pair - a_src
                dec = jnp.exp2(jnp.where(causal2, seg, -jnp.inf))
                w2_ref[idx] = (dec * cb_ref[c * SSM_GROUPS + gi]).astype(BF16)
                x2 = xdt_ref[rows, cols]
                top = _pad_rows(jnp.where(left, x2, jnp.zeros_like(x2)), p)
                bot = _pad_rows(jnp.where(left, jnp.zeros_like(x2), x2), p)
                rhs_ref[idx] = jnp.concatenate([top, bot], axis=0)

    for c, rows in chunks:
        for gi, _, _, _ in groups:
            for jj in range(pairs):
                idx = (c * SSM_GROUPS + gi) * pairs + jj
                cols = slice(gi * SSM_GROUP_W + jj * LANES, gi * SSM_GROUP_W + (jj + 1) * LANES)
                y_ref[rows, cols] = _dot(w2_ref[idx], rhs_ref[idx])

    for c, rows in chunks:
        for gi, gcols, _, _ in groups:
            a_g = aexp_ref[rows, gcols]
            xw = xdt_ref[rows, gcols] * jnp.exp2(a_g[cl - 1:cl, :] - a_g).astype(BF16)
            st_ref[c, :, gcols] = _dot(bmt_ref[c * SSM_GROUPS + gi], _pad_rows(xw, p))

    for c, rows in chunks:
        for gi, gcols, scol, ccol in groups:
            hprev = ht_ref[:, gcols]
            a_g = aexp_ref[rows, gcols]
            y_off = _dot(bc_ref[rows, ccol], hprev.astype(BF16)) * jnp.exp2(a_g)
            y_ref[rows, gcols] = y_ref[rows, gcols] + y_off
            ht_ref[:, gcols] = hprev * jnp.exp2(a_g[cl - 1:cl, :]) + st_ref[c, :, gcols]

    yo_ref[...] = (y_ref[...] + dskip_ref[...] * xs_ref[...]).astype(BF16)

    @pl.when(i == pl.num_programs(1) - 1)
    def _():
        hout_ref[...] = ht_ref[...].T


def _ssd(xs, bc, dec, h0, dskip, ts, cl):
    bsz, seq, _ = xs.shape
    tok = lambda width: pl.BlockSpec((None, ts, width), lambda b, i: (b, i, 0))
    st_spec = pl.BlockSpec((None, SSM_INNER, SSM_STATE), lambda b, i: (b, 0, 0))
    nc = ts // cl
    n_cg = nc * SSM_GROUPS
    n_pairs = n_cg * (SSM_GROUP_W // LANES)
    return pl.pallas_call(
        functools.partial(_ssd_kernel, ts, cl),
        grid=(bsz, seq // ts),
        in_specs=[tok(SSM_INNER), tok(2 * SSM_GN), tok(2 * LANES), st_spec,
                  _const_spec((1, SSM_INNER))],
        out_specs=[tok(SSM_INNER), st_spec],
        out_shape=[jax.ShapeDtypeStruct((bsz, seq, SSM_INNER), BF16),
                   jax.ShapeDtypeStruct((bsz, SSM_INNER, SSM_STATE), F32)],
        scratch_shapes=[pltpu.VMEM((SSM_STATE, SSM_INNER), F32), pltpu.VMEM((ts, SSM_INNER), F32),
                        pltpu.VMEM((LANES, SSM_INNER), BF16), pltpu.VMEM((ts, SSM_INNER), F32),
                        pltpu.VMEM((ts, SSM_INNER), BF16), pltpu.VMEM((nc, SSM_STATE, SSM_INNER), F32),
                        pltpu.VMEM((n_cg, cl, LANES), F32), pltpu.VMEM((n_cg, SSM_STATE, SSM_HEADDIM), BF16),
                        pltpu.VMEM((n_pairs, cl, LANES), BF16), pltpu.VMEM((n_pairs, LANES, LANES), BF16)],
        compiler_params=_params(),
        name="ssd_scan",
    )(xs, bc, dec, h0, dskip)


def _layer0_mixers(x, pos0, sconv_hist, k_cache, v_cache, wts, tiles):
    t_in, t_att, c_att = tiles
    q, k, v, gated, new_sconv = _attn_in(x, wts["g"][0][0], wts["ab_w_in"], sconv_hist, wts["sconv_w"], pos0, t_in)
    masked = k_cache is None
    k_prev, v_prev = (k, v) if masked else (k_cache, v_cache)
    att = _attn(wts["sinks"], q, k, v, k_prev, v_prev, t_att, c_att, masked)
    bsz, seq, _ = x.shape
    new_k = k[:, seq - min(WINDOW, seq):].reshape(bsz, -1, A_KV_HEADS, HEAD_DIM)
    new_v = v[:, seq - min(WINDOW, seq):].reshape(bsz, -1, A_KV_HEADS, HEAD_DIM)
    return [att, gated], (new_k, new_v, new_sconv)


def _layer1_mixer(x, conv_hist, ssm_state, wts, tiles):
    t_sin, t_ssd, c_ssd = tiles
    bsz = x.shape[0]
    z, xs, bc, dec, new_conv = _ssm_in(
        x, wts["g"][1][0], wts["ssm_w_in"], wts["w_dt"], conv_hist, wts["ssm_conv_w"],
        wts["ssm_conv_b"], wts["dt_bias"], wts["a_log"], t_sin, c_ssd)
    y, new_state = _ssd(xs, bc, dec, ssm_state.reshape(bsz, SSM_INNER, SSM_STATE),
                        wts["d_skip"], t_ssd, c_ssd)
    return [y, z], (new_conv, new_state.reshape(bsz, SSM_HEADS, SSM_HEADDIM, SSM_STATE))


def kernel(x_prompt, x_sample, cache_attn_k, cache_attn_v, state_sconv, state_ssm_conv, state_ssm, norm_g,
           ab_w_in, ab_w_out, attn_sinks, sconv_w, ssm_w_in, ssm_conv_w, ssm_conv_b, ssm_dt_bias, ssm_a_log,
           ssm_d, ssm_norm_g, ssm_w_out, ffn_w_gate, ffn_w_up, ffn_w_down):
    w_in1 = ssm_w_in[0]
    wts = {
        "ab_w_in": ab_w_in[0].astype(BF16),
        "ab_w_out": ab_w_out[0].astype(BF16),
        "sinks": attn_sinks[0],
        "sconv_w": sconv_w[0],
        "ssm_w_in": w_in1.astype(BF16),
        "w_dt": jnp.tile(w_in1[:, SSM_INNER + SSM_CONV_CH:], (1, DT_REP)).astype(BF16),
        "ssm_conv_w": ssm_conv_w[0],
        "ssm_conv_b": ssm_conv_b[0].reshape(1, -1),
        "dt_bias": jnp.tile(ssm_dt_bias[0], DT_REP).reshape(1, -1),
        "a_log": jnp.tile(ssm_a_log[0], DT_REP).reshape(1, -1),
        "d_skip": jnp.repeat(ssm_d[0], SSM_HEADDIM).reshape(1, -1),
        "ssm_norm_g": ssm_norm_g[0].reshape(1, -1),
        "ssm_w_out": ssm_w_out[0].astype(BF16),
        "wg": ffn_w_gate.astype(BF16),
        "wu": ffn_w_up.astype(BF16),
        "wd": ffn_w_down.astype(BF16),
    }
    wts["g"] = [[norm_g[l, j].reshape(1, -1) for j in range(4)] for l in range(norm_g.shape[0])]
    g = wts["g"]
    bp = x_prompt.shape[0]
    bs, ls = x_sample.shape[0], x_sample.shape[1]
    t_ffn = 512

    mix_p, (kp, vp, scp) = _layer0_mixers(x_prompt, 0, jnp.zeros((bp, B_CONV - 1, B_WIDTH), F32), None, None,
                                          wts, (1024, 1024, CHUNK))
    mix_s, (ks, vs, scs) = _layer0_mixers(x_sample, PAST_LEN, state_sconv[0],
                                          cache_attn_k[0].reshape(bs, -1, KV_WIDTH),
                                          cache_attn_v[0].reshape(bs, -1, KV_WIDTH), wts, (bs * ls, ls, ls))
    xp, xs = _proj_ffn(mix_p, mix_s, None, wts["ab_w_out"], x_prompt, x_sample, g[0][1], g[0][2],
                       wts["wg"], wts["wu"], wts["wd"], 0, g[0][3], t_ffn)

    mix_p, (ccp, ssp) = _layer1_mixer(xp, jnp.zeros((bp, SSM_CONV - 1, SSM_CONV_CH), F32),
                                      jnp.zeros((bp, SSM_HEADS, SSM_HEADDIM, SSM_STATE), F32), wts, (512, 512, CHUNK))
    mix_s, (ccs, sss) = _layer1_mixer(xs, state_ssm_conv[0], state_ssm[0], wts, (bs * ls, ls, ls))
    yp, ys = _proj_ffn(mix_p, mix_s, wts["ssm_norm_g"], wts["ssm_w_out"], xp, xs, g[1][1], g[1][2],
                       wts["wg"], wts["wu"], wts["wd"], 1, g[1][3], t_ffn)

    lead = lambda a: a[None]
    return (yp, ys, lead(kp), lead(vp), lead(scp), lead(ccp), lead(ssp),
            lead(ks), lead(vs), lead(scs), lead(ccs), lead(sss))
```

```python
import functools

import jax
import jax.numpy as jnp
from jax import lax
from jax.experimental import pallas as pl
from jax.experimental.pallas import tpu as pltpu

F32 = jnp.float32
BF16 = jnp.bfloat16

D_MODEL = 1024
CHUNK = 64
EPS = 1e-6
LOG2E = 1.4426950408889634
PAST_LEN = 4096

A_HEADS = 8
A_KV_HEADS = 2
A_GROUP = A_HEADS // A_KV_HEADS
HEAD_DIM = 64
A_WIDTH = A_HEADS * HEAD_DIM
KV_WIDTH = A_KV_HEADS * HEAD_DIM
WINDOW = 128
ROPE_DIM = HEAD_DIM // 4
ROPE_HALF = ROPE_DIM // 2
ROPE_THETA = 500000.0
ATTN_SCALE = HEAD_DIM ** -0.5

B_WIDTH = D_MODEL // 2
B_CONV = 3
AB_IN = A_WIDTH + 2 * KV_WIDTH + 3 * B_WIDTH

SSM_INNER = 2 * D_MODEL
SSM_HEADDIM = 64
SSM_HEADS = SSM_INNER // SSM_HEADDIM
SSM_GROUPS = 4
SSM_GROUP_W = SSM_INNER // SSM_GROUPS
SSM_STATE = 128
SSM_CONV = 4
SSM_GN = SSM_GROUPS * SSM_STATE
SSM_CONV_CH = SSM_INNER + 2 * SSM_GN

D_FF = -(-8 * D_MODEL // (3 * 256)) * 256

LANES = 128
SUBLANES = 8
MXU_DIM = 256
DT_REP = LANES // SSM_HEADS
VMEM_LIMIT = 56 * 1024 * 1024
COL_CHUNK = MXU_DIM
FF_CHUNKS = (6 * MXU_DIM, 5 * MXU_DIM)
assert sum(FF_CHUNKS) == D_FF


def _rms(x, g):
    return x * lax.rsqrt(jnp.mean(x * x, axis=-1, keepdims=True) + EPS) * g


def _dot(a, b):
    return jnp.dot(a, b, preferred_element_type=F32)


def _dot_nt(a, b):
    return lax.dot_general(a, b, (((1,), (1,)), ((), ())), preferred_element_type=F32)


def _silu_of_twice(half):
    return half + half * jnp.tanh(half)


def _const_spec(shape):
    return pl.BlockSpec(shape, lambda *_: (0,) * len(shape), pipeline_mode=pl.Buffered(1))


def _layer_spec(shape, layer):
    return pl.BlockSpec((None,) + shape, lambda *_: (layer,) + (0,) * len(shape), pipeline_mode=pl.Buffered(1))


def _params():
    return pltpu.CompilerParams(dimension_semantics=("arbitrary", "arbitrary"),
                                vmem_limit_bytes=VMEM_LIMIT)


def _causal_conv(carry_ref, nh_ref, u, cw, cols):
    width = cw.shape[0]
    rows, ch = u.shape
    nseq = carry_ref.shape[0]
    u4 = u.reshape(nseq, rows // (nseq * SUBLANES), SUBLANES, ch)
    ext4 = jnp.concatenate([carry_ref[:, :, cols].reshape(nseq, 1, SUBLANES, ch), u4], axis=1)
    sub = lax.broadcasted_iota(jnp.int32, (1, 1, SUBLANES, ch), 2)

    def shift(above, here, k):
        return pltpu.roll(jnp.where(sub >= SUBLANES - k, above, here), k, 2)

    if width == 4:
        s1 = shift(jnp.concatenate([ext4[:, :1], ext4[:, :-1]], axis=1), ext4, 1)
        b = s1 * cw[0:1] + ext4 * cw[1:2]
        y4 = shift(b[:, :-1], b[:, 1:], 2) + (s1[:, 1:] * cw[2:3] + u4 * cw[3:4])
    else:
        y4 = None
        for j in range(width - 1):
            term = shift(ext4[:, :-1], u4, width - 1 - j) * cw[j:j + 1]
            y4 = term if y4 is None else y4 + term
        y4 = y4 + u4 * cw[width - 1:width]
    carry_ref[:, :, cols] = u4[:, -1]
    nh_ref[:, :, cols] = u4[:, -1, SUBLANES - (width - 1):]
    return y4.reshape(rows, ch)


def _rope_freq(shape):
    lane = lax.broadcasted_iota(jnp.int32, shape, 1)
    d = lane % HEAD_DIM
    expo = -((d % ROPE_HALF).astype(F32)) / ROPE_HALF
    freq = jnp.power(jnp.full(shape, ROPE_THETA, F32), expo)
    return jnp.where(d < ROPE_DIM, freq, 0.0), d


def _attn_in_kernel(pos0, tm, seq_rows, x_ref, g_ref, w_ref, hist_ref, cw_ref,
                    q_ref, k_ref, v_ref, gated_ref, nh_ref,
                    cr_ref, sr_ref, carry_ref, p_ref):
    b = pl.program_id(0)
    i = pl.program_id(1)

    @pl.when((b == 0) & (i == 0))
    def _():
        freq, _ = _rope_freq((tm, LANES))
        ang = (lax.broadcasted_iota(jnp.int32, (tm, LANES), 0) % seq_rows).astype(F32) * freq
        cr_ref[...] = jnp.cos(ang)
        sr_ref[...] = jnp.sin(ang)

    @pl.when(i == 0)
    def _():
        carry_ref[:, SUBLANES - (B_CONV - 1):, :] = hist_ref[...]

    freq1, d1 = _rope_freq((1, LANES))
    base = (pos0 + i * tm).astype(F32) * freq1
    cb = jnp.cos(base)
    sb = jnp.sin(base)
    cr = cr_ref[...]
    sr = sr_ref[...]
    cos_t = cr * cb - sr * sb
    sin_t = sr * cb + cr * sb
    m_lo = jnp.where(d1 < ROPE_HALF, -1.0, 0.0)
    m_hi = jnp.where((d1 >= ROPE_HALF) & (d1 < ROPE_DIM), 1.0, 0.0)

    def rope(t):
        partner = pltpu.roll(t, LANES - ROPE_HALF, 1) * m_lo + pltpu.roll(t, ROPE_HALF, 1) * m_hi
        return t * cos_t + partner * sin_t

    h = _rms(x_ref[...], g_ref[...]).astype(BF16)
    o1 = A_WIDTH
    o2 = o1 + KV_WIDTH
    o3 = o2 + KV_WIDTH
    o4 = o3 + B_WIDTH
    o5 = o4 + B_WIDTH

    for n in range(AB_IN // MXU_DIM):
        cols = slice(n * MXU_DIM, (n + 1) * MXU_DIM)
        p_ref[:, cols] = _dot(h, w_ref[:, cols])

    u = p_ref[:, o4:o5] * p_ref[:, o5:]
    y = _causal_conv(carry_ref, nh_ref, u, cw_ref[...], slice(None))
    gated_ref[...] = (p_ref[:, o3:o4] * y).astype(BF16)
    k_ref[...] = rope(p_ref[:, o1:o2])
    v_ref[...] = p_ref[:, o2:o3]
    for s in range(A_WIDTH // LANES):
        q_ref[:, s * LANES:(s + 1) * LANES] = (rope(p_ref[:, s * LANES:(s + 1) * LANES]) * ATTN_SCALE).astype(BF16)


def _tile_rows(x, tm):
    bsz, seq, d = x.shape
    nseq = max(tm // seq, 1)
    return x.reshape(bsz // nseq, nseq * seq, d), nseq


def _attn_in(x, g, w, hist, cw, pos0, tm):
    bsz, seq, _ = x.shape
    xt, nseq = _tile_rows(x, tm)
    groups, rows, _ = xt.shape
    tok = lambda width: pl.BlockSpec((None, tm, width), lambda b, i: (b, i, 0))
    hist_spec = pl.BlockSpec((nseq, B_CONV - 1, B_WIDTH), lambda b, i: (b, 0, 0))
    outs = pl.pallas_call(
        functools.partial(_attn_in_kernel, pos0, tm, tm // nseq),
        grid=(groups, rows // tm),
        in_specs=[tok(D_MODEL), _const_spec((1, D_MODEL)), _const_spec((D_MODEL, AB_IN)), hist_spec,
                  _const_spec((B_CONV, B_WIDTH))],
        out_specs=[tok(A_WIDTH), tok(KV_WIDTH), tok(KV_WIDTH), tok(B_WIDTH), hist_spec],
        out_shape=[jax.ShapeDtypeStruct((groups, rows, A_WIDTH), BF16),
                   jax.ShapeDtypeStruct((groups, rows, KV_WIDTH), F32),
                   jax.ShapeDtypeStruct((groups, rows, KV_WIDTH), F32),
                   jax.ShapeDtypeStruct((groups, rows, B_WIDTH), BF16),
                   jax.ShapeDtypeStruct((bsz, B_CONV - 1, B_WIDTH), F32)],
        scratch_shapes=[pltpu.VMEM((tm, LANES), F32), pltpu.VMEM((tm, LANES), F32),
                        pltpu.VMEM((nseq, SUBLANES, B_WIDTH), F32), pltpu.VMEM((tm, AB_IN), F32)],
        compiler_params=_params(),
        name="attn_in_proj",
    )(xt, g, w, hist, cw)
    return [o.reshape(bsz, seq, o.shape[-1]) for o in outs[:-1]] + [outs[-1]]


def _attn_kernel(tq, cq, masked, sinks_ref, q_ref, kc_ref, vc_ref, kp_ref, vp_ref, att_ref,
                 kk_ref, vx_ref, s_ref, e_ref):
    i = pl.program_id(1)
    nk = WINDOW + cq
    nkp = MXU_DIM
    rows_kv = WINDOW + tq
    hd = HEAD_DIM

    kk_ref[0:WINDOW, :] = kp_ref[...].astype(BF16)
    kk_ref[WINDOW:rows_kv, :] = kc_ref[...].astype(BF16)
    kk_ref[rows_kv:, :] = jnp.zeros((kk_ref.shape[0] - rows_kv, KV_WIDTH), BF16)
    vall = jnp.concatenate([vp_ref[...], vc_ref[...]], axis=0)
    vx_ref[0:rows_kv, :] = jnp.concatenate([vall[:, :hd], vall[:, :hd], vall[:, hd:], vall[:, hd:]], axis=1)
    vx_ref[rows_kv:, :] = jnp.zeros((vx_ref.shape[0] - rows_kv, 2 * LANES), F32)

    col = lax.broadcasted_iota(jnp.int32, (A_GROUP * cq, nkp), 1)
    krow = lax.broadcasted_iota(jnp.int32, (nkp, LANES), 0)
    ones = jnp.ones((nkp, LANES), BF16)
    bodies = [(c, kh) for c in range(tq // cq) for kh in range(A_KV_HEADS)]
    fills = []
    for kh in range(A_KV_HEADS):
        sink = jnp.concatenate(
            [jnp.full((cq, nkp), sinks_ref[kh * A_GROUP + g], F32) for g in range(A_GROUP)], axis=0)
        fills.append(jnp.where(col == nk, sink, -jnp.inf))

    for n, (c, kh) in enumerate(bodies):
        r0 = c * cq
        heads = [kh * A_GROUP + g for g in range(A_GROUP)]
        kt = kk_ref[r0:r0 + nkp, kh * hd:(kh + 1) * hd]
        qs = jnp.concatenate([q_ref[r0:r0 + cq, hh * hd:(hh + 1) * hd] for hh in heads], axis=0)
        s = _dot_nt(qs, kt)
        valid = col < nk
        if masked and r0 < WINDOW:
            valid = valid & ((col >= WINDOW - r0) | (i > 0))
        s_ref[n] = jnp.where(valid, s, fills[kh])

    for n in range(len(bodies)):
        s = s_ref[n]
        m = jnp.max(s, axis=-1, keepdims=True)
        e_ref[n] = jnp.exp(s - m).astype(BF16)

    for n, (c, kh) in enumerate(bodies):
        r0 = c * cq
        vt = jnp.where(krow == nk, 0.0, vx_ref[r0:r0 + nkp, kh * LANES:(kh + 1) * LANES]).astype(BF16)
        ox = _dot(e_ref[n], jnp.concatenate([vt, ones], axis=1))
        o = ox[:, :LANES] / ox[:, LANES:]
        for g in range(A_GROUP):
            hh = kh * A_GROUP + g
            lanes = slice((hh % 2) * hd, (hh % 2 + 1) * hd)
            att_ref[r0:r0 + cq, hh * hd:(hh + 1) * hd] = o[g * cq:(g + 1) * cq, lanes].astype(BF16)


def _attn(sinks, q, k, v, k_prev, v_prev, tq, cq, masked):
    bsz, seq, _ = q.shape
    grid = (bsz, seq // tq)
    tok = lambda width: pl.BlockSpec((None, tq, width), lambda b, i: (b, i, 0))
    if masked:
        blocks_per_tile = tq // WINDOW
        prev = pl.BlockSpec((None, WINDOW, KV_WIDTH),
                            lambda b, i: (b, jnp.maximum(i * blocks_per_tile - 1, 0), 0))
    else:
        prev = pl.BlockSpec((None, WINDOW, KV_WIDTH), lambda b, i: (b, 0, 0))
    n_bodies = (tq // cq) * A_KV_HEADS
    kv_rows = (tq // cq - 1) * cq + MXU_DIM
    return pl.pallas_call(
        functools.partial(_attn_kernel, tq, cq, masked),
        grid=grid,
        in_specs=[pl.BlockSpec(memory_space=pltpu.SMEM),
                  tok(A_WIDTH), tok(KV_WIDTH), tok(KV_WIDTH), prev, prev],
        out_specs=tok(A_WIDTH),
        out_shape=jax.ShapeDtypeStruct((bsz, seq, A_WIDTH), BF16),
        scratch_shapes=[pltpu.VMEM((kv_rows, KV_WIDTH), BF16), pltpu.VMEM((kv_rows, 2 * LANES), F32),
                        pltpu.VMEM((n_bodies, A_GROUP * cq, MXU_DIM), F32),
                        pltpu.VMEM((n_bodies, A_GROUP * cq, MXU_DIM), BF16)],
        compiler_params=_params(),
        name="band_attention",
    )(sinks, q, k, v, k_prev, v_prev)


def _proj_ffn_kernel(n_in, gate, n_tiles, *refs):
    long_in, short_in = refs[:n_in], refs[n_in:2 * n_in]
    refs = refs[2 * n_in:]
    ng_ref = None
    if gate:
        ng_ref, refs = refs[0], refs[1:]
    wo_ref, xl_ref, xs_ref, g1_ref, g2_ref, wg_ref, wu_ref, wd_ref, g3_ref, ol_ref, os_ref = refs
    t = pl.program_id(0)

    def tile(ins, x_ref, o_ref):
        acc = None
        if gate:
            y_ref, z_ref = ins
            for gi in range(SSM_GROUPS):
                gcols = slice(gi * SSM_GROUP_W, (gi + 1) * SSM_GROUP_W)
                gg = y_ref[:, gcols].astype(F32) * z_ref[:, gcols].astype(F32)
                gg = gg * lax.rsqrt(jnp.mean(gg * gg, axis=-1, keepdims=True) + EPS)
                part = _dot((gg * ng_ref[:, gcols]).astype(BF16), wo_ref[gcols, :])
                acc = part if acc is None else acc + part
        else:
            off = 0
            for a_ref in ins:
                kdim = a_ref.shape[-1]
                part = _dot(a_ref[...], wo_ref[off:off + kdim, :])
                acc = part if acc is None else acc + part
                off += kdim

        x = x_ref[...] + _rms(acc, g1_ref[...])
        h = _rms(x, g2_ref[...]).astype(BF16)
        h_half = h * 0.5
        f = None
        off = 0
        for width in FF_CHUNKS:
            fc = slice(off, off + width)
            m = (_silu_of_twice(_dot(h_half, wg_ref[:, fc])) * _dot(h, wu_ref[:, fc])).astype(BF16)
            part = _dot(m, wd_ref[fc, :])
            f = part if f is None else f + part
            off += width
        o_ref[...] = x + _rms(f, g3_ref[...])

    pl.when(t < n_tiles)(lambda: tile(long_in, xl_ref, ol_ref))
    pl.when(t == n_tiles)(lambda: tile(short_in, xs_ref, os_ref))


def _proj_ffn(mix_long, mix_short, ng, wo, x_long, x_short, g1, g2, wg, wu, wd, layer, g3, tm):
    bsz, seq, _ = x_long.shape
    tiles_per_seq = seq // tm
    n_tiles = bsz * tiles_per_seq
    n_short = x_short.shape[0] * x_short.shape[1]
    flat = lambda a: a.reshape(1, n_short, a.shape[-1])
    tile_idx = lambda t: jnp.minimum(t, n_tiles - 1)
    ltok = lambda width: pl.BlockSpec((None, tm, width),
                                      lambda t: (tile_idx(t) // tiles_per_seq, tile_idx(t) % tiles_per_seq, 0))
    stok = lambda width: pl.BlockSpec((None, n_short, width), lambda t: (0, 0, 0))
    gate = ng is not None
    ng_args, ng_specs = ([ng], [_const_spec((1, SSM_INNER))]) if gate else ([], [])
    out_long, out_short = pl.pallas_call(
        functools.partial(_proj_ffn_kernel, len(mix_long), gate, n_tiles),
        grid=(n_tiles + 1,),
        in_specs=[ltok(a.shape[-1]) for a in mix_long] + [stok(a.shape[-1]) for a in mix_short] + ng_specs + [
            _const_spec(wo.shape), ltok(D_MODEL), stok(D_MODEL), _const_spec((1, D_MODEL)), _const_spec((1, D_MODEL)),
            _layer_spec((D_MODEL, D_FF), layer), _layer_spec((D_MODEL, D_FF), layer),
            _layer_spec((D_FF, D_MODEL), layer), _const_spec((1, D_MODEL))],
        out_specs=[ltok(D_MODEL), stok(D_MODEL)],
        out_shape=[jax.ShapeDtypeStruct(x_long.shape, F32), jax.ShapeDtypeStruct((1, n_short, D_MODEL), F32)],
        compiler_params=pltpu.CompilerParams(dimension_semantics=("arbitrary",), vmem_limit_bytes=VMEM_LIMIT),
        name="out_proj_ffn",
    )(*mix_long, *[flat(a) for a in mix_short], *ng_args, wo, x_long, flat(x_short), g1, g2, wg, wu, wd, g3)
    return out_long, out_short.reshape(x_short.shape)


def _split3(v):
    t1 = v.astype(BF16)
    r = v - t1.astype(F32)
    t2 = r.astype(BF16)
    t3 = (r - t2.astype(F32)).astype(BF16)
    return t1, t2, t3


def _pack3(v):
    t1, t2, t3 = _split3(v)
    lane = lax.broadcasted_iota(jnp.int32, v.shape, 1)
    return jnp.where(lane < SSM_HEADS, t1,
                     jnp.where(lane < 2 * SSM_HEADS, t2,
                               jnp.where(lane < 3 * SSM_HEADS, t3, jnp.zeros_like(t3))))


def _ssm_in_kernel(tm, cl, x_ref, g_ref, w_ref, wdt_ref, hist_ref, cw_ref, cb_ref, dtb_ref, alog_ref,
                   z_ref, xs_ref, bm_ref, cm_ref, ta_ref, tdt_ref, nh_ref, carry_ref):
    i = pl.program_id(1)

    @pl.when(i == 0)
    def _():
        carry_ref[:, SUBLANES - (SSM_CONV - 1):, :] = hist_ref[...]

    h = _rms(x_ref[...], g_ref[...]).astype(BF16)

    dt = jax.nn.softplus(_dot(h, wdt_ref[...]) + dtb_ref[...])
    tdt_ref[...] = _pack3(dt)
    d1, d2, d3 = _split3(dt * (-jnp.exp(alog_ref[...]) * LOG2E))
    tcol = lax.broadcasted_iota(jnp.int32, (cl, 3 * cl), 1) % cl
    trow = lax.broadcasted_iota(jnp.int32, (cl, 3 * cl), 0)
    tril = jnp.where(tcol <= trow, 1.0, 0.0).astype(BF16)
    acum = []
    for c in range(tm // cl):
        rows = slice(c * cl, (c + 1) * cl)
        acum.append(_dot(tril, jnp.concatenate([d1[rows], d2[rows], d3[rows]], axis=0)))
    ta_ref[...] = _pack3(jnp.concatenate(acum, axis=0))

    h_half = h * 0.5
    for n in range(SSM_INNER // COL_CHUNK):
        cols = slice(n * COL_CHUNK, (n + 1) * COL_CHUNK)
        z_ref[:, cols] = _silu_of_twice(_dot(h_half, w_ref[:, cols])).astype(BF16)

    for n in range(SSM_CONV_CH // COL_CHUNK):
        cols = slice(n * COL_CHUNK, (n + 1) * COL_CHUNK)
        raw = _dot(h, w_ref[:, SSM_INNER + cols.start:SSM_INNER + cols.stop])
        y = _silu_of_twice(_causal_conv(carry_ref, nh_ref, raw, 0.5 * cw_ref[:, cols], cols) + 0.5 * cb_ref[:, cols])
        if cols.stop <= SSM_INNER:
            xs_ref[:, cols] = y
        elif cols.stop <= SSM_INNER + SSM_GN:
            bm_ref[:, cols.start - SSM_INNER:cols.stop - SSM_INNER] = y.astype(BF16)
        else:
            lo = SSM_INNER + SSM_GN
            cm_ref[:, cols.start - lo:cols.stop - lo] = y.astype(BF16)


def _ssm_in(x, g, w, wdt, hist, cw, cb, dtb, alog, tm, cl):
    bsz, seq, _ = x.shape
    xt, nseq = _tile_rows(x, tm)
    groups, rows, _ = xt.shape
    tok = lambda width: pl.BlockSpec((None, tm, width), lambda b, i: (b, i, 0))
    hist_spec = pl.BlockSpec((nseq, SSM_CONV - 1, SSM_CONV_CH), lambda b, i: (b, 0, 0))
    outs = pl.pallas_call(
        functools.partial(_ssm_in_kernel, tm, cl),
        grid=(groups, rows // tm),
        in_specs=[tok(D_MODEL), _const_spec((1, D_MODEL)), _const_spec(w.shape), _const_spec((D_MODEL, LANES)),
                  hist_spec,
                  _const_spec((SSM_CONV, SSM_CONV_CH)), _const_spec((1, SSM_CONV_CH)), _const_spec((1, LANES)),
                  _const_spec((1, LANES))],
        out_specs=[tok(SSM_INNER), tok(SSM_INNER), tok(SSM_GN), tok(SSM_GN), tok(LANES), tok(LANES), hist_spec],
        out_shape=[jax.ShapeDtypeStruct((groups, rows, SSM_INNER), BF16),
                   jax.ShapeDtypeStruct((groups, rows, SSM_INNER), F32),
                   jax.ShapeDtypeStruct((groups, rows, SSM_GN), BF16),
                   jax.ShapeDtypeStruct((groups, rows, SSM_GN), BF16),
                   jax.ShapeDtypeStruct((groups, rows, LANES), BF16),
                   jax.ShapeDtypeStruct((groups, rows, LANES), BF16),
                   jax.ShapeDtypeStruct((bsz, SSM_CONV - 1, SSM_CONV_CH), F32)],
        scratch_shapes=[pltpu.VMEM((nseq, SUBLANES, SSM_CONV_CH), F32)],
        compiler_params=_params(),
        name="ssm_in_proj",
    )(xt, g, w, wdt, hist, cw, cb, dtb, alog)
    return [o.reshape(bsz, seq, o.shape[-1]) for o in outs[:-1]] + [outs[-1]]


def _pad_rows(a, rows):
    if a.shape[0] == rows:
        return a
    return jnp.concatenate([a, jnp.zeros((rows - a.shape[0], a.shape[1]), a.dtype)], axis=0)


def _ssd_kernel(ts, cl, xs_ref, bm_ref, cm_ref, ta_ref, tdt_ref, h0_ref, dskip_ref,
                yo_ref, hout_ref, ht_ref, y_ref, e3_ref, aexp_ref, xdt_ref, st_ref, cb_ref, bmt_ref, w2_ref, rhs_ref):
    b = pl.program_id(0)
    i = pl.program_id(1)
    p = SSM_HEADDIM
    nc = ts // cl
    pairs = SSM_GROUP_W // LANES

    @pl.when((b == 0) & (i == 0))
    def _():
        kk = lax.broadcasted_iota(jnp.int32, (LANES, SSM_INNER), 0)
        cc = lax.broadcasted_iota(jnp.int32, (LANES, SSM_INNER), 1)
        hit = (kk % SSM_HEADS == cc // p) & (kk < 3 * SSM_HEADS)
        e3_ref[...] = jnp.where(hit, 1.0, 0.0).astype(BF16)

    @pl.when(i == 0)
    def _():
        ht_ref[...] = h0_ref[...].T

    lane = lax.broadcasted_iota(jnp.int32, (cl, LANES), 1)
    row = lax.broadcasted_iota(jnp.int32, (cl, LANES), 0)
    causal2 = (lane % p) <= row
    left = lane.astype(F32).astype(BF16) < p
    diag2 = (lane % p) == row
    groups = [(gi, slice(gi * SSM_GROUP_W, (gi + 1) * SSM_GROUP_W), slice(gi * SSM_STATE, (gi + 1) * SSM_STATE))
              for gi in range(SSM_GROUPS)]
    chunks = [(c, slice(c * cl, (c + 1) * cl)) for c in range(nc)]

    for gi, gcols, _ in groups:
        e3g = e3_ref[:, gcols]
        aexp_ref[:, gcols] = _dot(ta_ref[...], e3g)
        xdt_ref[:, gcols] = (xs_ref[:, gcols] * _dot(tdt_ref[...], e3g)).astype(BF16)

    for c, rows in chunks:
        for gi, _, scol in groups:
            bmp = _pad_rows(bm_ref[rows, scol], p)
            cbm = _dot_nt(cm_ref[rows, scol], bmp)
            cb_ref[c * SSM_GROUPS + gi] = jnp.concatenate([cbm, cbm], axis=1)
            bmt_ref[c * SSM_GROUPS + gi] = bmp.astype(F32).T.astype(BF16)

    for c, rows in chunks:
        for gi, _, _ in groups:
            for jj in range(pairs):
                idx = (c * SSM_GROUPS + gi) * pairs + jj
                cols = slice(gi * SSM_GROUP_W + jj * LANES, gi * SSM_GROUP_W + (jj + 1) * LANES)
                a_pair = aexp_ref[rows, cols]
                a_src = jnp.sum(jnp.where(diag2, a_pair, 0.0), axis=0, keepdims=True)
                seg = a_pair - a_src
                dec = jnp.exp2(jnp.where(causal2, seg, -jnp.inf))
                w2_ref[idx] = (dec * cb_ref[c * SSM_GROUPS + gi]).astype(BF16)
                x2 = xdt_ref[rows, cols]
                top = _pad_rows(jnp.where(left, x2, jnp.zeros_like(x2)), p)
                bot = _pad_rows(jnp.where(left, jnp.zeros_like(x2), x2), p)
                rhs_ref[idx] = jnp.concatenate([top, bot], axis=0)

    for c, rows in chunks:
        for gi, _, _ in groups:
            for jj in range(pairs):
                idx = (c * SSM_GROUPS + gi) * pairs + jj
                cols = slice(gi * SSM_GROUP_W + jj * LANES, gi * SSM_GROUP_W + (jj + 1) * LANES)
                y_ref[rows, cols] = _dot(w2_ref[idx], rhs_ref[idx])

    for c, rows in chunks:
        for gi, gcols, _ in groups:
            a_g = aexp_ref[rows, gcols]
            xw = xdt_ref[rows, gcols] * jnp.exp2(a_g[cl - 1:cl, :] - a_g).astype(BF16)
            st_ref[c, :, gcols] = _dot(bmt_ref[c * SSM_GROUPS + gi], _pad_rows(xw, p))

    for c, rows in chunks:
        for gi, gcols, scol in groups:
            hprev = ht_ref[:, gcols]
            a_g = aexp_ref[rows, gcols]
            y_off = _dot(cm_ref[rows, scol], hprev.astype(BF16)) * jnp.exp2(a_g)
            y_ref[rows, gcols] = y_ref[rows, gcols] + y_off
            ht_ref[:, gcols] = hprev * jnp.exp2(a_g[cl - 1:cl, :]) + st_ref[c, :, gcols]

    yo_ref[...] = (y_ref[...] + dskip_ref[...] * xs_ref[...]).astype(BF16)

    @pl.when(i == pl.num_programs(1) - 1)
    def _():
        hout_ref[...] = ht_ref[...].T


def _ssd(xs, bm, cm, ta, tdt, h0, dskip, ts, cl):
    bsz, seq, _ = xs.shape
    tok = lambda width: pl.BlockSpec((None, ts, width), lambda b, i: (b, i, 0))
    st_spec = pl.BlockSpec((None, SSM_INNER, SSM_STATE), lambda b, i: (b, 0, 0))
    nc = ts // cl
    n_cg = nc * SSM_GROUPS
    n_pairs = n_cg * (SSM_GROUP_W // LANES)
    return pl.pallas_call(
        functools.partial(_ssd_kernel, ts, cl),
        grid=(bsz, seq // ts),
        in_specs=[tok(SSM_INNER), tok(SSM_GN), tok(SSM_GN), tok(LANES), tok(LANES), st_spec,
                  _const_spec((1, SSM_INNER))],
        out_specs=[tok(SSM_INNER), st_spec],
        out_shape=[jax.ShapeDtypeStruct((bsz, seq, SSM_INNER), BF16),
                   jax.ShapeDtypeStruct((bsz, SSM_INNER, SSM_STATE), F32)],
        scratch_shapes=[pltpu.VMEM((SSM_STATE, SSM_INNER), F32), pltpu.VMEM((ts, SSM_INNER), F32),
                        pltpu.VMEM((LANES, SSM_INNER), BF16), pltpu.VMEM((ts, SSM_INNER), F32),
                        pltpu.VMEM((ts, SSM_INNER), BF16), pltpu.VMEM((nc, SSM_STATE, SSM_INNER), F32),
                        pltpu.VMEM((n_cg, cl, LANES), F32), pltpu.VMEM((n_cg, SSM_STATE, SSM_HEADDIM), BF16),
                        pltpu.VMEM((n_pairs, cl, LANES), BF16), pltpu.VMEM((n_pairs, LANES, LANES), BF16)],
        compiler_params=_params(),
        name="ssd_scan",
    )(xs, bm, cm, ta, tdt, h0, dskip)


def _layer0_mixers(x, pos0, sconv_hist, k_cache, v_cache, wts, tiles):
    t_in, t_att, c_att = tiles
    q, k, v, gated, new_sconv = _attn_in(x, wts["g"][0][0], wts["ab_w_in"], sconv_hist, wts["sconv_w"], pos0, t_in)
    masked = k_cache is None
    k_prev, v_prev = (k, v) if masked else (k_cache, v_cache)
    att = _attn(wts["sinks"], q, k, v, k_prev, v_prev, t_att, c_att, masked)
    bsz, seq, _ = x.shape
    new_k = k[:, seq - min(WINDOW, seq):].reshape(bsz, -1, A_KV_HEADS, HEAD_DIM)
    new_v = v[:, seq - min(WINDOW, seq):].reshape(bsz, -1, A_KV_HEADS, HEAD_DIM)
    return [att, gated], (new_k, new_v, new_sconv)


def _layer1_mixer(x, conv_hist, ssm_state, wts, tiles):
    t_sin, t_ssd, c_ssd = tiles
    bsz = x.shape[0]
    z, xs, bm, cm, ta, tdt, new_conv = _ssm_in(
        x, wts["g"][1][0], wts["ssm_w_in"], wts["w_dt"], conv_hist, wts["ssm_conv_w"],
        wts["ssm_conv_b"], wts["dt_bias"], wts["a_log"], t_sin, c_ssd)
    y, new_state = _ssd(xs, bm, cm, ta, tdt, ssm_state.reshape(bsz, SSM_INNER, SSM_STATE),
                        wts["d_skip"], t_ssd, c_ssd)
    return [y, z], (new_conv, new_state.reshape(bsz, SSM_HEADS, SSM_HEADDIM, SSM_STATE))


def kernel(x_prompt, x_sample, cache_attn_k, cache_attn_v, state_sconv, state_ssm_conv, state_ssm, norm_g,
           ab_w_in, ab_w_out, attn_sinks, sconv_w, ssm_w_in, ssm_conv_w, ssm_conv_b, ssm_dt_bias, ssm_a_log,
           ssm_d, ssm_norm_g, ssm_w_out, ffn_w_gate, ffn_w_up, ffn_w_down):
    w_in1 = ssm_w_in[0]
    wts = {
        "ab_w_in": ab_w_in[0].astype(BF16),
        "ab_w_out": ab_w_out[0].astype(BF16),
        "sinks": attn_sinks[0],
        "sconv_w": sconv_w[0],
        "ssm_w_in": w_in1.astype(BF16),
        "w_dt": jnp.tile(w_in1[:, SSM_INNER + SSM_CONV_CH:], (1, DT_REP)).astype(BF16),
        "ssm_conv_w": ssm_conv_w[0],
        "ssm_conv_b": ssm_conv_b[0].reshape(1, -1),
        "dt_bias": jnp.tile(ssm_dt_bias[0], DT_REP).reshape(1, -1),
        "a_log": jnp.tile(ssm_a_log[0], DT_REP).reshape(1, -1),
        "d_skip": jnp.repeat(ssm_d[0], SSM_HEADDIM).reshape(1, -1),
        "ssm_norm_g": ssm_norm_g[0].reshape(1, -1),
        "ssm_w_out": ssm_w_out[0].astype(BF16),
        "wg": ffn_w_gate.astype(BF16),
        "wu": ffn_w_up.astype(BF16),
        "wd": ffn_w_down.astype(BF16),
    }
    wts["g"] = [[norm_g[l, j].reshape(1, -1) for j in range(4)] for l in range(norm_g.shape[0])]
    g = wts["g"]
    bp = x_prompt.shape[0]
    bs, ls = x_sample.shape[0], x_sample.shape[1]
    t_ffn0, t_ffn1 = 1024, 512

    mix_p, (kp, vp, scp) = _layer0_mixers(x_prompt, 0, jnp.zeros((bp, B_CONV - 1, B_WIDTH), F32), None, None,
                                          wts, (1024, 1024, CHUNK))
    mix_s, (ks, vs, scs) = _layer0_mixers(x_sample, PAST_LEN, state_sconv[0],
                                          cache_attn_k[0].reshape(bs, -1, KV_WIDTH),
                                          cache_attn_v[0].reshape(bs, -1, KV_WIDTH), wts, (bs * ls, ls, ls))
    xp, xs = _proj_ffn(mix_p, mix_s, None, wts["ab_w_out"], x_prompt, x_sample, g[0][1], g[0][2],
                       wts["wg"], wts["wu"], wts["wd"], 0, g[0][3], t_ffn0)

    mix_p, (ccp, ssp) = _layer1_mixer(xp, jnp.zeros((bp, SSM_CONV - 1, SSM_CONV_CH), F32),
                                      jnp.zeros((bp, SSM_HEADS, SSM_HEADDIM, SSM_STATE), F32), wts, (512, 512, CHUNK))
    mix_s, (ccs, sss) = _layer1_mixer(xs, state_ssm_conv[0], state_ssm[0], wts, (bs * ls, ls, ls))
    yp, ys = _proj_ffn(mix_p, mix_s, wts["ssm_norm_g"], wts["ssm_w_out"], xp, xs, g[1][1], g[1][2],
                       wts["wg"], wts["wu"], wts["wd"], 1, g[1][3], t_ffn1)

    lead = lambda a: a[None]
    return (yp, ys, lead(kp), lead(vp), lead(scp), lead(ccp), lead(ssp),
            lead(ks), lead(vs), lead(scs), lead(ccs), lead(sss))
```

```python
import functools

import jax
import jax.numpy as jnp
from jax import lax
from jax.experimental import pallas as pl
from jax.experimental.pallas import tpu as pltpu

F32 = jnp.float32
BF16 = jnp.bfloat16

D_MODEL = 1024
CHUNK = 64
EPS = 1e-6
LOG2E = 1.4426950408889634
PAST_LEN = 4096

A_HEADS = 8
A_KV_HEADS = 2
A_GROUP = A_HEADS // A_KV_HEADS
HEAD_DIM = 64
A_WIDTH = A_HEADS * HEAD_DIM
KV_WIDTH = A_KV_HEADS * HEAD_DIM
WINDOW = 128
ROPE_DIM = HEAD_DIM // 4
ROPE_HALF = ROPE_DIM // 2
ROPE_THETA = 500000.0
ATTN_SCALE = HEAD_DIM ** -0.5

B_WIDTH = D_MODEL // 2
B_CONV = 3
AB_IN = A_WIDTH + 2 * KV_WIDTH + 3 * B_WIDTH

SSM_INNER = 2 * D_MODEL
SSM_HEADDIM = 64
SSM_HEADS = SSM_INNER // SSM_HEADDIM
SSM_GROUPS = 4
SSM_GROUP_W = SSM_INNER // SSM_GROUPS
SSM_STATE = 128
SSM_CONV = 4
SSM_GN = SSM_GROUPS * SSM_STATE
SSM_CONV_CH = SSM_INNER + 2 * SSM_GN

D_FF = -(-8 * D_MODEL // (3 * 256)) * 256

LANES = 128
SUBLANES = 8
BF16_SUBLANES = 16
MXU_DIM = 256
DT_REP = LANES // SSM_HEADS
VMEM_LIMIT = 56 * 1024 * 1024
COL_CHUNK = MXU_DIM
FF_CHUNKS = (6 * MXU_DIM, 5 * MXU_DIM)
assert sum(FF_CHUNKS) == D_FF


def _rms(x, g):
    return x * lax.rsqrt(jnp.mean(x * x, axis=-1, keepdims=True) + EPS) * g


def _dot(a, b):
    return jnp.dot(a, b, preferred_element_type=F32)


def _dot_nt(a, b):
    return lax.dot_general(a, b, (((1,), (1,)), ((), ())), preferred_element_type=F32)


def _silu_of_twice(half):
    return half + half * jnp.tanh(half)


def _const_spec(shape):
    return pl.BlockSpec(shape, lambda *_: (0,) * len(shape), pipeline_mode=pl.Buffered(1))


def _layer_spec(shape, layer):
    return pl.BlockSpec((None,) + shape, lambda *_: (layer,) + (0,) * len(shape), pipeline_mode=pl.Buffered(1))


def _params():
    return pltpu.CompilerParams(dimension_semantics=("arbitrary", "arbitrary"),
                                vmem_limit_bytes=VMEM_LIMIT)


def _causal_conv(carry_ref, nh_ref, u, cw, cols):
    width = cw.shape[0]
    rows, ch = u.shape
    nseq = carry_ref.shape[0]
    u4 = u.reshape(nseq, rows // (nseq * SUBLANES), SUBLANES, ch)
    ext4 = jnp.concatenate([carry_ref[:, :, cols].reshape(nseq, 1, SUBLANES, ch), u4], axis=1)
    sub = lax.broadcasted_iota(jnp.int32, (1, 1, SUBLANES, ch), 2)

    def shift(above, here, k):
        return pltpu.roll(jnp.where(sub >= SUBLANES - k, above, here), k, 2)

    if width == 4:
        s1 = shift(jnp.concatenate([ext4[:, :1], ext4[:, :-1]], axis=1), ext4, 1)
        b = s1 * cw[0:1] + ext4 * cw[1:2]
        y4 = shift(b[:, :-1], b[:, 1:], 2) + (s1[:, 1:] * cw[2:3] + u4 * cw[3:4])
    else:
        y4 = None
        for j in range(width - 1):
            term = shift(ext4[:, :-1], u4, width - 1 - j) * cw[j:j + 1]
            y4 = term if y4 is None else y4 + term
        y4 = y4 + u4 * cw[width - 1:width]
    carry_ref[:, :, cols] = u4[:, -1]
    nh_ref[:, :, cols] = u4[:, -1, SUBLANES - (width - 1):]
    return y4.reshape(rows, ch)


def _rope_freq(shape):
    lane = lax.broadcasted_iota(jnp.int32, shape, 1)
    d = lane % HEAD_DIM
    expo = -((d % ROPE_HALF).astype(F32)) / ROPE_HALF
    freq = jnp.power(jnp.full(shape, ROPE_THETA, F32), expo)
    return jnp.where(d < ROPE_DIM, freq, 0.0), d


def _attn_in_kernel(pos0, tm, seq_rows, n_cast, *refs):
    x_ref, g_ref, w_ref, hist_ref, cw_ref = refs[:5]
    src_refs = refs[5:5 + n_cast]
    q_ref, k_ref, v_ref, gated_ref, nh_ref = refs[5 + n_cast:10 + n_cast]
    dst_refs = refs[10 + n_cast:10 + 2 * n_cast]
    cr_ref, sr_ref, carry_ref, p_ref = refs[10 + 2 * n_cast:]
    b = pl.program_id(0)
    i = pl.program_id(1)

    @pl.when((b == 0) & (i == 0))
    def _():
        freq, _ = _rope_freq((tm, LANES))
        ang = (lax.broadcasted_iota(jnp.int32, (tm, LANES), 0) % seq_rows).astype(F32) * freq
        cr_ref[...] = jnp.cos(ang)
        sr_ref[...] = jnp.sin(ang)

    @pl.when(i == 0)
    def _():
        carry_ref[:, SUBLANES - (B_CONV - 1):, :] = hist_ref[...]

    freq1, d1 = _rope_freq((1, LANES))
    base = (pos0 + i * tm).astype(F32) * freq1
    cb = jnp.cos(base)
    sb = jnp.sin(base)
    cr = cr_ref[...]
    sr = sr_ref[...]
    cos_t = cr * cb - sr * sb
    sin_t = sr * cb + cr * sb
    m_lo = jnp.where(d1 < ROPE_HALF, -1.0, 0.0)
    m_hi = jnp.where((d1 >= ROPE_HALF) & (d1 < ROPE_DIM), 1.0, 0.0)

    def rope(t):
        partner = pltpu.roll(t, LANES - ROPE_HALF, 1) * m_lo + pltpu.roll(t, ROPE_HALF, 1) * m_hi
        return t * cos_t + partner * sin_t

    h = _rms(x_ref[...], g_ref[...]).astype(BF16)
    o1 = A_WIDTH
    o2 = o1 + KV_WIDTH
    o3 = o2 + KV_WIDTH
    o4 = o3 + B_WIDTH
    o5 = o4 + B_WIDTH

    for n in range(AB_IN // MXU_DIM):
        cols = slice(n * MXU_DIM, (n + 1) * MXU_DIM)
        p_ref[:, cols] = _dot(h, w_ref[:, cols])

    u = p_ref[:, o4:o5] * p_ref[:, o5:]
    y = _causal_conv(carry_ref, nh_ref, u, cw_ref[...], slice(None))
    gated_ref[...] = (p_ref[:, o3:o4] * y).astype(BF16)
    k_ref[...] = rope(p_ref[:, o1:o2])
    v_ref[...] = p_ref[:, o2:o3]
    for s in range(A_WIDTH // LANES):
        q_ref[:, s * LANES:(s + 1) * LANES] = (rope(p_ref[:, s * LANES:(s + 1) * LANES]) * ATTN_SCALE).astype(BF16)

    for src, dst in zip(src_refs, dst_refs):
        dst[...] = src[...].astype(BF16)


def _tile_rows(x, tm):
    bsz, seq, d = x.shape
    nseq = max(tm // seq, 1)
    return x.reshape(bsz // nseq, nseq * seq, d), nseq


def _attn_in(x, g, w, hist, cw, pos0, tm, cast=()):
    bsz, seq, _ = x.shape
    xt, nseq = _tile_rows(x, tm)
    groups, rows, _ = xt.shape
    nt = rows // tm
    steps = groups * nt
    tok = lambda width: pl.BlockSpec((None, tm, width), lambda b, i: (b, i, 0))
    hist_spec = pl.BlockSpec((nseq, B_CONV - 1, B_WIDTH), lambda b, i: (b, 0, 0))
    flat = [a.reshape(-1, a.shape[-1]) for a in cast]
    for a in flat:
        assert a.shape[0] % (steps * BF16_SUBLANES) == 0, a.shape
    slab = lambda a: pl.BlockSpec((a.shape[0] // steps, a.shape[1]), lambda b, i: (b * nt + i, 0))
    n_main = 5
    outs = pl.pallas_call(
        functools.partial(_attn_in_kernel, pos0, tm, tm // nseq, len(flat)),
        grid=(groups, nt),
        in_specs=[tok(D_MODEL), _const_spec((1, D_MODEL)), _const_spec((D_MODEL, AB_IN)), hist_spec,
                  _const_spec((B_CONV, B_WIDTH))] + [slab(a) for a in flat],
        out_specs=[tok(A_WIDTH), tok(KV_WIDTH), tok(KV_WIDTH), tok(B_WIDTH), hist_spec] + [slab(a) for a in flat],
        out_shape=[jax.ShapeDtypeStruct((groups, rows, A_WIDTH), BF16),
                   jax.ShapeDtypeStruct((groups, rows, KV_WIDTH), F32),
                   jax.ShapeDtypeStruct((groups, rows, KV_WIDTH), F32),
                   jax.ShapeDtypeStruct((groups, rows, B_WIDTH), BF16),
                   jax.ShapeDtypeStruct((bsz, B_CONV - 1, B_WIDTH), F32)]
                  + [jax.ShapeDtypeStruct(a.shape, BF16) for a in flat],
        scratch_shapes=[pltpu.VMEM((tm, LANES), F32), pltpu.VMEM((tm, LANES), F32),
                        pltpu.VMEM((nseq, SUBLANES, B_WIDTH), F32), pltpu.VMEM((tm, AB_IN), F32)],
        compiler_params=_params(),
        name="attn_in_proj",
    )(xt, g, w, hist, cw, *flat)
    main = [o.reshape(bsz, seq, o.shape[-1]) for o in outs[:n_main - 1]] + [outs[n_main - 1]]
    return main, [o.reshape(a.shape) for o, a in zip(outs[n_main:], cast)]


def _attn_kernel(tq, cq, masked, sinks_ref, q_ref, kc_ref, vc_ref, kp_ref, vp_ref, att_ref,
                 kk_ref, vx_ref, s_ref, e_ref):
    i = pl.program_id(1)
    nk = WINDOW + cq
    nkp = MXU_DIM
    rows_kv = WINDOW + tq
    hd = HEAD_DIM

    kk_ref[0:WINDOW, :] = kp_ref[...].astype(BF16)
    kk_ref[WINDOW:rows_kv, :] = kc_ref[...].astype(BF16)
    kk_ref[rows_kv:, :] = jnp.zeros((kk_ref.shape[0] - rows_kv, KV_WIDTH), BF16)
    vall = jnp.concatenate([vp_ref[...], vc_ref[...]], axis=0)
    vx_ref[0:rows_kv, :] = jnp.concatenate([vall[:, :hd], vall[:, :hd], vall[:, hd:], vall[:, hd:]], axis=1)
    vx_ref[rows_kv:, :] = jnp.zeros((vx_ref.shape[0] - rows_kv, 2 * LANES), F32)

    col = lax.broadcasted_iota(jnp.int32, (A_GROUP * cq, nkp), 1)
    krow = lax.broadcasted_iota(jnp.int32, (nkp, LANES), 0)
    ones = jnp.ones((nkp, LANES), BF16)
    bodies = [(c, kh) for c in range(tq // cq) for kh in range(A_KV_HEADS)]
    fills = []
    for kh in range(A_KV_HEADS):
        sink = jnp.concatenate(
            [jnp.full((cq, nkp), sinks_ref[kh * A_GROUP + g], F32) for g in range(A_GROUP)], axis=0)
        fills.append(jnp.where(col == nk, sink, -jnp.inf))

    for n, (c, kh) in enumerate(bodies):
        r0 = c * cq
        heads = [kh * A_GROUP + g for g in range(A_GROUP)]
        kt = kk_ref[r0:r0 + nkp, kh * hd:(kh + 1) * hd]
        qs = jnp.concatenate([q_ref[r0:r0 + cq, hh * hd:(hh + 1) * hd] for hh in heads], axis=0)
        s = _dot_nt(qs, kt)
        valid = col < nk
        if masked and r0 < WINDOW:
            valid = valid & ((col >= WINDOW - r0) | (i > 0))
        s_ref[n] = jnp.where(valid, s, fills[kh])

    for n in range(len(bodies)):
        s = s_ref[n]
        m = jnp.max(s, axis=-1, keepdims=True)
        e_ref[n] = jnp.exp(s - m).astype(BF16)

    for n, (c, kh) in enumerate(bodies):
        r0 = c * cq
        vt = jnp.where(krow == nk, 0.0, vx_ref[r0:r0 + nkp, kh * LANES:(kh + 1) * LANES]).astype(BF16)
        ox = _dot(e_ref[n], jnp.concatenate([vt, ones], axis=1))
        o = ox[:, :LANES] / ox[:, LANES:]
        for g in range(A_GROUP):
            hh = kh * A_GROUP + g
            lanes = slice((hh % 2) * hd, (hh % 2 + 1) * hd)
            att_ref[r0:r0 + cq, hh * hd:(hh + 1) * hd] = o[g * cq:(g + 1) * cq, lanes].astype(BF16)


def _attn(sinks, q, k, v, k_prev, v_prev, tq, cq, masked):
    bsz, seq, _ = q.shape
    grid = (bsz, seq // tq)
    tok = lambda width: pl.BlockSpec((None, tq, width), lambda b, i: (b, i, 0))
    if masked:
        blocks_per_tile = tq // WINDOW
        prev = pl.BlockSpec((None, WINDOW, KV_WIDTH),
                            lambda b, i: (b, jnp.maximum(i * blocks_per_tile - 1, 0), 0))
    else:
        prev = pl.BlockSpec((None, WINDOW, KV_WIDTH), lambda b, i: (b, 0, 0))
    n_bodies = (tq // cq) * A_KV_HEADS
    kv_rows = (tq // cq - 1) * cq + MXU_DIM
    return pl.pallas_call(
        functools.partial(_attn_kernel, tq, cq, masked),
        grid=grid,
        in_specs=[pl.BlockSpec(memory_space=pltpu.SMEM),
                  tok(A_WIDTH), tok(KV_WIDTH), tok(KV_WIDTH), prev, prev],
        out_specs=tok(A_WIDTH),
        out_shape=jax.ShapeDtypeStruct((bsz, seq, A_WIDTH), BF16),
        scratch_shapes=[pltpu.VMEM((kv_rows, KV_WIDTH), BF16), pltpu.VMEM((kv_rows, 2 * LANES), F32),
                        pltpu.VMEM((n_bodies, A_GROUP * cq, MXU_DIM), F32),
                        pltpu.VMEM((n_bodies, A_GROUP * cq, MXU_DIM), BF16)],
        compiler_params=_params(),
        name="band_attention",
    )(sinks, q, k, v, k_prev, v_prev)


def _proj_ffn_kernel(n_in, gate, n_tiles, *refs):
    long_in, short_in = refs[:n_in], refs[n_in:2 * n_in]
    refs = refs[2 * n_in:]
    ng_ref = None
    if gate:
        ng_ref, refs = refs[0], refs[1:]
    wo_ref, xl_ref, xs_ref, g1_ref, g2_ref, wg_ref, wu_ref, wd_ref, g3_ref, ol_ref, os_ref = refs
    t = pl.program_id(0)

    def tile(ins, x_ref, o_ref):
        acc = None
        if gate:
            y_ref, z_ref = ins
            for gi in range(SSM_GROUPS):
                gcols = slice(gi * SSM_GROUP_W, (gi + 1) * SSM_GROUP_W)
                gg = y_ref[:, gcols].astype(F32) * z_ref[:, gcols].astype(F32)
                gg = gg * lax.rsqrt(jnp.mean(gg * gg, axis=-1, keepdims=True) + EPS)
                part = _dot((gg * ng_ref[:, gcols]).astype(BF16), wo_ref[gcols, :])
                acc = part if acc is None else acc + part
        else:
            off = 0
            for a_ref in ins:
                kdim = a_ref.shape[-1]
                part = _dot(a_ref[...], wo_ref[off:off + kdim, :])
                acc = part if acc is None else acc + part
                off += kdim

        x = x_ref[...] + _rms(acc, g1_ref[...])
        h = _rms(x, g2_ref[...]).astype(BF16)
        h_half = h * 0.5
        f = None
        off = 0
        for width in FF_CHUNKS:
            fc = slice(off, off + width)
            m = (_silu_of_twice(_dot(h_half, wg_ref[:, fc])) * _dot(h, wu_ref[:, fc])).astype(BF16)
            part = _dot(m, wd_ref[fc, :])
            f = part if f is None else f + part
            off += width
        o_ref[...] = x + _rms(f, g3_ref[...])

    pl.when(t < n_tiles)(lambda: tile(long_in, xl_ref, ol_ref))
    pl.when(t == n_tiles)(lambda: tile(short_in, xs_ref, os_ref))


def _proj_ffn(mix_long, mix_short, ng, wo, x_long, x_short, g1, g2, wg, wu, wd, layer, g3, tm):
    bsz, seq, _ = x_long.shape
    tiles_per_seq = seq // tm
    n_tiles = bsz * tiles_per_seq
    n_short = x_short.shape[0] * x_short.shape[1]
    flat = lambda a: a.reshape(1, n_short, a.shape[-1])
    tile_idx = lambda t: jnp.minimum(t, n_tiles - 1)
    ltok = lambda width: pl.BlockSpec((None, tm, width),
                                      lambda t: (tile_idx(t) // tiles_per_seq, tile_idx(t) % tiles_per_seq, 0))
    stok = lambda width: pl.BlockSpec((None, n_short, width), lambda t: (0, 0, 0))
    gate = ng is not None
    ng_args, ng_specs = ([ng], [_const_spec((1, SSM_INNER))]) if gate else ([], [])
    out_long, out_short = pl.pallas_call(
        functools.partial(_proj_ffn_kernel, len(mix_long), gate, n_tiles),
        grid=(n_tiles + 1,),
        in_specs=[ltok(a.shape[-1]) for a in mix_long] + [stok(a.shape[-1]) for a in mix_short] + ng_specs + [
            _const_spec(wo.shape), ltok(D_MODEL), stok(D_MODEL), _const_spec((1, D_MODEL)), _const_spec((1, D_MODEL)),
            _layer_spec((D_MODEL, D_FF), layer), _layer_spec((D_MODEL, D_FF), layer),
            _layer_spec((D_FF, D_MODEL), layer), _const_spec((1, D_MODEL))],
        out_specs=[ltok(D_MODEL), stok(D_MODEL)],
        out_shape=[jax.ShapeDtypeStruct(x_long.shape, F32), jax.ShapeDtypeStruct((1, n_short, D_MODEL), F32)],
        compiler_params=pltpu.CompilerParams(dimension_semantics=("arbitrary",), vmem_limit_bytes=VMEM_LIMIT),
        name="out_proj_ffn",
    )(*mix_long, *[flat(a) for a in mix_short], *ng_args, wo, x_long, flat(x_short), g1, g2, wg, wu, wd, g3)
    return out_long, out_short.reshape(x_short.shape)


def _split3(v):
    t1 = v.astype(BF16)
    r = v - t1.astype(F32)
    t2 = r.astype(BF16)
    t3 = (r - t2.astype(F32)).astype(BF16)
    return t1, t2, t3


def _pack3(v):
    t1, t2, t3 = _split3(v)
    lane = lax.broadcasted_iota(jnp.int32, v.shape, 1)
    return jnp.where(lane < SSM_HEADS, t1,
                     jnp.where(lane < 2 * SSM_HEADS, t2,
                               jnp.where(lane < 3 * SSM_HEADS, t3, jnp.zeros_like(t3))))


def _ssm_in_kernel(tm, cl, x_ref, g_ref, w_ref, wdt_ref, hist_ref, cw_ref, cb_ref, dtb_ref, alog_ref,
                   z_ref, xs_ref, bm_ref, cm_ref, ta_ref, tdt_ref, nh_ref, carry_ref):
    i = pl.program_id(1)

    @pl.when(i == 0)
    def _():
        carry_ref[:, SUBLANES - (SSM_CONV - 1):, :] = hist_ref[...]

    h = _rms(x_ref[...], g_ref[...]).astype(BF16)

    dt = jax.nn.softplus(_dot(h, wdt_ref[...]) + dtb_ref[...])
    tdt_ref[...] = _pack3(dt)
    d1, d2, d3 = _split3(dt * (-jnp.exp(alog_ref[...]) * LOG2E))
    tcol = lax.broadcasted_iota(jnp.int32, (cl, 3 * cl), 1) % cl
    trow = lax.broadcasted_iota(jnp.int32, (cl, 3 * cl), 0)
    tril = jnp.where(tcol <= trow, 1.0, 0.0).astype(BF16)
    acum = []
    for c in range(tm // cl):
        rows = slice(c * cl, (c + 1) * cl)
        acum.append(_dot(tril, jnp.concatenate([d1[rows], d2[rows], d3[rows]], axis=0)))
    ta_ref[...] = _pack3(jnp.concatenate(acum, axis=0))

    h_half = h * 0.5
    for n in range(SSM_INNER // COL_CHUNK):
        cols = slice(n * COL_CHUNK, (n + 1) * COL_CHUNK)
        z_ref[:, cols] = _silu_of_twice(_dot(h_half, w_ref[:, cols])).astype(BF16)

    for n in range(SSM_CONV_CH // COL_CHUNK):
        cols = slice(n * COL_CHUNK, (n + 1) * COL_CHUNK)
        raw = _dot(h, w_ref[:, SSM_INNER + cols.start:SSM_INNER + cols.stop])
        y = _silu_of_twice(_causal_conv(carry_ref, nh_ref, raw, 0.5 * cw_ref[:, cols], cols) + 0.5 * cb_ref[:, cols])
        if cols.stop <= SSM_INNER:
            xs_ref[:, cols] = y
        elif cols.stop <= SSM_INNER + SSM_GN:
            bm_ref[:, cols.start - SSM_INNER:cols.stop - SSM_INNER] = y.astype(BF16)
        else:
            lo = SSM_INNER + SSM_GN
            cm_ref[:, cols.start - lo:cols.stop - lo] = y.astype(BF16)


def _ssm_in(x, g, w, wdt, hist, cw, cb, dtb, alog, tm, cl):
    bsz, seq, _ = x.shape
    xt, nseq = _tile_rows(x, tm)
    groups, rows, _ = xt.shape
    tok = lambda width: pl.BlockSpec((None, tm, width), lambda b, i: (b, i, 0))
    hist_spec = pl.BlockSpec((nseq, SSM_CONV - 1, SSM_CONV_CH), lambda b, i: (b, 0, 0))
    outs = pl.pallas_call(
        functools.partial(_ssm_in_kernel, tm, cl),
        grid=(groups, rows // tm),
        in_specs=[tok(D_MODEL), _const_spec((1, D_MODEL)), _const_spec(w.shape), _const_spec((D_MODEL, LANES)),
                  hist_spec,
                  _const_spec((SSM_CONV, SSM_CONV_CH)), _const_spec((1, SSM_CONV_CH)), _const_spec((1, LANES)),
                  _const_spec((1, LANES))],
        out_specs=[tok(SSM_INNER), tok(SSM_INNER), tok(SSM_GN), tok(SSM_GN), tok(LANES), tok(LANES), hist_spec],
        out_shape=[jax.ShapeDtypeStruct((groups, rows, SSM_INNER), BF16),
                   jax.ShapeDtypeStruct((groups, rows, SSM_INNER), F32),
                   jax.ShapeDtypeStruct((groups, rows, SSM_GN), BF16),
                   jax.ShapeDtypeStruct((groups, rows, SSM_GN), BF16),
                   jax.ShapeDtypeStruct((groups, rows, LANES), BF16),
                   jax.ShapeDtypeStruct((groups, rows, LANES), BF16),
                   jax.ShapeDtypeStruct((bsz, SSM_CONV - 1, SSM_CONV_CH), F32)],
        scratch_shapes=[pltpu.VMEM((nseq, SUBLANES, SSM_CONV_CH), F32)],
        compiler_params=_params(),
        name="ssm_in_proj",
    )(xt, g, w, wdt, hist, cw, cb, dtb, alog)
    return [o.reshape(bsz, seq, o.shape[-1]) for o in outs[:-1]] + [outs[-1]]


def _pad_rows(a, rows):
    if a.shape[0] == rows:
        return a
    return jnp.concatenate([a, jnp.zeros((rows - a.shape[0], a.shape[1]), a.dtype)], axis=0)


def _ssd_kernel(ts, cl, xs_ref, bm_ref, cm_ref, ta_ref, tdt_ref, h0_ref, dskip_ref,
                yo_ref, hout_ref, ht_ref, y_ref, e3_ref, aexp_ref, xdt_ref, st_ref, cb_ref, bmt_ref, w2_ref, rhs_ref):
    b = pl.program_id(0)
    i = pl.program_id(1)
    p = SSM_HEADDIM
    nc = ts // cl
    pairs = SSM_GROUP_W // LANES

    @pl.when((b == 0) & (i == 0))
    def _():
        kk = lax.broadcasted_iota(jnp.int32, (LANES, SSM_INNER), 0)
        cc = lax.broadcasted_iota(jnp.int32, (LANES, SSM_INNER), 1)
        hit = (kk % SSM_HEADS == cc // p) & (kk < 3 * SSM_HEADS)
        e3_ref[...] = jnp.where(hit, 1.0, 0.0).astype(BF16)

    @pl.when(i == 0)
    def _():
        ht_ref[...] = h0_ref[...].T

    lane = lax.broadcasted_iota(jnp.int32, (cl, LANES), 1)
    row = lax.broadcasted_iota(jnp.int32, (cl, LANES), 0)
    causal2 = (lane % p) <= row
    left = lane.astype(F32).astype(BF16) < p
    diag2 = (lane % p) == row
    groups = [(gi, slice(gi * SSM_GROUP_W, (gi + 1) * SSM_GROUP_W), slice(gi * SSM_STATE, (gi + 1) * SSM_STATE))
              for gi in range(SSM_GROUPS)]
    chunks = [(c, slice(c * cl, (c + 1) * cl)) for c in range(nc)]

    for gi, gcols, _ in groups:
        e3g = e3_ref[:, gcols]
        aexp_ref[:, gcols] = _dot(ta_ref[...], e3g)
        xdt_ref[:, gcols] = (xs_ref[:, gcols] * _dot(tdt_ref[...], e3g)).astype(BF16)

    for c, rows in chunks:
        for gi, _, scol in groups:
            bmp = _pad_rows(bm_ref[rows, scol], p)
            cbm = _dot_nt(cm_ref[rows, scol], bmp)
            cb_ref[c * SSM_GROUPS + gi] = jnp.concatenate([cbm, cbm], axis=1)
            bmt_ref[c * SSM_GROUPS + gi] = bmp.astype(F32).T.astype(BF16)

    for c, rows in chunks:
        for gi, _, _ in groups:
            for jj in range(pairs):
                idx = (c * SSM_GROUPS + gi) * pairs + jj
                cols = slice(gi * SSM_GROUP_W + jj * LANES, gi * SSM_GROUP_W + (jj + 1) * LANES)
                a_pair = aexp_ref[rows, cols]
                a_src = jnp.sum(jnp.where(diag2, a_pair, 0.0), axis=0, keepdims=True)
                seg = a_pair - a_src
                dec = jnp.exp2(jnp.where(causal2, seg, -jnp.inf))
                w2_ref[idx] = (dec * cb_ref[c * SSM_GROUPS + gi]).astype(BF16)
                x2 = xdt_ref[rows, cols]
                top = _pad_rows(jnp.where(left, x2, jnp.zeros_like(x2)), p)
                bot = _pad_rows(jnp.where(left, jnp.zeros_like(x2), x2), p)
                rhs_ref[idx] = jnp.concatenate([top, bot], axis=0)

    for c, rows in chunks:
        for gi, _, _ in groups:
            for jj in range(pairs):
                idx = (c * SSM_GROUPS + gi) * pairs + jj
                cols = slice(gi * SSM_GROUP_W + jj * LANES, gi * SSM_GROUP_W + (jj + 1) * LANES)
                y_ref[rows, cols] = _dot(w2_ref[idx], rhs_ref[idx])

    for c, rows in chunks:
        for gi, gcols, _ in groups:
            a_g = aexp_ref[rows, gcols]
            xw = xdt_ref[rows, gcols] * jnp.exp2(a_g[cl - 1:cl, :] - a_g).astype(BF16)
            st_ref[c, :, gcols] = _dot(bmt_ref[c * SSM_GROUPS + gi], _pad_rows(xw, p))

    for c, rows in chunks:
        for gi, gcols, scol in groups:
            hprev = ht_ref[:, gcols]
            a_g = aexp_ref[rows, gcols]
            y_off = _dot(cm_ref[rows, scol], hprev.astype(BF16)) * jnp.exp2(a_g)
            y_ref[rows, gcols] = y_ref[rows, gcols] + y_off
            ht_ref[:, gcols] = hprev * jnp.exp2(a_g[cl - 1:cl, :]) + st_ref[c, :, gcols]

    yo_ref[...] = (y_ref[...] + dskip_ref[...] * xs_ref[...]).astype(BF16)

    @pl.when(i == pl.num_programs(1) - 1)
    def _():
        hout_ref[...] = ht_ref[...].T


def _ssd(xs, bm, cm, ta, tdt, h0, dskip, ts, cl):
    bsz, seq, _ = xs.shape
    tok = lambda width: pl.BlockSpec((None, ts, width), lambda b, i: (b, i, 0))
    st_spec = pl.BlockSpec((None, SSM_INNER, SSM_STATE), lambda b, i: (b, 0, 0))
    nc = ts // cl
    n_cg = nc * SSM_GROUPS
    n_pairs = n_cg * (SSM_GROUP_W // LANES)
    return pl.pallas_call(
        functools.partial(_ssd_kernel, ts, cl),
        grid=(bsz, seq // ts),
        in_specs=[tok(SSM_INNER), tok(SSM_GN), tok(SSM_GN), tok(LANES), tok(LANES), st_spec,
                  _const_spec((1, SSM_INNER))],
        out_specs=[tok(SSM_INNER), st_spec],
        out_shape=[jax.ShapeDtypeStruct((bsz, seq, SSM_INNER), BF16),
                   jax.ShapeDtypeStruct((bsz, SSM_INNER, SSM_STATE), F32)],
        scratch_shapes=[pltpu.VMEM((SSM_STATE, SSM_INNER), F32), pltpu.VMEM((ts, SSM_INNER), F32),
                        pltpu.VMEM((LANES, SSM_INNER), BF16), pltpu.VMEM((ts, SSM_INNER), F32),
                        pltpu.VMEM((ts, SSM_INNER), BF16), pltpu.VMEM((nc, SSM_STATE, SSM_INNER), F32),
                        pltpu.VMEM((n_cg, cl, LANES), F32), pltpu.VMEM((n_cg, SSM_STATE, SSM_HEADDIM), BF16),
                        pltpu.VMEM((n_pairs, cl, LANES), BF16), pltpu.VMEM((n_pairs, LANES, LANES), BF16)],
        compiler_params=_params(),
        name="ssd_scan",
    )(xs, bm, cm, ta, tdt, h0, dskip)


def _layer0_mixers(x, pos0, sconv_hist, k_cache, v_cache, wts, tiles, cast=()):
    t_in, t_att, c_att = tiles
    (q, k, v, gated, new_sconv), casted = _attn_in(x, wts["g"][0][0], wts["ab_w_in"], sconv_hist, wts["sconv_w"],
                                                   pos0, t_in, cast)
    masked = k_cache is None
    k_prev, v_prev = (k, v) if masked else (k_cache, v_cache)
    att = _attn(wts["sinks"], q, k, v, k_prev, v_prev, t_att, c_att, masked)
    bsz, seq, _ = x.shape
    new_k = k[:, seq - min(WINDOW, seq):].reshape(bsz, -1, A_KV_HEADS, HEAD_DIM)
    new_v = v[:, seq - min(WINDOW, seq):].reshape(bsz, -1, A_KV_HEADS, HEAD_DIM)
    return [att, gated], (new_k, new_v, new_sconv), casted


def _layer1_mixer(x, conv_hist, ssm_state, wts, tiles):
    t_sin, t_ssd, c_ssd = tiles
    bsz = x.shape[0]
    z, xs, bm, cm, ta, tdt, new_conv = _ssm_in(
        x, wts["g"][1][0], wts["ssm_w_in"], wts["w_dt"], conv_hist, wts["ssm_conv_w"],
        wts["ssm_conv_b"], wts["dt_bias"], wts["a_log"], t_sin, c_ssd)
    y, new_state = _ssd(xs, bm, cm, ta, tdt, ssm_state.reshape(bsz, SSM_INNER, SSM_STATE),
                        wts["d_skip"], t_ssd, c_ssd)
    return [y, z], (new_conv, new_state.reshape(bsz, SSM_HEADS, SSM_HEADDIM, SSM_STATE))


def kernel(x_prompt, x_sample, cache_attn_k, cache_attn_v, state_sconv, state_ssm_conv, state_ssm, norm_g,
           ab_w_in, ab_w_out, attn_sinks, sconv_w, ssm_w_in, ssm_conv_w, ssm_conv_b, ssm_dt_bias, ssm_a_log,
           ssm_d, ssm_norm_g, ssm_w_out, ffn_w_gate, ffn_w_up, ffn_w_down):
    w_in1 = ssm_w_in[0]
    wts = {
        "ab_w_in": ab_w_in[0].astype(BF16),
        "sinks": attn_sinks[0],
        "sconv_w": sconv_w[0],
        "ssm_w_in": w_in1.astype(BF16),
        "w_dt": jnp.tile(w_in1[:, SSM_INNER + SSM_CONV_CH:], (1, DT_REP)).astype(BF16),
        "ssm_conv_w": ssm_conv_w[0],
        "ssm_conv_b": ssm_conv_b[0].reshape(1, -1),
        "dt_bias": jnp.tile(ssm_dt_bias[0], DT_REP).reshape(1, -1),
        "a_log": jnp.tile(ssm_a_log[0], DT_REP).reshape(1, -1),
        "d_skip": jnp.repeat(ssm_d[0], SSM_HEADDIM).reshape(1, -1),
        "ssm_norm_g": ssm_norm_g[0].reshape(1, -1),
    }
    wts["g"] = [[norm_g[l, j].reshape(1, -1) for j in range(4)] for l in range(norm_g.shape[0])]
    g = wts["g"]
    bp = x_prompt.shape[0]
    bs, ls = x_sample.shape[0], x_sample.shape[1]
    t_ffn0, t_ffn1 = 1024, 512

    later = (ab_w_out[0], ssm_w_out[0], ffn_w_gate, ffn_w_up, ffn_w_down)
    mix_p, (kp, vp, scp), casted = _layer0_mixers(x_prompt, 0, jnp.zeros((bp, B_CONV - 1, B_WIDTH), F32), None, None,
                                                  wts, (1024, 1024, CHUNK), later)
    wts["ab_w_out"], wts["ssm_w_out"], wts["wg"], wts["wu"], wts["wd"] = casted
    mix_s, (ks, vs, scs), _ = _layer0_mixers(x_sample, PAST_LEN, state_sconv[0],
                                             cache_attn_k[0].reshape(bs, -1, KV_WIDTH),
                                             cache_attn_v[0].reshape(bs, -1, KV_WIDTH), wts, (bs * ls, ls, ls))
    xp, xs = _proj_ffn(mix_p, mix_s, None, wts["ab_w_out"], x_prompt, x_sample, g[0][1], g[0][2],
                       wts["wg"], wts["wu"], wts["wd"], 0, g[0][3], t_ffn0)

    mix_p, (ccp, ssp) = _layer1_mixer(xp, jnp.zeros((bp, SSM_CONV - 1, SSM_CONV_CH), F32),
                                      jnp.zeros((bp, SSM_HEADS, SSM_HEADDIM, SSM_STATE), F32), wts, (512, 512, CHUNK))
    mix_s, (ccs, sss) = _layer1_mixer(xs, state_ssm_conv[0], state_ssm[0], wts, (bs * ls, ls, ls))
    yp, ys = _proj_ffn(mix_p, mix_s, wts["ssm_norm_g"], wts["ssm_w_out"], xp, xs, g[1][1], g[1][2],
                       wts["wg"], wts["wu"], wts["wd"], 1, g[1][3], t_ffn1)

    lead = lambda a: a[None]
    return (yp, ys, lead(kp), lead(vp), lead(scp), lead(ccp), lead(ssp),
            lead(ks), lead(vs), lead(scs), lead(ccs), lead(sss))
```

```python
import functools

import jax
import jax.numpy as jnp
from jax import lax
from jax.experimental import pallas as pl
from jax.experimental.pallas import tpu as pltpu

F32 = jnp.float32
BF16 = jnp.bfloat16

D_MODEL = 1024
CHUNK = 64
EPS = 1e-6
LOG2E = 1.4426950408889634
PAST_LEN = 4096

A_HEADS = 8
A_KV_HEADS = 2
A_GROUP = A_HEADS // A_KV_HEADS
HEAD_DIM = 64
A_WIDTH = A_HEADS * HEAD_DIM
KV_WIDTH = A_KV_HEADS * HEAD_DIM
WINDOW = 128
ROPE_DIM = HEAD_DIM // 4
ROPE_HALF = ROPE_DIM // 2
ROPE_THETA = 500000.0
ATTN_SCALE = HEAD_DIM ** -0.5

B_WIDTH = D_MODEL // 2
B_CONV = 3
AB_IN = A_WIDTH + 2 * KV_WIDTH + 3 * B_WIDTH

SSM_INNER = 2 * D_MODEL
SSM_HEADDIM = 64
SSM_HEADS = SSM_INNER // SSM_HEADDIM
SSM_GROUPS = 4
SSM_GROUP_W = SSM_INNER // SSM_GROUPS
SSM_STATE = 128
SSM_CONV = 4
SSM_GN = SSM_GROUPS * SSM_STATE
SSM_CONV_CH = SSM_INNER + 2 * SSM_GN

D_FF = -(-8 * D_MODEL // (3 * 256)) * 256

LANES = 128
SUBLANES = 8
BF16_SUBLANES = 16
MXU_DIM = 256
DT_REP = LANES // SSM_HEADS
VMEM_LIMIT = 56 * 1024 * 1024
COL_CHUNK = MXU_DIM
FF_CHUNKS = (6 * MXU_DIM, 5 * MXU_DIM)
assert sum(FF_CHUNKS) == D_FF


def _rms(x, g):
    return x * lax.rsqrt(jnp.mean(x * x, axis=-1, keepdims=True) + EPS) * g


def _dot(a, b):
    return jnp.dot(a, b, preferred_element_type=F32)


def _dot_nt(a, b):
    return lax.dot_general(a, b, (((1,), (1,)), ((), ())), preferred_element_type=F32)


def _silu_of_twice(half):
    return half + half * jnp.tanh(half)


def _const_spec(shape):
    return pl.BlockSpec(shape, lambda *_: (0,) * len(shape), pipeline_mode=pl.Buffered(1))


def _layer_spec(shape, layer):
    return pl.BlockSpec((None,) + shape, lambda *_: (layer,) + (0,) * len(shape), pipeline_mode=pl.Buffered(1))


def _params():
    return pltpu.CompilerParams(dimension_semantics=("arbitrary", "arbitrary"),
                                vmem_limit_bytes=VMEM_LIMIT)


def _causal_conv(carry_ref, nh_ref, u, cw, cols):
    width = cw.shape[0]
    rows, ch = u.shape
    nseq = carry_ref.shape[0]
    u4 = u.reshape(nseq, rows // (nseq * SUBLANES), SUBLANES, ch)
    ext4 = jnp.concatenate([carry_ref[:, :, cols].reshape(nseq, 1, SUBLANES, ch), u4], axis=1)
    sub = lax.broadcasted_iota(jnp.int32, (1, 1, SUBLANES, ch), 2)

    def shift(above, here, k):
        return pltpu.roll(jnp.where(sub >= SUBLANES - k, above, here), k, 2)

    if width == 4:
        s1 = shift(jnp.concatenate([ext4[:, :1], ext4[:, :-1]], axis=1), ext4, 1)
        b = s1 * cw[0:1] + ext4 * cw[1:2]
        y4 = shift(b[:, :-1], b[:, 1:], 2) + (s1[:, 1:] * cw[2:3] + u4 * cw[3:4])
    else:
        y4 = None
        for j in range(width - 1):
            term = shift(ext4[:, :-1], u4, width - 1 - j) * cw[j:j + 1]
            y4 = term if y4 is None else y4 + term
        y4 = y4 + u4 * cw[width - 1:width]
    carry_ref[:, :, cols] = u4[:, -1]
    nh_ref[:, :, cols] = u4[:, -1, SUBLANES - (width - 1):]
    return y4.reshape(rows, ch)


def _rope_freq(shape):
    lane = lax.broadcasted_iota(jnp.int32, shape, 1)
    d = lane % HEAD_DIM
    expo = -((d % ROPE_HALF).astype(F32)) / ROPE_HALF
    freq = jnp.power(jnp.full(shape, ROPE_THETA, F32), expo)
    return jnp.where(d < ROPE_DIM, freq, 0.0), d


def _attn_in_kernel(pos0, tm, seq_rows, n_cast, *refs):
    x_ref, g_ref, w_ref, hist_ref, cw_ref = refs[:5]
    src_refs = refs[5:5 + n_cast]
    q_ref, k_ref, v_ref, gated_ref, nh_ref = refs[5 + n_cast:10 + n_cast]
    dst_refs = refs[10 + n_cast:10 + 2 * n_cast]
    cr_ref, sr_ref, carry_ref, p_ref = refs[10 + 2 * n_cast:]
    b = pl.program_id(0)
    i = pl.program_id(1)

    @pl.when((b == 0) & (i == 0))
    def _():
        freq, _ = _rope_freq((tm, LANES))
        ang = (lax.broadcasted_iota(jnp.int32, (tm, LANES), 0) % seq_rows).astype(F32) * freq
        cr_ref[...] = jnp.cos(ang)
        sr_ref[...] = jnp.sin(ang)

    @pl.when(i == 0)
    def _():
        carry_ref[:, SUBLANES - (B_CONV - 1):, :] = hist_ref[...]

    freq1, d1 = _rope_freq((1, LANES))
    base = (pos0 + i * tm).astype(F32) * freq1
    cb = jnp.cos(base)
    sb = jnp.sin(base)
    cr = cr_ref[...]
    sr = sr_ref[...]
    cos_t = cr * cb - sr * sb
    sin_t = sr * cb + cr * sb
    m_lo = jnp.where(d1 < ROPE_HALF, -1.0, 0.0)
    m_hi = jnp.where((d1 >= ROPE_HALF) & (d1 < ROPE_DIM), 1.0, 0.0)

    def rope(t):
        partner = pltpu.roll(t, LANES - ROPE_HALF, 1) * m_lo + pltpu.roll(t, ROPE_HALF, 1) * m_hi
        return t * cos_t + partner * sin_t

    h = _rms(x_ref[...], g_ref[...]).astype(BF16)
    o1 = A_WIDTH
    o2 = o1 + KV_WIDTH
    o3 = o2 + KV_WIDTH
    o4 = o3 + B_WIDTH
    o5 = o4 + B_WIDTH

    for n in range(AB_IN // MXU_DIM):
        cols = slice(n * MXU_DIM, (n + 1) * MXU_DIM)
        p_ref[:, cols] = _dot(h, w_ref[:, cols])

    u = p_ref[:, o4:o5] * p_ref[:, o5:]
    y = _causal_conv(carry_ref, nh_ref, u, cw_ref[...], slice(None))
    gated_ref[...] = (p_ref[:, o3:o4] * y).astype(BF16)
    k_ref[...] = rope(p_ref[:, o1:o2])
    v_ref[...] = p_ref[:, o2:o3]
    for s in range(A_WIDTH // LANES):
        q_ref[:, s * LANES:(s + 1) * LANES] = (rope(p_ref[:, s * LANES:(s + 1) * LANES]) * ATTN_SCALE).astype(BF16)

    for src, dst in zip(src_refs, dst_refs):
        dst[...] = src[...].astype(BF16)


def _tile_rows(x, tm):
    bsz, seq, d = x.shape
    nseq = max(tm // seq, 1)
    return x.reshape(bsz // nseq, nseq * seq, d), nseq


def _attn_in(x, g, w, hist, cw, pos0, tm, cast=()):
    bsz, seq, _ = x.shape
    xt, nseq = _tile_rows(x, tm)
    groups, rows, _ = xt.shape
    nt = rows // tm
    steps = groups * nt
    tok = lambda width: pl.BlockSpec((None, tm, width), lambda b, i: (b, i, 0))
    hist_spec = pl.BlockSpec((nseq, B_CONV - 1, B_WIDTH), lambda b, i: (b, 0, 0))
    flat = [a.reshape(-1, a.shape[-1]) for a in cast]
    for a in flat:
        assert a.shape[0] % (steps * BF16_SUBLANES) == 0, a.shape
    slab = lambda a: pl.BlockSpec((a.shape[0] // steps, a.shape[1]), lambda b, i: (b * nt + i, 0))
    n_main = 5
    outs = pl.pallas_call(
        functools.partial(_attn_in_kernel, pos0, tm, tm // nseq, len(flat)),
        grid=(groups, nt),
        in_specs=[tok(D_MODEL), _const_spec((1, D_MODEL)), _const_spec((D_MODEL, AB_IN)), hist_spec,
                  _const_spec((B_CONV, B_WIDTH))] + [slab(a) for a in flat],
        out_specs=[tok(A_WIDTH), tok(KV_WIDTH), tok(KV_WIDTH), tok(B_WIDTH), hist_spec] + [slab(a) for a in flat],
        out_shape=[jax.ShapeDtypeStruct((groups, rows, A_WIDTH), BF16),
                   jax.ShapeDtypeStruct((groups, rows, KV_WIDTH), F32),
                   jax.ShapeDtypeStruct((groups, rows, KV_WIDTH), F32),
                   jax.ShapeDtypeStruct((groups, rows, B_WIDTH), BF16),
                   jax.ShapeDtypeStruct((bsz, B_CONV - 1, B_WIDTH), F32)]
                  + [jax.ShapeDtypeStruct(a.shape, BF16) for a in flat],
        scratch_shapes=[pltpu.VMEM((tm, LANES), F32), pltpu.VMEM((tm, LANES), F32),
                        pltpu.VMEM((nseq, SUBLANES, B_WIDTH), F32), pltpu.VMEM((tm, AB_IN), F32)],
        compiler_params=_params(),
        name="attn_in_proj",
    )(xt, g, w, hist, cw, *flat)
    main = [o.reshape(bsz, seq, o.shape[-1]) for o in outs[:n_main - 1]] + [outs[n_main - 1]]
    return main, [o.reshape(a.shape) for o, a in zip(outs[n_main:], cast)]


def _attn_kernel(tq, cq, masked, sinks_ref, q_ref, kc_ref, vc_ref, kp_ref, vp_ref, att_ref,
                 kk_ref, vx_ref, s_ref, e_ref):
    i = pl.program_id(1)
    nk = WINDOW + cq
    nkp = MXU_DIM
    rows_kv = WINDOW + tq
    hd = HEAD_DIM

    kk_ref[0:WINDOW, :] = kp_ref[...].astype(BF16)
    kk_ref[WINDOW:rows_kv, :] = kc_ref[...].astype(BF16)
    kk_ref[rows_kv:, :] = jnp.zeros((kk_ref.shape[0] - rows_kv, KV_WIDTH), BF16)
    vall = jnp.concatenate([vp_ref[...], vc_ref[...]], axis=0)
    vx_ref[0:rows_kv, :] = jnp.concatenate([vall[:, :hd], vall[:, :hd], vall[:, hd:], vall[:, hd:]], axis=1)
    vx_ref[rows_kv:, :] = jnp.zeros((vx_ref.shape[0] - rows_kv, 2 * LANES), F32)

    col = lax.broadcasted_iota(jnp.int32, (A_GROUP * cq, nkp), 1)
    krow = lax.broadcasted_iota(jnp.int32, (nkp, LANES), 0)
    ones = jnp.ones((nkp, LANES), BF16)
    bodies = [(c, kh) for c in range(tq // cq) for kh in range(A_KV_HEADS)]
    fills = []
    for kh in range(A_KV_HEADS):
        sink = jnp.concatenate(
            [jnp.full((cq, nkp), sinks_ref[kh * A_GROUP + g], F32) for g in range(A_GROUP)], axis=0)
        fills.append(jnp.where(col == nk, sink, -jnp.inf))

    for n, (c, kh) in enumerate(bodies):
        r0 = c * cq
        heads = [kh * A_GROUP + g for g in range(A_GROUP)]
        kt = kk_ref[r0:r0 + nkp, kh * hd:(kh + 1) * hd]
        qs = jnp.concatenate([q_ref[r0:r0 + cq, hh * hd:(hh + 1) * hd] for hh in heads], axis=0)
        s = _dot_nt(qs, kt)
        valid = col < nk
        if masked and r0 < WINDOW:
            valid = valid & ((col >= WINDOW - r0) | (i > 0))
        s_ref[n] = jnp.where(valid, s, fills[kh])

    for n in range(len(bodies)):
        s = s_ref[n]
        m = jnp.max(s, axis=-1, keepdims=True)
        e_ref[n] = jnp.exp(s - m).astype(BF16)

    for n, (c, kh) in enumerate(bodies):
        r0 = c * cq
        vt = jnp.where(krow == nk, 0.0, vx_ref[r0:r0 + nkp, kh * LANES:(kh + 1) * LANES]).astype(BF16)
        ox = _dot(e_ref[n], jnp.concatenate([vt, ones], axis=1))
        o = ox[:, :LANES] / ox[:, LANES:]
        for g in range(A_GROUP):
            hh = kh * A_GROUP + g
            lanes = slice((hh % 2) * hd, (hh % 2 + 1) * hd)
            att_ref[r0:r0 + cq, hh * hd:(hh + 1) * hd] = o[g * cq:(g + 1) * cq, lanes].astype(BF16)


def _attn(sinks, q, k, v, k_prev, v_prev, tq, cq, masked):
    bsz, seq, _ = q.shape
    grid = (bsz, seq // tq)
    tok = lambda width: pl.BlockSpec((None, tq, width), lambda b, i: (b, i, 0))
    if masked:
        blocks_per_tile = tq // WINDOW
        prev = pl.BlockSpec((None, WINDOW, KV_WIDTH),
                            lambda b, i: (b, jnp.maximum(i * blocks_per_tile - 1, 0), 0))
    else:
        prev = pl.BlockSpec((None, WINDOW, KV_WIDTH), lambda b, i: (b, 0, 0))
    n_bodies = (tq // cq) * A_KV_HEADS
    kv_rows = (tq // cq - 1) * cq + MXU_DIM
    return pl.pallas_call(
        functools.partial(_attn_kernel, tq, cq, masked),
        grid=grid,
        in_specs=[pl.BlockSpec(memory_space=pltpu.SMEM),
                  tok(A_WIDTH), tok(KV_WIDTH), tok(KV_WIDTH), prev, prev],
        out_specs=tok(A_WIDTH),
        out_shape=jax.ShapeDtypeStruct((bsz, seq, A_WIDTH), BF16),
        scratch_shapes=[pltpu.VMEM((kv_rows, KV_WIDTH), BF16), pltpu.VMEM((kv_rows, 2 * LANES), F32),
                        pltpu.VMEM((n_bodies, A_GROUP * cq, MXU_DIM), F32),
                        pltpu.VMEM((n_bodies, A_GROUP * cq, MXU_DIM), BF16)],
        compiler_params=_params(),
        name="band_attention",
    )(sinks, q, k, v, k_prev, v_prev)


def _proj_ffn_kernel(n_in, gate, n_tiles, *refs):
    long_in, short_in = refs[:n_in], refs[n_in:2 * n_in]
    refs = refs[2 * n_in:]
    ng_ref = None
    if gate:
        ng_ref, refs = refs[0], refs[1:]
    wo_ref, xl_ref, xs_ref, g1_ref, g2_ref, wg_ref, wu_ref, wd_ref, g3_ref, ol_ref, os_ref = refs
    t = pl.program_id(0)

    def tile(ins, x_ref, o_ref):
        acc = None
        if gate:
            y_ref, z_ref = ins
            for gi in range(SSM_GROUPS):
                gcols = slice(gi * SSM_GROUP_W, (gi + 1) * SSM_GROUP_W)
                gg = y_ref[:, gcols].astype(F32) * z_ref[:, gcols].astype(F32)
                gg = gg * lax.rsqrt(jnp.mean(gg * gg, axis=-1, keepdims=True) + EPS)
                part = _dot((gg * ng_ref[:, gcols]).astype(BF16), wo_ref[gcols, :])
                acc = part if acc is None else acc + part
        else:
            off = 0
            for a_ref in ins:
                kdim = a_ref.shape[-1]
                part = _dot(a_ref[...], wo_ref[off:off + kdim, :])
                acc = part if acc is None else acc + part
                off += kdim

        x = x_ref[...] + _rms(acc, g1_ref[...])
        h = _rms(x, g2_ref[...]).astype(BF16)
        h_half = h * 0.5
        f = None
        off = 0
        for width in FF_CHUNKS:
            fc = slice(off, off + width)
            m = (_silu_of_twice(_dot(h_half, wg_ref[:, fc])) * _dot(h, wu_ref[:, fc])).astype(BF16)
            part = _dot(m, wd_ref[fc, :])
            f = part if f is None else f + part
            off += width
        o_ref[...] = x + _rms(f, g3_ref[...])

    pl.when(t < n_tiles)(lambda: tile(long_in, xl_ref, ol_ref))
    pl.when(t == n_tiles)(lambda: tile(short_in, xs_ref, os_ref))


def _proj_ffn(mix_long, mix_short, ng, wo, x_long, x_short, g1, g2, wg, wu, wd, layer, g3, tm):
    bsz, seq, _ = x_long.shape
    tiles_per_seq = seq // tm
    n_tiles = bsz * tiles_per_seq
    n_short = x_short.shape[0] * x_short.shape[1]
    flat = lambda a: a.reshape(1, n_short, a.shape[-1])
    tile_idx = lambda t: jnp.minimum(t, n_tiles - 1)
    ltok = lambda width: pl.BlockSpec((None, tm, width),
                                      lambda t: (tile_idx(t) // tiles_per_seq, tile_idx(t) % tiles_per_seq, 0))
    stok = lambda width: pl.BlockSpec((None, n_short, width), lambda t: (0, 0, 0))
    gate = ng is not None
    ng_args, ng_specs = ([ng], [_const_spec((1, SSM_INNER))]) if gate else ([], [])
    out_long, out_short = pl.pallas_call(
        functools.partial(_proj_ffn_kernel, len(mix_long), gate, n_tiles),
        grid=(n_tiles + 1,),
        in_specs=[ltok(a.shape[-1]) for a in mix_long] + [stok(a.shape[-1]) for a in mix_short] + ng_specs + [
            _const_spec(wo.shape), ltok(D_MODEL), stok(D_MODEL), _const_spec((1, D_MODEL)), _const_spec((1, D_MODEL)),
            _layer_spec((D_MODEL, D_FF), layer), _layer_spec((D_MODEL, D_FF), layer),
            _layer_spec((D_FF, D_MODEL), layer), _const_spec((1, D_MODEL))],
        out_specs=[ltok(D_MODEL), stok(D_MODEL)],
        out_shape=[jax.ShapeDtypeStruct(x_long.shape, F32), jax.ShapeDtypeStruct((1, n_short, D_MODEL), F32)],
        compiler_params=pltpu.CompilerParams(dimension_semantics=("arbitrary",), vmem_limit_bytes=VMEM_LIMIT),
        name="out_proj_ffn",
    )(*mix_long, *[flat(a) for a in mix_short], *ng_args, wo, x_long, flat(x_short), g1, g2, wg, wu, wd, g3)
    return out_long, out_short.reshape(x_short.shape)


def _split3(v):
    t1 = v.astype(BF16)
    r = v - t1.astype(F32)
    t2 = r.astype(BF16)
    t3 = (r - t2.astype(F32)).astype(BF16)
    return t1, t2, t3


def _pack3(v):
    t1, t2, t3 = _split3(v)
    lane = lax.broadcasted_iota(jnp.int32, v.shape, 1)
    return jnp.where(lane < SSM_HEADS, t1,
                     jnp.where(lane < 2 * SSM_HEADS, t2,
                               jnp.where(lane < 3 * SSM_HEADS, t3, jnp.zeros_like(t3))))


def _ssm_in_kernel(tm, cl, x_ref, g_ref, w_ref, wdt_ref, hist_ref, cw_ref, cb_ref, dtb_ref, alog_ref,
                   z_ref, xs_ref, bm_ref, cm_ref, ta_ref, tdt_ref, nh_ref, carry_ref):
    i = pl.program_id(1)

    @pl.when(i == 0)
    def _():
        carry_ref[:, SUBLANES - (SSM_CONV - 1):, :] = hist_ref[...]

    h = _rms(x_ref[...], g_ref[...]).astype(BF16)

    dt = jax.nn.softplus(_dot(h, wdt_ref[...]) + dtb_ref[...])
    tdt_ref[...] = _pack3(dt)
    d1, d2, d3 = _split3(dt * (-jnp.exp(alog_ref[...]) * LOG2E))
    tcol = lax.broadcasted_iota(jnp.int32, (cl, 3 * cl), 1) % cl
    trow = lax.broadcasted_iota(jnp.int32, (cl, 3 * cl), 0)
    tril = jnp.where(tcol <= trow, 1.0, 0.0).astype(BF16)
    acum = []
    for c in range(tm // cl):
        rows = slice(c * cl, (c + 1) * cl)
        acum.append(_dot(tril, jnp.concatenate([d1[rows], d2[rows], d3[rows]], axis=0)))
    ta_ref[...] = _pack3(jnp.concatenate(acum, axis=0))

    h_half = h * 0.5
    for n in range(SSM_INNER // COL_CHUNK):
        cols = slice(n * COL_CHUNK, (n + 1) * COL_CHUNK)
        z_ref[:, cols] = _silu_of_twice(_dot(h_half, w_ref[:, cols])).astype(BF16)

    for n in range(SSM_CONV_CH // COL_CHUNK):
        cols = slice(n * COL_CHUNK, (n + 1) * COL_CHUNK)
        raw = _dot(h, w_ref[:, SSM_INNER + cols.start:SSM_INNER + cols.stop])
        y = _silu_of_twice(_causal_conv(carry_ref, nh_ref, raw, 0.5 * cw_ref[:, cols], cols) + 0.5 * cb_ref[:, cols])
        if cols.stop <= SSM_INNER:
            xs_ref[:, cols] = y
        elif cols.stop <= SSM_INNER + SSM_GN:
            bm_ref[:, cols.start - SSM_INNER:cols.stop - SSM_INNER] = y.astype(BF16)
        else:
            lo = SSM_INNER + SSM_GN
            cm_ref[:, cols.start - lo:cols.stop - lo] = y.astype(BF16)


def _ssm_in(x, g, w, wdt, hist, cw, cb, dtb, alog, tm, cl):
    bsz, seq, _ = x.shape
    xt, nseq = _tile_rows(x, tm)
    groups, rows, _ = xt.shape
    tok = lambda width: pl.BlockSpec((None, tm, width), lambda b, i: (b, i, 0))
    hist_spec = pl.BlockSpec((nseq, SSM_CONV - 1, SSM_CONV_CH), lambda b, i: (b, 0, 0))
    outs = pl.pallas_call(
        functools.partial(_ssm_in_kernel, tm, cl),
        grid=(groups, rows // tm),
        in_specs=[tok(D_MODEL), _const_spec((1, D_MODEL)), _const_spec(w.shape), _const_spec((D_MODEL, LANES)),
                  hist_spec,
                  _const_spec((SSM_CONV, SSM_CONV_CH)), _const_spec((1, SSM_CONV_CH)), _const_spec((1, LANES)),
                  _const_spec((1, LANES))],
        out_specs=[tok(SSM_INNER), tok(SSM_INNER), tok(SSM_GN), tok(SSM_GN), tok(LANES), tok(LANES), hist_spec],
        out_shape=[jax.ShapeDtypeStruct((groups, rows, SSM_INNER), BF16),
                   jax.ShapeDtypeStruct((groups, rows, SSM_INNER), F32),
                   jax.ShapeDtypeStruct((groups, rows, SSM_GN), BF16),
                   jax.ShapeDtypeStruct((groups, rows, SSM_GN), BF16),
                   jax.ShapeDtypeStruct((groups, rows, LANES), BF16),
                   jax.ShapeDtypeStruct((groups, rows, LANES), BF16),
                   jax.ShapeDtypeStruct((bsz, SSM_CONV - 1, SSM_CONV_CH), F32)],
        scratch_shapes=[pltpu.VMEM((nseq, SUBLANES, SSM_CONV_CH), F32)],
        compiler_params=_params(),
        name="ssm_in_proj",
    )(xt, g, w, wdt, hist, cw, cb, dtb, alog)
    return [o.reshape(bsz, seq, o.shape[-1]) for o in outs[:-1]] + [outs[-1]]


def _pad_rows(a, rows):
    if a.shape[0] == rows:
        return a
    return jnp.concatenate([a, jnp.zeros((rows - a.shape[0], a.shape[1]), a.dtype)], axis=0)


def _ssd_kernel(ts, cl, xs_ref, bm_ref, cm_ref, ta_ref, tdt_ref, h0_ref, dskip_ref,
                yo_ref, hout_ref, ht_ref, y_ref, e3_ref, aexp_ref, xdt_ref, st_ref, cb_ref, bmt_ref, w2_ref, rhs_ref):
    b = pl.program_id(0)
    i = pl.program_id(1)
    p = SSM_HEADDIM
    nc = ts // cl
    pairs = SSM_GROUP_W // LANES

    @pl.when((b == 0) & (i == 0))
    def _():
        kk = lax.broadcasted_iota(jnp.int32, (LANES, SSM_INNER), 0)
        cc = lax.broadcasted_iota(jnp.int32, (LANES, SSM_INNER), 1)
        hit = (kk % SSM_HEADS == cc // p) & (kk < 3 * SSM_HEADS)
        e3_ref[...] = jnp.where(hit, 1.0, 0.0).astype(BF16)

    @pl.when(i == 0)
    def _():
        ht_ref[...] = h0_ref[...].T

    lane = lax.broadcasted_iota(jnp.int32, (cl, LANES), 1)
    row = lax.broadcasted_iota(jnp.int32, (cl, LANES), 0)
    causal2 = (lane % p) <= row
    left = lane.astype(F32).astype(BF16) < p
    diag2 = (lane % p) == row
    groups = [(gi, slice(gi * SSM_GROUP_W, (gi + 1) * SSM_GROUP_W), slice(gi * SSM_STATE, (gi + 1) * SSM_STATE))
              for gi in range(SSM_GROUPS)]
    chunks = [(c, slice(c * cl, (c + 1) * cl)) for c in range(nc)]

    for gi, gcols, _ in groups:
        e3g = e3_ref[:, gcols]
        aexp_ref[:, gcols] = _dot(ta_ref[...], e3g)
        xdt_ref[:, gcols] = (xs_ref[:, gcols] * _dot(tdt_ref[...], e3g)).astype(BF16)

    for c, rows in chunks:
        for gi, _, scol in groups:
            bmp = _pad_rows(bm_ref[rows, scol], p)
            cbm = _dot_nt(cm_ref[rows, scol], bmp)
            cb_ref[c * SSM_GROUPS + gi] = jnp.concatenate([cbm, cbm], axis=1)
            bmt_ref[c * SSM_GROUPS + gi] = bmp.astype(F32).T.astype(BF16)

    for c, rows in chunks:
        for gi, _, _ in groups:
            for jj in range(pairs):
                idx = (c * SSM_GROUPS + gi) * pairs + jj
                cols = slice(gi * SSM_GROUP_W + jj * LANES, gi * SSM_GROUP_W + (jj + 1) * LANES)
                a_pair = aexp_ref[rows, cols]
                a_src = jnp.sum(jnp.where(diag2, a_pair, 0.0), axis=0, keepdims=True)
                seg = a_pair - a_src
                dec = jnp.exp2(jnp.where(causal2, seg, -jnp.inf))
                w2_ref[idx] = (dec * cb_ref[c * SSM_GROUPS + gi]).astype(BF16)
                x2 = xdt_ref[rows, cols]
                top = _pad_rows(jnp.where(left, x2, jnp.zeros_like(x2)), p)
                bot = _pad_rows(jnp.where(left, jnp.zeros_like(x2), x2), p)
                rhs_ref[idx] = jnp.concatenate([top, bot], axis=0)

    for c, rows in chunks:
        for gi, _, _ in groups:
            for jj in range(pairs):
                idx = (c * SSM_GROUPS + gi) * pairs + jj
                cols = slice(gi * SSM_GROUP_W + jj * LANES, gi * SSM_GROUP_W + (jj + 1) * LANES)
                y_ref[rows, cols] = _dot(w2_ref[idx], rhs_ref[idx])

    for c, rows in chunks:
        for gi, gcols, _ in groups:
            a_g = aexp_ref[rows, gcols]
            xw = xdt_ref[rows, gcols] * jnp.exp2(a_g[cl - 1:cl, :] - a_g).astype(BF16)
            st_ref[c, :, gcols] = _dot(bmt_ref[c * SSM_GROUPS + gi], _pad_rows(xw, p))

    for c, rows in chunks:
        for gi, gcols, scol in groups:
            hprev = ht_ref[:, gcols]
            a_g = aexp_ref[rows, gcols]
            y_off = _dot(cm_ref[rows, scol], hprev.astype(BF16)) * jnp.exp2(a_g)
            y_ref[rows, gcols] = y_ref[rows, gcols] + y_off
            ht_ref[:, gcols] = hprev * jnp.exp2(a_g[cl - 1:cl, :]) + st_ref[c, :, gcols]

    yo_ref[...] = (y_ref[...] + dskip_ref[...] * xs_ref[...]).astype(BF16)

    @pl.when(i == pl.num_programs(1) - 1)
    def _():
        hout_ref[...] = ht_ref[...].T


def _ssd(xs, bm, cm, ta, tdt, h0, dskip, ts, cl):
    bsz, seq, _ = xs.shape
    tok = lambda width: pl.BlockSpec((None, ts, width), lambda b, i: (b, i, 0))
    st_spec = pl.BlockSpec((None, SSM_INNER, SSM_STATE), lambda b, i: (b, 0, 0))
    nc = ts // cl
    n_cg = nc * SSM_GROUPS
    n_pairs = n_cg * (SSM_GROUP_W // LANES)
    return pl.pallas_call(
        functools.partial(_ssd_kernel, ts, cl),
        grid=(bsz, seq // ts),
        in_specs=[tok(SSM_INNER), tok(SSM_GN), tok(SSM_GN), tok(LANES), tok(LANES), st_spec,
                  _const_spec((1, SSM_INNER))],
        out_specs=[tok(SSM_INNER), st_spec],
        out_shape=[jax.ShapeDtypeStruct((bsz, seq, SSM_INNER), BF16),
                   jax.ShapeDtypeStruct((bsz, SSM_INNER, SSM_STATE), F32)],
        scratch_shapes=[pltpu.VMEM((SSM_STATE, SSM_INNER), F32), pltpu.VMEM((ts, SSM_INNER), F32),
                        pltpu.VMEM((LANES, SSM_INNER), BF16), pltpu.VMEM((ts, SSM_INNER), F32),
                        pltpu.VMEM((ts, SSM_INNER), BF16), pltpu.VMEM((nc, SSM_STATE, SSM_INNER), F32),
                        pltpu.VMEM((n_cg, cl, LANES), F32), pltpu.VMEM((n_cg, SSM_STATE, SSM_HEADDIM), BF16),
                        pltpu.VMEM((n_pairs, cl, LANES), BF16), pltpu.VMEM((n_pairs, LANES, LANES), BF16)],
        compiler_params=_params(),
        name="ssd_scan",
    )(xs, bm, cm, ta, tdt, h0, dskip)


def _layer0_mixers(x, pos0, sconv_hist, k_cache, v_cache, wts, tiles, cast=()):
    t_in, t_att, c_att = tiles
    (q, k, v, gated, new_sconv), casted = _attn_in(x, wts["g"][0][0], wts["ab_w_in"], sconv_hist, wts["sconv_w"],
                                                   pos0, t_in, cast)
    masked = k_cache is None
    k_prev, v_prev = (k, v) if masked else (k_cache, v_cache)
    att = _attn(wts["sinks"], q, k, v, k_prev, v_prev, t_att, c_att, masked)
    bsz, seq, _ = x.shape
    new_k = k[:, seq - min(WINDOW, seq):].reshape(bsz, -1, A_KV_HEADS, HEAD_DIM)
    new_v = v[:, seq - min(WINDOW, seq):].reshape(bsz, -1, A_KV_HEADS, HEAD_DIM)
    return [att, gated], (new_k, new_v, new_sconv), casted


def _layer1_mixer(x, conv_hist, ssm_state, wts, tiles):
    t_sin, t_ssd, c_ssd = tiles
    bsz = x.shape[0]
    z, xs, bm, cm, ta, tdt, new_conv = _ssm_in(
        x, wts["g"][1][0], wts["ssm_w_in"], wts["w_dt"], conv_hist, wts["ssm_conv_w"],
        wts["ssm_conv_b"], wts["dt_bias"], wts["a_log"], t_sin, c_ssd)
    y, new_state = _ssd(xs, bm, cm, ta, tdt, ssm_state.reshape(bsz, SSM_INNER, SSM_STATE),
                        wts["d_skip"], t_ssd, c_ssd)
    return [y, z], (new_conv, new_state.reshape(bsz, SSM_HEADS, SSM_HEADDIM, SSM_STATE))


def kernel(x_prompt, x_sample, cache_attn_k, cache_attn_v, state_sconv, state_ssm_conv, state_ssm, norm_g,
           ab_w_in, ab_w_out, attn_sinks, sconv_w, ssm_w_in, ssm_conv_w, ssm_conv_b, ssm_dt_bias, ssm_a_log,
           ssm_d, ssm_norm_g, ssm_w_out, ffn_w_gate, ffn_w_up, ffn_w_down):
    w_in1 = ssm_w_in[0]
    wts = {
        "ab_w_in": ab_w_in[0].astype(BF16),
        "ab_w_out": ab_w_out[0].astype(BF16),
        "sinks": attn_sinks[0],
        "sconv_w": sconv_w[0],
        "ssm_w_in": w_in1.astype(BF16),
        "w_dt": jnp.tile(w_in1[:, SSM_INNER + SSM_CONV_CH:], (1, DT_REP)).astype(BF16),
        "ssm_conv_w": ssm_conv_w[0],
        "ssm_conv_b": ssm_conv_b[0].reshape(1, -1),
        "dt_bias": jnp.tile(ssm_dt_bias[0], DT_REP).reshape(1, -1),
        "a_log": jnp.tile(ssm_a_log[0], DT_REP).reshape(1, -1),
        "d_skip": jnp.repeat(ssm_d[0], SSM_HEADDIM).reshape(1, -1),
        "ssm_norm_g": ssm_norm_g[0].reshape(1, -1),
    }
    wts["g"] = [[norm_g[l, j].reshape(1, -1) for j in range(4)] for l in range(norm_g.shape[0])]
    g = wts["g"]
    bp = x_prompt.shape[0]
    bs, ls = x_sample.shape[0], x_sample.shape[1]
    t_ffn0, t_ffn1 = 1024, 512

    later = (ssm_w_out[0], ffn_w_gate, ffn_w_up, ffn_w_down)
    mix_p, (kp, vp, scp), casted = _layer0_mixers(x_prompt, 0, jnp.zeros((bp, B_CONV - 1, B_WIDTH), F32), None, None,
                                                  wts, (1024, 1024, CHUNK), later)
    wts["ssm_w_out"], wts["wg"], wts["wu"], wts["wd"] = casted
    mix_s, (ks, vs, scs), _ = _layer0_mixers(x_sample, PAST_LEN, state_sconv[0],
                                             cache_attn_k[0].reshape(bs, -1, KV_WIDTH),
                                             cache_attn_v[0].reshape(bs, -1, KV_WIDTH), wts, (bs * ls, ls, ls))
    xp, xs = _proj_ffn(mix_p, mix_s, None, wts["ab_w_out"], x_prompt, x_sample, g[0][1], g[0][2],
                       wts["wg"], wts["wu"], wts["wd"], 0, g[0][3], t_ffn0)

    mix_p, (ccp, ssp) = _layer1_mixer(xp, jnp.zeros((bp, SSM_CONV - 1, SSM_CONV_CH), F32),
                                      jnp.zeros((bp, SSM_HEADS, SSM_HEADDIM, SSM_STATE), F32), wts, (512, 512, CHUNK))
    mix_s, (ccs, sss) = _layer1_mixer(xs, state_ssm_conv[0], state_ssm[0], wts, (bs * ls, ls, ls))
    yp, ys = _proj_ffn(mix_p, mix_s, wts["ssm_norm_g"], wts["ssm_w_out"], xp, xs, g[1][1], g[1][2],
                       wts["wg"], wts["wu"], wts["wd"], 1, g[1][3], t_ffn1)

    lead = lambda a: a[None]
    return (yp, ys, lead(kp), lead(vp), lead(scp), lead(ccp), lead(ssp),
            lead(ks), lead(vs), lead(scs), lead(ccs), lead(sss))
```
